```python
import math
import jax, jax.numpy as jnp
from jax import lax
import numpy as np

D_MODEL = 1024
BATCH = 8
SEQ = 4096
DEPTH = 4

MIX_WIDTH = D_MODEL
FOX_HEADS = 8
FOX_HEAD_DIM = MIX_WIDTH // 2 // FOX_HEADS
FOX_WIDTH = FOX_HEADS * FOX_HEAD_DIM
MLSTM_HEADS = 4
MLSTM_HEAD_DIM = MIX_WIDTH // 2 // MLSTM_HEADS
MLSTM_WIDTH = MLSTM_HEADS * MLSTM_HEAD_DIM
QUERY_BLOCK = 128
MLSTM_CHUNK = 64
CONV_WIDTH = 4
N_EXPERTS = 16
N_GROUPS = 4
EXPERTS_PER_GROUP = N_EXPERTS // N_GROUPS
TOP_K = 2
D_FF_EXPERT = D_MODEL // 2
N_ADA = 6
EPS = 1e-6

FOX_Q = 0
FOX_K = FOX_Q + FOX_WIDTH
FOX_V = FOX_K + FOX_WIDTH
FOX_F = FOX_V + FOX_WIDTH
ML_Q = FOX_F + FOX_HEADS
ML_K = ML_Q + MLSTM_WIDTH
ML_V = ML_K + MLSTM_WIDTH
ML_I = ML_V + MLSTM_WIDTH
ML_F = ML_I + MLSTM_HEADS
ML_O = ML_F + MLSTM_HEADS
IN_COLS = ML_O + MLSTM_WIDTH

kernel_name = "fox_mlstm_hybrid_moe_adaln"


def rmsnorm(x, g):
    xf = x.astype(jnp.float32)
    y = xf * lax.rsqrt(jnp.mean(xf * xf, axis=-1, keepdims=True) + EPS)
    return (y * g.astype(jnp.float32)).astype(x.dtype)


def split_heads(t, n_heads):
    b, s, _ = t.shape
    return t.reshape(b, s, n_heads, -1).transpose(0, 2, 1, 3)


def causal_conv_silu(u, w, b):
    k = w.shape[0]
    s = u.shape[1]
    up = jnp.pad(u, ((0, 0), (k - 1, 0), (0, 0)))
    y = b + up[:, 0:s] * w[0]
    for j in range(1, k):
        y = y + up[:, j:j + s] * w[j]
    return jax.nn.silu(y)


def fox_attention(q, k, v, f_pre):
    s = q.shape[2]
    scale = q.shape[-1] ** -0.5
    cum = jnp.cumsum(jax.nn.log_sigmoid(f_pre), axis=-1)
    outs = []
    for blk in range(s // QUERY_BLOCK):
        q0 = blk * QUERY_BLOCK
        q1 = q0 + QUERY_BLOCK
        logits = (jnp.einsum('bhqd,bhkd->bhqk', q[:, :, q0:q1], k[:, :, :q1]) * scale
                  + cum[:, :, q0:q1, None] - cum[:, :, None, :q1])
        causal = (q0 + jnp.arange(QUERY_BLOCK))[:, None] >= jnp.arange(q1)[None, :]
        logits = jnp.where(causal, logits, -jnp.inf)
        p = jax.nn.softmax(logits, axis=-1)
        outs.append(jnp.einsum('bhqk,bhkd->bhqd', p, v[:, :, :q1]))
    return jnp.concatenate(outs, axis=2)


def mlstm_chunkwise(q, k, v, i_pre, f_pre):
    b, h, s, d = q.shape
    L = MLSTM_CHUNK
    nc = s // L
    qc = q.reshape(b, h, nc, L, d)
    kc = (k * (d ** -0.5)).reshape(b, h, nc, L, d)
    vc = v.reshape(b, h, nc, L, d)
    lf = jax.nn.log_sigmoid(f_pre).reshape(b, h, nc, L)
    ig = i_pre.reshape(b, h, nc, L)
    bcum = jnp.cumsum(lf, axis=-1)
    gtot = bcum[..., -1]
    a = gtot[..., None] - bcum + ig
    m_loc = jnp.max(a, axis=-1)
    wloc = jnp.exp(a - m_loc[..., None])
    c_loc = jnp.einsum('bhcl,bhcld,bhcle->bhcde', wloc, vc, kc)
    n_loc = jnp.einsum('bhcl,bhcle->bhce', wloc, kc)

    def step(carry, inp):
        c_st, n_st, m_st = carry
        cl, nl, ml, gc = inp
        m_new = jnp.maximum(gc + m_st, ml)
        sp = jnp.exp(gc + m_st - m_new)
        sl = jnp.exp(ml - m_new)
        c_new = sp[..., None, None] * c_st + sl[..., None, None] * cl
        n_new = sp[..., None] * n_st + sl[..., None] * nl
        return (c_new, n_new, m_new), (c_st, n_st, m_st)

    init = (jnp.zeros((b, h, d, d), jnp.float32), jnp.zeros((b, h, d), jnp.float32),
            jnp.zeros((b, h), jnp.float32))
    xs = (jnp.moveaxis(c_loc, 2, 0), jnp.moveaxis(n_loc, 2, 0),
          jnp.moveaxis(m_loc, 2, 0), jnp.moveaxis(gtot, 2, 0))
    _, (c_prev, n_prev, m_prev) = lax.scan(step, init, xs)
    c_prev = jnp.moveaxis(c_prev, 0, 2)
    n_prev = jnp.moveaxis(n_prev, 0, 2)
    m_prev = jnp.moveaxis(m_prev, 0, 2)

    log_d = bcum[..., :, None] - bcum[..., None, :] + ig[..., None, :]
    causal = jnp.tril(jnp.ones((L, L), dtype=bool))
    log_d = jnp.where(causal, log_d, -jnp.inf)
    m_intra = jnp.max(log_d, axis=-1)
    m_inter = bcum + m_prev[..., None]
    m_t = jnp.maximum(m_intra, m_inter)
    dmat = jnp.exp(log_d - m_t[..., None])
    scores = jnp.einsum('bhcld,bhcsd->bhcls', qc, kc) * dmat
    inter_scale = jnp.exp(m_inter - m_t)
    num = (jnp.einsum('bhcls,bhcsd->bhcld', scores, vc)
           + jnp.einsum('bhcle,bhcde->bhcld', qc, c_prev) * inter_scale[..., None])
    den = jnp.sum(scores, axis=-1) + jnp.einsum('bhcle,bhce->bhcl', qc, n_prev) * inter_scale
    den = jnp.maximum(jnp.abs(den), jnp.exp(-m_t))
    return (num / den[..., None]).reshape(b, h, s, d)


def head_rmsnorm(t, g):
    tf = t.astype(jnp.float32)
    y = tf * lax.rsqrt(jnp.mean(tf * tf, axis=-1, keepdims=True) + EPS)
    return y * g.astype(jnp.float32)


def hybrid_mixer(hn, w_in, conv_w, conv_b, fox_fb, ml_ib, ml_fb, fox_ng, ml_ng, w_out):
    bsz, s, _ = hn.shape
    p = hn @ w_in
    pf = p.astype(jnp.float32)
    fq = split_heads(pf[..., FOX_Q:FOX_K], FOX_HEADS)
    fk = split_heads(pf[..., FOX_K:FOX_V], FOX_HEADS)
    fv = split_heads(pf[..., FOX_V:FOX_F], FOX_HEADS)
    ff = (pf[..., FOX_F:ML_Q] + fox_fb.astype(jnp.float32)).transpose(0, 2, 1)
    h_fox = fox_attention(fq, fk, fv, ff).transpose(0, 2, 1, 3)
    h_fox = head_rmsnorm(h_fox, fox_ng).reshape(bsz, s, FOX_WIDTH)
    qk = causal_conv_silu(pf[..., ML_Q:ML_V], conv_w.astype(jnp.float32), conv_b.astype(jnp.float32))
    mq = split_heads(qk[..., :MLSTM_WIDTH], MLSTM_HEADS)
    mk = split_heads(qk[..., MLSTM_WIDTH:], MLSTM_HEADS)
    mv = split_heads(pf[..., ML_V:ML_I], MLSTM_HEADS)
    mi = (pf[..., ML_I:ML_F] + ml_ib.astype(jnp.float32)).transpose(0, 2, 1)
    mf = (pf[..., ML_F:ML_O] + ml_fb.astype(jnp.float32)).transpose(0, 2, 1)
    mo = jax.nn.sigmoid(pf[..., ML_O:IN_COLS])
    h_ml = mlstm_chunkwise(mq, mk, mv, mi, mf).transpose(0, 2, 1, 3)
    h_ml = mo * head_rmsnorm(h_ml, ml_ng).reshape(bsz, s, MLSTM_WIDTH)
    mixed = jnp.concatenate([h_fox, h_ml], axis=-1).astype(hn.dtype)
    return mixed @ w_out


def route(t, w_router, b_router):
    n_tok = t.shape[0]
    aff = jax.nn.sigmoid((t @ w_router).astype(jnp.float32))
    sel = (aff + b_router.astype(jnp.float32)).reshape(n_tok, N_GROUPS, EXPERTS_PER_GROUP)
    top_vals, top_idx = lax.top_k(sel, TOP_K)
    grp = jnp.argmax(jnp.sum(top_vals, axis=-1), axis=-1)
    idx_in = jnp.take_along_axis(top_idx, grp[:, None, None], axis=1)[:, 0]
    expert_idx = grp[:, None] * EXPERTS_PER_GROUP + idx_in
    w = jnp.take_along_axis(aff, expert_idx, axis=1)
    w = w / jnp.sum(w, axis=-1, keepdims=True)
    return jnp.sum(jax.nn.one_hot(expert_idx, N_EXPERTS, dtype=jnp.float32) * w[..., None], axis=1)


def grouped_moe(hn, w_router, b_router, w_gate, w_up, w_down):
    bsz, s, d = hn.shape
    t = hn.reshape(bsz * s, d)
    gates = route(t, w_router, b_router).astype(t.dtype)
    y = None
    for e in range(N_EXPERTS):
        a = jax.nn.silu(t @ w_gate[e]) * (t @ w_up[e])
        ye = (a @ w_down[e]) * gates[:, e:e + 1]
        y = ye if y is None else y + ye
    return y.reshape(bsz, s, d)


def setup_inputs(seed: int = 0) -> dict:
    key = jax.random.key(seed)
    ks = jax.random.split(key, 24)
    f32 = jnp.float32
    nrm = lambda k, shape, sc: jax.random.normal(k, shape, f32) * sc
    return {
        "x": nrm(ks[0], (BATCH, SEQ, D_MODEL), 1.0),
        "c": nrm(ks[1], (BATCH, D_MODEL), 1.0),
        "w_in": nrm(ks[2], (DEPTH, D_MODEL, IN_COLS), D_MODEL ** -0.5),
        "conv_w": nrm(ks[3], (DEPTH, CONV_WIDTH, 2 * MLSTM_WIDTH), CONV_WIDTH ** -0.5),
        "conv_b": nrm(ks[4], (DEPTH, 2 * MLSTM_WIDTH), 0.02),
        "fox_f_bias": jax.random.uniform(ks[5], (DEPTH, FOX_HEADS), f32, 2.0, 5.0),
        "mlstm_i_bias": nrm(ks[6], (DEPTH, MLSTM_HEADS), 0.1),
        "mlstm_f_bias": jax.random.uniform(ks[7], (DEPTH, MLSTM_HEADS), f32, 3.0, 6.0),
        "fox_out_norm": 1.0 + nrm(ks[8], (DEPTH, FOX_HEADS, FOX_HEAD_DIM), 0.02),
        "mlstm_out_norm": 1.0 + nrm(ks[9], (DEPTH, MLSTM_HEADS, MLSTM_HEAD_DIM), 0.02),
        "w_out": nrm(ks[10], (DEPTH, MIX_WIDTH, D_MODEL), MIX_WIDTH ** -0.5),
        "w_ada": nrm(ks[11], (DEPTH, D_MODEL, N_ADA * D_MODEL), 0.5 * D_MODEL ** -0.5),
        "b_ada": nrm(ks[12], (DEPTH, N_ADA * D_MODEL), 0.02),
        "norm_mix": 1.0 + nrm(ks[13], (DEPTH, D_MODEL), 0.02),
        "norm_ffn": 1.0 + nrm(ks[14], (DEPTH, D_MODEL), 0.02),
        "w_router": nrm(ks[15], (D_MODEL, N_EXPERTS), D_MODEL ** -0.5),
        "b_router": nrm(ks[16], (N_EXPERTS,), 0.01),
        "w_gate": nrm(ks[17], (DEPTH, N_EXPERTS, D_MODEL, D_FF_EXPERT), D_MODEL ** -0.5),
        "w_up": nrm(ks[18], (DEPTH, N_EXPERTS, D_MODEL, D_FF_EXPERT), D_MODEL ** -0.5),
        "w_down": nrm(ks[19], (DEPTH, N_EXPERTS, D_FF_EXPERT, D_MODEL), D_FF_EXPERT ** -0.5),
        "norm_final": 1.0 + nrm(ks[20], (D_MODEL,), 0.02),
    }


def reference(x, c, w_in, conv_w, conv_b, fox_f_bias, mlstm_i_bias, mlstm_f_bias,
              fox_out_norm, mlstm_out_norm, w_out, w_ada, b_ada, norm_mix, norm_ffn,
              w_router, b_router, w_gate, w_up, w_down, norm_final):
    c_act = jax.nn.silu(c)
    for l in range(DEPTH):
        mod = c_act @ w_ada[l] + b_ada[l]
        sh1, sc1, g1, sh2, sc2, g2 = [m[:, None, :] for m in jnp.split(mod, N_ADA, axis=-1)]
        hn = rmsnorm(x, norm_mix[l]) * (1.0 + sc1) + sh1
        x = x + g1 * hybrid_mixer(hn, w_in[l], conv_w[l], conv_b[l], fox_f_bias[l],
                                  mlstm_i_bias[l], mlstm_f_bias[l], fox_out_norm[l],
                                  mlstm_out_norm[l], w_out[l])
        hn = rmsnorm(x, norm_ffn[l]) * (1.0 + sc2) + sh2
        x = x + g2 * grouped_moe(hn, w_router, b_router, w_gate[l], w_up[l], w_down[l])
    return rmsnorm(x, norm_final)
```

```python
import functools

import jax
import jax.numpy as jnp
from jax import lax
from jax.experimental import pallas as pl
from jax.experimental.pallas import tpu as pltpu

F32 = jnp.float32
BF16 = jnp.bfloat16

LANES = 128
SUBLANES = 8
VMEM_BYTES_V7X = 64 * 1024 * 1024

D_MODEL = 1024
FOX_HEADS = 8
FOX_HEAD_DIM = 64
FOX_WIDTH = FOX_HEADS * FOX_HEAD_DIM
ML_HEADS = 4
ML_HEAD_DIM = 128
ML_WIDTH = ML_HEADS * ML_HEAD_DIM
CONV_WIDTH = 4
N_EXPERTS = 16
N_GROUPS = 4
EXPERTS_PER_GROUP = 4
D_FF = 512
N_ADA = 6
EPS = 1e-6
NEG = -1e30

FOX_Q = 0
FOX_F = 3 * FOX_WIDTH
ML_Q = FOX_F + FOX_HEADS
ML_I = ML_Q + 3 * ML_WIDTH
ML_F = ML_I + ML_HEADS
ML_O = ML_F + ML_HEADS
IN_COLS = ML_O + ML_WIDTH

MAIN_COLS = 7 * 512
BLK_FQ, BLK_FK, BLK_FV, BLK_MQ, BLK_MK, BLK_MV, BLK_MO = range(7)
GATE_FOX = 0
GATE_ML = FOX_HEADS
GATE_ROWS = 16

TOK_ROWS = D_MODEL // LANES


def _cparams(n_grid, vmem_mb):
    return pltpu.CompilerParams(
        dimension_semantics=("arbitrary",) * n_grid,
        vmem_limit_bytes=vmem_mb * 1024 * 1024)


def _silu(x):
    return x * jax.nn.sigmoid(x)


def _log_sigmoid(z):
    return jnp.minimum(z, 0.0) - jnp.log1p(jnp.exp(-jnp.abs(z)))


def _cumsum_rows(x):
    n = x.shape[0]
    row = lax.broadcasted_iota(jnp.int32, x.shape, 0)
    s = 1
    while s < n:
        x = x + jnp.where(row >= s, pltpu.roll(x, s, axis=0), 0.0)
        s *= 2
    return x


def _from_token_tiles(ref, n_tok):
    return jnp.concatenate(
        [ref[pl.ds(c, n_tok, stride=TOK_ROWS), :] for c in range(TOK_ROWS)], axis=1)


def _to_token_tiles(ref, val, row0=0):
    n = val.shape[0]
    for c in range(TOK_ROWS):
        ref[pl.ds(row0 * TOK_ROWS + c, n, stride=TOK_ROWS), :] = val[:, c * LANES:(c + 1) * LANES]


def _ada_kernel(c_ref, w_ref, b_ref, o_ref):
    c = c_ref[...]
    o_ref[0] = jnp.dot(_silu(c), w_ref[0], preferred_element_type=F32,
                       precision=lax.Precision.HIGHEST) + b_ref[0]


def _ada_call(c, w_ada, b_ada):
    depth, d, n = w_ada.shape
    bsz = c.shape[0]
    tn = 1536
    return pl.pallas_call(
        _ada_kernel,
        grid=(depth, n // tn),
        in_specs=[
            pl.BlockSpec((bsz, d), lambda l, j: (0, 0)),
            pl.BlockSpec((1, d, tn), lambda l, j: (l, 0, j)),
            pl.BlockSpec((1, 1, tn), lambda l, j: (l, 0, j)),
        ],
        out_specs=pl.BlockSpec((1, bsz, tn), lambda l, j: (l, 0, j)),
        out_shape=jax.ShapeDtypeStruct((depth, bsz, n), F32),
        compiler_params=_cparams(2, 32),
        name="ada_mod",
    )(c, w_ada, b_ada.reshape(depth, 1, n))


def _in_proj_kernel(*refs, tm, combine):
    if combine:
        (x_ref, ott_ref, g2_ref, nm_ref, sc_ref, sh_ref, wm_ref, wg_ref, gb_ref, cw_ref, cb_ref,
         xn_ref, qkv_ref, fc_ref, gc_ref, fr_ref, gr_ref, fcar, ccar) = refs
    else:
        (x_ref, nm_ref, sc_ref, sh_ref, wm_ref, wg_ref, gb_ref, cw_ref, cb_ref,
         qkv_ref, fc_ref, gc_ref, fr_ref, gr_ref, fcar, ccar) = refs

    @pl.when(pl.program_id(1) == 0)
    def _():
        fcar[...] = jnp.zeros_like(fcar)
        ccar[...] = jnp.zeros_like(ccar)

    x = x_ref[...]
    if combine:
        x = x + g2_ref[0] * _from_token_tiles(ott_ref, tm)
        xn_ref[...] = x
    ms = jnp.mean(x * x, axis=1, keepdims=True)
    hn = x * lax.rsqrt(ms + EPS) * nm_ref[...]
    hn = hn * (1.0 + sc_ref[0]) + sh_ref[0]
    hb = hn.astype(BF16)

    gp = jnp.dot(hb, wg_ref[...], preferred_element_type=F32) + gb_ref[...]
    lf = _log_sigmoid(gp[:, :LANES])
    fcum = _cumsum_rows(lf) + fcar[...]
    fcar[...] = fcum[tm - 1:tm, :]
    gml = gp[:, LANES:] - fcum
    fc_ref[...] = fcum
    gc_ref[...] = gml
    fr_ref[0] = fcum.T[:GATE_ROWS, :]
    gr_ref[0] = gml.T[:GATE_ROWS, :]

    def mm(j):
        return jnp.dot(hb, wm_ref[:, j * 512:(j + 1) * 512], preferred_element_type=F32)

    def put(j, v):
        qkv_ref[:, j * 512:(j + 1) * 512] = v.astype(BF16)

    put(BLK_FQ, mm(BLK_FQ) * (FOX_HEAD_DIM ** -0.5))
    put(BLK_FK, mm(BLK_FK))
    put(BLK_FV, mm(BLK_FV))
    put(BLK_MV, mm(BLK_MV))
    put(BLK_MO, jax.nn.sigmoid(mm(BLK_MO)))

    u = jnp.concatenate([mm(BLK_MQ), mm(BLK_MK)], axis=1)
    prev = ccar[...]
    ccar[...] = u[tm - SUBLANES:tm, :]
    row8 = lax.broadcasted_iota(jnp.int32, prev.shape, 0)
    y = cb_ref[...] + cw_ref[CONV_WIDTH - 1:CONV_WIDTH, :] * u
    for k in range(1, CONV_WIDTH):
        r = pltpu.roll(u, k, axis=0)
        top = jnp.where(row8 < k, pltpu.roll(prev, k, axis=0), r[:SUBLANES])
        shifted = jnp.concatenate([top, r[SUBLANES:]], axis=0)
        y = y + cw_ref[CONV_WIDTH - 1 - k:CONV_WIDTH - k, :] * shifted
    act = _silu(y)
    put(BLK_MQ, act[:, :ML_WIDTH])
    put(BLK_MK, act[:, ML_WIDTH:] * (ML_HEAD_DIM ** -0.5))


def _in_proj_call(x, moe_tt, g2, nm, sc, sh, wm, wg, gb, cw, cb, *, bsz, seq, tm=512):
    t, d = x.shape
    ns = seq // tm
    combine = moe_tt is not None
    row = lambda b, s: (b * ns + s, 0)
    per_b = lambda b, s: (b, 0, 0)
    const2 = lambda b, s: (0, 0)
    in_specs = [pl.BlockSpec((tm, d), row)]
    args = [x]
    if combine:
        in_specs += [pl.BlockSpec((tm * TOK_ROWS, LANES), row), pl.BlockSpec((1, 1, d), per_b)]
        args += [moe_tt, g2]
    in_specs += [
        pl.BlockSpec((1, d), const2),
        pl.BlockSpec((1, 1, d), per_b),
        pl.BlockSpec((1, 1, d), per_b),
        pl.BlockSpec((d, MAIN_COLS), const2),
        pl.BlockSpec((d, 2 * LANES), const2),
        pl.BlockSpec((1, 2 * LANES), const2),
        pl.BlockSpec((CONV_WIDTH, 2 * ML_WIDTH), const2),
        pl.BlockSpec((1, 2 * ML_WIDTH), const2),
    ]
    args += [nm, sc, sh, wm, wg, gb, cw, cb]
    out_specs = []
    out_shape = []
    if combine:
        out_specs.append(pl.BlockSpec((tm, d), row))
        out_shape.append(jax.ShapeDtypeStruct((t, d), F32))
    out_specs += [
        pl.BlockSpec((tm, MAIN_COLS), row),
        pl.BlockSpec((tm, LANES), row),
        pl.BlockSpec((tm, LANES), row),
        pl.BlockSpec((1, GATE_ROWS, tm), lambda b, s: (b, 0, s)),
        pl.BlockSpec((1, GATE_ROWS, tm), lambda b, s: (b, 0, s)),
    ]
    out_shape += [
        jax.ShapeDtypeStruct((t, MAIN_COLS), BF16),
        jax.ShapeDtypeStruct((t, LANES), F32),
        jax.ShapeDtypeStruct((t, LANES), F32),
        jax.ShapeDtypeStruct((bsz, GATE_ROWS, seq), F32),
        jax.ShapeDtypeStruct((bsz, GATE_ROWS, seq), F32),
    ]
    outs = pl.pallas_call(
        functools.partial(_in_proj_kernel, tm=tm, combine=combine),
        grid=(bsz, ns),
        in_specs=in_specs,
        out_specs=out_specs,
        out_shape=out_shape,
        scratch_shapes=[pltpu.VMEM((1, LANES), F32), pltpu.VMEM((SUBLANES, 2 * ML_WIDTH), F32)],
        compiler_params=_cparams(2, 48),
        name="in_proj",
    )(*args)
    if combine:
        return outs
    return [x] + list(outs)


def _fox_kernel(q_ref, k_ref, v_ref, fc_ref, fr_ref, ng_ref, o_ref, m_sc, l_sc, acc_sc, *, tq, tk):
    hp = pl.program_id(1)
    qi = pl.program_id(2)
    lane = lax.broadcasted_iota(jnp.int32, (tq, LANES), 1)
    lane1 = lax.broadcasted_iota(jnp.int32, (1, LANES), 1)
    lo = lane < FOX_HEAD_DIM
    q = q_ref[...]
    qh = [q * (lane1 < FOX_HEAD_DIM).astype(BF16), q * (lane1 >= FOX_HEAD_DIM).astype(BF16)]
    fc = fc_ref[...]
    cq = [jnp.sum(jnp.where(lane == 2 * hp + h, fc, 0.0), axis=1, keepdims=True) for h in range(2)]
    for h in range(2):
        m_sc[h] = jnp.full((tq, 1), NEG, F32)
        l_sc[h] = jnp.zeros((tq, 1), F32)
        acc_sc[h] = jnp.zeros((tq, LANES), F32)

    def step(j, masked):
        k0 = pl.multiple_of(j * tk, tk)
        kt = k_ref[pl.ds(k0, tk), :]
        vt = v_ref[pl.ds(k0, tk), :]
        if masked:
            causal = (lax.broadcasted_iota(jnp.int32, (tq, tk), 1)
                      <= lax.broadcasted_iota(jnp.int32, (tq, tk), 0))
        for h in range(2):
            ck = fr_ref[0, pl.ds(2 * hp + h, 1), pl.ds(k0, tk)]
            s = lax.dot_general(qh[h], kt, (((1,), (1,)), ((), ())), preferred_element_type=F32)
            z = s + (cq[h] - ck)
            if masked:
                z = jnp.where(causal, z, NEG)
            m_prev = m_sc[h]
            m_new = jnp.maximum(m_prev, jnp.max(z, axis=1, keepdims=True))
            alpha = jnp.exp(m_prev - m_new)
            p = jnp.exp(z - m_new)
            l_sc[h] = alpha * l_sc[h] + jnp.sum(p, axis=1, keepdims=True)
            acc_sc[h] = alpha * acc_sc[h] + jnp.dot(p.astype(BF16), vt, preferred_element_type=F32)
            m_sc[h] = m_new

    def body(j, carry):
        step(j, False)
        return carry

    lax.fori_loop(0, qi, body, 0)
    step(qi, True)

    o = jnp.where(lo, acc_sc[0] / l_sc[0], acc_sc[1] / l_sc[1])
    sq = o * o
    ms0 = jnp.sum(jnp.where(lo, sq, 0.0), axis=1, keepdims=True) * (1.0 / FOX_HEAD_DIM)
    ms1 = jnp.sum(jnp.where(lo, 0.0, sq), axis=1, keepdims=True) * (1.0 / FOX_HEAD_DIM)
    inv = jnp.where(lo, lax.rsqrt(ms0 + EPS), lax.rsqrt(ms1 + EPS))
    o_ref[...] = (o * inv * ng_ref[...]).astype(BF16)


def _fox_call(qkv, fc, fr, ng, *, bsz, seq, tq=512):
    t = qkv.shape[0]
    nq = seq // tq
    npair = FOX_HEADS // 2
    return pl.pallas_call(
        functools.partial(_fox_kernel, tq=tq, tk=tq),
        grid=(bsz, npair, nq),
        in_specs=[
            pl.BlockSpec((tq, LANES), lambda b, p, i: (b * nq + i, BLK_FQ * npair + p)),
            pl.BlockSpec((seq, LANES), lambda b, p, i: (b, BLK_FK * npair + p)),
            pl.BlockSpec((seq, LANES), lambda b, p, i: (b, BLK_FV * npair + p)),
            pl.BlockSpec((tq, LANES), lambda b, p, i: (b * nq + i, 0)),
            pl.BlockSpec((1, GATE_ROWS, seq), lambda b, p, i: (b, 0, 0)),
            pl.BlockSpec((1, LANES), lambda b, p, i: (0, p)),
        ],
        out_specs=pl.BlockSpec((tq, LANES), lambda b, p, i: (b * nq + i, p)),
        out_shape=jax.ShapeDtypeStruct((t, FOX_WIDTH), BF16),
        scratch_shapes=[pltpu.VMEM((2, tq, 1), F32), pltpu.VMEM((2, tq, 1), F32),
                        pltpu.VMEM((2, tq, LANES), F32)],
        compiler_params=_cparams(3, 48),
        name="fox_attn",
    )(qkv, qkv, qkv, fc, fr, ng)


def _mlstm_kernel(q_ref, k_ref, v_ref, og_ref, gc_ref, fc_ref, gr_ref, ng_ref, o_ref, ct_sc, u_sc, *, ch):
    @pl.when(pl.program_id(1) == 0)
    def _():
        ct_sc[...] = jnp.zeros_like(ct_sc)
        u_sc[...] = jnp.zeros_like(u_sc)

    causal = (lax.broadcasted_iota(jnp.int32, (ch, ch), 1)
              <= lax.broadcasted_iota(jnp.int32, (ch, ch), 0))
    lane = lax.broadcasted_iota(jnp.int32, (ch, LANES), 1)
    ones_col = jnp.where(lane == 0, 1.0, 0.0).astype(BF16)
    for h in range(ML_HEADS):
        sl = slice(h * ML_HEAD_DIM, (h + 1) * ML_HEAD_DIM)
        q = q_ref[:, sl]
        k = k_ref[:, sl]
        vp = jnp.concatenate([v_ref[:, sl], ones_col], axis=1)
        g_row = gr_ref[0, GATE_ML + h:GATE_ML + h + 1, :]
        g_col = gc_ref[:, GATE_ML + h:GATE_ML + h + 1]
        f_col = fc_ref[:, GATE_ML + h:GATE_ML + h + 1]
        u_prev = u_sc[h][:, :1]
        gm = jnp.where(causal, g_row, NEG)
        u_i = jnp.maximum(u_prev, jnp.max(gm, axis=1, keepdims=True))
        dmat = jnp.exp(gm - u_i)
        s = lax.dot_general(q, k, (((1,), (1,)), ((), ())), preferred_element_type=F32)
        scores = (s * dmat).astype(BF16)
        inter = jnp.exp(u_prev - u_i)
        ct = ct_sc[h]
        nd = (jnp.dot(scores, vp, preferred_element_type=F32)
              + jnp.dot(q, ct.astype(BF16), preferred_element_type=F32) * inter)
        num = nd[:, :ML_HEAD_DIM]
        den = nd[:, ML_HEAD_DIM:ML_HEAD_DIM + 1]
        den = jnp.maximum(jnp.abs(den), jnp.exp(-(f_col + u_i)))
        hh = num / den
        ms = jnp.mean(hh * hh, axis=1, keepdims=True)
        y = hh * lax.rsqrt(ms + EPS) * ng_ref[:, sl] * og_ref[:, sl].astype(F32)
        o_ref[:, sl] = y.astype(BF16)
        u_new = jnp.maximum(u_prev, jnp.max(g_row, axis=1, keepdims=True))
        wv = (jnp.exp(g_col - u_new) * vp.astype(F32)).astype(BF16)
        ct_sc[h] = (jnp.exp(u_prev - u_new) * ct
                    + lax.dot_general(k, wv, (((0,), (0,)), ((), ())), preferred_element_type=F32))
        u_sc[h] = jnp.broadcast_to(u_new, (1, LANES))


def _mlstm_call(qkv, gc, fc, gr, ng, *, bsz, seq, ch=256):
    t = qkv.shape[0]
    nc = seq // ch
    row = lambda b, c: (b * nc + c, 0)
    return pl.pallas_call(
        functools.partial(_mlstm_kernel, ch=ch),
        grid=(bsz, nc),
        in_specs=[
            pl.BlockSpec((ch, ML_WIDTH), lambda b, c: (b * nc + c, BLK_MQ)),
            pl.BlockSpec((ch, ML_WIDTH), lambda b, c: (b * nc + c, BLK_MK)),
            pl.BlockSpec((ch, ML_WIDTH), lambda b, c: (b * nc + c, BLK_MV)),
            pl.BlockSpec((ch, ML_WIDTH), lambda b, c: (b * nc + c, BLK_MO)),
            pl.BlockSpec((ch, LANES), row),
            pl.BlockSpec((ch, LANES), row),
            pl.BlockSpec((1, GATE_ROWS, ch), lambda b, c: (b, 0, c)),
            pl.BlockSpec((1, ML_WIDTH), lambda b, c: (0, 0)),
        ],
        out_specs=pl.BlockSpec((ch, ML_WIDTH), row),
        out_shape=jax.ShapeDtypeStruct((t, ML_WIDTH), BF16),
        scratch_shapes=[pltpu.VMEM((ML_HEADS, ML_HEAD_DIM, 2 * ML_HEAD_DIM), F32),
                        pltpu.VMEM((ML_HEADS, 1, LANES), F32)],
        compiler_params=_cparams(2, 32),
        name="mlstm",
    )(qkv, qkv, qkv, qkv, gc, fc, gr, ng)


def _post_kernel(x_ref, hf_ref, hm_ref, wo_ref, g1_ref, nf_ref, sc_ref, sh_ref, wr_ref, br_ref, tri_ref,
                 x1_ref, hn_ref, gsel_ref, grp_ref, rank_ref, tot_ref, cnt_sc, *, te, steps_per_chunk):
    @pl.when(pl.program_id(0) % steps_per_chunk == 0)
    def _():
        cnt_sc[...] = jnp.zeros_like(cnt_sc)

    mix = (jnp.dot(hf_ref[...], wo_ref[:FOX_WIDTH, :], preferred_element_type=F32)
           + jnp.dot(hm_ref[...], wo_ref[FOX_WIDTH:, :], preferred_element_type=F32))
    x1 = x_ref[...] + g1_ref[0] * mix
    x1_ref[...] = x1
    ms = jnp.mean(x1 * x1, axis=1, keepdims=True)
    hn = x1 * lax.rsqrt(ms + EPS) * nf_ref[...]
    hn = hn * (1.0 + sc_ref[0]) + sh_ref[0]
    _to_token_tiles(hn_ref, hn)

    logits = jnp.dot(hn, wr_ref[...], preferred_element_type=F32, precision=lax.Precision.HIGHEST)
    aff = jax.nn.sigmoid(logits.T[:N_EXPERTS, :])
    sel = aff + br_ref[...]
    selr = [sel[e:e + 1, :] for e in range(N_EXPERTS)]
    affr = [aff[e:e + 1, :] for e in range(N_EXPERTS)]
    keep = [None] * N_EXPERTS
    score = []
    for g in range(N_GROUPS):
        vs = selr[g * EXPERTS_PER_GROUP:(g + 1) * EXPERTS_PER_GROUP]
        sg = jnp.zeros_like(vs[0])
        for i in range(EXPERTS_PER_GROUP):
            beaten = jnp.zeros_like(vs[0])
            for j in range(EXPERTS_PER_GROUP):
                if j != i:
                    b = (vs[j] >= vs[i]) if j < i else (vs[j] > vs[i])
                    beaten = beaten + jnp.where(b, 1.0, 0.0)
            kp = beaten < 2.0
            keep[g * EXPERTS_PER_GROUP + i] = kp
            sg = sg + jnp.where(kp, vs[i], 0.0)
        score.append(sg)
    chosen = []
    for g in range(N_GROUPS):
        lost = jnp.zeros_like(score[0])
        for g2 in range(N_GROUPS):
            if g2 != g:
                b = (score[g2] >= score[g]) if g2 < g else (score[g2] > score[g])
                lost = lost + jnp.where(b, 1.0, 0.0)
        chosen.append(jnp.where(lost < 0.5, 1.0, 0.0))
    wsel = []
    for i in range(EXPERTS_PER_GROUP):
        wi = jnp.zeros_like(score[0])
        for g in range(N_GROUPS):
            e = g * EXPERTS_PER_GROUP + i
            wi = wi + chosen[g] * jnp.where(keep[e], affr[e], 0.0)
        wsel.append(wi)
    wsum = wsel[0] + wsel[1] + wsel[2] + wsel[3]
    wsel = [w / wsum for w in wsel]

    row8 = lax.broadcasted_iota(jnp.int32, (SUBLANES, te), 0)
    gmat = jnp.zeros((SUBLANES, te), F32)
    wmat = jnp.zeros((SUBLANES, te), F32)
    grp = jnp.zeros_like(score[0])
    for g in range(N_GROUPS):
        gmat = jnp.where(row8 == g, chosen[g], gmat)
        wmat = jnp.where(row8 == g, wsel[g], wmat)
        grp = grp + g * chosen[g]
    pref = jnp.dot(gmat.astype(BF16), tri_ref[...], preferred_element_type=F32) + cnt_sc[:, :1]
    rank = jnp.sum(gmat * pref, axis=0, keepdims=True)
    tot = cnt_sc[:, :1] + jnp.sum(gmat, axis=1, keepdims=True)
    cnt_sc[...] = jnp.broadcast_to(tot, cnt_sc.shape)
    rank_ref[0] = rank.astype(jnp.int32)
    grp_ref[0] = grp.astype(jnp.int32)
    tot_ref[0] = jnp.broadcast_to(tot, (SUBLANES, LANES)).astype(jnp.int32)
    wfull = jnp.concatenate([wmat, jnp.zeros((LANES - SUBLANES, te), F32)], axis=0)
    gsel_ref[...] = wfull.T


def _post_call(x, hf, hm, wo, g1, nf, sc, sh, wr, br, tri, *, bsz, seq, chunk, te=512):
    t, d = x.shape
    steps_per_chunk = chunk // te
    per_seq = seq // te
    n_steps = t // te
    n_chunks = t // chunk
    row = lambda i: (i, 0)
    per_b = lambda i: (i // per_seq, 0, 0)
    const2 = lambda i: (0, 0)
    return pl.pallas_call(
        functools.partial(_post_kernel, te=te, steps_per_chunk=steps_per_chunk),
        grid=(n_steps,),
        in_specs=[
            pl.BlockSpec((te, d), row),
            pl.BlockSpec((te, FOX_WIDTH), row),
            pl.BlockSpec((te, ML_WIDTH), row),
            pl.BlockSpec((d, d), const2),
            pl.BlockSpec((1, 1, d), per_b),
            pl.BlockSpec((1, d), const2),
            pl.BlockSpec((1, 1, d), per_b),
            pl.BlockSpec((1, 1, d), per_b),
            pl.BlockSpec((d, LANES), const2),
            pl.BlockSpec((N_EXPERTS, 1), const2),
            pl.BlockSpec((te, te), const2),
        ],
        out_specs=[
            pl.BlockSpec((te, d), row),
            pl.BlockSpec((te * TOK_ROWS, LANES), row),
            pl.BlockSpec((te, LANES), row),
            pl.BlockSpec((1, 1, te), lambda i: (i, 0, 0)),
            pl.BlockSpec((1, 1, te), lambda i: (i, 0, 0)),
            pl.BlockSpec((1, SUBLANES, LANES), lambda i: (i // steps_per_chunk, 0, 0)),
        ],
        out_shape=[
            jax.ShapeDtypeStruct((t, d), F32),
            jax.ShapeDtypeStruct((t * TOK_ROWS, LANES), F32),
            jax.ShapeDtypeStruct((t, LANES), F32),
            jax.ShapeDtypeStruct((n_steps, 1, te), jnp.int32),
            jax.ShapeDtypeStruct((n_steps, 1, te), jnp.int32),
            jax.ShapeDtypeStruct((n_chunks, SUBLANES, LANES), jnp.int32),
        ],
        scratch_shapes=[pltpu.VMEM((SUBLANES, LANES), F32)],
        compiler_params=_cparams(1, 48),
        name="post_router",
    )(x, hf, hm, wo, g1, nf, sc, sh, wr, br, tri)


def _scatter_kernel(grp_ref, rank_ref, off_ref, hn_ref, gsel_ref, xb_ref, gs_ref, xs_sc, *, chunk, rows, tm):
    xs_sc[...] = jnp.zeros_like(xs_sc)
    gs_ref[...] = jnp.zeros_like(gs_ref)

    def body(t, carry):
        p = off_ref[0, 0, grp_ref[0, 0, t]] + rank_ref[0, 0, t]
        src = pl.multiple_of(t * TOK_ROWS, TOK_ROWS)
        dst = pl.multiple_of(p * TOK_ROWS, TOK_ROWS)
        xs_sc[pl.ds(dst, TOK_ROWS), :] = hn_ref[pl.ds(src, TOK_ROWS), :]
        gs_ref[pl.ds(p, 1), :] = gsel_ref[pl.ds(t, 1), :]
        return carry

    lax.fori_loop(0, chunk, body, 0, unroll=8)
    for j in range(rows // tm):
        for c in range(TOK_ROWS):
            xb_ref[j * tm:(j + 1) * tm, c * LANES:(c + 1) * LANES] = (
                xs_sc[pl.ds(j * tm * TOK_ROWS + c, tm, stride=TOK_ROWS), :].astype(BF16))


def _scatter_call(grp, rank, off, hn_tt, gsel, *, chunk, rows, tm):
    t = gsel.shape[0]
    n_chunks = t // chunk
    smem3 = lambda shape: pl.BlockSpec(shape, lambda c: (c, 0, 0), memory_space=pltpu.SMEM)
    return pl.pallas_call(
        functools.partial(_scatter_kernel, chunk=chunk, rows=rows, tm=tm),
        grid=(n_chunks,),
        in_specs=[
            smem3((1, 1, chunk)),
            smem3((1, 1, chunk)),
            smem3((1, 1, N_GROUPS)),
            pl.BlockSpec((chunk * TOK_ROWS, LANES), lambda c: (c, 0)),
            pl.BlockSpec((chunk, LANES), lambda c: (c, 0)),
        ],
        out_specs=[
            pl.BlockSpec((rows, D_MODEL), lambda c: (c, 0)),
            pl.BlockSpec((rows, LANES), lambda c: (c, 0)),
        ],
        out_shape=[
            jax.ShapeDtypeStruct((n_chunks * rows, D_MODEL), BF16),
            jax.ShapeDtypeStruct((n_chunks * rows, LANES), F32),
        ],
        scratch_shapes=[pltpu.VMEM((rows * TOK_ROWS, LANES), F32)],
        compiler_params=_cparams(1, 56),
        name="moe_scatter",
    )(grp, rank, off, hn_tt, gsel)


def _experts_kernel(blk_ref, grp_ref, xb_ref, gs_ref, wg_ref, wu_ref, wd_ref, y_ref, *, tm):
    g = grp_ref[pl.program_id(0)]

    @pl.when(g < N_GROUPS)
    def _():
        x = xb_ref[...]
        gs = gs_ref[...]
        acc = jnp.zeros((tm, D_MODEL), F32)
        for i in range(EXPERTS_PER_GROUP):
            hg = jnp.dot(x, wg_ref[i], preferred_element_type=F32)
            hu = jnp.dot(x, wu_ref[i], preferred_element_type=F32)
            a = _silu(hg) * hu * gs[:, i:i + 1]
            acc = acc + jnp.dot(a.astype(BF16), wd_ref[i], preferred_element_type=F32)
        _to_token_tiles(y_ref, acc)

    @pl.when(g >= N_GROUPS)
    def _():
        y_ref[...] = jnp.zeros_like(y_ref)


def _experts_call(tile_blk, tile_grp, xb, gs, wg, wu, wd, *, tm):
    n_rows = xb.shape[0]
    n_slots = n_rows // tm
    wmap = lambda s, blk, grp: (jnp.minimum(grp[s], N_GROUPS - 1), 0, 0)
    grid_spec = pltpu.PrefetchScalarGridSpec(
        num_scalar_prefetch=2,
        grid=(n_slots,),
        in_specs=[
            pl.BlockSpec((tm, D_MODEL), lambda s, blk, grp: (blk[s], 0)),
            pl.BlockSpec((tm, LANES), lambda s, blk, grp: (blk[s], 0)),
            pl.BlockSpec((EXPERTS_PER_GROUP, D_MODEL, D_FF), wmap),
            pl.BlockSpec((EXPERTS_PER_GROUP, D_MODEL, D_FF), wmap),
            pl.BlockSpec((EXPERTS_PER_GROUP, D_FF, D_MODEL), wmap),
        ],
        out_specs=pl.BlockSpec((tm * TOK_ROWS, LANES), lambda s, blk, grp: (blk[s], 0)),
    )
    return pl.pallas_call(
        functools.partial(_experts_kernel, tm=tm),
        grid_spec=grid_spec,
        out_shape=jax.ShapeDtypeStruct((n_rows * TOK_ROWS, LANES), F32),
        compiler_params=_cparams(1, 48),
        name="moe_experts",
    )(tile_blk, tile_grp, xb, gs, wg, wu, wd)


def _gather_kernel(grp_ref, rank_ref, off_ref, ys_ref, o_ref, *, chunk):
    def body(t, carry):
        p = off_ref[0, 0, grp_ref[0, 0, t]] + rank_ref[0, 0, t]
        src = pl.multiple_of(p * TOK_ROWS, TOK_ROWS)
        dst = pl.multiple_of(t * TOK_ROWS, TOK_ROWS)
        o_ref[pl.ds(dst, TOK_ROWS), :] = ys_ref[pl.ds(src, TOK_ROWS), :]
        return carry

    lax.fori_loop(0, chunk, body, 0, unroll=8)


def _gather_call(grp, rank, off, ys_tt, *, chunk, rows, n_tok):
    n_chunks = n_tok // chunk
    smem3 = lambda shape: pl.BlockSpec(shape, lambda c: (c, 0, 0), memory_space=pltpu.SMEM)
    return pl.pallas_call(
        functools.partial(_gather_kernel, chunk=chunk),
        grid=(n_chunks,),
        in_specs=[
            smem3((1, 1, chunk)),
            smem3((1, 1, chunk)),
            smem3((1, 1, N_GROUPS)),
            pl.BlockSpec((rows * TOK_ROWS, LANES), lambda c: (c, 0)),
        ],
        out_specs=pl.BlockSpec((chunk * TOK_ROWS, LANES), lambda c: (c, 0)),
        out_shape=jax.ShapeDtypeStruct((n_tok * TOK_ROWS, LANES), F32),
        compiler_params=_cparams(1, 56),
        name="moe_gather",
    )(grp, rank, off, ys_tt)


def _final_kernel(x_ref, ott_ref, g2_ref, nf_ref, o_ref, *, tm):
    x = x_ref[...] + g2_ref[0] * _from_token_tiles(ott_ref, tm)
    ms = jnp.mean(x * x, axis=1, keepdims=True)
    o_ref[...] = x * lax.rsqrt(ms + EPS) * nf_ref[...]


def _final_call(x, moe_tt, g2, nf, *, seq, tm=512):
    t, d = x.shape
    per_seq = seq // tm
    return pl.pallas_call(
        functools.partial(_final_kernel, tm=tm),
        grid=(t // tm,),
        in_specs=[
            pl.BlockSpec((tm, d), lambda i: (i, 0)),
            pl.BlockSpec((tm * TOK_ROWS, LANES), lambda i: (i, 0)),
            pl.BlockSpec((1, 1, d), lambda i: (i // per_seq, 0, 0)),
            pl.BlockSpec((1, d), lambda i: (0, 0)),
        ],
        out_specs=pl.BlockSpec((tm, d), lambda i: (i, 0)),
        out_shape=jax.ShapeDtypeStruct((t, d), F32),
        compiler_params=_cparams(1, 32),
        name="final_norm",
    )(x, moe_tt, g2, nf)


def _moe_tiles(tot, *, tm, tiles_per_chunk):
    nt = (tot + tm - 1) // tm
    ts = jnp.cumsum(nt, axis=1) - nt
    off = (ts * tm).astype(jnp.int32)
    j = jnp.arange(tiles_per_chunk, dtype=jnp.int32)[None, :, None]
    inside = (j >= ts[:, None, :]) & (j < (ts + nt)[:, None, :])
    key = jnp.where(jnp.any(inside, axis=-1), jnp.argmax(inside, axis=-1), N_GROUPS).reshape(-1)
    order = jnp.argsort(key, stable=True).astype(jnp.int32)
    return off, order, key[order].astype(jnp.int32)


def kernel(x, c, w_in, conv_w, conv_b, fox_f_bias, mlstm_i_bias, mlstm_f_bias, fox_out_norm,
           mlstm_out_norm, w_out, w_ada, b_ada, norm_mix, norm_ffn, w_router, b_router, w_gate,
           w_up, w_down, norm_final):
    bsz, seq, d = x.shape
    depth = w_in.shape[0]
    t = bsz * seq
    assert d == D_MODEL and w_in.shape[-1] == IN_COLS
    chunk = min(2048, seq)
    tm = 256
    rows = chunk + N_GROUPS * tm
    te = min(512, seq)
    assert seq % 512 == 0 and seq % chunk == 0 and chunk % te == 0

    mods = _ada_call(c, w_ada, b_ada).reshape(depth, bsz, N_ADA, 1, d)
    xf = x.reshape(t, d)

    tri = (jnp.arange(te)[:, None] < jnp.arange(te)[None, :]).astype(BF16)
    wr = jnp.pad(w_router, ((0, 0), (0, LANES - N_EXPERTS)))
    br = b_router.reshape(N_EXPERTS, 1).astype(F32)

    moe_tt = None
    g2_prev = None
    for l in range(depth):
        sh1, sc1, g1, sh2, sc2, g2 = [mods[l, :, i] for i in range(N_ADA)]
        wl = w_in[l]
        wm = jnp.concatenate([wl[:, FOX_Q:FOX_F], wl[:, ML_Q:ML_I], wl[:, ML_O:IN_COLS]], axis=1).astype(BF16)
        zpad = lambda n: jnp.zeros((d, n), F32)
        wg = jnp.concatenate([
            wl[:, FOX_F:ML_Q], wl[:, ML_F:ML_O], zpad(LANES - FOX_HEADS - ML_HEADS),
            zpad(GATE_ML), wl[:, ML_I:ML_F], zpad(LANES - GATE_ML - ML_HEADS)], axis=1).astype(BF16)
        zb = lambda n: jnp.zeros((n,), F32)
        gb = jnp.concatenate([
            fox_f_bias[l], mlstm_f_bias[l], zb(LANES - FOX_HEADS - ML_HEADS),
            zb(GATE_ML), mlstm_i_bias[l], zb(LANES - GATE_ML - ML_HEADS)]).reshape(1, 2 * LANES)

        xf, qkv, fc, gc, fr, gr = _in_proj_call(
            xf, moe_tt, g2_prev, norm_mix[l].reshape(1, d), sc1, sh1, wm, wg, gb,
            conv_w[l], conv_b[l].reshape(1, -1), bsz=bsz, seq=seq)
        hf = _fox_call(qkv, fc, fr, fox_out_norm[l].reshape(1, FOX_WIDTH), bsz=bsz, seq=seq)
        hm = _mlstm_call(qkv, gc, fc, gr, mlstm_out_norm[l].reshape(1, ML_WIDTH), bsz=bsz, seq=seq)
        xf, hn_tt, gsel, grp, rank, tot = _post_call(
            xf, hf, hm, w_out[l].astype(BF16), g1, norm_ffn[l].reshape(1, d), sc2, sh2, wr, br, tri,
            bsz=bsz, seq=seq, chunk=chunk, te=te)

        n_chunks = t // chunk
        grp = grp.reshape(n_chunks, 1, chunk)
        rank = rank.reshape(n_chunks, 1, chunk)
        off, tile_blk, tile_grp = _moe_tiles(tot[:, :N_GROUPS, 0], tm=tm, tiles_per_chunk=rows // tm)
        off = off.reshape(n_chunks, 1, N_GROUPS)
        xb, gs = _scatter_call(grp, rank, off, hn_tt, gsel, chunk=chunk, rows=rows, tm=tm)
        ys_tt = _experts_call(tile_blk, tile_grp, xb, gs, w_gate[l].astype(BF16),
                              w_up[l].astype(BF16), w_down[l].astype(BF16), tm=tm)
        moe_tt = _gather_call(grp, rank, off, ys_tt, chunk=chunk, rows=rows, n_tok=t)
        g2_prev = g2

    out = _final_call(xf, moe_tt, g2_prev, norm_final.reshape(1, d), seq=seq)
    return out.reshape(bsz, seq, d)
```

```python
import functools

import numpy as np
import jax
import jax.numpy as jnp
from jax import lax
from jax.experimental import pallas as pl
from jax.experimental.pallas import tpu as pltpu

F32 = jnp.float32
BF16 = jnp.bfloat16

LANES = 128
SUBLANES = 8
VMEM_BYTES_V7X = 64 * 1024 * 1024

D_MODEL = 1024
FOX_HEADS = 8
FOX_HEAD_DIM = 64
FOX_WIDTH = FOX_HEADS * FOX_HEAD_DIM
ML_HEADS = 4
ML_HEAD_DIM = 128
ML_WIDTH = ML_HEADS * ML_HEAD_DIM
CONV_WIDTH = 4
N_EXPERTS = 16
N_GROUPS = 4
EXPERTS_PER_GROUP = 4
D_FF = 512
N_ADA = 6
EPS = 1e-6
NEG = -1e30

FOX_Q = 0
FOX_F = 3 * FOX_WIDTH
ML_Q = FOX_F + FOX_HEADS
ML_I = ML_Q + 3 * ML_WIDTH
ML_F = ML_I + ML_HEADS
ML_O = ML_F + ML_HEADS
IN_COLS = ML_O + ML_WIDTH

MAIN_COLS = 7 * 512
BLK_FQ, BLK_FK, BLK_FV, BLK_MQ, BLK_MK, BLK_MV, BLK_MO = range(7)
REST_COLS = 5 * 512
OUT_FV, OUT_MQ, OUT_MK, OUT_MV, OUT_MO = range(5)
AUG_COLS = FOX_HEADS * 128
BIAS_TERMS = 3
LOG2E = 1.4426950408889634
GATE_FOX = 0
GATE_ML = FOX_HEADS
GATE_ROWS = 16

TOK_ROWS = D_MODEL // LANES


def _cparams(n_grid, vmem_mb):
    return pltpu.CompilerParams(
        dimension_semantics=("arbitrary",) * n_grid,
        vmem_limit_bytes=vmem_mb * 1024 * 1024)


def _silu(x):
    return x * jax.nn.sigmoid(x)


def _log_sigmoid(z):
    return jnp.minimum(z, 0.0) - jnp.log1p(jnp.exp(-jnp.abs(z)))


def _cumsum_rows(x):
    n = x.shape[0]
    row = lax.broadcasted_iota(jnp.int32, x.shape, 0)
    s = 1
    while s < n:
        x = x + jnp.where(row >= s, pltpu.roll(x, s, axis=0), 0.0)
        s *= 2
    return x


def _from_token_tiles(ref, n_tok):
    return jnp.concatenate(
        [ref[pl.ds(c, n_tok, stride=TOK_ROWS), :] for c in range(TOK_ROWS)], axis=1)


def _to_token_tiles(ref, val, row0=0):
    n = val.shape[0]
    for c in range(TOK_ROWS):
        ref[pl.ds(row0 * TOK_ROWS + c, n, stride=TOK_ROWS), :] = val[:, c * LANES:(c + 1) * LANES]


def _ada_kernel(c_ref, w_ref, b_ref, o_ref):
    c = c_ref[...]
    o_ref[0] = jnp.dot(_silu(c), w_ref[0], preferred_element_type=F32,
                       precision=lax.Precision.HIGHEST) + b_ref[0]


def _ada_call(c, w_ada, b_ada):
    depth, d, n = w_ada.shape
    bsz = c.shape[0]
    tn = 1536
    return pl.pallas_call(
        _ada_kernel,
        grid=(depth, n // tn),
        in_specs=[
            pl.BlockSpec((bsz, d), lambda l, j: (0, 0)),
            pl.BlockSpec((1, d, tn), lambda l, j: (l, 0, j)),
            pl.BlockSpec((1, 1, tn), lambda l, j: (l, 0, j)),
        ],
        out_specs=pl.BlockSpec((1, bsz, tn), lambda l, j: (l, 0, j)),
        out_shape=jax.ShapeDtypeStruct((depth, bsz, n), F32),
        compiler_params=_cparams(2, 32),
        name="ada_mod",
    )(c, w_ada, b_ada.reshape(depth, 1, n))


def _in_proj_kernel(*refs, tm, combine):
    if combine:
        (x_ref, ott_ref, g2_ref, nm_ref, sc_ref, sh_ref, wm_ref, wg_ref, gb_ref, cw_ref, cb_ref,
         place_ref, ones_ref,
         xn_ref, qa_ref, ka_ref, qkv_ref, fc_ref, gc_ref, gr_ref, fcar, ccar) = refs
    else:
        (x_ref, nm_ref, sc_ref, sh_ref, wm_ref, wg_ref, gb_ref, cw_ref, cb_ref,
         place_ref, ones_ref,
         qa_ref, ka_ref, qkv_ref, fc_ref, gc_ref, gr_ref, fcar, ccar) = refs

    @pl.when(pl.program_id(1) == 0)
    def _():
        fcar[...] = jnp.zeros_like(fcar)
        ccar[...] = jnp.zeros_like(ccar)

    x = x_ref[...]
    if combine:
        x = x + g2_ref[0] * _from_token_tiles(ott_ref, tm)
        xn_ref[...] = x
    ms = jnp.mean(x * x, axis=1, keepdims=True)
    hn = x * lax.rsqrt(ms + EPS) * nm_ref[...]
    hn = hn * (1.0 + sc_ref[0]) + sh_ref[0]
    hb = hn.astype(BF16)

    gp = jnp.dot(hb, wg_ref[...], preferred_element_type=F32) + gb_ref[...]
    lf = _log_sigmoid(gp[:, :LANES])
    fcum = _cumsum_rows(lf) + fcar[...]
    fcar[...] = fcum[tm - 1:tm, :]
    gml = gp[:, LANES:] - fcum
    fc_ref[...] = fcum
    gc_ref[...] = gml
    gr_ref[0] = gml.T[:GATE_ROWS, :]

    def mm(j):
        return jnp.dot(hb, wm_ref[:, j * 512:(j + 1) * 512], preferred_element_type=F32)

    def put(j, v):
        qkv_ref[:, j * 512:(j + 1) * 512] = v.astype(BF16)

    lane = lax.broadcasted_iota(jnp.int32, (tm, LANES), 1)
    fs = fcum * LOG2E
    hi = fs.astype(BF16).astype(F32)
    mid = (fs - hi).astype(BF16).astype(F32)
    low = (fs - hi - mid).astype(BF16).astype(F32)
    packed = jnp.where(lane < FOX_HEADS, hi,
                       jnp.where(lane < 2 * FOX_HEADS, pltpu.roll(mid, FOX_HEADS, axis=1),
                                 jnp.where(lane < 3 * FOX_HEADS, pltpu.roll(low, 2 * FOX_HEADS, axis=1), 0.0)))
    bias = jnp.dot(packed.astype(BF16), place_ref[...], preferred_element_type=F32) + ones_ref[...]

    def put_heads(ref, val, col0):
        lo = lane < FOX_HEAD_DIM
        for p in range(FOX_HEADS // 2):
            slab = val[:, p * LANES:(p + 1) * LANES]
            for h, data in ((2 * p, slab), (2 * p + 1, pltpu.roll(slab, FOX_HEAD_DIM, axis=1))):
                blk = jnp.where(lo, data, 0.0) + bias[:, col0 + h * LANES:col0 + (h + 1) * LANES]
                ref[:, h * LANES:(h + 1) * LANES] = blk.astype(BF16)

    put_heads(qa_ref, mm(BLK_FQ) * (FOX_HEAD_DIM ** -0.5 * LOG2E), 0)
    put_heads(ka_ref, mm(BLK_FK), AUG_COLS)
    put(OUT_FV, mm(BLK_FV))
    put(OUT_MV, mm(BLK_MV))
    put(OUT_MO, jax.nn.sigmoid(mm(BLK_MO)))

    u = jnp.concatenate([mm(BLK_MQ), mm(BLK_MK)], axis=1)
    prev = ccar[...]
    ccar[...] = u[tm - SUBLANES:tm, :]
    row8 = lax.broadcasted_iota(jnp.int32, prev.shape, 0)
    y = cb_ref[...] + cw_ref[CONV_WIDTH - 1:CONV_WIDTH, :] * u
    for k in range(1, CONV_WIDTH):
        r = pltpu.roll(u, k, axis=0)
        top = jnp.where(row8 < k, pltpu.roll(prev, k, axis=0), r[:SUBLANES])
        shifted = jnp.concatenate([top, r[SUBLANES:]], axis=0)
        y = y + cw_ref[CONV_WIDTH - 1 - k:CONV_WIDTH - k, :] * shifted
    act = _silu(y)
    put(OUT_MQ, act[:, :ML_WIDTH])
    put(OUT_MK, act[:, ML_WIDTH:] * (ML_HEAD_DIM ** -0.5))


def _bias_placement():
    place = np.zeros((LANES, 2 * AUG_COLS), np.float32)
    ones = np.zeros((1, 2 * AUG_COLS), np.float32)
    for h in range(FOX_HEADS):
        for term in range(BIAS_TERMS):
            src = term * FOX_HEADS + h
            place[src, h * LANES + FOX_HEAD_DIM + term] = 1.0
            ones[0, h * LANES + FOX_HEAD_DIM + BIAS_TERMS + term] = 1.0
            place[src, AUG_COLS + h * LANES + FOX_HEAD_DIM + BIAS_TERMS + term] = -1.0
            ones[0, AUG_COLS + h * LANES + FOX_HEAD_DIM + term] = 1.0
    return jnp.asarray(place, BF16), jnp.asarray(ones, F32)


def _in_proj_call(x, moe_tt, g2, nm, sc, sh, wm, wg, gb, cw, cb, *, bsz, seq, tm=512):
    t, d = x.shape
    ns = seq // tm
    combine = moe_tt is not None
    row = lambda b, s: (b * ns + s, 0)
    per_b = lambda b, s: (b, 0, 0)
    const2 = lambda b, s: (0, 0)
    in_specs = [pl.BlockSpec((tm, d), row)]
    args = [x]
    if combine:
        in_specs += [pl.BlockSpec((tm * TOK_ROWS, LANES), row), pl.BlockSpec((1, 1, d), per_b)]
        args += [moe_tt, g2]
    in_specs += [
        pl.BlockSpec((1, d), const2),
        pl.BlockSpec((1, 1, d), per_b),
        pl.BlockSpec((1, 1, d), per_b),
        pl.BlockSpec((d, MAIN_COLS), const2),
        pl.BlockSpec((d, 2 * LANES), const2),
        pl.BlockSpec((1, 2 * LANES), const2),
        pl.BlockSpec((CONV_WIDTH, 2 * ML_WIDTH), const2),
        pl.BlockSpec((1, 2 * ML_WIDTH), const2),
        pl.BlockSpec((LANES, 2 * AUG_COLS), const2),
        pl.BlockSpec((1, 2 * AUG_COLS), const2),
    ]
    place, ones = _bias_placement()
    args += [nm, sc, sh, wm, wg, gb, cw, cb, place, ones]
    out_specs = []
    out_shape = []
    if combine:
        out_specs.append(pl.BlockSpec((tm, d), row))
        out_shape.append(jax.ShapeDtypeStruct((t, d), F32))
    out_specs += [
        pl.BlockSpec((tm, AUG_COLS), row),
        pl.BlockSpec((tm, AUG_COLS), row),
        pl.BlockSpec((tm, REST_COLS), row),
        pl.BlockSpec((tm, LANES), row),
        pl.BlockSpec((tm, LANES), row),
        pl.BlockSpec((1, GATE_ROWS, tm), lambda b, s: (b, 0, s)),
    ]
    out_shape += [
        jax.ShapeDtypeStruct((t, AUG_COLS), BF16),
        jax.ShapeDtypeStruct((t, AUG_COLS), BF16),
        jax.ShapeDtypeStruct((t, REST_COLS), BF16),
        jax.ShapeDtypeStruct((t, LANES), F32),
        jax.ShapeDtypeStruct((t, LANES), F32),
        jax.ShapeDtypeStruct((bsz, GATE_ROWS, seq), F32),
    ]
    outs = pl.pallas_call(
        functools.partial(_in_proj_kernel, tm=tm, combine=combine),
        grid=(bsz, ns),
        in_specs=in_specs,
        out_specs=out_specs,
        out_shape=out_shape,
        scratch_shapes=[pltpu.VMEM((1, LANES), F32), pltpu.VMEM((SUBLANES, 2 * ML_WIDTH), F32)],
        compiler_params=_cparams(2, 48),
        name="in_proj",
    )(*args)
    if combine:
        return outs
    return [x] + list(outs)


def _fox_kernel(q_ref, k_ref, v_ref, ng_ref, cm_ref, o_ref, m_sc, acc_sc, *, tq, tk, nh):
    qi = pl.program_id(2)
    lane = lax.broadcasted_iota(jnp.int32, (tq, LANES), 1)
    lo = lane < FOX_HEAD_DIM
    ones_col = jnp.where(lax.broadcasted_iota(jnp.int32, (tk, LANES), 1) == 0, 1.0, 0.0).astype(BF16)
    for h in range(nh):
        m_sc[h] = jnp.full((tq, LANES), NEG, F32)
        acc_sc[h] = jnp.zeros((tq, 2 * LANES), F32)

    def body(j, carry):
        k0 = pl.multiple_of(j * tk, tk)
        slot = (j == qi).astype(jnp.int32)
        for h in range(nh):
            hs = slice(h * LANES, (h + 1) * LANES)
            vs = slice((h // 2) * LANES, (h // 2 + 1) * LANES)
            va = jnp.concatenate([v_ref[pl.ds(k0, tk), vs], ones_col], axis=1)
            z = lax.dot_general(q_ref[:, hs], k_ref[pl.ds(k0, tk), hs], (((1,), (1,)), ((), ())),
                                preferred_element_type=F32)
            z = z + cm_ref[slot]
            m_prev = m_sc[h]
            m_new = jnp.maximum(m_prev, jnp.max(z, axis=1, keepdims=True))
            alpha = jnp.exp2(m_prev - m_new)
            p = jnp.exp2(z - jnp.concatenate([m_new] * (tk // LANES), axis=1))
            acc_sc[h] = (jnp.concatenate([alpha, alpha], axis=1) * acc_sc[h]
                         + jnp.dot(p.astype(BF16), va, preferred_element_type=F32))
            m_sc[h] = m_new
        return carry

    lax.fori_loop(0, qi + 1, body, 0)

    for p in range(nh // 2):
        outs = []
        for h in (2 * p, 2 * p + 1):
            acc = acc_sc[h]
            outs.append(acc[:, :LANES] / acc[:, LANES:LANES + 1])
        o = jnp.where(lo, outs[0], outs[1])
        sq = o * o
        ms0 = jnp.sum(jnp.where(lo, sq, 0.0), axis=1, keepdims=True) * (1.0 / FOX_HEAD_DIM)
        ms1 = jnp.sum(jnp.where(lo, 0.0, sq), axis=1, keepdims=True) * (1.0 / FOX_HEAD_DIM)
        inv = jnp.where(lo, lax.rsqrt(ms0 + EPS), lax.rsqrt(ms1 + EPS))
        ps = slice(p * LANES, (p + 1) * LANES)
        o_ref[:, ps] = (o * inv * ng_ref[:, ps]).astype(BF16)


def _fox_call(qa, ka, rest, ng, *, bsz, seq, tq=512, nh=8):
    t = qa.shape[0]
    nq = seq // tq
    ngrp = FOX_HEADS // nh
    vw = nh // 2 * LANES
    causal = jnp.where(jnp.arange(tq)[None, :] <= jnp.arange(tq)[:, None], 0.0, NEG).astype(F32)
    cmask = jnp.stack([jnp.zeros_like(causal), causal])
    return pl.pallas_call(
        functools.partial(_fox_kernel, tq=tq, tk=tq, nh=nh),
        grid=(bsz, ngrp, nq),
        in_specs=[
            pl.BlockSpec((tq, nh * LANES), lambda b, p, i: (b * nq + i, p)),
            pl.BlockSpec((seq, nh * LANES), lambda b, p, i: (b, p)),
            pl.BlockSpec((seq, vw), lambda b, p, i: (b, OUT_FV * ngrp + p)),
            pl.BlockSpec((1, vw), lambda b, p, i: (0, p)),
            pl.BlockSpec((2, tq, tq), lambda b, p, i: (0, 0, 0)),
        ],
        out_specs=pl.BlockSpec((tq, vw), lambda b, p, i: (b * nq + i, p)),
        out_shape=jax.ShapeDtypeStruct((t, FOX_WIDTH), BF16),
        scratch_shapes=[pltpu.VMEM((nh, tq, LANES), F32), pltpu.VMEM((nh, tq, 2 * LANES), F32)],
        compiler_params=_cparams(3, 48),
        name="fox_attn",
    )(qa, ka, rest, ng, cmask)


def _mlstm_kernel(q_ref, k_ref, v_ref, og_ref, gc_ref, fc_ref, gr_ref, ng_ref, o_ref, ct_sc, u_sc, *, ch):
    @pl.when(pl.program_id(1) == 0)
    def _():
        ct_sc[...] = jnp.zeros_like(ct_sc)
        u_sc[...] = jnp.zeros_like(u_sc)

    causal = (lax.broadcasted_iota(jnp.int32, (ch, ch), 1)
              <= lax.broadcasted_iota(jnp.int32, (ch, ch), 0))
    lane = lax.broadcasted_iota(jnp.int32, (ch, LANES), 1)
    ones_col = jnp.where(lane == 0, 1.0, 0.0).astype(BF16)
    for h in range(ML_HEADS):
        sl = slice(h * ML_HEAD_DIM, (h + 1) * ML_HEAD_DIM)
        q = q_ref[:, sl]
        k = k_ref[:, sl]
        vp = jnp.concatenate([v_ref[:, sl], ones_col], axis=1)
        g_row = gr_ref[0, GATE_ML + h:GATE_ML + h + 1, :]
        g_col = gc_ref[:, GATE_ML + h:GATE_ML + h + 1]
        f_col = fc_ref[:, GATE_ML + h:GATE_ML + h + 1]
        u_prev = u_sc[h][:, :1]
        gm = jnp.where(causal, g_row, NEG)
        u_i = jnp.maximum(u_prev, jnp.max(gm, axis=1, keepdims=True))
        dmat = jnp.exp(gm - u_i)
        s = lax.dot_general(q, k, (((1,), (1,)), ((), ())), preferred_element_type=F32)
        scores = (s * dmat).astype(BF16)
        inter = jnp.exp(u_prev - u_i)
        ct = ct_sc[h]
        nd = (jnp.dot(scores, vp, preferred_element_type=F32)
              + jnp.dot(q, ct.astype(BF16), preferred_element_type=F32) * inter)
        num = nd[:, :ML_HEAD_DIM]
        den = nd[:, ML_HEAD_DIM:ML_HEAD_DIM + 1]
        den = jnp.maximum(jnp.abs(den), jnp.exp(-(f_col + u_i)))
        hh = num / den
        ms = jnp.mean(hh * hh, axis=1, keepdims=True)
        y = hh * lax.rsqrt(ms + EPS) * ng_ref[:, sl] * og_ref[:, sl].astype(F32)
        o_ref[:, sl] = y.astype(BF16)
        u_new = jnp.maximum(u_prev, jnp.max(g_row, axis=1, keepdims=True))
        wv = (jnp.exp(g_col - u_new) * vp.astype(F32)).astype(BF16)
        ct_sc[h] = (jnp.exp(u_prev - u_new) * ct
                    + lax.dot_general(k, wv, (((0,), (0,)), ((), ())), preferred_element_type=F32))
        u_sc[h] = jnp.broadcast_to(u_new, (1, LANES))


def _mlstm_call(qkv, gc, fc, gr, ng, *, bsz, seq, ch=256):
    t = qkv.shape[0]
    nc = seq // ch
    row = lambda b, c: (b * nc + c, 0)
    return pl.pallas_call(
        functools.partial(_mlstm_kernel, ch=ch),
        grid=(bsz, nc),
        in_specs=[
            pl.BlockSpec((ch, ML_WIDTH), lambda b, c: (b * nc + c, OUT_MQ)),
            pl.BlockSpec((ch, ML_WIDTH), lambda b, c: (b * nc + c, OUT_MK)),
            pl.BlockSpec((ch, ML_WIDTH), lambda b, c: (b * nc + c, OUT_MV)),
            pl.BlockSpec((ch, ML_WIDTH), lambda b, c: (b * nc + c, OUT_MO)),
            pl.BlockSpec((ch, LANES), row),
            pl.BlockSpec((ch, LANES), row),
            pl.BlockSpec((1, GATE_ROWS, ch), lambda b, c: (b, 0, c)),
            pl.BlockSpec((1, ML_WIDTH), lambda b, c: (0, 0)),
        ],
        out_specs=pl.BlockSpec((ch, ML_WIDTH), row),
        out_shape=jax.ShapeDtypeStruct((t, ML_WIDTH), BF16),
        scratch_shapes=[pltpu.VMEM((ML_HEADS, ML_HEAD_DIM, 2 * ML_HEAD_DIM), F32),
                        pltpu.VMEM((ML_HEADS, 1, LANES), F32)],
        compiler_params=_cparams(2, 32),
        name="mlstm",
    )(qkv, qkv, qkv, qkv, gc, fc, gr, ng)


def _post_kernel(x_ref, hf_ref, hm_ref, wo_ref, g1_ref, nf_ref, sc_ref, sh_ref, wr_ref, br_ref, tri_ref,
                 x1_ref, hn_ref, gsel_ref, grp_ref, rank_ref, tot_ref, cnt_sc, *, te, steps_per_chunk):
    @pl.when(pl.program_id(0) % steps_per_chunk == 0)
    def _():
        cnt_sc[...] = jnp.zeros_like(cnt_sc)

    mix = (jnp.dot(hf_ref[...], wo_ref[:FOX_WIDTH, :], preferred_element_type=F32)
           + jnp.dot(hm_ref[...], wo_ref[FOX_WIDTH:, :], preferred_element_type=F32))
    x1 = x_ref[...] + g1_ref[0] * mix
    x1_ref[...] = x1
    ms = jnp.mean(x1 * x1, axis=1, keepdims=True)
    hn = x1 * lax.rsqrt(ms + EPS) * nf_ref[...]
    hn = hn * (1.0 + sc_ref[0]) + sh_ref[0]
    _to_token_tiles(hn_ref, hn)

    logits = jnp.dot(hn, wr_ref[...], preferred_element_type=F32, precision=lax.Precision.HIGHEST)
    aff = jax.nn.sigmoid(logits.T[:N_EXPERTS, :])
    sel = aff + br_ref[...]
    selr = [sel[e:e + 1, :] for e in range(N_EXPERTS)]
    affr = [aff[e:e + 1, :] for e in range(N_EXPERTS)]
    keep = [None] * N_EXPERTS
    score = []
    for g in range(N_GROUPS):
        vs = selr[g * EXPERTS_PER_GROUP:(g + 1) * EXPERTS_PER_GROUP]
        sg = jnp.zeros_like(vs[0])
        for i in range(EXPERTS_PER_GROUP):
            beaten = jnp.zeros_like(vs[0])
            for j in range(EXPERTS_PER_GROUP):
                if j != i:
                    b = (vs[j] >= vs[i]) if j < i else (vs[j] > vs[i])
                    beaten = beaten + jnp.where(b, 1.0, 0.0)
            kp = beaten < 2.0
            keep[g * EXPERTS_PER_GROUP + i] = kp
            sg = sg + jnp.where(kp, vs[i], 0.0)
        score.append(sg)
    chosen = []
    for g in range(N_GROUPS):
        lost = jnp.zeros_like(score[0])
        for g2 in range(N_GROUPS):
            if g2 != g:
                b = (score[g2] >= score[g]) if g2 < g else (score[g2] > score[g])
                lost = lost + jnp.where(b, 1.0, 0.0)
        chosen.append(jnp.where(lost < 0.5, 1.0, 0.0))
    wsel = []
    for i in range(EXPERTS_PER_GROUP):
        wi = jnp.zeros_like(score[0])
        for g in range(N_GROUPS):
            e = g * EXPERTS_PER_GROUP + i
            wi = wi + chosen[g] * jnp.where(keep[e], affr[e], 0.0)
        wsel.append(wi)
    wsum = wsel[0] + wsel[1] + wsel[2] + wsel[3]
    wsel = [w / wsum for w in wsel]

    row8 = lax.broadcasted_iota(jnp.int32, (SUBLANES, te), 0)
    gmat = jnp.zeros((SUBLANES, te), F32)
    wmat = jnp.zeros((SUBLANES, te), F32)
    grp = jnp.zeros_like(score[0])
    for g in range(N_GROUPS):
        gmat = jnp.where(row8 == g, chosen[g], gmat)
        wmat = jnp.where(row8 == g, wsel[g], wmat)
        grp = grp + g * chosen[g]
    pref = jnp.dot(gmat.astype(BF16), tri_ref[...], preferred_element_type=F32) + cnt_sc[:, :1]
    rank = jnp.sum(gmat * pref, axis=0, keepdims=True)
    tot = cnt_sc[:, :1] + jnp.sum(gmat, axis=1, keepdims=True)
    cnt_sc[...] = jnp.broadcast_to(tot, cnt_sc.shape)
    rank_ref[0] = rank.astype(jnp.int32)
    grp_ref[0] = grp.astype(jnp.int32)
    tot_ref[0] = jnp.broadcast_to(tot, (SUBLANES, LANES)).astype(jnp.int32)
    wfull = jnp.concatenate([wmat, jnp.zeros((LANES - SUBLANES, te), F32)], axis=0)
    gsel_ref[...] = wfull.T


def _post_call(x, hf, hm, wo, g1, nf, sc, sh, wr, br, tri, *, bsz, seq, chunk, te=512):
    t, d = x.shape
    steps_per_chunk = chunk // te
    per_seq = seq // te
    n_steps = t // te
    n_chunks = t // chunk
    row = lambda i: (i, 0)
    per_b = lambda i: (i // per_seq, 0, 0)
    const2 = lambda i: (0, 0)
    return pl.pallas_call(
        functools.partial(_post_kernel, te=te, steps_per_chunk=steps_per_chunk),
        grid=(n_steps,),
        in_specs=[
            pl.BlockSpec((te, d), row),
            pl.BlockSpec((te, FOX_WIDTH), row),
            pl.BlockSpec((te, ML_WIDTH), row),
            pl.BlockSpec((d, d), const2),
            pl.BlockSpec((1, 1, d), per_b),
            pl.BlockSpec((1, d), const2),
            pl.BlockSpec((1, 1, d), per_b),
            pl.BlockSpec((1, 1, d), per_b),
            pl.BlockSpec((d, LANES), const2),
            pl.BlockSpec((N_EXPERTS, 1), const2),
            pl.BlockSpec((te, te), const2),
        ],
        out_specs=[
            pl.BlockSpec((te, d), row),
            pl.BlockSpec((te * TOK_ROWS, LANES), row),
            pl.BlockSpec((te, LANES), row),
            pl.BlockSpec((1, 1, te), lambda i: (i, 0, 0)),
            pl.BlockSpec((1, 1, te), lambda i: (i, 0, 0)),
            pl.BlockSpec((1, SUBLANES, LANES), lambda i: (i // steps_per_chunk, 0, 0)),
        ],
        out_shape=[
            jax.ShapeDtypeStruct((t, d), F32),
            jax.ShapeDtypeStruct((t * TOK_ROWS, LANES), F32),
            jax.ShapeDtypeStruct((t, LANES), F32),
            jax.ShapeDtypeStruct((n_steps, 1, te), jnp.int32),
            jax.ShapeDtypeStruct((n_steps, 1, te), jnp.int32),
            jax.ShapeDtypeStruct((n_chunks, SUBLANES, LANES), jnp.int32),
        ],
        scratch_shapes=[pltpu.VMEM((SUBLANES, LANES), F32)],
        compiler_params=_cparams(1, 48),
        name="post_router",
    )(x, hf, hm, wo, g1, nf, sc, sh, wr, br, tri)


def _scatter_kernel(grp_ref, rank_ref, off_ref, hn_ref, gsel_ref, xb_ref, gs_ref, xs_sc, *, chunk, rows, tm):
    xs_sc[...] = jnp.zeros_like(xs_sc)
    gs_ref[...] = jnp.zeros_like(gs_ref)

    def body(t, carry):
        p = off_ref[0, 0, grp_ref[0, 0, t]] + rank_ref[0, 0, t]
        src = pl.multiple_of(t * TOK_ROWS, TOK_ROWS)
        dst = pl.multiple_of(p * TOK_ROWS, TOK_ROWS)
        xs_sc[pl.ds(dst, TOK_ROWS), :] = hn_ref[pl.ds(src, TOK_ROWS), :]
        gs_ref[pl.ds(p, 1), :] = gsel_ref[pl.ds(t, 1), :]
        return carry

    lax.fori_loop(0, chunk, body, 0, unroll=8)
    for j in range(rows // tm):
        for c in range(TOK_ROWS):
            xb_ref[j * tm:(j + 1) * tm, c * LANES:(c + 1) * LANES] = (
                xs_sc[pl.ds(j * tm * TOK_ROWS + c, tm, stride=TOK_ROWS), :].astype(BF16))


def _scatter_call(grp, rank, off, hn_tt, gsel, *, chunk, rows, tm):
    t = gsel.shape[0]
    n_chunks = t // chunk
    smem3 = lambda shape: pl.BlockSpec(shape, lambda c: (c, 0, 0), memory_space=pltpu.SMEM)
    return pl.pallas_call(
        functools.partial(_scatter_kernel, chunk=chunk, rows=rows, tm=tm),
        grid=(n_chunks,),
        in_specs=[
            smem3((1, 1, chunk)),
            smem3((1, 1, chunk)),
            smem3((1, 1, N_GROUPS)),
            pl.BlockSpec((chunk * TOK_ROWS, LANES), lambda c: (c, 0)),
            pl.BlockSpec((chunk, LANES), lambda c: (c, 0)),
        ],
        out_specs=[
            pl.BlockSpec((rows, D_MODEL), lambda c: (c, 0)),
            pl.BlockSpec((rows, LANES), lambda c: (c, 0)),
        ],
        out_shape=[
            jax.ShapeDtypeStruct((n_chunks * rows, D_MODEL), BF16),
            jax.ShapeDtypeStruct((n_chunks * rows, LANES), F32),
        ],
        scratch_shapes=[pltpu.VMEM((rows * TOK_ROWS, LANES), F32)],
        compiler_params=_cparams(1, 56),
        name="moe_scatter",
    )(grp, rank, off, hn_tt, gsel)


def _experts_kernel(blk_ref, grp_ref, xb_ref, gs_ref, wg_ref, wu_ref, wd_ref, y_ref, *, tm):
    g = grp_ref[pl.program_id(0)]

    @pl.when(g < N_GROUPS)
    def _():
        x = xb_ref[...]
        gs = gs_ref[...]
        acc = jnp.zeros((tm, D_MODEL), F32)
        for i in range(EXPERTS_PER_GROUP):
            hg = jnp.dot(x, wg_ref[i], preferred_element_type=F32)
            hu = jnp.dot(x, wu_ref[i], preferred_element_type=F32)
            a = _silu(hg) * hu * gs[:, i:i + 1]
            acc = acc + jnp.dot(a.astype(BF16), wd_ref[i], preferred_element_type=F32)
        _to_token_tiles(y_ref, acc)

    @pl.when(g >= N_GROUPS)
    def _():
        y_ref[...] = jnp.zeros_like(y_ref)


def _experts_call(tile_blk, tile_grp, xb, gs, wg, wu, wd, *, tm):
    n_rows = xb.shape[0]
    n_slots = n_rows // tm
    wmap = lambda s, blk, grp: (jnp.minimum(grp[s], N_GROUPS - 1), 0, 0)
    grid_spec = pltpu.PrefetchScalarGridSpec(
        num_scalar_prefetch=2,
        grid=(n_slots,),
        in_specs=[
            pl.BlockSpec((tm, D_MODEL), lambda s, blk, grp: (blk[s], 0)),
            pl.BlockSpec((tm, LANES), lambda s, blk, grp: (blk[s], 0)),
            pl.BlockSpec((EXPERTS_PER_GROUP, D_MODEL, D_FF), wmap),
            pl.BlockSpec((EXPERTS_PER_GROUP, D_MODEL, D_FF), wmap),
            pl.BlockSpec((EXPERTS_PER_GROUP, D_FF, D_MODEL), wmap),
        ],
        out_specs=pl.BlockSpec((tm * TOK_ROWS, LANES), lambda s, blk, grp: (blk[s], 0)),
    )
    return pl.pallas_call(
        functools.partial(_experts_kernel, tm=tm),
        grid_spec=grid_spec,
        out_shape=jax.ShapeDtypeStruct((n_rows * TOK_ROWS, LANES), F32),
        compiler_params=_cparams(1, 48),
        name="moe_experts",
    )(tile_blk, tile_grp, xb, gs, wg, wu, wd)


def _gather_kernel(grp_ref, rank_ref, off_ref, ys_ref, o_ref, *, chunk):
    def body(t, carry):
        p = off_ref[0, 0, grp_ref[0, 0, t]] + rank_ref[0, 0, t]
        src = pl.multiple_of(p * TOK_ROWS, TOK_ROWS)
        dst = pl.multiple_of(t * TOK_ROWS, TOK_ROWS)
        o_ref[pl.ds(dst, TOK_ROWS), :] = ys_ref[pl.ds(src, TOK_ROWS), :]
        return carry

    lax.fori_loop(0, chunk, body, 0, unroll=8)


def _gather_call(grp, rank, off, ys_tt, *, chunk, rows, n_tok):
    n_chunks = n_tok // chunk
    smem3 = lambda shape: pl.BlockSpec(shape, lambda c: (c, 0, 0), memory_space=pltpu.SMEM)
    return pl.pallas_call(
        functools.partial(_gather_kernel, chunk=chunk),
        grid=(n_chunks,),
        in_specs=[
            smem3((1, 1, chunk)),
            smem3((1, 1, chunk)),
            smem3((1, 1, N_GROUPS)),
            pl.BlockSpec((rows * TOK_ROWS, LANES), lambda c: (c, 0)),
        ],
        out_specs=pl.BlockSpec((chunk * TOK_ROWS, LANES), lambda c: (c, 0)),
        out_shape=jax.ShapeDtypeStruct((n_tok * TOK_ROWS, LANES), F32),
        compiler_params=_cparams(1, 56),
        name="moe_gather",
    )(grp, rank, off, ys_tt)


def _final_kernel(x_ref, ott_ref, g2_ref, nf_ref, o_ref, *, tm):
    x = x_ref[...] + g2_ref[0] * _from_token_tiles(ott_ref, tm)
    ms = jnp.mean(x * x, axis=1, keepdims=True)
    o_ref[...] = x * lax.rsqrt(ms + EPS) * nf_ref[...]


def _final_call(x, moe_tt, g2, nf, *, seq, tm=512):
    t, d = x.shape
    per_seq = seq // tm
    return pl.pallas_call(
        functools.partial(_final_kernel, tm=tm),
        grid=(t // tm,),
        in_specs=[
            pl.BlockSpec((tm, d), lambda i: (i, 0)),
            pl.BlockSpec((tm * TOK_ROWS, LANES), lambda i: (i, 0)),
            pl.BlockSpec((1, 1, d), lambda i: (i // per_seq, 0, 0)),
            pl.BlockSpec((1, d), lambda i: (0, 0)),
        ],
        out_specs=pl.BlockSpec((tm, d), lambda i: (i, 0)),
        out_shape=jax.ShapeDtypeStruct((t, d), F32),
        compiler_params=_cparams(1, 32),
        name="final_norm",
    )(x, moe_tt, g2, nf)


def _moe_tiles(tot, *, tm, tiles_per_chunk):
    nt = (tot + tm - 1) // tm
    ts = jnp.cumsum(nt, axis=1) - nt
    off = (ts * tm).astype(jnp.int32)
    j = jnp.arange(tiles_per_chunk, dtype=jnp.int32)[None, :, None]
    inside = (j >= ts[:, None, :]) & (j < (ts + nt)[:, None, :])
    key = jnp.where(jnp.any(inside, axis=-1), jnp.argmax(inside, axis=-1), N_GROUPS).reshape(-1)
    order = jnp.argsort(key, stable=True).astype(jnp.int32)
    return off, order, key[order].astype(jnp.int32)


def kernel(x, c, w_in, conv_w, conv_b, fox_f_bias, mlstm_i_bias, mlstm_f_bias, fox_out_norm,
           mlstm_out_norm, w_out, w_ada, b_ada, norm_mix, norm_ffn, w_router, b_router, w_gate,
           w_up, w_down, norm_final):
    bsz, seq, d = x.shape
    depth = w_in.shape[0]
    t = bsz * seq
    assert d == D_MODEL and w_in.shape[-1] == IN_COLS
    chunk = min(2048, seq)
    tm = 256
    rows = chunk + N_GROUPS * tm
    te = min(512, seq)
    assert seq % 512 == 0 and seq % chunk == 0 and chunk % te == 0

    mods = _ada_call(c, w_ada, b_ada).reshape(depth, bsz, N_ADA, 1, d)
    xf = x.reshape(t, d)

    tri = (jnp.arange(te)[:, None] < jnp.arange(te)[None, :]).astype(BF16)
    wr = jnp.pad(w_router, ((0, 0), (0, LANES - N_EXPERTS)))
    br = b_router.reshape(N_EXPERTS, 1).astype(F32)

    moe_tt = None
    g2_prev = None
    for l in range(depth):
        sh1, sc1, g1, sh2, sc2, g2 = [mods[l, :, i] for i in range(N_ADA)]
        wl = w_in[l]
        wm = jnp.concatenate([wl[:, FOX_Q:FOX_F], wl[:, ML_Q:ML_I], wl[:, ML_O:IN_COLS]], axis=1).astype(BF16)
        zpad = lambda n: jnp.zeros((d, n), F32)
        wg = jnp.concatenate([
            wl[:, FOX_F:ML_Q], wl[:, ML_F:ML_O], zpad(LANES - FOX_HEADS - ML_HEADS),
            zpad(GATE_ML), wl[:, ML_I:ML_F], zpad(LANES - GATE_ML - ML_HEADS)], axis=1).astype(BF16)
        zb = lambda n: jnp.zeros((n,), F32)
        gb = jnp.concatenate([
            fox_f_bias[l], mlstm_f_bias[l], zb(LANES - FOX_HEADS - ML_HEADS),
            zb(GATE_ML), mlstm_i_bias[l], zb(LANES - GATE_ML - ML_HEADS)]).reshape(1, 2 * LANES)

        xf, qa, ka, qkv, fc, gc, gr = _in_proj_call(
            xf, moe_tt, g2_prev, norm_mix[l].reshape(1, d), sc1, sh1, wm, wg, gb,
            conv_w[l], conv_b[l].reshape(1, -1), bsz=bsz, seq=seq)
        hf = _fox_call(qa, ka, qkv, fox_out_norm[l].reshape(1, FOX_WIDTH), bsz=bsz, seq=seq)
        hm = _mlstm_call(qkv, gc, fc, gr, mlstm_out_norm[l].reshape(1, ML_WIDTH), bsz=bsz, seq=seq)
        xf, hn_tt, gsel, grp, rank, tot = _post_call(
            xf, hf, hm, w_out[l].astype(BF16), g1, norm_ffn[l].reshape(1, d), sc2, sh2, wr, br, tri,
            bsz=bsz, seq=seq, chunk=chunk, te=te)

        n_chunks = t // chunk
        grp = grp.reshape(n_chunks, 1, chunk)
        rank = rank.reshape(n_chunks, 1, chunk)
        off, tile_blk, tile_grp = _moe_tiles(tot[:, :N_GROUPS, 0], tm=tm, tiles_per_chunk=rows // tm)
        off = off.reshape(n_chunks, 1, N_GROUPS)
        xb, gs = _scatter_call(grp, rank, off, hn_tt, gsel, chunk=chunk, rows=rows, tm=tm)
        ys_tt = _experts_call(tile_blk, tile_grp, xb, gs, w_gate[l].astype(BF16),
                              w_up[l].astype(BF16), w_down[l].astype(BF16), tm=tm)
        moe_tt = _gather_call(grp, rank, off, ys_tt, chunk=chunk, rows=rows, n_tok=t)
        g2_prev = g2

    out = _final_call(xf, moe_tt, g2_prev, norm_final.reshape(1, d), seq=seq)
    return out.reshape(bsz, seq, d)
```

```python
import functools

import numpy as np
import jax
import jax.numpy as jnp
from jax import lax
from jax.experimental import pallas as pl
from jax.experimental.pallas import tpu as pltpu

F32 = jnp.float32
BF16 = jnp.bfloat16

LANES = 128
SUBLANES = 8
VMEM_BYTES_V7X = 64 * 1024 * 1024

D_MODEL = 1024
FOX_HEADS = 8
FOX_HEAD_DIM = 64
FOX_WIDTH = FOX_HEADS * FOX_HEAD_DIM
ML_HEADS = 4
ML_HEAD_DIM = 128
ML_WIDTH = ML_HEADS * ML_HEAD_DIM
CONV_WIDTH = 4
N_EXPERTS = 16
N_GROUPS = 4
EXPERTS_PER_GROUP = 4
D_FF = 512
N_ADA = 6
EPS = 1e-6
NEG = -1e30

FOX_Q = 0
FOX_F = 3 * FOX_WIDTH
ML_Q = FOX_F + FOX_HEADS
ML_I = ML_Q + 3 * ML_WIDTH
ML_F = ML_I + ML_HEADS
ML_O = ML_F + ML_HEADS
IN_COLS = ML_O + ML_WIDTH

MAIN_COLS = 7 * 512
BLK_FQ, BLK_FK, BLK_FV, BLK_MQ, BLK_MK, BLK_MV, BLK_MO = range(7)
REST_COLS = 4 * 512
OUT_FV, OUT_MQ, OUT_MV, OUT_MO = range(4)
AUG_COLS = FOX_HEADS * 128
BIAS_TERMS = 3
LOG2E = 1.4426950408889634
GATE_FOX = 0
GATE_ML = FOX_HEADS
GATE_ROWS = 16

TOK_ROWS = D_MODEL // LANES


def _cparams(n_grid, vmem_mb):
    return pltpu.CompilerParams(
        dimension_semantics=("arbitrary",) * n_grid,
        vmem_limit_bytes=vmem_mb * 1024 * 1024)


def _silu(x):
    return x * jax.nn.sigmoid(x)


def _log_sigmoid(z):
    return jnp.minimum(z, 0.0) - jnp.log1p(jnp.exp(-jnp.abs(z)))


def _cumsum_rows(x):
    n = x.shape[0]
    row = lax.broadcasted_iota(jnp.int32, x.shape, 0)
    s = 1
    while s < n:
        x = x + jnp.where(row >= s, pltpu.roll(x, s, axis=0), 0.0)
        s *= 2
    return x


def _from_token_tiles(ref, n_tok):
    return jnp.concatenate(
        [ref[pl.ds(c, n_tok, stride=TOK_ROWS), :] for c in range(TOK_ROWS)], axis=1)


def _to_token_tiles(ref, val, row0=0):
    n = val.shape[0]
    for c in range(TOK_ROWS):
        ref[pl.ds(row0 * TOK_ROWS + c, n, stride=TOK_ROWS), :] = val[:, c * LANES:(c + 1) * LANES]


def _ada_kernel(c_ref, w_ref, b_ref, o_ref):
    c = c_ref[...]
    o_ref[0] = jnp.dot(_silu(c), w_ref[0], preferred_element_type=F32,
                       precision=lax.Precision.HIGHEST) + b_ref[0]


def _ada_call(c, w_ada, b_ada):
    depth, d, n = w_ada.shape
    bsz = c.shape[0]
    tn = 1536
    return pl.pallas_call(
        _ada_kernel,
        grid=(depth, n // tn),
        in_specs=[
            pl.BlockSpec((bsz, d), lambda l, j: (0, 0)),
            pl.BlockSpec((1, d, tn), lambda l, j: (l, 0, j)),
            pl.BlockSpec((1, 1, tn), lambda l, j: (l, 0, j)),
        ],
        out_specs=pl.BlockSpec((1, bsz, tn), lambda l, j: (l, 0, j)),
        out_shape=jax.ShapeDtypeStruct((depth, bsz, n), F32),
        compiler_params=_cparams(2, 32),
        name="ada_mod",
    )(c, w_ada, b_ada.reshape(depth, 1, n))


def _in_proj_kernel(*refs, tm, combine):
    if combine:
        (x_ref, ott_ref, g2_ref, nm_ref, sc_ref, sh_ref, wm_ref, wg_ref, gb_ref, cw_ref, cb_ref,
         place_ref, ones_ref,
         xn_ref, qa_ref, ka_ref, qkv_ref, kt_ref, fc_ref, gr_ref, fcar, ccar) = refs
    else:
        (x_ref, nm_ref, sc_ref, sh_ref, wm_ref, wg_ref, gb_ref, cw_ref, cb_ref,
         place_ref, ones_ref,
         qa_ref, ka_ref, qkv_ref, kt_ref, fc_ref, gr_ref, fcar, ccar) = refs

    @pl.when(pl.program_id(1) == 0)
    def _():
        fcar[...] = jnp.zeros_like(fcar)
        ccar[...] = jnp.zeros_like(ccar)

    x = x_ref[...]
    if combine:
        x = x + g2_ref[0] * _from_token_tiles(ott_ref, tm)
        xn_ref[...] = x
    ms = jnp.mean(x * x, axis=1, keepdims=True)
    hn = x * lax.rsqrt(ms + EPS) * nm_ref[...]
    hn = hn * (1.0 + sc_ref[0]) + sh_ref[0]
    hb = hn.astype(BF16)

    gp = jnp.dot(hb, wg_ref[...], preferred_element_type=F32) + gb_ref[...]
    lf = _log_sigmoid(gp[:, :LANES])
    fcum = _cumsum_rows(lf) + fcar[...]
    fcar[...] = fcum[tm - 1:tm, :]
    gml = gp[:, LANES:] - fcum
    fc_ref[...] = fcum
    gr_ref[0] = gml.T[:GATE_ROWS, :]

    def mm(j):
        return jnp.dot(hb, wm_ref[:, j * 512:(j + 1) * 512], preferred_element_type=F32)

    def put(j, v):
        qkv_ref[:, j * 512:(j + 1) * 512] = v.astype(BF16)

    lane = lax.broadcasted_iota(jnp.int32, (tm, LANES), 1)
    fs = fcum * LOG2E
    hi = fs.astype(BF16).astype(F32)
    mid = (fs - hi).astype(BF16).astype(F32)
    low = (fs - hi - mid).astype(BF16).astype(F32)
    packed = jnp.where(lane < FOX_HEADS, hi,
                       jnp.where(lane < 2 * FOX_HEADS, pltpu.roll(mid, FOX_HEADS, axis=1),
                                 jnp.where(lane < 3 * FOX_HEADS, pltpu.roll(low, 2 * FOX_HEADS, axis=1), 0.0)))
    bias = jnp.dot(packed.astype(BF16), place_ref[...], preferred_element_type=F32) + ones_ref[...]

    def put_heads(ref, val, col0):
        lo = lane < FOX_HEAD_DIM
        for p in range(FOX_HEADS // 2):
            slab = val[:, p * LANES:(p + 1) * LANES]
            for h, data in ((2 * p, slab), (2 * p + 1, pltpu.roll(slab, FOX_HEAD_DIM, axis=1))):
                blk = jnp.where(lo, data, 0.0) + bias[:, col0 + h * LANES:col0 + (h + 1) * LANES]
                ref[:, h * LANES:(h + 1) * LANES] = blk.astype(BF16)

    put_heads(qa_ref, mm(BLK_FQ) * (FOX_HEAD_DIM ** -0.5 * LOG2E), 0)
    put_heads(ka_ref, mm(BLK_FK), AUG_COLS)
    put(OUT_FV, mm(BLK_FV))
    put(OUT_MV, mm(BLK_MV))
    put(OUT_MO, jax.nn.sigmoid(mm(BLK_MO)))

    u = jnp.concatenate([mm(BLK_MQ), mm(BLK_MK)], axis=1)
    prev = ccar[...]
    ccar[...] = u[tm - SUBLANES:tm, :]
    row8 = lax.broadcasted_iota(jnp.int32, prev.shape, 0)
    y = cb_ref[...] + cw_ref[CONV_WIDTH - 1:CONV_WIDTH, :] * u
    for k in range(1, CONV_WIDTH):
        r = pltpu.roll(u, k, axis=0)
        top = jnp.where(row8 < k, pltpu.roll(prev, k, axis=0), r[:SUBLANES])
        shifted = jnp.concatenate([top, r[SUBLANES:]], axis=0)
        y = y + cw_ref[CONV_WIDTH - 1 - k:CONV_WIDTH - k, :] * shifted
    act = _silu(y)
    put(OUT_MQ, act[:, :ML_WIDTH])
    kt_ref[0] = (act[:, ML_WIDTH:] * (ML_HEAD_DIM ** -0.5)).T.astype(BF16)


def _bias_placement():
    place = np.zeros((LANES, 2 * AUG_COLS), np.float32)
    ones = np.zeros((1, 2 * AUG_COLS), np.float32)
    for h in range(FOX_HEADS):
        for term in range(BIAS_TERMS):
            src = term * FOX_HEADS + h
            place[src, h * LANES + FOX_HEAD_DIM + term] = 1.0
            ones[0, h * LANES + FOX_HEAD_DIM + BIAS_TERMS + term] = 1.0
            place[src, AUG_COLS + h * LANES + FOX_HEAD_DIM + BIAS_TERMS + term] = -1.0
            ones[0, AUG_COLS + h * LANES + FOX_HEAD_DIM + term] = 1.0
    return jnp.asarray(place, BF16), jnp.asarray(ones, F32)


def _in_proj_call(x, moe_tt, g2, nm, sc, sh, wm, wg, gb, cw, cb, *, bsz, seq, tm=512):
    t, d = x.shape
    ns = seq // tm
    combine = moe_tt is not None
    row = lambda b, s: (b * ns + s, 0)
    per_b = lambda b, s: (b, 0, 0)
    const2 = lambda b, s: (0, 0)
    in_specs = [pl.BlockSpec((tm, d), row)]
    args = [x]
    if combine:
        in_specs += [pl.BlockSpec((tm * TOK_ROWS, LANES), row), pl.BlockSpec((1, 1, d), per_b)]
        args += [moe_tt, g2]
    in_specs += [
        pl.BlockSpec((1, d), const2),
        pl.BlockSpec((1, 1, d), per_b),
        pl.BlockSpec((1, 1, d), per_b),
        pl.BlockSpec((d, MAIN_COLS), const2),
        pl.BlockSpec((d, 2 * LANES), const2),
        pl.BlockSpec((1, 2 * LANES), const2),
        pl.BlockSpec((CONV_WIDTH, 2 * ML_WIDTH), const2),
        pl.BlockSpec((1, 2 * ML_WIDTH), const2),
        pl.BlockSpec((LANES, 2 * AUG_COLS), const2),
        pl.BlockSpec((1, 2 * AUG_COLS), const2),
    ]
    place, ones = _bias_placement()
    args += [nm, sc, sh, wm, wg, gb, cw, cb, place, ones]
    out_specs = []
    out_shape = []
    if combine:
        out_specs.append(pl.BlockSpec((tm, d), row))
        out_shape.append(jax.ShapeDtypeStruct((t, d), F32))
    out_specs += [
        pl.BlockSpec((tm, AUG_COLS), row),
        pl.BlockSpec((tm, AUG_COLS), row),
        pl.BlockSpec((tm, REST_COLS), row),
        pl.BlockSpec((1, ML_WIDTH, tm), lambda b, s: (b, 0, s)),
        pl.BlockSpec((tm, LANES), row),
        pl.BlockSpec((1, GATE_ROWS, tm), lambda b, s: (b, 0, s)),
    ]
    out_shape += [
        jax.ShapeDtypeStruct((t, AUG_COLS), BF16),
        jax.ShapeDtypeStruct((t, AUG_COLS), BF16),
        jax.ShapeDtypeStruct((t, REST_COLS), BF16),
        jax.ShapeDtypeStruct((bsz, ML_WIDTH, seq), BF16),
        jax.ShapeDtypeStruct((t, LANES), F32),
        jax.ShapeDtypeStruct((bsz, GATE_ROWS, seq), F32),
    ]
    outs = pl.pallas_call(
        functools.partial(_in_proj_kernel, tm=tm, combine=combine),
        grid=(bsz, ns),
        in_specs=in_specs,
        out_specs=out_specs,
        out_shape=out_shape,
        scratch_shapes=[pltpu.VMEM((1, LANES), F32), pltpu.VMEM((SUBLANES, 2 * ML_WIDTH), F32)],
        compiler_params=_cparams(2, 48),
        name="in_proj",
    )(*args)
    if combine:
        return outs
    return [x] + list(outs)


def _fox_kernel(q_ref, k_ref, v_ref, ng_ref, cm_ref, o_ref, m_sc, acc_sc, *, tq, tk, nh):
    qi = pl.program_id(2)
    lane = lax.broadcasted_iota(jnp.int32, (tq, LANES), 1)
    lo = lane < FOX_HEAD_DIM
    ones_col = jnp.where(lax.broadcasted_iota(jnp.int32, (tk, LANES), 1) == 0, 1.0, 0.0).astype(BF16)
    for h in range(nh):
        m_sc[h] = jnp.full((tq, LANES), NEG, F32)
        acc_sc[h] = jnp.zeros((tq, 2 * LANES), F32)

    def body(j, carry):
        k0 = pl.multiple_of(j * tk, tk)
        slot = (j == qi).astype(jnp.int32)
        for h in range(nh):
            hs = slice(h * LANES, (h + 1) * LANES)
            vs = slice((h // 2) * LANES, (h // 2 + 1) * LANES)
            va = jnp.concatenate([v_ref[pl.ds(k0, tk), vs], ones_col], axis=1)
            z = lax.dot_general(q_ref[:, hs], k_ref[pl.ds(k0, tk), hs], (((1,), (1,)), ((), ())),
                                preferred_element_type=F32)
            z = z + cm_ref[slot]
            m_prev = m_sc[h]
            m_new = jnp.maximum(m_prev, jnp.max(z, axis=1, keepdims=True))
            alpha = jnp.exp2(m_prev - m_new)
            p = jnp.exp2(z - jnp.concatenate([m_new] * (tk // LANES), axis=1))
            acc_sc[h] = (jnp.concatenate([alpha, alpha], axis=1) * acc_sc[h]
                         + jnp.dot(p.astype(BF16), va, preferred_element_type=F32))
            m_sc[h] = m_new
        return carry

    lax.fori_loop(0, qi + 1, body, 0)

    for p in range(nh // 2):
        outs = []
        for h in (2 * p, 2 * p + 1):
            acc = acc_sc[h]
            outs.append(acc[:, :LANES] / acc[:, LANES:LANES + 1])
        o = jnp.where(lo, outs[0], outs[1])
        sq = o * o
        ms0 = jnp.sum(jnp.where(lo, sq, 0.0), axis=1, keepdims=True) * (1.0 / FOX_HEAD_DIM)
        ms1 = jnp.sum(jnp.where(lo, 0.0, sq), axis=1, keepdims=True) * (1.0 / FOX_HEAD_DIM)
        inv = jnp.where(lo, lax.rsqrt(ms0 + EPS), lax.rsqrt(ms1 + EPS))
        ps = slice(p * LANES, (p + 1) * LANES)
        o_ref[:, ps] = (o * inv * ng_ref[:, ps]).astype(BF16)


def _fox_call(qa, ka, rest, ng, *, bsz, seq, tq=512, nh=8):
    t = qa.shape[0]
    nq = seq // tq
    ngrp = FOX_HEADS // nh
    vw = nh // 2 * LANES
    causal = jnp.where(jnp.arange(tq)[None, :] <= jnp.arange(tq)[:, None], 0.0, NEG).astype(F32)
    cmask = jnp.stack([jnp.zeros_like(causal), causal])
    return pl.pallas_call(
        functools.partial(_fox_kernel, tq=tq, tk=tq, nh=nh),
        grid=(bsz, ngrp, nq),
        in_specs=[
            pl.BlockSpec((tq, nh * LANES), lambda b, p, i: (b * nq + i, p)),
            pl.BlockSpec((seq, nh * LANES), lambda b, p, i: (b, p)),
            pl.BlockSpec((seq, vw), lambda b, p, i: (b, OUT_FV * ngrp + p)),
            pl.BlockSpec((1, vw), lambda b, p, i: (0, p)),
            pl.BlockSpec((2, tq, tq), lambda b, p, i: (0, 0, 0)),
        ],
        out_specs=pl.BlockSpec((tq, vw), lambda b, p, i: (b * nq + i, p)),
        out_shape=jax.ShapeDtypeStruct((t, FOX_WIDTH), BF16),
        scratch_shapes=[pltpu.VMEM((nh, tq, LANES), F32), pltpu.VMEM((nh, tq, 2 * LANES), F32)],
        compiler_params=_cparams(3, 48),
        name="fox_attn",
    )(qa, ka, rest, ng, cmask)


def _mlstm_kernel(q_ref, kt_ref, v_ref, og_ref, fc_ref, gr_ref, ng_ref, o_ref, ct_sc, u_sc, *, ch):
    @pl.when(pl.program_id(1) == 0)
    def _():
        ct_sc[...] = jnp.zeros_like(ct_sc)
        u_sc[...] = jnp.zeros_like(u_sc)

    causal = (lax.broadcasted_iota(jnp.int32, (ch, ch), 1)
              <= lax.broadcasted_iota(jnp.int32, (ch, ch), 0))
    lane = lax.broadcasted_iota(jnp.int32, (ch, LANES), 1)
    fc = fc_ref[...]
    for h in range(ML_HEADS):
        sl = slice(h * ML_HEAD_DIM, (h + 1) * ML_HEAD_DIM)
        gl = GATE_ML + h
        q = q_ref[:, sl]
        kt = kt_ref[0, sl, :]
        vp = jnp.concatenate([v_ref[:, sl], jnp.where(lane == gl, 1.0, 0.0).astype(BF16)], axis=1)
        g_row = gr_ref[0, gl:gl + 1, :]
        u_prev = u_sc[h][:, :1]
        gm = jnp.where(causal, g_row, NEG)
        u_i = jnp.maximum(u_prev, jnp.max(gm, axis=1, keepdims=True))
        dmat = jnp.exp(gm - u_i)
        s = jnp.dot(q, kt, preferred_element_type=F32)
        scores = (s * dmat).astype(BF16)
        inter = jnp.exp(u_prev - u_i)
        ct = ct_sc[h]
        nd = (jnp.dot(scores, vp, preferred_element_type=F32)
              + jnp.dot(q, ct.astype(BF16), preferred_element_type=F32) * inter)
        num = nd[:, :ML_HEAD_DIM]
        den = jnp.maximum(jnp.abs(nd[:, ML_HEAD_DIM:]), jnp.exp(-(jnp.where(lane == gl, fc, 0.0) + u_i)))
        ms = jnp.mean(num * num, axis=1, keepdims=True)
        scale = lax.rsqrt(ms + EPS * den * den)[:, gl:gl + 1]
        y = num * scale * ng_ref[:, sl] * og_ref[:, sl].astype(F32)
        o_ref[:, sl] = y.astype(BF16)
        u_new = jnp.maximum(u_prev, jnp.max(g_row, axis=1, keepdims=True))
        ktw = (kt.astype(F32) * jnp.exp(g_row - u_new)).astype(BF16)
        ct_sc[h] = jnp.exp(u_prev - u_new) * ct + jnp.dot(ktw, vp, preferred_element_type=F32)
        u_sc[h] = jnp.broadcast_to(u_new, (1, LANES))


def _mlstm_call(qkv, mkt, fc, gr, ng, *, bsz, seq, ch=256):
    t = qkv.shape[0]
    nc = seq // ch
    row = lambda b, c: (b * nc + c, 0)
    return pl.pallas_call(
        functools.partial(_mlstm_kernel, ch=ch),
        grid=(bsz, nc),
        in_specs=[
            pl.BlockSpec((ch, ML_WIDTH), lambda b, c: (b * nc + c, OUT_MQ)),
            pl.BlockSpec((1, ML_WIDTH, ch), lambda b, c: (b, 0, c)),
            pl.BlockSpec((ch, ML_WIDTH), lambda b, c: (b * nc + c, OUT_MV)),
            pl.BlockSpec((ch, ML_WIDTH), lambda b, c: (b * nc + c, OUT_MO)),
            pl.BlockSpec((ch, LANES), row),
            pl.BlockSpec((1, GATE_ROWS, ch), lambda b, c: (b, 0, c)),
            pl.BlockSpec((1, ML_WIDTH), lambda b, c: (0, 0)),
        ],
        out_specs=pl.BlockSpec((ch, ML_WIDTH), row),
        out_shape=jax.ShapeDtypeStruct((t, ML_WIDTH), BF16),
        scratch_shapes=[pltpu.VMEM((ML_HEADS, ML_HEAD_DIM, 2 * ML_HEAD_DIM), F32),
                        pltpu.VMEM((ML_HEADS, 1, LANES), F32)],
        compiler_params=_cparams(2, 32),
        name="mlstm",
    )(qkv, mkt, qkv, qkv, fc, gr, ng)


def _post_kernel(x_ref, hf_ref, hm_ref, wo_ref, g1_ref, nf_ref, sc_ref, sh_ref, wr_ref, br_ref, tri_ref,
                 x1_ref, hn_ref, gsel_ref, grp_ref, rank_ref, tot_ref, cnt_sc, *, te, steps_per_chunk):
    @pl.when(pl.program_id(0) % steps_per_chunk == 0)
    def _():
        cnt_sc[...] = jnp.zeros_like(cnt_sc)

    mix = (jnp.dot(hf_ref[...], wo_ref[:FOX_WIDTH, :], preferred_element_type=F32)
           + jnp.dot(hm_ref[...], wo_ref[FOX_WIDTH:, :], preferred_element_type=F32))
    x1 = x_ref[...] + g1_ref[0] * mix
    x1_ref[...] = x1
    ms = jnp.mean(x1 * x1, axis=1, keepdims=True)
    hn = x1 * lax.rsqrt(ms + EPS) * nf_ref[...]
    hn = hn * (1.0 + sc_ref[0]) + sh_ref[0]
    _to_token_tiles(hn_ref, hn)

    hn_hi = hn.astype(BF16)
    hn_lo = (hn - hn_hi.astype(F32)).astype(BF16)
    l2 = jnp.dot(hn_hi, wr_ref[...], preferred_element_type=F32)
    logits = (l2[:, :LANES] + l2[:, LANES:]
              + jnp.dot(hn_lo, wr_ref[:, :LANES], preferred_element_type=F32))
    aff = jax.nn.sigmoid(logits.T[:N_EXPERTS, :])
    sel = aff + br_ref[...]
    selr = [sel[e:e + 1, :] for e in range(N_EXPERTS)]
    affr = [aff[e:e + 1, :] for e in range(N_EXPERTS)]
    keep = [None] * N_EXPERTS
    score = []
    for g in range(N_GROUPS):
        vs = selr[g * EXPERTS_PER_GROUP:(g + 1) * EXPERTS_PER_GROUP]
        sg = jnp.zeros_like(vs[0])
        for i in range(EXPERTS_PER_GROUP):
            beaten = jnp.zeros_like(vs[0])
            for j in range(EXPERTS_PER_GROUP):
                if j != i:
                    b = (vs[j] >= vs[i]) if j < i else (vs[j] > vs[i])
                    beaten = beaten + jnp.where(b, 1.0, 0.0)
            kp = beaten < 2.0
            keep[g * EXPERTS_PER_GROUP + i] = kp
            sg = sg + jnp.where(kp, vs[i], 0.0)
        score.append(sg)
    chosen = []
    for g in range(N_GROUPS):
        lost = jnp.zeros_like(score[0])
        for g2 in range(N_GROUPS):
            if g2 != g:
                b = (score[g2] >= score[g]) if g2 < g else (score[g2] > score[g])
                lost = lost + jnp.where(b, 1.0, 0.0)
        chosen.append(jnp.where(lost < 0.5, 1.0, 0.0))
    wsel = []
    for i in range(EXPERTS_PER_GROUP):
        wi = jnp.zeros_like(score[0])
        for g in range(N_GROUPS):
            e = g * EXPERTS_PER_GROUP + i
            wi = wi + chosen[g] * jnp.where(keep[e], affr[e], 0.0)
        wsel.append(wi)
    wsum = wsel[0] + wsel[1] + wsel[2] + wsel[3]
    wsel = [w / wsum for w in wsel]

    row8 = lax.broadcasted_iota(jnp.int32, (SUBLANES, te), 0)
    gmat = jnp.zeros((SUBLANES, te), F32)
    wmat = jnp.zeros((SUBLANES, te), F32)
    grp = jnp.zeros_like(score[0])
    for g in range(N_GROUPS):
        gmat = jnp.where(row8 == g, chosen[g], gmat)
        wmat = jnp.where(row8 == g, wsel[g], wmat)
        grp = grp + g * chosen[g]
    pref = jnp.dot(gmat.astype(BF16), tri_ref[...], preferred_element_type=F32) + cnt_sc[:, :1]
    rank = jnp.sum(gmat * pref, axis=0, keepdims=True)
    tot = cnt_sc[:, :1] + jnp.sum(gmat, axis=1, keepdims=True)
    cnt_sc[...] = jnp.broadcast_to(tot, cnt_sc.shape)
    rank_ref[0] = rank.astype(jnp.int32)
    grp_ref[0] = grp.astype(jnp.int32)
    tot_ref[0] = jnp.broadcast_to(tot, (SUBLANES, LANES)).astype(jnp.int32)
    wfull = jnp.concatenate([wmat, jnp.zeros((LANES - SUBLANES, te), F32)], axis=0)
    gsel_ref[...] = wfull.T


def _post_call(x, hf, hm, wo, g1, nf, sc, sh, wr, br, tri, *, bsz, seq, chunk, te=512):
    t, d = x.shape
    steps_per_chunk = chunk // te
    per_seq = seq // te
    n_steps = t // te
    n_chunks = t // chunk
    row = lambda i: (i, 0)
    per_b = lambda i: (i // per_seq, 0, 0)
    const2 = lambda i: (0, 0)
    return pl.pallas_call(
        functools.partial(_post_kernel, te=te, steps_per_chunk=steps_per_chunk),
        grid=(n_steps,),
        in_specs=[
            pl.BlockSpec((te, d), row),
            pl.BlockSpec((te, FOX_WIDTH), row),
            pl.BlockSpec((te, ML_WIDTH), row),
            pl.BlockSpec((d, d), const2),
            pl.BlockSpec((1, 1, d), per_b),
            pl.BlockSpec((1, d), const2),
            pl.BlockSpec((1, 1, d), per_b),
            pl.BlockSpec((1, 1, d), per_b),
            pl.BlockSpec((d, 2 * LANES), const2),
            pl.BlockSpec((N_EXPERTS, 1), const2),
            pl.BlockSpec((te, te), const2),
        ],
        out_specs=[
            pl.BlockSpec((te, d), row),
            pl.BlockSpec((te * TOK_ROWS, LANES), row),
            pl.BlockSpec((te, LANES), row),
            pl.BlockSpec((1, 1, te), lambda i: (i, 0, 0)),
            pl.BlockSpec((1, 1, te), lambda i: (i, 0, 0)),
            pl.BlockSpec((1, SUBLANES, LANES), lambda i: (i // steps_per_chunk, 0, 0)),
        ],
        out_shape=[
            jax.ShapeDtypeStruct((t, d), F32),
            jax.ShapeDtypeStruct((t * TOK_ROWS, LANES), F32),
            jax.ShapeDtypeStruct((t, LANES), F32),
            jax.ShapeDtypeStruct((n_steps, 1, te), jnp.int32),
            jax.ShapeDtypeStruct((n_steps, 1, te), jnp.int32),
            jax.ShapeDtypeStruct((n_chunks, SUBLANES, LANES), jnp.int32),
        ],
        scratch_shapes=[pltpu.VMEM((SUBLANES, LANES), F32)],
        compiler_params=_cparams(1, 48),
        name="post_router",
    )(x, hf, hm, wo, g1, nf, sc, sh, wr, br, tri)


def _scatter_kernel(grp_ref, rank_ref, off_ref, hn_ref, gsel_ref, xb_ref, gs_ref, xs_sc, *, chunk, rows, tm):
    xs_sc[...] = jnp.zeros_like(xs_sc)
    gs_ref[...] = jnp.zeros_like(gs_ref)

    def body(t, carry):
        p = off_ref[0, 0, grp_ref[0, 0, t]] + rank_ref[0, 0, t]
        src = pl.multiple_of(t * TOK_ROWS, TOK_ROWS)
        dst = pl.multiple_of(p * TOK_ROWS, TOK_ROWS)
        xs_sc[pl.ds(dst, TOK_ROWS), :] = hn_ref[pl.ds(src, TOK_ROWS), :]
        gs_ref[pl.ds(p, 1), :] = gsel_ref[pl.ds(t, 1), :]
        return carry

    lax.fori_loop(0, chunk, body, 0, unroll=8)
    for j in range(rows // tm):
        for c in range(TOK_ROWS):
            xb_ref[j * tm:(j + 1) * tm, c * LANES:(c + 1) * LANES] = (
                xs_sc[pl.ds(j * tm * TOK_ROWS + c, tm, stride=TOK_ROWS), :].astype(BF16))


def _scatter_call(grp, rank, off, hn_tt, gsel, *, chunk, rows, tm):
    t = gsel.shape[0]
    n_chunks = t // chunk
    smem3 = lambda shape: pl.BlockSpec(shape, lambda c: (c, 0, 0), memory_space=pltpu.SMEM)
    return pl.pallas_call(
        functools.partial(_scatter_kernel, chunk=chunk, rows=rows, tm=tm),
        grid=(n_chunks,),
        in_specs=[
            smem3((1, 1, chunk)),
            smem3((1, 1, chunk)),
            smem3((1, 1, N_GROUPS)),
            pl.BlockSpec((chunk * TOK_ROWS, LANES), lambda c: (c, 0)),
            pl.BlockSpec((chunk, LANES), lambda c: (c, 0)),
        ],
        out_specs=[
            pl.BlockSpec((rows, D_MODEL), lambda c: (c, 0)),
            pl.BlockSpec((rows, LANES), lambda c: (c, 0)),
        ],
        out_shape=[
            jax.ShapeDtypeStruct((n_chunks * rows, D_MODEL), BF16),
            jax.ShapeDtypeStruct((n_chunks * rows, LANES), F32),
        ],
        scratch_shapes=[pltpu.VMEM((rows * TOK_ROWS, LANES), F32)],
        compiler_params=_cparams(1, 56),
        name="moe_scatter",
    )(grp, rank, off, hn_tt, gsel)


def _experts_kernel(blk_ref, grp_ref, xb_ref, gs_ref, wg_ref, wu_ref, wd_ref, y_ref, *, tm):
    g = grp_ref[pl.program_id(0)]

    @pl.when(g < N_GROUPS)
    def _():
        x = xb_ref[...]
        gs = gs_ref[...]
        acc = jnp.zeros((tm, D_MODEL), F32)
        for i in range(EXPERTS_PER_GROUP):
            hg = jnp.dot(x, wg_ref[i], preferred_element_type=F32)
            hu = jnp.dot(x, wu_ref[i], preferred_element_type=F32)
            a = _silu(hg) * hu * gs[:, i:i + 1]
            acc = acc + jnp.dot(a.astype(BF16), wd_ref[i], preferred_element_type=F32)
        _to_token_tiles(y_ref, acc)

    @pl.when(g >= N_GROUPS)
    def _():
        y_ref[...] = jnp.zeros_like(y_ref)


def _experts_call(tile_blk, tile_grp, xb, gs, wg, wu, wd, *, tm):
    n_rows = xb.shape[0]
    n_slots = n_rows // tm
    wmap = lambda s, blk, grp: (jnp.minimum(grp[s], N_GROUPS - 1), 0, 0)
    grid_spec = pltpu.PrefetchScalarGridSpec(
        num_scalar_prefetch=2,
        grid=(n_slots,),
        in_specs=[
            pl.BlockSpec((tm, D_MODEL), lambda s, blk, grp: (blk[s], 0)),
            pl.BlockSpec((tm, LANES), lambda s, blk, grp: (blk[s], 0)),
            pl.BlockSpec((EXPERTS_PER_GROUP, D_MODEL, D_FF), wmap),
            pl.BlockSpec((EXPERTS_PER_GROUP, D_MODEL, D_FF), wmap),
            pl.BlockSpec((EXPERTS_PER_GROUP, D_FF, D_MODEL), wmap),
        ],
        out_specs=pl.BlockSpec((tm * TOK_ROWS, LANES), lambda s, blk, grp: (blk[s], 0)),
    )
    return pl.pallas_call(
        functools.partial(_experts_kernel, tm=tm),
        grid_spec=grid_spec,
        out_shape=jax.ShapeDtypeStruct((n_rows * TOK_ROWS, LANES), F32),
        compiler_params=_cparams(1, 48),
        name="moe_experts",
    )(tile_blk, tile_grp, xb, gs, wg, wu, wd)


def _gather_kernel(grp_ref, rank_ref, off_ref, ys_ref, o_ref, *, chunk):
    def body(t, carry):
        p = off_ref[0, 0, grp_ref[0, 0, t]] + rank_ref[0, 0, t]
        src = pl.multiple_of(p * TOK_ROWS, TOK_ROWS)
        dst = pl.multiple_of(t * TOK_ROWS, TOK_ROWS)
        o_ref[pl.ds(dst, TOK_ROWS), :] = ys_ref[pl.ds(src, TOK_ROWS), :]
        return carry

    lax.fori_loop(0, chunk, body, 0, unroll=8)


def _gather_call(grp, rank, off, ys_tt, *, chunk, rows, n_tok):
    n_chunks = n_tok // chunk
    smem3 = lambda shape: pl.BlockSpec(shape, lambda c: (c, 0, 0), memory_space=pltpu.SMEM)
    return pl.pallas_call(
        functools.partial(_gather_kernel, chunk=chunk),
        grid=(n_chunks,),
        in_specs=[
            smem3((1, 1, chunk)),
            smem3((1, 1, chunk)),
            smem3((1, 1, N_GROUPS)),
            pl.BlockSpec((rows * TOK_ROWS, LANES), lambda c: (c, 0)),
        ],
        out_specs=pl.BlockSpec((chunk * TOK_ROWS, LANES), lambda c: (c, 0)),
        out_shape=jax.ShapeDtypeStruct((n_tok * TOK_ROWS, LANES), F32),
        compiler_params=_cparams(1, 56),
        name="moe_gather",
    )(grp, rank, off, ys_tt)


def _final_kernel(x_ref, ott_ref, g2_ref, nf_ref, o_ref, *, tm):
    x = x_ref[...] + g2_ref[0] * _from_token_tiles(ott_ref, tm)
    ms = jnp.mean(x * x, axis=1, keepdims=True)
    o_ref[...] = x * lax.rsqrt(ms + EPS) * nf_ref[...]


def _final_call(x, moe_tt, g2, nf, *, seq, tm=512):
    t, d = x.shape
    per_seq = seq // tm
    return pl.pallas_call(
        functools.partial(_final_kernel, tm=tm),
        grid=(t // tm,),
        in_specs=[
            pl.BlockSpec((tm, d), lambda i: (i, 0)),
            pl.BlockSpec((tm * TOK_ROWS, LANES), lambda i: (i, 0)),
            pl.BlockSpec((1, 1, d), lambda i: (i // per_seq, 0, 0)),
            pl.BlockSpec((1, d), lambda i: (0, 0)),
        ],
        out_specs=pl.BlockSpec((tm, d), lambda i: (i, 0)),
        out_shape=jax.ShapeDtypeStruct((t, d), F32),
        compiler_params=_cparams(1, 32),
        name="final_norm",
    )(x, moe_tt, g2, nf)


def _moe_tiles(tot, *, tm, tiles_per_chunk):
    nt = (tot + tm - 1) // tm
    ts = jnp.cumsum(nt, axis=1) - nt
    off = (ts * tm).astype(jnp.int32)
    j = jnp.arange(tiles_per_chunk, dtype=jnp.int32)[None, :, None]
    inside = (j >= ts[:, None, :]) & (j < (ts + nt)[:, None, :])
    key = jnp.where(jnp.any(inside, axis=-1), jnp.argmax(inside, axis=-1), N_GROUPS).reshape(-1)
    order = jnp.argsort(key, stable=True).astype(jnp.int32)
    return off, order, key[order].astype(jnp.int32)


def kernel(x, c, w_in, conv_w, conv_b, fox_f_bias, mlstm_i_bias, mlstm_f_bias, fox_out_norm,
           mlstm_out_norm, w_out, w_ada, b_ada, norm_mix, norm_ffn, w_router, b_router, w_gate,
           w_up, w_down, norm_final):
    bsz, seq, d = x.shape
    depth = w_in.shape[0]
    t = bsz * seq
    assert d == D_MODEL and w_in.shape[-1] == IN_COLS
    chunk = min(2048, seq)
    tm = 256
    rows = chunk + N_GROUPS * tm
    te = min(512, seq)
    assert seq % 512 == 0 and seq % chunk == 0 and chunk % te == 0

    mods = _ada_call(c, w_ada, b_ada).reshape(depth, bsz, N_ADA, 1, d)
    xf = x.reshape(t, d)

    tri = (jnp.arange(te)[:, None] < jnp.arange(te)[None, :]).astype(BF16)
    wr_f = jnp.pad(w_router, ((0, 0), (0, LANES - N_EXPERTS))).astype(F32)
    wr_hi = wr_f.astype(BF16)
    wr = jnp.concatenate([wr_hi, (wr_f - wr_hi.astype(F32)).astype(BF16)], axis=1)
    br = b_router.reshape(N_EXPERTS, 1).astype(F32)

    moe_tt = None
    g2_prev = None
    for l in range(depth):
        sh1, sc1, g1, sh2, sc2, g2 = [mods[l, :, i] for i in range(N_ADA)]
        wl = w_in[l]
        wm = jnp.concatenate([wl[:, FOX_Q:FOX_F], wl[:, ML_Q:ML_I], wl[:, ML_O:IN_COLS]], axis=1).astype(BF16)
        zpad = lambda n: jnp.zeros((d, n), F32)
        wg = jnp.concatenate([
            wl[:, FOX_F:ML_Q], wl[:, ML_F:ML_O], zpad(LANES - FOX_HEADS - ML_HEADS),
            zpad(GATE_ML), wl[:, ML_I:ML_F], zpad(LANES - GATE_ML - ML_HEADS)], axis=1).astype(BF16)
        zb = lambda n: jnp.zeros((n,), F32)
        gb = jnp.concatenate([
            fox_f_bias[l], mlstm_f_bias[l], zb(LANES - FOX_HEADS - ML_HEADS),
            zb(GATE_ML), mlstm_i_bias[l], zb(LANES - GATE_ML - ML_HEADS)]).reshape(1, 2 * LANES)

        xf, qa, ka, qkv, mkt, fc, gr = _in_proj_call(
            xf, moe_tt, g2_prev, norm_mix[l].reshape(1, d), sc1, sh1, wm, wg, gb,
            conv_w[l], conv_b[l].reshape(1, -1), bsz=bsz, seq=seq)
        hf = _fox_call(qa, ka, qkv, fox_out_norm[l].reshape(1, FOX_WIDTH), bsz=bsz, seq=seq)
        hm = _mlstm_call(qkv, mkt, fc, gr, mlstm_out_norm[l].reshape(1, ML_WIDTH), bsz=bsz, seq=seq)
        xf, hn_tt, gsel, grp, rank, tot = _post_call(
            xf, hf, hm, w_out[l].astype(BF16), g1, norm_ffn[l].reshape(1, d), sc2, sh2, wr, br, tri,
            bsz=bsz, seq=seq, chunk=chunk, te=te)

        n_chunks = t // chunk
        grp = grp.reshape(n_chunks, 1, chunk)
        rank = rank.reshape(n_chunks, 1, chunk)
        off, tile_blk, tile_grp = _moe_tiles(tot[:, :N_GROUPS, 0], tm=tm, tiles_per_chunk=rows // tm)
        off = off.reshape(n_chunks, 1, N_GROUPS)
        xb, gs = _scatter_call(grp, rank, off, hn_tt, gsel, chunk=chunk, rows=rows, tm=tm)
        ys_tt = _experts_call(tile_blk, tile_grp, xb, gs, w_gate[l].astype(BF16),
                              w_up[l].astype(BF16), w_down[l].astype(BF16), tm=tm)
        moe_tt = _gather_call(grp, rank, off, ys_tt, chunk=chunk, rows=rows, n_tok=t)
        g2_prev = g2

    out = _final_call(xf, moe_tt, g2_prev, norm_final.reshape(1, d), seq=seq)
    return out.reshape(bsz, seq, d)
```

```python
import functools

import numpy as np
import jax
import jax.numpy as jnp
from jax import lax
from jax.experimental import pallas as pl
from jax.experimental.pallas import tpu as pltpu

F32 = jnp.float32
BF16 = jnp.bfloat16

LANES = 128
SUBLANES = 8
VMEM_BYTES_V7X = 64 * 1024 * 1024

D_MODEL = 1024
FOX_HEADS = 8
FOX_HEAD_DIM = 64
FOX_WIDTH = FOX_HEADS * FOX_HEAD_DIM
ML_HEADS = 4
ML_HEAD_DIM = 128
ML_WIDTH = ML_HEADS * ML_HEAD_DIM
CONV_WIDTH = 4
N_EXPERTS = 16
N_GROUPS = 4
EXPERTS_PER_GROUP = 4
D_FF = 512
N_ADA = 6
EPS = 1e-6
NEG = -1e30

FOX_Q = 0
FOX_F = 3 * FOX_WIDTH
ML_Q = FOX_F + FOX_HEADS
ML_I = ML_Q + 3 * ML_WIDTH
ML_F = ML_I + ML_HEADS
ML_O = ML_F + ML_HEADS
IN_COLS = ML_O + ML_WIDTH

MAIN_COLS = 7 * 512
BLK_FQ, BLK_FK, BLK_FV, BLK_MQ, BLK_MK, BLK_MV, BLK_MO = range(7)
REST_COLS = 4 * 512
OUT_FV, OUT_MQ, OUT_MV, OUT_MO = range(4)
AUG_COLS = FOX_HEADS * 128
BIAS_TERMS = 3
LOG2E = 1.4426950408889634
GATE_FOX = 0
GATE_ML = FOX_HEADS
GATE_ROWS = 16

TOK_ROWS = D_MODEL // LANES


def _cparams(n_grid, vmem_mb):
    return pltpu.CompilerParams(
        dimension_semantics=("arbitrary",) * n_grid,
        vmem_limit_bytes=vmem_mb * 1024 * 1024)


def _silu(x):
    return x * jax.nn.sigmoid(x)


def _log_sigmoid(z):
    return jnp.minimum(z, 0.0) - jnp.log1p(jnp.exp(-jnp.abs(z)))


def _cumsum_rows(x):
    n = x.shape[0]
    row = lax.broadcasted_iota(jnp.int32, x.shape, 0)
    s = 1
    while s < n:
        x = x + jnp.where(row >= s, pltpu.roll(x, s, axis=0), 0.0)
        s *= 2
    return x


def _from_token_tiles(ref, n_tok):
    return jnp.concatenate(
        [ref[pl.ds(c, n_tok, stride=TOK_ROWS), :] for c in range(TOK_ROWS)], axis=1)


def _to_token_tiles(ref, val, row0=0):
    n = val.shape[0]
    for c in range(TOK_ROWS):
        ref[pl.ds(row0 * TOK_ROWS + c, n, stride=TOK_ROWS), :] = val[:, c * LANES:(c + 1) * LANES]


def _ada_kernel(c_ref, w_ref, b_ref, o_ref):
    c = c_ref[...]
    o_ref[0] = jnp.dot(_silu(c), w_ref[0], preferred_element_type=F32,
                       precision=lax.Precision.HIGHEST) + b_ref[0]


def _ada_call(c, w_ada, b_ada):
    depth, d, n = w_ada.shape
    bsz = c.shape[0]
    tn = 1536
    return pl.pallas_call(
        _ada_kernel,
        grid=(depth, n // tn),
        in_specs=[
            pl.BlockSpec((bsz, d), lambda l, j: (0, 0)),
            pl.BlockSpec((1, d, tn), lambda l, j: (l, 0, j)),
            pl.BlockSpec((1, 1, tn), lambda l, j: (l, 0, j)),
        ],
        out_specs=pl.BlockSpec((1, bsz, tn), lambda l, j: (l, 0, j)),
        out_shape=jax.ShapeDtypeStruct((depth, bsz, n), F32),
        compiler_params=_cparams(2, 32),
        name="ada_mod",
    )(c, w_ada, b_ada.reshape(depth, 1, n))


def _in_proj_kernel(*refs, tm, combine):
    if combine:
        (x_ref, ott_ref, g2_ref, nm_ref, sc_ref, sh_ref, wm_ref, wg_ref, gb_ref, cw_ref, cb_ref,
         place_ref, ones_ref,
         xn_ref, qa_ref, ka_ref, qkv_ref, kt_ref, fc_ref, gr_ref, fcar, ccar) = refs
    else:
        (x_ref, nm_ref, sc_ref, sh_ref, wm_ref, wg_ref, gb_ref, cw_ref, cb_ref,
         place_ref, ones_ref,
         qa_ref, ka_ref, qkv_ref, kt_ref, fc_ref, gr_ref, fcar, ccar) = refs

    @pl.when(pl.program_id(1) == 0)
    def _():
        fcar[...] = jnp.zeros_like(fcar)
        ccar[...] = jnp.zeros_like(ccar)

    x = x_ref[...]
    if combine:
        x = x + g2_ref[0] * _from_token_tiles(ott_ref, tm)
        xn_ref[...] = x
    ms = jnp.mean(x * x, axis=1, keepdims=True)
    hn = x * lax.rsqrt(ms + EPS) * nm_ref[...]
    hn = hn * (1.0 + sc_ref[0]) + sh_ref[0]
    hb = hn.astype(BF16)

    gp = jnp.dot(hb, wg_ref[...], preferred_element_type=F32) + gb_ref[...]
    lf = _log_sigmoid(gp[:, :LANES])
    fcum = _cumsum_rows(lf) + fcar[...]
    fcar[...] = fcum[tm - 1:tm, :]
    gml = gp[:, LANES:] - fcum
    fc_ref[...] = fcum
    gr_ref[0] = gml.T[:GATE_ROWS, :]

    def mm(j):
        return jnp.dot(hb, wm_ref[:, j * 512:(j + 1) * 512], preferred_element_type=F32)

    def put(j, v):
        qkv_ref[:, j * 512:(j + 1) * 512] = v.astype(BF16)

    lane = lax.broadcasted_iota(jnp.int32, (tm, LANES), 1)
    fs = fcum * LOG2E
    hi = fs.astype(BF16).astype(F32)
    mid = (fs - hi).astype(BF16).astype(F32)
    low = (fs - hi - mid).astype(BF16).astype(F32)
    packed = jnp.where(lane < FOX_HEADS, hi,
                       jnp.where(lane < 2 * FOX_HEADS, pltpu.roll(mid, FOX_HEADS, axis=1),
                                 jnp.where(lane < 3 * FOX_HEADS, pltpu.roll(low, 2 * FOX_HEADS, axis=1), 0.0)))
    bias = jnp.dot(packed.astype(BF16), place_ref[...], preferred_element_type=F32) + ones_ref[...]

    def put_heads(ref, val, col0):
        lo = lane < FOX_HEAD_DIM
        for p in range(FOX_HEADS // 2):
            slab = val[:, p * LANES:(p + 1) * LANES]
            for h, data in ((2 * p, slab), (2 * p + 1, pltpu.roll(slab, FOX_HEAD_DIM, axis=1))):
                blk = jnp.where(lo, data, 0.0) + bias[:, col0 + h * LANES:col0 + (h + 1) * LANES]
                ref[:, h * LANES:(h + 1) * LANES] = blk.astype(BF16)

    u = jnp.concatenate([mm(BLK_MQ), mm(BLK_MK)], axis=1)
    prev = ccar[...]
    ccar[...] = u[tm - SUBLANES:tm, :]
    row8 = lax.broadcasted_iota(jnp.int32, prev.shape, 0)
    y = cb_ref[...] + cw_ref[CONV_WIDTH - 1:CONV_WIDTH, :] * u
    for k in range(1, CONV_WIDTH):
        r = pltpu.roll(u, k, axis=0)
        top = jnp.where(row8 < k, pltpu.roll(prev, k, axis=0), r[:SUBLANES])
        shifted = jnp.concatenate([top, r[SUBLANES:]], axis=0)
        y = y + cw_ref[CONV_WIDTH - 1 - k:CONV_WIDTH - k, :] * shifted
    act = _silu(y)
    put(OUT_MQ, act[:, :ML_WIDTH])
    kt_ref[0] = (act[:, ML_WIDTH:] * (ML_HEAD_DIM ** -0.5)).T.astype(BF16)

    put(OUT_MO, jax.nn.sigmoid(mm(BLK_MO)))
    put(OUT_FV, mm(BLK_FV))
    put(OUT_MV, mm(BLK_MV))
    put_heads(qa_ref, mm(BLK_FQ) * (FOX_HEAD_DIM ** -0.5 * LOG2E), 0)
    put_heads(ka_ref, mm(BLK_FK), AUG_COLS)


def _bias_placement():
    place = np.zeros((LANES, 2 * AUG_COLS), np.float32)
    ones = np.zeros((1, 2 * AUG_COLS), np.float32)
    for h in range(FOX_HEADS):
        for term in range(BIAS_TERMS):
            src = term * FOX_HEADS + h
            place[src, h * LANES + FOX_HEAD_DIM + term] = 1.0
            ones[0, h * LANES + FOX_HEAD_DIM + BIAS_TERMS + term] = 1.0
            place[src, AUG_COLS + h * LANES + FOX_HEAD_DIM + BIAS_TERMS + term] = -1.0
            ones[0, AUG_COLS + h * LANES + FOX_HEAD_DIM + term] = 1.0
    return jnp.asarray(place, BF16), jnp.asarray(ones, F32)


def _in_proj_call(x, moe_tt, g2, nm, sc, sh, wm, wg, gb, cw, cb, *, bsz, seq, tm=512):
    t, d = x.shape
    ns = seq // tm
    combine = moe_tt is not None
    row = lambda b, s: (b * ns + s, 0)
    per_b = lambda b, s: (b, 0, 0)
    const2 = lambda b, s: (0, 0)
    in_specs = [pl.BlockSpec((tm, d), row)]
    args = [x]
    if combine:
        in_specs += [pl.BlockSpec((tm * TOK_ROWS, LANES), row), pl.BlockSpec((1, 1, d), per_b)]
        args += [moe_tt, g2]
    in_specs += [
        pl.BlockSpec((1, d), const2),
        pl.BlockSpec((1, 1, d), per_b),
        pl.BlockSpec((1, 1, d), per_b),
        pl.BlockSpec((d, MAIN_COLS), const2),
        pl.BlockSpec((d, 2 * LANES), const2),
        pl.BlockSpec((1, 2 * LANES), const2),
        pl.BlockSpec((CONV_WIDTH, 2 * ML_WIDTH), const2),
        pl.BlockSpec((1, 2 * ML_WIDTH), const2),
        pl.BlockSpec((LANES, 2 * AUG_COLS), const2),
        pl.BlockSpec((1, 2 * AUG_COLS), const2),
    ]
    place, ones = _bias_placement()
    args += [nm, sc, sh, wm, wg, gb, cw, cb, place, ones]
    out_specs = []
    out_shape = []
    if combine:
        out_specs.append(pl.BlockSpec((tm, d), row))
        out_shape.append(jax.ShapeDtypeStruct((t, d), F32))
    out_specs += [
        pl.BlockSpec((tm, AUG_COLS), row),
        pl.BlockSpec((tm, AUG_COLS), row),
        pl.BlockSpec((tm, REST_COLS), row),
        pl.BlockSpec((1, ML_WIDTH, tm), lambda b, s: (b, 0, s)),
        pl.BlockSpec((tm, LANES), row),
        pl.BlockSpec((1, GATE_ROWS, tm), lambda b, s: (b, 0, s)),
    ]
    out_shape += [
        jax.ShapeDtypeStruct((t, AUG_COLS), BF16),
        jax.ShapeDtypeStruct((t, AUG_COLS), BF16),
        jax.ShapeDtypeStruct((t, REST_COLS), BF16),
        jax.ShapeDtypeStruct((bsz, ML_WIDTH, seq), BF16),
        jax.ShapeDtypeStruct((t, LANES), F32),
        jax.ShapeDtypeStruct((bsz, GATE_ROWS, seq), F32),
    ]
    outs = pl.pallas_call(
        functools.partial(_in_proj_kernel, tm=tm, combine=combine),
        grid=(bsz, ns),
        in_specs=in_specs,
        out_specs=out_specs,
        out_shape=out_shape,
        scratch_shapes=[pltpu.VMEM((1, LANES), F32), pltpu.VMEM((SUBLANES, 2 * ML_WIDTH), F32)],
        compiler_params=_cparams(2, 48),
        name="in_proj",
    )(*args)
    if combine:
        return outs
    return [x] + list(outs)


def _fox_kernel(q_ref, k_ref, v_ref, ng_ref, cm_ref, o_ref, m_sc, acc_sc, *, tq, tk, nh):
    qi = pl.program_id(2)
    lane = lax.broadcasted_iota(jnp.int32, (tq, LANES), 1)
    lo = lane < FOX_HEAD_DIM
    ones_col = jnp.where(lax.broadcasted_iota(jnp.int32, (tk, LANES), 1) == 0, 1.0, 0.0).astype(BF16)
    for h in range(nh):
        m_sc[h] = jnp.full((tq, LANES), NEG, F32)
        acc_sc[h] = jnp.zeros((tq, 2 * LANES), F32)

    def body(j, carry):
        k0 = pl.multiple_of(j * tk, tk)
        slot = (j == qi).astype(jnp.int32)
        for h in range(nh):
            hs = slice(h * LANES, (h + 1) * LANES)
            vs = slice((h // 2) * LANES, (h // 2 + 1) * LANES)
            va = jnp.concatenate([v_ref[pl.ds(k0, tk), vs], ones_col], axis=1)
            z = lax.dot_general(q_ref[:, hs], k_ref[pl.ds(k0, tk), hs], (((1,), (1,)), ((), ())),
                                preferred_element_type=F32)
            z = z + cm_ref[slot]
            m_prev = m_sc[h]
            m_new = jnp.maximum(m_prev, jnp.max(z, axis=1, keepdims=True))
            alpha = jnp.exp2(m_prev - m_new)
            p = jnp.exp2(z - jnp.concatenate([m_new] * (tk // LANES), axis=1))
            acc_sc[h] = (jnp.concatenate([alpha, alpha], axis=1) * acc_sc[h]
                         + jnp.dot(p.astype(BF16), va, preferred_element_type=F32))
            m_sc[h] = m_new
        return carry

    lax.fori_loop(0, qi + 1, body, 0)

    for p in range(nh // 2):
        outs = []
        for a, h in enumerate((2 * p, 2 * p + 1)):
            acc = acc_sc[h]
            num = acc[:, :LANES]
            mine = lo if a == 0 else jnp.logical_not(lo)
            ms = jnp.sum(jnp.where(mine, num * num, 0.0), axis=1, keepdims=True) * (1.0 / FOX_HEAD_DIM)
            lblk = acc[:, LANES:]
            outs.append(num * lax.rsqrt(ms + EPS * lblk * lblk)[:, :1])
        ps = slice(p * LANES, (p + 1) * LANES)
        o_ref[:, ps] = (jnp.where(lo, outs[0], outs[1]) * ng_ref[:, ps]).astype(BF16)


def _fox_call(qa, ka, rest, ng, *, bsz, seq, tq=512, nh=8):
    t = qa.shape[0]
    nq = seq // tq
    ngrp = FOX_HEADS // nh
    vw = nh // 2 * LANES
    causal = jnp.where(jnp.arange(tq)[None, :] <= jnp.arange(tq)[:, None], 0.0, NEG).astype(F32)
    cmask = jnp.stack([jnp.zeros_like(causal), causal])
    return pl.pallas_call(
        functools.partial(_fox_kernel, tq=tq, tk=tq, nh=nh),
        grid=(bsz, ngrp, nq),
        in_specs=[
            pl.BlockSpec((tq, nh * LANES), lambda b, p, i: (b * nq + i, p)),
            pl.BlockSpec((seq, nh * LANES), lambda b, p, i: (b, p)),
            pl.BlockSpec((seq, vw), lambda b, p, i: (b, OUT_FV * ngrp + p)),
            pl.BlockSpec((1, vw), lambda b, p, i: (0, p)),
            pl.BlockSpec((2, tq, tq), lambda b, p, i: (0, 0, 0)),
        ],
        out_specs=pl.BlockSpec((tq, vw), lambda b, p, i: (b * nq + i, p)),
        out_shape=jax.ShapeDtypeStruct((t, FOX_WIDTH), BF16),
        scratch_shapes=[pltpu.VMEM((nh, tq, LANES), F32), pltpu.VMEM((nh, tq, 2 * LANES), F32)],
        compiler_params=_cparams(3, 48),
        name="fox_attn",
    )(qa, ka, rest, ng, cmask)


def _mlstm_kernel(q_ref, kt_ref, v_ref, og_ref, fc_ref, gr_ref, ng_ref, o_ref, ct_sc, u_sc, *, ch):
    @pl.when(pl.program_id(1) == 0)
    def _():
        ct_sc[...] = jnp.zeros_like(ct_sc)
        u_sc[...] = jnp.zeros_like(u_sc)

    causal = (lax.broadcasted_iota(jnp.int32, (ch, ch), 1)
              <= lax.broadcasted_iota(jnp.int32, (ch, ch), 0))
    lane = lax.broadcasted_iota(jnp.int32, (ch, LANES), 1)
    fc = fc_ref[...]
    for h in range(ML_HEADS):
        sl = slice(h * ML_HEAD_DIM, (h + 1) * ML_HEAD_DIM)
        gl = GATE_ML + h
        q = q_ref[:, sl]
        kt = kt_ref[0, sl, :]
        vp = jnp.concatenate([v_ref[:, sl], jnp.where(lane == gl, 1.0, 0.0).astype(BF16)], axis=1)
        g_row = gr_ref[0, gl:gl + 1, :]
        u_prev = u_sc[h][:, :1]
        gm = jnp.where(causal, g_row, NEG)
        u_i = jnp.maximum(u_prev, jnp.max(gm, axis=1, keepdims=True))
        dmat = jnp.exp(gm - u_i)
        s = jnp.dot(q, kt, preferred_element_type=F32)
        scores = (s * dmat).astype(BF16)
        inter = jnp.exp(u_prev - u_i)
        ct = ct_sc[h]
        nd = (jnp.dot(scores, vp, preferred_element_type=F32)
              + jnp.dot(q, ct.astype(BF16), preferred_element_type=F32) * inter)
        num = nd[:, :ML_HEAD_DIM]
        den = jnp.maximum(jnp.abs(nd[:, ML_HEAD_DIM:]), jnp.exp(-(jnp.where(lane == gl, fc, 0.0) + u_i)))
        ms = jnp.mean(num * num, axis=1, keepdims=True)
        scale = lax.rsqrt(ms + EPS * den * den)[:, gl:gl + 1]
        y = num * scale * ng_ref[:, sl] * og_ref[:, sl].astype(F32)
        o_ref[:, sl] = y.astype(BF16)
        u_new = jnp.maximum(u_prev, jnp.max(g_row, axis=1, keepdims=True))
        ktw = (kt.astype(F32) * jnp.exp(g_row - u_new)).astype(BF16)
        ct_sc[h] = jnp.exp(u_prev - u_new) * ct + jnp.dot(ktw, vp, preferred_element_type=F32)
        u_sc[h] = jnp.broadcast_to(u_new, (1, LANES))


def _mlstm_call(qkv, mkt, fc, gr, ng, *, bsz, seq, ch=256):
    t = qkv.shape[0]
    nc = seq // ch
    row = lambda b, c: (b * nc + c, 0)
    return pl.pallas_call(
        functools.partial(_mlstm_kernel, ch=ch),
        grid=(bsz, nc),
        in_specs=[
            pl.BlockSpec((ch, ML_WIDTH), lambda b, c: (b * nc + c, OUT_MQ)),
            pl.BlockSpec((1, ML_WIDTH, ch), lambda b, c: (b, 0, c)),
            pl.BlockSpec((ch, ML_WIDTH), lambda b, c: (b * nc + c, OUT_MV)),
            pl.BlockSpec((ch, ML_WIDTH), lambda b, c: (b * nc + c, OUT_MO)),
            pl.BlockSpec((ch, LANES), row),
            pl.BlockSpec((1, GATE_ROWS, ch), lambda b, c: (b, 0, c)),
            pl.BlockSpec((1, ML_WIDTH), lambda b, c: (0, 0)),
        ],
        out_specs=pl.BlockSpec((ch, ML_WIDTH), row),
        out_shape=jax.ShapeDtypeStruct((t, ML_WIDTH), BF16),
        scratch_shapes=[pltpu.VMEM((ML_HEADS, ML_HEAD_DIM, 2 * ML_HEAD_DIM), F32),
                        pltpu.VMEM((ML_HEADS, 1, LANES), F32)],
        compiler_params=_cparams(2, 32),
        name="mlstm",
    )(qkv, mkt, qkv, qkv, fc, gr, ng)


def _post_kernel(x_ref, hf_ref, hm_ref, wo_ref, g1_ref, nf_ref, sc_ref, sh_ref, wr_ref, br_ref, tri_ref,
                 x1_ref, hn_ref, gsel_ref, grp_ref, rank_ref, tot_ref, cnt_sc, *, te, steps_per_chunk):
    @pl.when(pl.program_id(0) % steps_per_chunk == 0)
    def _():
        cnt_sc[...] = jnp.zeros_like(cnt_sc)

    mix = (jnp.dot(hf_ref[...], wo_ref[:FOX_WIDTH, :], preferred_element_type=F32)
           + jnp.dot(hm_ref[...], wo_ref[FOX_WIDTH:, :], preferred_element_type=F32))
    x1 = x_ref[...] + g1_ref[0] * mix
    x1_ref[...] = x1
    ms = jnp.mean(x1 * x1, axis=1, keepdims=True)
    hn = x1 * lax.rsqrt(ms + EPS) * nf_ref[...]
    hn = hn * (1.0 + sc_ref[0]) + sh_ref[0]
    _to_token_tiles(hn_ref, hn)

    hn_hi = hn.astype(BF16)
    hn_lo = (hn - hn_hi.astype(F32)).astype(BF16)
    l2 = jnp.dot(hn_hi, wr_ref[...], preferred_element_type=F32)
    logits = (l2[:, :LANES] + l2[:, LANES:]
              + jnp.dot(hn_lo, wr_ref[:, :LANES], preferred_element_type=F32))
    aff = jax.nn.sigmoid(logits.T[:N_EXPERTS, :])
    sel = aff + br_ref[...]
    selr = [sel[e:e + 1, :] for e in range(N_EXPERTS)]
    affr = [aff[e:e + 1, :] for e in range(N_EXPERTS)]
    keep = [None] * N_EXPERTS
    score = []
    for g in range(N_GROUPS):
        vs = selr[g * EXPERTS_PER_GROUP:(g + 1) * EXPERTS_PER_GROUP]
        sg = jnp.zeros_like(vs[0])
        for i in range(EXPERTS_PER_GROUP):
            beaten = jnp.zeros_like(vs[0])
            for j in range(EXPERTS_PER_GROUP):
                if j != i:
                    b = (vs[j] >= vs[i]) if j < i else (vs[j] > vs[i])
                    beaten = beaten + jnp.where(b, 1.0, 0.0)
            kp = beaten < 2.0
            keep[g * EXPERTS_PER_GROUP + i] = kp
            sg = sg + jnp.where(kp, vs[i], 0.0)
        score.append(sg)
    chosen = []
    for g in range(N_GROUPS):
        lost = jnp.zeros_like(score[0])
        for g2 in range(N_GROUPS):
            if g2 != g:
                b = (score[g2] >= score[g]) if g2 < g else (score[g2] > score[g])
                lost = lost + jnp.where(b, 1.0, 0.0)
        chosen.append(jnp.where(lost < 0.5, 1.0, 0.0))
    wsel = []
    for i in range(EXPERTS_PER_GROUP):
        wi = jnp.zeros_like(score[0])
        for g in range(N_GROUPS):
            e = g * EXPERTS_PER_GROUP + i
            wi = wi + chosen[g] * jnp.where(keep[e], affr[e], 0.0)
        wsel.append(wi)
    wsum = wsel[0] + wsel[1] + wsel[2] + wsel[3]
    wsel = [w / wsum for w in wsel]

    row8 = lax.broadcasted_iota(jnp.int32, (SUBLANES, te), 0)
    gmat = jnp.zeros((SUBLANES, te), F32)
    wmat = jnp.zeros((SUBLANES, te), F32)
    grp = jnp.zeros_like(score[0])
    for g in range(N_GROUPS):
        gmat = jnp.where(row8 == g, chosen[g], gmat)
        wmat = jnp.where(row8 == g, wsel[g], wmat)
        grp = grp + g * chosen[g]
    pref = jnp.dot(gmat.astype(BF16), tri_ref[...], preferred_element_type=F32) + cnt_sc[:, :1]
    rank = jnp.sum(gmat * pref, axis=0, keepdims=True)
    tot = cnt_sc[:, :1] + jnp.sum(gmat, axis=1, keepdims=True)
    cnt_sc[...] = jnp.broadcast_to(tot, cnt_sc.shape)
    rank_ref[0] = rank.astype(jnp.int32)
    grp_ref[0] = grp.astype(jnp.int32)
    tot_ref[0] = jnp.broadcast_to(tot, (SUBLANES, LANES)).astype(jnp.int32)
    wfull = jnp.concatenate([wmat, jnp.zeros((LANES - SUBLANES, te), F32)], axis=0)
    gsel_ref[...] = wfull.T


def _post_call(x, hf, hm, wo, g1, nf, sc, sh, wr, br, tri, *, bsz, seq, chunk, te=512):
    t, d = x.shape
    steps_per_chunk = chunk // te
    per_seq = seq // te
    n_steps = t // te
    n_chunks = t // chunk
    row = lambda i: (i, 0)
    per_b = lambda i: (i // per_seq, 0, 0)
    const2 = lambda i: (0, 0)
    return pl.pallas_call(
        functools.partial(_post_kernel, te=te, steps_per_chunk=steps_per_chunk),
        grid=(n_steps,),
        in_specs=[
            pl.BlockSpec((te, d), row),
            pl.BlockSpec((te, FOX_WIDTH), row),
            pl.BlockSpec((te, ML_WIDTH), row),
            pl.BlockSpec((d, d), const2),
            pl.BlockSpec((1, 1, d), per_b),
            pl.BlockSpec((1, d), const2),
            pl.BlockSpec((1, 1, d), per_b),
            pl.BlockSpec((1, 1, d), per_b),
            pl.BlockSpec((d, 2 * LANES), const2),
            pl.BlockSpec((N_EXPERTS, 1), const2),
            pl.BlockSpec((te, te), const2),
        ],
        out_specs=[
            pl.BlockSpec((te, d), row),
            pl.BlockSpec((te * TOK_ROWS, LANES), row),
            pl.BlockSpec((te, LANES), row),
            pl.BlockSpec((1, 1, te), lambda i: (i, 0, 0)),
            pl.BlockSpec((1, 1, te), lambda i: (i, 0, 0)),
            pl.BlockSpec((1, SUBLANES, LANES), lambda i: (i // steps_per_chunk, 0, 0)),
        ],
        out_shape=[
            jax.ShapeDtypeStruct((t, d), F32),
            jax.ShapeDtypeStruct((t * TOK_ROWS, LANES), F32),
            jax.ShapeDtypeStruct((t, LANES), F32),
            jax.ShapeDtypeStruct((n_steps, 1, te), jnp.int32),
            jax.ShapeDtypeStruct((n_steps, 1, te), jnp.int32),
            jax.ShapeDtypeStruct((n_chunks, SUBLANES, LANES), jnp.int32),
        ],
        scratch_shapes=[pltpu.VMEM((SUBLANES, LANES), F32)],
        compiler_params=_cparams(1, 48),
        name="post_router",
    )(x, hf, hm, wo, g1, nf, sc, sh, wr, br, tri)


def _scatter_kernel(pos_ref, hn_ref, gsel_ref, xb_ref, gs_ref, xs_sc, *, chunk, rows, tm):
    xs_sc[...] = jnp.zeros_like(xs_sc)
    gs_ref[...] = jnp.zeros_like(gs_ref)

    def body(t, carry):
        p = pos_ref[0, 0, t]
        src = pl.multiple_of(t * TOK_ROWS, TOK_ROWS)
        dst = pl.multiple_of(p * TOK_ROWS, TOK_ROWS)
        xs_sc[pl.ds(dst, TOK_ROWS), :] = hn_ref[pl.ds(src, TOK_ROWS), :]
        gs_ref[pl.ds(p, 1), :] = gsel_ref[pl.ds(t, 1), :]
        return carry

    lax.fori_loop(0, chunk, body, 0, unroll=8)
    for j in range(rows // tm):
        for c in range(TOK_ROWS):
            xb_ref[j * tm:(j + 1) * tm, c * LANES:(c + 1) * LANES] = (
                xs_sc[pl.ds(j * tm * TOK_ROWS + c, tm, stride=TOK_ROWS), :].astype(BF16))


def _scatter_call(pos, hn_tt, gsel, *, chunk, rows, tm):
    t = gsel.shape[0]
    n_chunks = t // chunk
    return pl.pallas_call(
        functools.partial(_scatter_kernel, chunk=chunk, rows=rows, tm=tm),
        grid=(n_chunks,),
        in_specs=[
            pl.BlockSpec((1, 1, chunk), lambda c: (c, 0, 0), memory_space=pltpu.SMEM),
            pl.BlockSpec((chunk * TOK_ROWS, LANES), lambda c: (c, 0)),
            pl.BlockSpec((chunk, LANES), lambda c: (c, 0)),
        ],
        out_specs=[
            pl.BlockSpec((rows, D_MODEL), lambda c: (c, 0)),
            pl.BlockSpec((rows, LANES), lambda c: (c, 0)),
        ],
        out_shape=[
            jax.ShapeDtypeStruct((n_chunks * rows, D_MODEL), BF16),
            jax.ShapeDtypeStruct((n_chunks * rows, LANES), F32),
        ],
        scratch_shapes=[pltpu.VMEM((rows * TOK_ROWS, LANES), F32)],
        compiler_params=_cparams(1, 56),
        name="moe_scatter",
    )(pos, hn_tt, gsel)


def _experts_kernel(blk_ref, grp_ref, xb_ref, gs_ref, wg_ref, wu_ref, wd_ref, y_ref, *, tm):
    g = grp_ref[pl.program_id(0)]

    @pl.when(g < N_GROUPS)
    def _():
        x = xb_ref[...]
        gs = gs_ref[...]
        acc = jnp.zeros((tm, D_MODEL), F32)
        for i in range(EXPERTS_PER_GROUP):
            hg = jnp.dot(x, wg_ref[i], preferred_element_type=F32)
            hu = jnp.dot(x, wu_ref[i], preferred_element_type=F32)
            a = _silu(hg) * hu * gs[:, i:i + 1]
            acc = acc + jnp.dot(a.astype(BF16), wd_ref[i], preferred_element_type=F32)
        _to_token_tiles(y_ref, acc)

    @pl.when(g >= N_GROUPS)
    def _():
        y_ref[...] = jnp.zeros_like(y_ref)


def _experts_call(tile_blk, tile_grp, xb, gs, wg, wu, wd, *, tm):
    n_rows = xb.shape[0]
    n_slots = n_rows // tm
    wmap = lambda s, blk, grp: (jnp.minimum(grp[s], N_GROUPS - 1), 0, 0)
    grid_spec = pltpu.PrefetchScalarGridSpec(
        num_scalar_prefetch=2,
        grid=(n_slots,),
        in_specs=[
            pl.BlockSpec((tm, D_MODEL), lambda s, blk, grp: (blk[s], 0)),
            pl.BlockSpec((tm, LANES), lambda s, blk, grp: (blk[s], 0)),
            pl.BlockSpec((EXPERTS_PER_GROUP, D_MODEL, D_FF), wmap),
            pl.BlockSpec((EXPERTS_PER_GROUP, D_MODEL, D_FF), wmap),
            pl.BlockSpec((EXPERTS_PER_GROUP, D_FF, D_MODEL), wmap),
        ],
        out_specs=pl.BlockSpec((tm * TOK_ROWS, LANES), lambda s, blk, grp: (blk[s], 0)),
    )
    return pl.pallas_call(
        functools.partial(_experts_kernel, tm=tm),
        grid_spec=grid_spec,
        out_shape=jax.ShapeDtypeStruct((n_rows * TOK_ROWS, LANES), F32),
        compiler_params=_cparams(1, 48),
        name="moe_experts",
    )(tile_blk, tile_grp, xb, gs, wg, wu, wd)


def _gather_kernel(pos_ref, ys_ref, o_ref, *, chunk):
    def body(t, carry):
        p = pos_ref[0, 0, t]
        src = pl.multiple_of(p * TOK_ROWS, TOK_ROWS)
        dst = pl.multiple_of(t * TOK_ROWS, TOK_ROWS)
        o_ref[pl.ds(dst, TOK_ROWS), :] = ys_ref[pl.ds(src, TOK_ROWS), :]
        return carry

    lax.fori_loop(0, chunk, body, 0, unroll=8)


def _gather_call(pos, ys_tt, *, chunk, rows, n_tok):
    n_chunks = n_tok // chunk
    return pl.pallas_call(
        functools.partial(_gather_kernel, chunk=chunk),
        grid=(n_chunks,),
        in_specs=[
            pl.BlockSpec((1, 1, chunk), lambda c: (c, 0, 0), memory_space=pltpu.SMEM),
            pl.BlockSpec((rows * TOK_ROWS, LANES), lambda c: (c, 0)),
        ],
        out_specs=pl.BlockSpec((chunk * TOK_ROWS, LANES), lambda c: (c, 0)),
        out_shape=jax.ShapeDtypeStruct((n_tok * TOK_ROWS, LANES), F32),
        compiler_params=_cparams(1, 56),
        name="moe_gather",
    )(pos, ys_tt)


def _final_kernel(x_ref, ott_ref, g2_ref, nf_ref, o_ref, *, tm):
    x = x_ref[...] + g2_ref[0] * _from_token_tiles(ott_ref, tm)
    ms = jnp.mean(x * x, axis=1, keepdims=True)
    o_ref[...] = x * lax.rsqrt(ms + EPS) * nf_ref[...]


def _final_call(x, moe_tt, g2, nf, *, seq, tm=512):
    t, d = x.shape
    per_seq = seq // tm
    return pl.pallas_call(
        functools.partial(_final_kernel, tm=tm),
        grid=(t // tm,),
        in_specs=[
            pl.BlockSpec((tm, d), lambda i: (i, 0)),
            pl.BlockSpec((tm * TOK_ROWS, LANES), lambda i: (i, 0)),
            pl.BlockSpec((1, 1, d), lambda i: (i // per_seq, 0, 0)),
            pl.BlockSpec((1, d), lambda i: (0, 0)),
        ],
        out_specs=pl.BlockSpec((tm, d), lambda i: (i, 0)),
        out_shape=jax.ShapeDtypeStruct((t, d), F32),
        compiler_params=_cparams(1, 32),
        name="final_norm",
    )(x, moe_tt, g2, nf)


def _moe_tiles(tot, *, tm, tiles_per_chunk):
    nt = (tot + tm - 1) // tm
    ts = jnp.cumsum(nt, axis=1) - nt
    off = (ts * tm).astype(jnp.int32)
    j = jnp.arange(tiles_per_chunk, dtype=jnp.int32)[None, :, None]
    inside = (j >= ts[:, None, :]) & (j < (ts + nt)[:, None, :])
    key = jnp.where(jnp.any(inside, axis=-1), jnp.argmax(inside, axis=-1), N_GROUPS).reshape(-1)
    order = jnp.argsort(key, stable=True).astype(jnp.int32)
    return off, order, key[order].astype(jnp.int32)


def kernel(x, c, w_in, conv_w, conv_b, fox_f_bias, mlstm_i_bias, mlstm_f_bias, fox_out_norm,
           mlstm_out_norm, w_out, w_ada, b_ada, norm_mix, norm_ffn, w_router, b_router, w_gate,
           w_up, w_down, norm_final):
    bsz, seq, d = x.shape
    depth = w_in.shape[0]
    t = bsz * seq
    assert d == D_MODEL and w_in.shape[-1] == IN_COLS
    chunk = min(2048, seq)
    tm = 272 if chunk == 2048 else 256
    rows = -(-(chunk + N_GROUPS * (tm - 1)) // tm) * tm
    te = min(512, seq)
    assert seq % 512 == 0 and seq % chunk == 0 and chunk % te == 0

    mods = _ada_call(c, w_ada, b_ada).reshape(depth, bsz, N_ADA, 1, d)
    xf = x.reshape(t, d)

    tri = (jnp.arange(te)[:, None] < jnp.arange(te)[None, :]).astype(BF16)
    wr_f = jnp.pad(w_router, ((0, 0), (0, LANES - N_EXPERTS))).astype(F32)
    wr_hi = wr_f.astype(BF16)
    wr = jnp.concatenate([wr_hi, (wr_f - wr_hi.astype(F32)).astype(BF16)], axis=1)
    br = b_router.reshape(N_EXPERTS, 1).astype(F32)

    moe_tt = None
    g2_prev = None
    for l in range(depth):
        sh1, sc1, g1, sh2, sc2, g2 = [mods[l, :, i] for i in range(N_ADA)]
        wl = w_in[l]
        wm = jnp.concatenate([wl[:, FOX_Q:FOX_F], wl[:, ML_Q:ML_I], wl[:, ML_O:IN_COLS]], axis=1).astype(BF16)
        zpad = lambda n: jnp.zeros((d, n), F32)
        wg = jnp.concatenate([
            wl[:, FOX_F:ML_Q], wl[:, ML_F:ML_O], zpad(LANES - FOX_HEADS - ML_HEADS),
            zpad(GATE_ML), wl[:, ML_I:ML_F], zpad(LANES - GATE_ML - ML_HEADS)], axis=1).astype(BF16)
        zb = lambda n: jnp.zeros((n,), F32)
        gb = jnp.concatenate([
            fox_f_bias[l], mlstm_f_bias[l], zb(LANES - FOX_HEADS - ML_HEADS),
            zb(GATE_ML), mlstm_i_bias[l], zb(LANES - GATE_ML - ML_HEADS)]).reshape(1, 2 * LANES)

        xf, qa, ka, qkv, mkt, fc, gr = _in_proj_call(
            xf, moe_tt, g2_prev, norm_mix[l].reshape(1, d), sc1, sh1, wm, wg, gb,
            conv_w[l], conv_b[l].reshape(1, -1), bsz=bsz, seq=seq)
        hf = _fox_call(qa, ka, qkv, fox_out_norm[l].reshape(1, FOX_WIDTH), bsz=bsz, seq=seq)
        hm = _mlstm_call(qkv, mkt, fc, gr, mlstm_out_norm[l].reshape(1, ML_WIDTH), bsz=bsz, seq=seq)
        xf, hn_tt, gsel, grp, rank, tot = _post_call(
            xf, hf, hm, w_out[l].astype(BF16), g1, norm_ffn[l].reshape(1, d), sc2, sh2, wr, br, tri,
            bsz=bsz, seq=seq, chunk=chunk, te=te)

        n_chunks = t // chunk
        grp = grp.reshape(n_chunks, 1, chunk)
        rank = rank.reshape(n_chunks, 1, chunk)
        off, tile_blk, tile_grp = _moe_tiles(tot[:, :N_GROUPS, 0], tm=tm, tiles_per_chunk=rows // tm)
        pos = rank
        for g in range(N_GROUPS):
            pos = pos + jnp.where(grp == g, off[:, g].reshape(n_chunks, 1, 1), 0)
        xb, gs = _scatter_call(pos, hn_tt, gsel, chunk=chunk, rows=rows, tm=tm)
        ys_tt = _experts_call(tile_blk, tile_grp, xb, gs, w_gate[l].astype(BF16),
                              w_up[l].astype(BF16), w_down[l].astype(BF16), tm=tm)
        moe_tt = _gather_call(pos, ys_tt, chunk=chunk, rows=rows, n_tok=t)
        g2_prev = g2

    out = _final_call(xf, moe_tt, g2_prev, norm_final.reshape(1, d), seq=seq)
    return out.reshape(bsz, seq, d)
```

```python
import functools

import numpy as np
import jax
import jax.numpy as jnp
from jax import lax
from jax.experimental import pallas as pl
from jax.experimental.pallas import tpu as pltpu

F32 = jnp.float32
BF16 = jnp.bfloat16

LANES = 128
SUBLANES = 8
VMEM_BYTES_V7X = 64 * 1024 * 1024

D_MODEL = 1024
FOX_HEADS = 8
FOX_HEAD_DIM = 64
FOX_WIDTH = FOX_HEADS * FOX_HEAD_DIM
ML_HEADS = 4
ML_HEAD_DIM = 128
ML_WIDTH = ML_HEADS * ML_HEAD_DIM
CONV_WIDTH = 4
N_EXPERTS = 16
N_GROUPS = 4
EXPERTS_PER_GROUP = 4
D_FF = 512
N_ADA = 6
EPS = 1e-6
NEG = -1e30

FOX_Q = 0
FOX_F = 3 * FOX_WIDTH
ML_Q = FOX_F + FOX_HEADS
ML_I = ML_Q + 3 * ML_WIDTH
ML_F = ML_I + ML_HEADS
ML_O = ML_F + ML_HEADS
IN_COLS = ML_O + ML_WIDTH

MAIN_COLS = 7 * 512
BLK_FQ, BLK_FK, BLK_FV, BLK_MQ, BLK_MK, BLK_MV, BLK_MO = range(7)
REST_COLS = 3 * 512
OUT_MQ, OUT_MV, OUT_MO = range(3)
AUG_COLS = FOX_HEADS * 128
BIAS_TERMS = 3
VT_PAD = 16
LOG2E = 1.4426950408889634
GATE_FOX = 0
GATE_ML = FOX_HEADS
GATE_ROWS = 16

TOK_ROWS = D_MODEL // LANES


def _cparams(n_grid, vmem_mb):
    return pltpu.CompilerParams(
        dimension_semantics=("arbitrary",) * n_grid,
        vmem_limit_bytes=vmem_mb * 1024 * 1024)


def _silu(x):
    return x * jax.nn.sigmoid(x)


def _log_sigmoid(z):
    return jnp.minimum(z, 0.0) - jnp.log1p(jnp.exp(-jnp.abs(z)))


def _cumsum_rows(x):
    n = x.shape[0]
    row = lax.broadcasted_iota(jnp.int32, x.shape, 0)
    s = 1
    while s < n:
        x = x + jnp.where(row >= s, pltpu.roll(x, s, axis=0), 0.0)
        s *= 2
    return x


def _from_token_tiles(ref, n_tok):
    return jnp.concatenate(
        [ref[pl.ds(c, n_tok, stride=TOK_ROWS), :] for c in range(TOK_ROWS)], axis=1)


def _to_token_tiles(ref, val, row0=0):
    n = val.shape[0]
    for c in range(TOK_ROWS):
        ref[pl.ds(row0 * TOK_ROWS + c, n, stride=TOK_ROWS), :] = val[:, c * LANES:(c + 1) * LANES]


def _ada_kernel(c_ref, w_ref, b_ref, o_ref):
    c = c_ref[...]
    o_ref[0] = jnp.dot(_silu(c), w_ref[0], preferred_element_type=F32,
                       precision=lax.Precision.HIGHEST) + b_ref[0]


def _ada_call(c, w_ada, b_ada):
    depth, d, n = w_ada.shape
    bsz = c.shape[0]
    tn = 1536
    return pl.pallas_call(
        _ada_kernel,
        grid=(depth, n // tn),
        in_specs=[
            pl.BlockSpec((bsz, d), lambda l, j: (0, 0)),
            pl.BlockSpec((1, d, tn), lambda l, j: (l, 0, j)),
            pl.BlockSpec((1, 1, tn), lambda l, j: (l, 0, j)),
        ],
        out_specs=pl.BlockSpec((1, bsz, tn), lambda l, j: (l, 0, j)),
        out_shape=jax.ShapeDtypeStruct((depth, bsz, n), F32),
        compiler_params=_cparams(2, 32),
        name="ada_mod",
    )(c, w_ada, b_ada.reshape(depth, 1, n))


def _in_proj_kernel(*refs, tm, combine):
    if combine:
        (x_ref, ott_ref, g2_ref, nm_ref, sc_ref, sh_ref, wm_ref, wg_ref, gb_ref, cw_ref, cb_ref,
         place_ref, ones_ref,
         xn_ref, qa_ref, ka_ref, vt_ref, qkv_ref, kt_ref, fc_ref, gr_ref, fcar, ccar) = refs
    else:
        (x_ref, nm_ref, sc_ref, sh_ref, wm_ref, wg_ref, gb_ref, cw_ref, cb_ref,
         place_ref, ones_ref,
         qa_ref, ka_ref, vt_ref, qkv_ref, kt_ref, fc_ref, gr_ref, fcar, ccar) = refs

    @pl.when(pl.program_id(1) == 0)
    def _():
        fcar[...] = jnp.zeros_like(fcar)
        ccar[...] = jnp.zeros_like(ccar)

    x = x_ref[...]
    if combine:
        x = x + g2_ref[0] * _from_token_tiles(ott_ref, tm)
        xn_ref[...] = x
    ms = jnp.mean(x * x, axis=1, keepdims=True)
    hn = x * lax.rsqrt(ms + EPS) * nm_ref[...]
    hn = hn * (1.0 + sc_ref[0]) + sh_ref[0]
    hb = hn.astype(BF16)

    gp = jnp.dot(hb, wg_ref[...], preferred_element_type=F32) + gb_ref[...]
    lf = _log_sigmoid(gp[:, :LANES])
    fcum = _cumsum_rows(lf) + fcar[...]
    fcar[...] = fcum[tm - 1:tm, :]
    gml = gp[:, LANES:] - fcum
    fc_ref[...] = fcum
    gr_ref[0] = gml.T[:GATE_ROWS, :]

    def mm(j):
        return jnp.dot(hb, wm_ref[:, j * 512:(j + 1) * 512], preferred_element_type=F32)

    def put(j, v):
        qkv_ref[:, j * 512:(j + 1) * 512] = v.astype(BF16)

    lane = lax.broadcasted_iota(jnp.int32, (tm, LANES), 1)
    fs = fcum * LOG2E
    hi = fs.astype(BF16).astype(F32)
    mid = (fs - hi).astype(BF16).astype(F32)
    low = (fs - hi - mid).astype(BF16).astype(F32)
    packed = jnp.where(lane < FOX_HEADS, hi,
                       jnp.where(lane < 2 * FOX_HEADS, pltpu.roll(mid, FOX_HEADS, axis=1),
                                 jnp.where(lane < 3 * FOX_HEADS, pltpu.roll(low, 2 * FOX_HEADS, axis=1), 0.0)))
    bias = jnp.dot(packed.astype(BF16), place_ref[...], preferred_element_type=F32) + ones_ref[...]

    def put_heads(ref, val, col0, transposed):
        lo = lane < FOX_HEAD_DIM
        for p in range(FOX_HEADS // 2):
            slab = val[:, p * LANES:(p + 1) * LANES]
            for h, data in ((2 * p, slab), (2 * p + 1, pltpu.roll(slab, FOX_HEAD_DIM, axis=1))):
                blk = jnp.where(lo, data, 0.0) + bias[:, col0 + h * LANES:col0 + (h + 1) * LANES]
                if transposed:
                    ref[0, h * LANES:(h + 1) * LANES, :] = blk.T.astype(BF16)
                else:
                    ref[:, h * LANES:(h + 1) * LANES] = blk.astype(BF16)

    u = jnp.concatenate([mm(BLK_MQ), mm(BLK_MK)], axis=1)
    prev = ccar[...]
    ccar[...] = u[tm - SUBLANES:tm, :]
    row8 = lax.broadcasted_iota(jnp.int32, prev.shape, 0)
    y = cb_ref[...] + cw_ref[CONV_WIDTH - 1:CONV_WIDTH, :] * u
    for k in range(1, CONV_WIDTH):
        r = pltpu.roll(u, k, axis=0)
        top = jnp.where(row8 < k, pltpu.roll(prev, k, axis=0), r[:SUBLANES])
        shifted = jnp.concatenate([top, r[SUBLANES:]], axis=0)
        y = y + cw_ref[CONV_WIDTH - 1 - k:CONV_WIDTH - k, :] * shifted
    act = _silu(y)
    put(OUT_MQ, act[:, :ML_WIDTH])
    kt_ref[0] = (act[:, ML_WIDTH:] * (ML_HEAD_DIM ** -0.5)).T.astype(BF16)

    put(OUT_MO, jax.nn.sigmoid(mm(BLK_MO)))
    put(OUT_MV, mm(BLK_MV))
    vt_ref[0] = mm(BLK_FV).astype(BF16).T
    put_heads(qa_ref, mm(BLK_FQ) * (FOX_HEAD_DIM ** -0.5 * LOG2E), 0, True)
    put_heads(ka_ref, mm(BLK_FK), AUG_COLS, False)


def _bias_placement():
    place = np.zeros((LANES, 2 * AUG_COLS), np.float32)
    ones = np.zeros((1, 2 * AUG_COLS), np.float32)
    for h in range(FOX_HEADS):
        for term in range(BIAS_TERMS):
            src = term * FOX_HEADS + h
            place[src, h * LANES + FOX_HEAD_DIM + term] = 1.0
            ones[0, h * LANES + FOX_HEAD_DIM + BIAS_TERMS + term] = 1.0
            place[src, AUG_COLS + h * LANES + FOX_HEAD_DIM + BIAS_TERMS + term] = -1.0
            ones[0, AUG_COLS + h * LANES + FOX_HEAD_DIM + term] = 1.0
    return jnp.asarray(place, BF16), jnp.asarray(ones, F32)


def _in_proj_call(x, moe_tt, g2, nm, sc, sh, wm, wg, gb, cw, cb, *, bsz, seq, tm=512):
    t, d = x.shape
    ns = seq // tm
    combine = moe_tt is not None
    row = lambda b, s: (b * ns + s, 0)
    per_b = lambda b, s: (b, 0, 0)
    const2 = lambda b, s: (0, 0)
    in_specs = [pl.BlockSpec((tm, d), row)]
    args = [x]
    if combine:
        in_specs += [pl.BlockSpec((tm * TOK_ROWS, LANES), row), pl.BlockSpec((1, 1, d), per_b)]
        args += [moe_tt, g2]
    in_specs += [
        pl.BlockSpec((1, d), const2),
        pl.BlockSpec((1, 1, d), per_b),
        pl.BlockSpec((1, 1, d), per_b),
        pl.BlockSpec((d, MAIN_COLS), const2),
        pl.BlockSpec((d, 2 * LANES), const2),
        pl.BlockSpec((1, 2 * LANES), const2),
        pl.BlockSpec((CONV_WIDTH, 2 * ML_WIDTH), const2),
        pl.BlockSpec((1, 2 * ML_WIDTH), const2),
        pl.BlockSpec((LANES, 2 * AUG_COLS), const2),
        pl.BlockSpec((1, 2 * AUG_COLS), const2),
    ]
    place, ones = _bias_placement()
    args += [nm, sc, sh, wm, wg, gb, cw, cb, place, ones]
    out_specs = []
    out_shape = []
    if combine:
        out_specs.append(pl.BlockSpec((tm, d), row))
        out_shape.append(jax.ShapeDtypeStruct((t, d), F32))
    out_specs += [
        pl.BlockSpec((1, AUG_COLS, tm), lambda b, s: (b, 0, s)),
        pl.BlockSpec((tm, AUG_COLS), row),
        pl.BlockSpec((1, FOX_WIDTH, tm), lambda b, s: (b, 0, s)),
        pl.BlockSpec((tm, REST_COLS), row),
        pl.BlockSpec((1, ML_WIDTH, tm), lambda b, s: (b, 0, s)),
        pl.BlockSpec((tm, LANES), row),
        pl.BlockSpec((1, GATE_ROWS, tm), lambda b, s: (b, 0, s)),
    ]
    out_shape += [
        jax.ShapeDtypeStruct((bsz, AUG_COLS, seq), BF16),
        jax.ShapeDtypeStruct((t, AUG_COLS), BF16),
        jax.ShapeDtypeStruct((bsz, FOX_WIDTH, seq), BF16),
        jax.ShapeDtypeStruct((t, REST_COLS), BF16),
        jax.ShapeDtypeStruct((bsz, ML_WIDTH, seq), BF16),
        jax.ShapeDtypeStruct((t, LANES), F32),
        jax.ShapeDtypeStruct((bsz, GATE_ROWS, seq), F32),
    ]
    outs = pl.pallas_call(
        functools.partial(_in_proj_kernel, tm=tm, combine=combine),
        grid=(bsz, ns),
        in_specs=in_specs,
        out_specs=out_specs,
        out_shape=out_shape,
        scratch_shapes=[pltpu.VMEM((1, LANES), F32), pltpu.VMEM((SUBLANES, 2 * ML_WIDTH), F32)],
        compiler_params=_cparams(2, 48),
        name="in_proj",
    )(*args)
    if combine:
        return outs
    return [x] + list(outs)


def _fox_kernel(qt_ref, k_ref, vt_ref, ng_ref, cm_ref, o_ref, m_sc, acc_sc, *, tq, tk, nh):
    qi = pl.program_id(2)
    ones_rows = jnp.where(lax.broadcasted_iota(jnp.int32, (VT_PAD, tk), 0) == 0, 1.0, 0.0).astype(BF16)
    for h in range(nh):
        m_sc[h] = jnp.full((SUBLANES, tq), NEG, F32)
        acc_sc[h] = jnp.zeros((FOX_HEAD_DIM + VT_PAD, tq), F32)

    def body(j, carry):
        k0 = pl.multiple_of(j * tk, tk)
        slot = (j == qi).astype(jnp.int32)

        def logits(h):
            hs = slice(h * LANES, (h + 1) * LANES)
            return (jnp.dot(k_ref[pl.ds(k0, tk), hs], qt_ref[0, hs, :], preferred_element_type=F32)
                    + cm_ref[slot])

        ahead = 2
        zs = [logits(h) for h in range(min(ahead, nh))]
        for h in range(nh):
            ds_ = slice(h * FOX_HEAD_DIM, (h + 1) * FOX_HEAD_DIM)
            z = zs[h]
            if h + ahead < nh:
                zs.append(logits(h + ahead))
            m_prev = m_sc[h]
            m_new = jnp.maximum(m_prev, jnp.max(z, axis=0, keepdims=True))
            alpha = jnp.exp2(m_prev - m_new)
            p = jnp.exp2(z - m_new[:1, :])
            va = jnp.concatenate([vt_ref[0, ds_, pl.ds(k0, tk)], ones_rows], axis=0)
            acc_sc[h] = alpha[:1, :] * acc_sc[h] + jnp.dot(va, p.astype(BF16), preferred_element_type=F32)
            m_sc[h] = m_new
        return carry

    lax.fori_loop(0, qi + 1, body, 0)

    for p in range(nh // 2):
        outs = []
        for h in (2 * p, 2 * p + 1):
            acc = acc_sc[h]
            num = acc[:FOX_HEAD_DIM, :]
            l = acc[FOX_HEAD_DIM:FOX_HEAD_DIM + 1, :]
            ms = jnp.mean(num * num, axis=0, keepdims=True)
            outs.append(num * lax.rsqrt(ms + EPS * l * l))
        ps = slice(p * LANES, (p + 1) * LANES)
        o_ref[:, ps] = (jnp.concatenate(outs, axis=0).T * ng_ref[:, ps]).astype(BF16)


def _fox_call(qat, ka, fvt, ng, *, bsz, seq, tq=512, nh=8):
    t = ka.shape[0]
    nq = seq // tq
    ngrp = FOX_HEADS // nh
    vw = nh * FOX_HEAD_DIM
    causal = jnp.where(jnp.arange(tq)[:, None] <= jnp.arange(tq)[None, :], 0.0, NEG).astype(F32)
    cmask = jnp.stack([jnp.zeros_like(causal), causal])
    return pl.pallas_call(
        functools.partial(_fox_kernel, tq=tq, tk=tq, nh=nh),
        grid=(bsz, ngrp, nq),
        in_specs=[
            pl.BlockSpec((1, nh * LANES, tq), lambda b, p, i: (b, p, i)),
            pl.BlockSpec((seq, nh * LANES), lambda b, p, i: (b, p)),
            pl.BlockSpec((1, vw, seq), lambda b, p, i: (b, p, 0)),
            pl.BlockSpec((1, vw), lambda b, p, i: (0, p)),
            pl.BlockSpec((2, tq, tq), lambda b, p, i: (0, 0, 0)),
        ],
        out_specs=pl.BlockSpec((tq, vw), lambda b, p, i: (b * nq + i, p)),
        out_shape=jax.ShapeDtypeStruct((t, FOX_WIDTH), BF16),
        scratch_shapes=[pltpu.VMEM((nh, SUBLANES, tq), F32),
                        pltpu.VMEM((nh, FOX_HEAD_DIM + VT_PAD, tq), F32)],
        compiler_params=_cparams(3, 48),
        name="fox_attn",
    )(qat, ka, fvt, ng, cmask)


def _mlstm_kernel(q_ref, kt_ref, v_ref, og_ref, fc_ref, gr_ref, ng_ref, o_ref, ct_sc, u_sc, *, ch):
    @pl.when(pl.program_id(1) == 0)
    def _():
        ct_sc[...] = jnp.zeros_like(ct_sc)
        u_sc[...] = jnp.zeros_like(u_sc)

    causal = (lax.broadcasted_iota(jnp.int32, (ch, ch), 1)
              <= lax.broadcasted_iota(jnp.int32, (ch, ch), 0))
    lane = lax.broadcasted_iota(jnp.int32, (ch, LANES), 1)
    fc = fc_ref[...]
    for h in range(ML_HEADS):
        sl = slice(h * ML_HEAD_DIM, (h + 1) * ML_HEAD_DIM)
        gl = GATE_ML + h
        q = q_ref[:, sl]
        kt = kt_ref[0, sl, :]
        vp = jnp.concatenate([v_ref[:, sl], jnp.where(lane == gl, 1.0, 0.0).astype(BF16)], axis=1)
        g_row = gr_ref[0, gl:gl + 1, :]
        u_prev = u_sc[h][:, :1]
        gm = jnp.where(causal, g_row, NEG)
        u_i = jnp.maximum(u_prev, jnp.max(gm, axis=1, keepdims=True))
        dmat = jnp.exp(gm - u_i)
        s = jnp.dot(q, kt, preferred_element_type=F32)
        scores = (s * dmat).astype(BF16)
        inter = jnp.exp(u_prev - u_i)
        ct = ct_sc[h]
        nd = (jnp.dot(scores, vp, preferred_element_type=F32)
              + jnp.dot(q, ct.astype(BF16), preferred_element_type=F32) * inter)
        num = nd[:, :ML_HEAD_DIM]
        den = jnp.maximum(jnp.abs(nd[:, ML_HEAD_DIM:]), jnp.exp(-(jnp.where(lane == gl, fc, 0.0) + u_i)))
        ms = jnp.mean(num * num, axis=1, keepdims=True)
        scale = lax.rsqrt(ms + EPS * den * den)[:, gl:gl + 1]
        y = num * scale * ng_ref[:, sl] * og_ref[:, sl].astype(F32)
        o_ref[:, sl] = y.astype(BF16)
        u_new = jnp.maximum(u_prev, jnp.max(g_row, axis=1, keepdims=True))
        ktw = (kt.astype(F32) * jnp.exp(g_row - u_new)).astype(BF16)
        ct_sc[h] = jnp.exp(u_prev - u_new) * ct + jnp.dot(ktw, vp, preferred_element_type=F32)
        u_sc[h] = jnp.broadcast_to(u_new, (1, LANES))


def _mlstm_call(qkv, mkt, fc, gr, ng, *, bsz, seq, ch=256):
    t = qkv.shape[0]
    nc = seq // ch
    row = lambda b, c: (b * nc + c, 0)
    return pl.pallas_call(
        functools.partial(_mlstm_kernel, ch=ch),
        grid=(bsz, nc),
        in_specs=[
            pl.BlockSpec((ch, ML_WIDTH), lambda b, c: (b * nc + c, OUT_MQ)),
            pl.BlockSpec((1, ML_WIDTH, ch), lambda b, c: (b, 0, c)),
            pl.BlockSpec((ch, ML_WIDTH), lambda b, c: (b * nc + c, OUT_MV)),
            pl.BlockSpec((ch, ML_WIDTH), lambda b, c: (b * nc + c, OUT_MO)),
            pl.BlockSpec((ch, LANES), row),
            pl.BlockSpec((1, GATE_ROWS, ch), lambda b, c: (b, 0, c)),
            pl.BlockSpec((1, ML_WIDTH), lambda b, c: (0, 0)),
        ],
        out_specs=pl.BlockSpec((ch, ML_WIDTH), row),
        out_shape=jax.ShapeDtypeStruct((t, ML_WIDTH), BF16),
        scratch_shapes=[pltpu.VMEM((ML_HEADS, ML_HEAD_DIM, 2 * ML_HEAD_DIM), F32),
                        pltpu.VMEM((ML_HEADS, 1, LANES), F32)],
        compiler_params=_cparams(2, 32),
        name="mlstm",
    )(qkv, mkt, qkv, qkv, fc, gr, ng)


def _post_kernel(x_ref, hf_ref, hm_ref, wo_ref, g1_ref, nf_ref, sc_ref, sh_ref, wr_ref, br_ref, tri_ref,
                 x1_ref, hn_ref, gsel_ref, grp_ref, rank_ref, tot_ref, cnt_sc, *, te, steps_per_chunk):
    @pl.when(pl.program_id(0) % steps_per_chunk == 0)
    def _():
        cnt_sc[...] = jnp.zeros_like(cnt_sc)

    mix = (jnp.dot(hf_ref[...], wo_ref[:FOX_WIDTH, :], preferred_element_type=F32)
           + jnp.dot(hm_ref[...], wo_ref[FOX_WIDTH:, :], preferred_element_type=F32))
    x1 = x_ref[...] + g1_ref[0] * mix
    x1_ref[...] = x1
    ms = jnp.mean(x1 * x1, axis=1, keepdims=True)
    hn = x1 * lax.rsqrt(ms + EPS) * nf_ref[...]
    hn = hn * (1.0 + sc_ref[0]) + sh_ref[0]
    _to_token_tiles(hn_ref, hn)

    hn_hi = hn.astype(BF16)
    hn_lo = (hn - hn_hi.astype(F32)).astype(BF16)
    l2 = jnp.dot(hn_hi, wr_ref[...], preferred_element_type=F32)
    logits = (l2[:, :LANES] + l2[:, LANES:]
              + jnp.dot(hn_lo, wr_ref[:, :LANES], preferred_element_type=F32))
    aff = jax.nn.sigmoid(logits.T[:N_EXPERTS, :])
    sel = aff + br_ref[...]
    selr = [sel[e:e + 1, :] for e in range(N_EXPERTS)]
    affr = [aff[e:e + 1, :] for e in range(N_EXPERTS)]
    keep = [None] * N_EXPERTS
    score = []
    for g in range(N_GROUPS):
        vs = selr[g * EXPERTS_PER_GROUP:(g + 1) * EXPERTS_PER_GROUP]
        sg = jnp.zeros_like(vs[0])
        for i in range(EXPERTS_PER_GROUP):
            beaten = jnp.zeros_like(vs[0])
            for j in range(EXPERTS_PER_GROUP):
                if j != i:
                    b = (vs[j] >= vs[i]) if j < i else (vs[j] > vs[i])
                    beaten = beaten + jnp.where(b, 1.0, 0.0)
            kp = beaten < 2.0
            keep[g * EXPERTS_PER_GROUP + i] = kp
            sg = sg + jnp.where(kp, vs[i], 0.0)
        score.append(sg)
    chosen = []
    for g in range(N_GROUPS):
        lost = jnp.zeros_like(score[0])
        for g2 in range(N_GROUPS):
            if g2 != g:
                b = (score[g2] >= score[g]) if g2 < g else (score[g2] > score[g])
                lost = lost + jnp.where(b, 1.0, 0.0)
        chosen.append(jnp.where(lost < 0.5, 1.0, 0.0))
    wsel = []
    for i in range(EXPERTS_PER_GROUP):
        wi = jnp.zeros_like(score[0])
        for g in range(N_GROUPS):
            e = g * EXPERTS_PER_GROUP + i
            wi = wi + chosen[g] * jnp.where(keep[e], affr[e], 0.0)
        wsel.append(wi)
    wsum = wsel[0] + wsel[1] + wsel[2] + wsel[3]
    wsel = [w / wsum for w in wsel]

    row8 = lax.broadcasted_iota(jnp.int32, (SUBLANES, te), 0)
    gmat = jnp.zeros((SUBLANES, te), F32)
    wmat = jnp.zeros((SUBLANES, te), F32)
    grp = jnp.zeros_like(score[0])
    for g in range(N_GROUPS):
        gmat = jnp.where(row8 == g, chosen[g], gmat)
        wmat = jnp.where(row8 == g, wsel[g], wmat)
        grp = grp + g * chosen[g]
    pref = jnp.dot(gmat.astype(BF16), tri_ref[...], preferred_element_type=F32) + cnt_sc[:, :1]
    rank = jnp.sum(gmat * pref, axis=0, keepdims=True)
    tot = cnt_sc[:, :1] + jnp.sum(gmat, axis=1, keepdims=True)
    cnt_sc[...] = jnp.broadcast_to(tot, cnt_sc.shape)
    rank_ref[0] = rank.astype(jnp.int32)
    grp_ref[0] = grp.astype(jnp.int32)
    tot_ref[0] = jnp.broadcast_to(tot, (SUBLANES, LANES)).astype(jnp.int32)
    wfull = jnp.concatenate([wmat, jnp.zeros((LANES - SUBLANES, te), F32)], axis=0)
    gsel_ref[...] = wfull.T


def _post_call(x, hf, hm, wo, g1, nf, sc, sh, wr, br, tri, *, bsz, seq, chunk, te=512):
    t, d = x.shape
    steps_per_chunk = chunk // te
    per_seq = seq // te
    n_steps = t // te
    n_chunks = t // chunk
    row = lambda i: (i, 0)
    per_b = lambda i: (i // per_seq, 0, 0)
    const2 = lambda i: (0, 0)
    return pl.pallas_call(
        functools.partial(_post_kernel, te=te, steps_per_chunk=steps_per_chunk),
        grid=(n_steps,),
        in_specs=[
            pl.BlockSpec((te, d), row),
            pl.BlockSpec((te, FOX_WIDTH), row),
            pl.BlockSpec((te, ML_WIDTH), row),
            pl.BlockSpec((d, d), const2),
            pl.BlockSpec((1, 1, d), per_b),
            pl.BlockSpec((1, d), const2),
            pl.BlockSpec((1, 1, d), per_b),
            pl.BlockSpec((1, 1, d), per_b),
            pl.BlockSpec((d, 2 * LANES), const2),
            pl.BlockSpec((N_EXPERTS, 1), const2),
            pl.BlockSpec((te, te), const2),
        ],
        out_specs=[
            pl.BlockSpec((te, d), row),
            pl.BlockSpec((te * TOK_ROWS, LANES), row),
            pl.BlockSpec((te, LANES), row),
            pl.BlockSpec((1, 1, te), lambda i: (i, 0, 0)),
            pl.BlockSpec((1, 1, te), lambda i: (i, 0, 0)),
            pl.BlockSpec((1, SUBLANES, LANES), lambda i: (i // steps_per_chunk, 0, 0)),
        ],
        out_shape=[
            jax.ShapeDtypeStruct((t, d), F32),
            jax.ShapeDtypeStruct((t * TOK_ROWS, LANES), F32),
            jax.ShapeDtypeStruct((t, LANES), F32),
            jax.ShapeDtypeStruct((n_steps, 1, te), jnp.int32),
            jax.ShapeDtypeStruct((n_steps, 1, te), jnp.int32),
            jax.ShapeDtypeStruct((n_chunks, SUBLANES, LANES), jnp.int32),
        ],
        scratch_shapes=[pltpu.VMEM((SUBLANES, LANES), F32)],
        compiler_params=_cparams(1, 48),
        name="post_router",
    )(x, hf, hm, wo, g1, nf, sc, sh, wr, br, tri)


def _scatter_kernel(pos_ref, hn_ref, gsel_ref, xb_ref, gs_ref, xs_sc, *, chunk, rows, tm):
    xs_sc[...] = jnp.zeros_like(xs_sc)
    gs_ref[...] = jnp.zeros_like(gs_ref)

    def body(t, carry):
        p = pos_ref[0, 0, t]
        src = pl.multiple_of(t * TOK_ROWS, TOK_ROWS)
        dst = pl.multiple_of(p * TOK_ROWS, TOK_ROWS)
        xs_sc[pl.ds(dst, TOK_ROWS), :] = hn_ref[pl.ds(src, TOK_ROWS), :]
        gs_ref[pl.ds(p, 1), :] = gsel_ref[pl.ds(t, 1), :]
        return carry

    lax.fori_loop(0, chunk, body, 0, unroll=8)
    for j in range(rows // tm):
        for c in range(TOK_ROWS):
            xb_ref[j * tm:(j + 1) * tm, c * LANES:(c + 1) * LANES] = (
                xs_sc[pl.ds(j * tm * TOK_ROWS + c, tm, stride=TOK_ROWS), :].astype(BF16))


def _scatter_call(pos, hn_tt, gsel, *, chunk, rows, tm):
    t = gsel.shape[0]
    n_chunks = t // chunk
    return pl.pallas_call(
        functools.partial(_scatter_kernel, chunk=chunk, rows=rows, tm=tm),
        grid=(n_chunks,),
        in_specs=[
            pl.BlockSpec((1, 1, chunk), lambda c: (c, 0, 0), memory_space=pltpu.SMEM),
            pl.BlockSpec((chunk * TOK_ROWS, LANES), lambda c: (c, 0)),
            pl.BlockSpec((chunk, LANES), lambda c: (c, 0)),
        ],
        out_specs=[
            pl.BlockSpec((rows, D_MODEL), lambda c: (c, 0)),
            pl.BlockSpec((rows, LANES), lambda c: (c, 0)),
        ],
        out_shape=[
            jax.ShapeDtypeStruct((n_chunks * rows, D_MODEL), BF16),
            jax.ShapeDtypeStruct((n_chunks * rows, LANES), F32),
        ],
        scratch_shapes=[pltpu.VMEM((rows * TOK_ROWS, LANES), F32)],
        compiler_params=_cparams(1, 56),
        name="moe_scatter",
    )(pos, hn_tt, gsel)


def _experts_kernel(blk_ref, grp_ref, xb_ref, gs_ref, wg_ref, wu_ref, wd_ref, y_ref, *, tm):
    g = grp_ref[pl.program_id(0)]

    @pl.when(g < N_GROUPS)
    def _():
        x = xb_ref[...]
        gs = gs_ref[...]
        acc = jnp.zeros((tm, D_MODEL), F32)
        for i in range(EXPERTS_PER_GROUP):
            hg = jnp.dot(x, wg_ref[i], preferred_element_type=F32)
            hu = jnp.dot(x, wu_ref[i], preferred_element_type=F32)
            a = _silu(hg) * hu * gs[:, i:i + 1]
            acc = acc + jnp.dot(a.astype(BF16), wd_ref[i], preferred_element_type=F32)
        _to_token_tiles(y_ref, acc)

    @pl.when(g >= N_GROUPS)
    def _():
        y_ref[...] = jnp.zeros_like(y_ref)


def _experts_call(tile_blk, tile_grp, xb, gs, wg, wu, wd, *, tm):
    n_rows = xb.shape[0]
    n_slots = n_rows // tm
    wmap = lambda s, blk, grp: (jnp.minimum(grp[s], N_GROUPS - 1), 0, 0)
    grid_spec = pltpu.PrefetchScalarGridSpec(
        num_scalar_prefetch=2,
        grid=(n_slots,),
        in_specs=[
            pl.BlockSpec((tm, D_MODEL), lambda s, blk, grp: (blk[s], 0)),
            pl.BlockSpec((tm, LANES), lambda s, blk, grp: (blk[s], 0)),
            pl.BlockSpec((EXPERTS_PER_GROUP, D_MODEL, D_FF), wmap),
            pl.BlockSpec((EXPERTS_PER_GROUP, D_MODEL, D_FF), wmap),
            pl.BlockSpec((EXPERTS_PER_GROUP, D_FF, D_MODEL), wmap),
        ],
        out_specs=pl.BlockSpec((tm * TOK_ROWS, LANES), lambda s, blk, grp: (blk[s], 0)),
    )
    return pl.pallas_call(
        functools.partial(_experts_kernel, tm=tm),
        grid_spec=grid_spec,
        out_shape=jax.ShapeDtypeStruct((n_rows * TOK_ROWS, LANES), F32),
        compiler_params=_cparams(1, 48),
        name="moe_experts",
    )(tile_blk, tile_grp, xb, gs, wg, wu, wd)


def _gather_kernel(pos_ref, ys_ref, o_ref, *, chunk):
    def body(t, carry):
        p = pos_ref[0, 0, t]
        src = pl.multiple_of(p * TOK_ROWS, TOK_ROWS)
        dst = pl.multiple_of(t * TOK_ROWS, TOK_ROWS)
        o_ref[pl.ds(dst, TOK_ROWS), :] = ys_ref[pl.ds(src, TOK_ROWS), :]
        return carry

    lax.fori_loop(0, chunk, body, 0, unroll=8)


def _gather_call(pos, ys_tt, *, chunk, rows, n_tok):
    n_chunks = n_tok // chunk
    return pl.pallas_call(
        functools.partial(_gather_kernel, chunk=chunk),
        grid=(n_chunks,),
        in_specs=[
            pl.BlockSpec((1, 1, chunk), lambda c: (c, 0, 0), memory_space=pltpu.SMEM),
            pl.BlockSpec((rows * TOK_ROWS, LANES), lambda c: (c, 0)),
        ],
        out_specs=pl.BlockSpec((chunk * TOK_ROWS, LANES), lambda c: (c, 0)),
        out_shape=jax.ShapeDtypeStruct((n_tok * TOK_ROWS, LANES), F32),
        compiler_params=_cparams(1, 56),
        name="moe_gather",
    )(pos, ys_tt)


def _final_kernel(x_ref, ott_ref, g2_ref, nf_ref, o_ref, *, tm):
    x = x_ref[...] + g2_ref[0] * _from_token_tiles(ott_ref, tm)
    ms = jnp.mean(x * x, axis=1, keepdims=True)
    o_ref[...] = x * lax.rsqrt(ms + EPS) * nf_ref[...]


def _final_call(x, moe_tt, g2, nf, *, seq, tm=512):
    t, d = x.shape
    per_seq = seq // tm
    return pl.pallas_call(
        functools.partial(_final_kernel, tm=tm),
        grid=(t // tm,),
        in_specs=[
            pl.BlockSpec((tm, d), lambda i: (i, 0)),
            pl.BlockSpec((tm * TOK_ROWS, LANES), lambda i: (i, 0)),
            pl.BlockSpec((1, 1, d), lambda i: (i // per_seq, 0, 0)),
            pl.BlockSpec((1, d), lambda i: (0, 0)),
        ],
        out_specs=pl.BlockSpec((tm, d), lambda i: (i, 0)),
        out_shape=jax.ShapeDtypeStruct((t, d), F32),
        compiler_params=_cparams(1, 32),
        name="final_norm",
    )(x, moe_tt, g2, nf)


def _moe_tiles(tot, *, tm, tiles_per_chunk):
    nt = (tot + tm - 1) // tm
    ts = jnp.cumsum(nt, axis=1) - nt
    off = (ts * tm).astype(jnp.int32)
    j = jnp.arange(tiles_per_chunk, dtype=jnp.int32)[None, :, None]
    inside = (j >= ts[:, None, :]) & (j < (ts + nt)[:, None, :])
    key = jnp.where(jnp.any(inside, axis=-1), jnp.argmax(inside, axis=-1), N_GROUPS).reshape(-1)
    order = jnp.argsort(key, stable=True).astype(jnp.int32)
    return off, order, key[order].astype(jnp.int32)


def kernel(x, c, w_in, conv_w, conv_b, fox_f_bias, mlstm_i_bias, mlstm_f_bias, fox_out_norm,
           mlstm_out_norm, w_out, w_ada, b_ada, norm_mix, norm_ffn, w_router, b_router, w_gate,
           w_up, w_down, norm_final):
    bsz, seq, d = x.shape
    depth = w_in.shape[0]
    t = bsz * seq
    assert d == D_MODEL and w_in.shape[-1] == IN_COLS
    chunk = min(2048, seq)
    tm = 272 if chunk == 2048 else 256
    rows = -(-(chunk + N_GROUPS * (tm - 1)) // tm) * tm
    te = min(512, seq)
    assert seq % 512 == 0 and seq % chunk == 0 and chunk % te == 0

    mods = _ada_call(c, w_ada, b_ada).reshape(depth, bsz, N_ADA, 1, d)
    xf = x.reshape(t, d)

    tri = (jnp.arange(te)[:, None] < jnp.arange(te)[None, :]).astype(BF16)
    wr_f = jnp.pad(w_router, ((0, 0), (0, LANES - N_EXPERTS))).astype(F32)
    wr_hi = wr_f.astype(BF16)
    wr = jnp.concatenate([wr_hi, (wr_f - wr_hi.astype(F32)).astype(BF16)], axis=1)
    br = b_router.reshape(N_EXPERTS, 1).astype(F32)

    moe_tt = None
    g2_prev = None
    for l in range(depth):
        sh1, sc1, g1, sh2, sc2, g2 = [mods[l, :, i] for i in range(N_ADA)]
        wl = w_in[l]
        wm = jnp.concatenate([wl[:, FOX_Q:FOX_F], wl[:, ML_Q:ML_I], wl[:, ML_O:IN_COLS]], axis=1).astype(BF16)
        zpad = lambda n: jnp.zeros((d, n), F32)
        wg = jnp.concatenate([
            wl[:, FOX_F:ML_Q], wl[:, ML_F:ML_O], zpad(LANES - FOX_HEADS - ML_HEADS),
            zpad(GATE_ML), wl[:, ML_I:ML_F], zpad(LANES - GATE_ML - ML_HEADS)], axis=1).astype(BF16)
        zb = lambda n: jnp.zeros((n,), F32)
        gb = jnp.concatenate([
            fox_f_bias[l], mlstm_f_bias[l], zb(LANES - FOX_HEADS - ML_HEADS),
            zb(GATE_ML), mlstm_i_bias[l], zb(LANES - GATE_ML - ML_HEADS)]).reshape(1, 2 * LANES)

        xf, qat, ka, fvt, qkv, mkt, fc, gr = _in_proj_call(
            xf, moe_tt, g2_prev, norm_mix[l].reshape(1, d), sc1, sh1, wm, wg, gb,
            conv_w[l], conv_b[l].reshape(1, -1), bsz=bsz, seq=seq)
        hf = _fox_call(qat, ka, fvt, fox_out_norm[l].reshape(1, FOX_WIDTH), bsz=bsz, seq=seq)
        hm = _mlstm_call(qkv, mkt, fc, gr, mlstm_out_norm[l].reshape(1, ML_WIDTH), bsz=bsz, seq=seq)
        xf, hn_tt, gsel, grp, rank, tot = _post_call(
            xf, hf, hm, w_out[l].astype(BF16), g1, norm_ffn[l].reshape(1, d), sc2, sh2, wr, br, tri,
            bsz=bsz, seq=seq, chunk=chunk, te=te)

        n_chunks = t // chunk
        grp = grp.reshape(n_chunks, 1, chunk)
        rank = rank.reshape(n_chunks, 1, chunk)
        off, tile_blk, tile_grp = _moe_tiles(tot[:, :N_GROUPS, 0], tm=tm, tiles_per_chunk=rows // tm)
        pos = rank
        for g in range(N_GROUPS):
            pos = pos + jnp.where(grp == g, off[:, g].reshape(n_chunks, 1, 1), 0)
        xb, gs = _scatter_call(pos, hn_tt, gsel, chunk=chunk, rows=rows, tm=tm)
        ys_tt = _experts_call(tile_blk, tile_grp, xb, gs, w_gate[l].astype(BF16),
                              w_up[l].astype(BF16), w_down[l].astype(BF16), tm=tm)
        moe_tt = _gather_call(pos, ys_tt, chunk=chunk, rows=rows, n_tok=t)
        g2_prev = g2

    out = _final_call(xf, moe_tt, g2_prev, norm_final.reshape(1, d), seq=seq)
    return out.reshape(bsz, seq, d)
```

```python
import functools

import numpy as np
import jax
import jax.numpy as jnp
from jax import lax
from jax.experimental import pallas as pl
from jax.experimental.pallas import tpu as pltpu

F32 = jnp.float32
BF16 = jnp.bfloat16

LANES = 128
SUBLANES = 8
VMEM_BYTES_V7X = 64 * 1024 * 1024

D_MODEL = 1024
FOX_HEADS = 8
FOX_HEAD_DIM = 64
FOX_WIDTH = FOX_HEADS * FOX_HEAD_DIM
ML_HEADS = 4
ML_HEAD_DIM = 128
ML_WIDTH = ML_HEADS * ML_HEAD_DIM
CONV_WIDTH = 4
N_EXPERTS = 16
N_GROUPS = 4
EXPERTS_PER_GROUP = 4
D_FF = 512
N_ADA = 6
EPS = 1e-6
NEG = -1e30

FOX_Q = 0
FOX_F = 3 * FOX_WIDTH
ML_Q = FOX_F + FOX_HEADS
ML_I = ML_Q + 3 * ML_WIDTH
ML_F = ML_I + ML_HEADS
ML_O = ML_F + ML_HEADS
IN_COLS = ML_O + ML_WIDTH

MAIN_COLS = 7 * 512
BLK_FQ, BLK_FK, BLK_FV, BLK_MQ, BLK_MK, BLK_MV, BLK_MO = range(7)
REST_COLS = 3 * 512
OUT_MQ, OUT_MV, OUT_MO = range(3)
AUG_COLS = FOX_HEADS * 128
BIAS_TERMS = 3
VT_PAD = 16
LOG2E = 1.4426950408889634
GATE_FOX = 0
GATE_ML = FOX_HEADS
GATE_ROWS = 16

TOK_ROWS = D_MODEL // LANES


def _cparams(n_grid, vmem_mb):
    return pltpu.CompilerParams(
        dimension_semantics=("arbitrary",) * n_grid,
        vmem_limit_bytes=vmem_mb * 1024 * 1024)


def _silu(x):
    return x * jax.nn.sigmoid(x)


def _log_sigmoid(z):
    return jnp.minimum(z, 0.0) - jnp.log1p(jnp.exp(-jnp.abs(z)))


def _cumsum_rows(x):
    n = x.shape[0]
    row = lax.broadcasted_iota(jnp.int32, x.shape, 0)
    s = 1
    while s < n:
        x = x + jnp.where(row >= s, pltpu.roll(x, s, axis=0), 0.0)
        s *= 2
    return x


def _from_token_tiles(ref, n_tok, row0=0):
    return jnp.concatenate(
        [ref[pl.ds(row0 * TOK_ROWS + c, n_tok, stride=TOK_ROWS), :] for c in range(TOK_ROWS)], axis=1)


def _to_token_tiles(ref, val, row0=0):
    n = val.shape[0]
    for c in range(TOK_ROWS):
        ref[pl.ds(row0 * TOK_ROWS + c, n, stride=TOK_ROWS), :] = val[:, c * LANES:(c + 1) * LANES]


def _ada_kernel(c_ref, w_ref, b_ref, o_ref):
    c = c_ref[...]
    o_ref[0] = jnp.dot(_silu(c), w_ref[0], preferred_element_type=F32,
                       precision=lax.Precision.HIGHEST) + b_ref[0]


def _ada_call(c, w_ada, b_ada):
    depth, d, n = w_ada.shape
    bsz = c.shape[0]
    tn = 1536
    return pl.pallas_call(
        _ada_kernel,
        grid=(depth, n // tn),
        in_specs=[
            pl.BlockSpec((bsz, d), lambda l, j: (0, 0)),
            pl.BlockSpec((1, d, tn), lambda l, j: (l, 0, j)),
            pl.BlockSpec((1, 1, tn), lambda l, j: (l, 0, j)),
        ],
        out_specs=pl.BlockSpec((1, bsz, tn), lambda l, j: (l, 0, j)),
        out_shape=jax.ShapeDtypeStruct((depth, bsz, n), F32),
        compiler_params=_cparams(2, 32),
        name="ada_mod",
    )(c, w_ada, b_ada.reshape(depth, 1, n))


def _in_proj_kernel(x_ref, nm_ref, sc_ref, sh_ref, wm_ref, wg_ref, gb_ref, cw_ref, cb_ref,
                    place_ref, ones_ref,
                    qa_ref, ka_ref, vt_ref, qkv_ref, kt_ref, fc_ref, gr_ref, fcar, ccar, *, tm):
    @pl.when(pl.program_id(1) == 0)
    def _():
        fcar[...] = jnp.zeros_like(fcar)
        ccar[...] = jnp.zeros_like(ccar)

    x = x_ref[...]
    ms = jnp.mean(x * x, axis=1, keepdims=True)
    hn = x * lax.rsqrt(ms + EPS) * nm_ref[...]
    hn = hn * (1.0 + sc_ref[0]) + sh_ref[0]
    hb = hn.astype(BF16)

    gp = jnp.dot(hb, wg_ref[...], preferred_element_type=F32) + gb_ref[...]
    lf = _log_sigmoid(gp[:, :LANES])
    fcum = _cumsum_rows(lf) + fcar[...]
    fcar[...] = fcum[tm - 1:tm, :]
    gml = gp[:, LANES:] - fcum
    fc_ref[...] = fcum
    gr_ref[0] = gml.T[:GATE_ROWS, :]

    def mm(j):
        return jnp.dot(hb, wm_ref[:, j * 512:(j + 1) * 512], preferred_element_type=F32)

    def put(j, v):
        qkv_ref[:, j * 512:(j + 1) * 512] = v.astype(BF16)

    lane = lax.broadcasted_iota(jnp.int32, (tm, LANES), 1)
    fs = fcum * LOG2E
    hi = fs.astype(BF16).astype(F32)
    mid = (fs - hi).astype(BF16).astype(F32)
    low = (fs - hi - mid).astype(BF16).astype(F32)
    packed = jnp.where(lane < FOX_HEADS, hi,
                       jnp.where(lane < 2 * FOX_HEADS, pltpu.roll(mid, FOX_HEADS, axis=1),
                                 jnp.where(lane < 3 * FOX_HEADS, pltpu.roll(low, 2 * FOX_HEADS, axis=1), 0.0)))
    bias = jnp.dot(packed.astype(BF16), place_ref[...], preferred_element_type=F32) + ones_ref[...]

    def put_heads(ref, val, col0, transposed):
        lo = lane < FOX_HEAD_DIM
        for p in range(FOX_HEADS // 2):
            slab = val[:, p * LANES:(p + 1) * LANES]
            for h, data in ((2 * p, slab), (2 * p + 1, pltpu.roll(slab, FOX_HEAD_DIM, axis=1))):
                blk = jnp.where(lo, data, 0.0) + bias[:, col0 + h * LANES:col0 + (h + 1) * LANES]
                if transposed:
                    ref[0, h * LANES:(h + 1) * LANES, :] = blk.T.astype(BF16)
                else:
                    ref[:, h * LANES:(h + 1) * LANES] = blk.astype(BF16)

    u = jnp.concatenate([mm(BLK_MQ), mm(BLK_MK)], axis=1)
    prev = ccar[...]
    ccar[...] = u[tm - SUBLANES:tm, :]
    row8 = lax.broadcasted_iota(jnp.int32, prev.shape, 0)
    y = cb_ref[...] + cw_ref[CONV_WIDTH - 1:CONV_WIDTH, :] * u
    for k in range(1, CONV_WIDTH):
        r = pltpu.roll(u, k, axis=0)
        top = jnp.where(row8 < k, pltpu.roll(prev, k, axis=0), r[:SUBLANES])
        shifted = jnp.concatenate([top, r[SUBLANES:]], axis=0)
        y = y + cw_ref[CONV_WIDTH - 1 - k:CONV_WIDTH - k, :] * shifted
    act = _silu(y)
    put(OUT_MQ, act[:, :ML_WIDTH])
    kt_ref[0] = (act[:, ML_WIDTH:] * (ML_HEAD_DIM ** -0.5)).T.astype(BF16)

    put(OUT_MO, jax.nn.sigmoid(mm(BLK_MO)))
    put(OUT_MV, mm(BLK_MV))
    vt_ref[0] = mm(BLK_FV).astype(BF16).T
    put_heads(qa_ref, mm(BLK_FQ) * (FOX_HEAD_DIM ** -0.5 * LOG2E), 0, True)
    put_heads(ka_ref, mm(BLK_FK), AUG_COLS, False)


def _bias_placement():
    place = np.zeros((LANES, 2 * AUG_COLS), np.float32)
    ones = np.zeros((1, 2 * AUG_COLS), np.float32)
    for h in range(FOX_HEADS):
        for term in range(BIAS_TERMS):
            src = term * FOX_HEADS + h
            place[src, h * LANES + FOX_HEAD_DIM + term] = 1.0
            ones[0, h * LANES + FOX_HEAD_DIM + BIAS_TERMS + term] = 1.0
            place[src, AUG_COLS + h * LANES + FOX_HEAD_DIM + BIAS_TERMS + term] = -1.0
            ones[0, AUG_COLS + h * LANES + FOX_HEAD_DIM + term] = 1.0
    return jnp.asarray(place, BF16), jnp.asarray(ones, F32)


def _in_proj_call(x, nm, sc, sh, wm, wg, gb, cw, cb, *, bsz, seq, tm=512):
    t, d = x.shape
    ns = seq // tm
    row = lambda b, s: (b * ns + s, 0)
    per_b = lambda b, s: (b, 0, 0)
    const2 = lambda b, s: (0, 0)
    in_specs = [
        pl.BlockSpec((tm, d), row),
        pl.BlockSpec((1, d), const2),
        pl.BlockSpec((1, 1, d), per_b),
        pl.BlockSpec((1, 1, d), per_b),
        pl.BlockSpec((d, MAIN_COLS), const2),
        pl.BlockSpec((d, 2 * LANES), const2),
        pl.BlockSpec((1, 2 * LANES), const2),
        pl.BlockSpec((CONV_WIDTH, 2 * ML_WIDTH), const2),
        pl.BlockSpec((1, 2 * ML_WIDTH), const2),
        pl.BlockSpec((LANES, 2 * AUG_COLS), const2),
        pl.BlockSpec((1, 2 * AUG_COLS), const2),
    ]
    place, ones = _bias_placement()
    out_specs = [
        pl.BlockSpec((1, AUG_COLS, tm), lambda b, s: (b, 0, s)),
        pl.BlockSpec((tm, AUG_COLS), row),
        pl.BlockSpec((1, FOX_WIDTH, tm), lambda b, s: (b, 0, s)),
        pl.BlockSpec((tm, REST_COLS), row),
        pl.BlockSpec((1, ML_WIDTH, tm), lambda b, s: (b, 0, s)),
        pl.BlockSpec((tm, LANES), row),
        pl.BlockSpec((1, GATE_ROWS, tm), lambda b, s: (b, 0, s)),
    ]
    out_shape = [
        jax.ShapeDtypeStruct((bsz, AUG_COLS, seq), BF16),
        jax.ShapeDtypeStruct((t, AUG_COLS), BF16),
        jax.ShapeDtypeStruct((bsz, FOX_WIDTH, seq), BF16),
        jax.ShapeDtypeStruct((t, REST_COLS), BF16),
        jax.ShapeDtypeStruct((bsz, ML_WIDTH, seq), BF16),
        jax.ShapeDtypeStruct((t, LANES), F32),
        jax.ShapeDtypeStruct((bsz, GATE_ROWS, seq), F32),
    ]
    return pl.pallas_call(
        functools.partial(_in_proj_kernel, tm=tm),
        grid=(bsz, ns),
        in_specs=in_specs,
        out_specs=out_specs,
        out_shape=out_shape,
        scratch_shapes=[pltpu.VMEM((1, LANES), F32), pltpu.VMEM((SUBLANES, 2 * ML_WIDTH), F32)],
        compiler_params=_cparams(2, 48),
        name="in_proj",
    )(x, nm, sc, sh, wm, wg, gb, cw, cb, place, ones)


def _fox_kernel(qt_ref, k_ref, vt_ref, ng_ref, cm_ref, o_ref, m_sc, acc_sc, *, tq, tk, nh):
    qi = pl.program_id(2)
    ones_rows = jnp.where(lax.broadcasted_iota(jnp.int32, (VT_PAD, tk), 0) == 0, 1.0, 0.0).astype(BF16)
    for h in range(nh):
        m_sc[h] = jnp.full((SUBLANES, tq), NEG, F32)
        acc_sc[h] = jnp.zeros((FOX_HEAD_DIM + VT_PAD, tq), F32)

    def body(j, carry):
        k0 = pl.multiple_of(j * tk, tk)
        slot = (j == qi).astype(jnp.int32)

        def logits(h):
            hs = slice(h * LANES, (h + 1) * LANES)
            return (jnp.dot(k_ref[pl.ds(k0, tk), hs], qt_ref[0, hs, :], preferred_element_type=F32)
                    + cm_ref[slot])

        ahead = 2
        zs = [logits(h) for h in range(min(ahead, nh))]
        for h in range(nh):
            ds_ = slice(h * FOX_HEAD_DIM, (h + 1) * FOX_HEAD_DIM)
            z = zs[h]
            if h + ahead < nh:
                zs.append(logits(h + ahead))
            m_prev = m_sc[h]
            m_new = jnp.maximum(m_prev, jnp.max(z, axis=0, keepdims=True))
            alpha = jnp.exp2(m_prev - m_new)
            p = jnp.exp2(z - m_new[:1, :])
            va = jnp.concatenate([vt_ref[0, ds_, pl.ds(k0, tk)], ones_rows], axis=0)
            acc_sc[h] = alpha[:1, :] * acc_sc[h] + jnp.dot(va, p.astype(BF16), preferred_element_type=F32)
            m_sc[h] = m_new
        return carry

    lax.fori_loop(0, qi + 1, body, 0)

    for p in range(nh // 2):
        outs = []
        for h in (2 * p, 2 * p + 1):
            acc = acc_sc[h]
            num = acc[:FOX_HEAD_DIM, :]
            l = acc[FOX_HEAD_DIM:FOX_HEAD_DIM + 1, :]
            ms = jnp.mean(num * num, axis=0, keepdims=True)
            outs.append(num * lax.rsqrt(ms + EPS * l * l))
        ps = slice(p * LANES, (p + 1) * LANES)
        o_ref[:, ps] = (jnp.concatenate(outs, axis=0).T * ng_ref[:, ps]).astype(BF16)


def _fox_call(qat, ka, fvt, ng, *, bsz, seq, tq=512, nh=8):
    t = ka.shape[0]
    nq = seq // tq
    ngrp = FOX_HEADS // nh
    vw = nh * FOX_HEAD_DIM
    causal = jnp.where(jnp.arange(tq)[:, None] <= jnp.arange(tq)[None, :], 0.0, NEG).astype(F32)
    cmask = jnp.stack([jnp.zeros_like(causal), causal])
    return pl.pallas_call(
        functools.partial(_fox_kernel, tq=tq, tk=tq, nh=nh),
        grid=(bsz, ngrp, nq),
        in_specs=[
            pl.BlockSpec((1, nh * LANES, tq), lambda b, p, i: (b, p, i)),
            pl.BlockSpec((seq, nh * LANES), lambda b, p, i: (b, p)),
            pl.BlockSpec((1, vw, seq), lambda b, p, i: (b, p, 0)),
            pl.BlockSpec((1, vw), lambda b, p, i: (0, p)),
            pl.BlockSpec((2, tq, tq), lambda b, p, i: (0, 0, 0)),
        ],
        out_specs=pl.BlockSpec((tq, vw), lambda b, p, i: (b * nq + i, p)),
        out_shape=jax.ShapeDtypeStruct((t, FOX_WIDTH), BF16),
        scratch_shapes=[pltpu.VMEM((nh, SUBLANES, tq), F32),
                        pltpu.VMEM((nh, FOX_HEAD_DIM + VT_PAD, tq), F32)],
        compiler_params=_cparams(3, 48),
        name="fox_attn",
    )(qat, ka, fvt, ng, cmask)


def _mlstm_kernel(q_ref, kt_ref, v_ref, og_ref, fc_ref, gr_ref, ng_ref, o_ref, ct_sc, u_sc, *, ch):
    @pl.when(pl.program_id(1) == 0)
    def _():
        ct_sc[...] = jnp.zeros_like(ct_sc)
        u_sc[...] = jnp.zeros_like(u_sc)

    causal = (lax.broadcasted_iota(jnp.int32, (ch, ch), 1)
              <= lax.broadcasted_iota(jnp.int32, (ch, ch), 0))
    lane = lax.broadcasted_iota(jnp.int32, (ch, LANES), 1)
    fc = fc_ref[...]
    for h in range(ML_HEADS):
        sl = slice(h * ML_HEAD_DIM, (h + 1) * ML_HEAD_DIM)
        gl = GATE_ML + h
        q = q_ref[:, sl]
        kt = kt_ref[0, sl, :]
        vp = jnp.concatenate([v_ref[:, sl], jnp.where(lane == gl, 1.0, 0.0).astype(BF16)], axis=1)
        g_row = gr_ref[0, gl:gl + 1, :]
        u_prev = u_sc[h][:, :1]
        gm = jnp.where(causal, g_row, NEG)
        u_i = jnp.maximum(u_prev, jnp.max(gm, axis=1, keepdims=True))
        dmat = jnp.exp(gm - u_i)
        s = jnp.dot(q, kt, preferred_element_type=F32)
        scores = (s * dmat).astype(BF16)
        inter = jnp.exp(u_prev - u_i)
        ct = ct_sc[h]
        nd = (jnp.dot(scores, vp, preferred_element_type=F32)
              + jnp.dot(q, ct.astype(BF16), preferred_element_type=F32) * inter)
        num = nd[:, :ML_HEAD_DIM]
        den = jnp.maximum(jnp.abs(nd[:, ML_HEAD_DIM:]), jnp.exp(-(jnp.where(lane == gl, fc, 0.0) + u_i)))
        ms = jnp.mean(num * num, axis=1, keepdims=True)
        scale = lax.rsqrt(ms + EPS * den * den)[:, gl:gl + 1]
        y = num * scale * ng_ref[:, sl] * og_ref[:, sl].astype(F32)
        o_ref[:, sl] = y.astype(BF16)
        u_new = jnp.maximum(u_prev, jnp.max(g_row, axis=1, keepdims=True))
        ktw = (kt.astype(F32) * jnp.exp(g_row - u_new)).astype(BF16)
        ct_sc[h] = jnp.exp(u_prev - u_new) * ct + jnp.dot(ktw, vp, preferred_element_type=F32)
        u_sc[h] = jnp.broadcast_to(u_new, (1, LANES))


def _mlstm_call(qkv, mkt, fc, gr, ng, *, bsz, seq, ch=256):
    t = qkv.shape[0]
    nc = seq // ch
    row = lambda b, c: (b * nc + c, 0)
    return pl.pallas_call(
        functools.partial(_mlstm_kernel, ch=ch),
        grid=(bsz, nc),
        in_specs=[
            pl.BlockSpec((ch, ML_WIDTH), lambda b, c: (b * nc + c, OUT_MQ)),
            pl.BlockSpec((1, ML_WIDTH, ch), lambda b, c: (b, 0, c)),
            pl.BlockSpec((ch, ML_WIDTH), lambda b, c: (b * nc + c, OUT_MV)),
            pl.BlockSpec((ch, ML_WIDTH), lambda b, c: (b * nc + c, OUT_MO)),
            pl.BlockSpec((ch, LANES), row),
            pl.BlockSpec((1, GATE_ROWS, ch), lambda b, c: (b, 0, c)),
            pl.BlockSpec((1, ML_WIDTH), lambda b, c: (0, 0)),
        ],
        out_specs=pl.BlockSpec((ch, ML_WIDTH), row),
        out_shape=jax.ShapeDtypeStruct((t, ML_WIDTH), BF16),
        scratch_shapes=[pltpu.VMEM((ML_HEADS, ML_HEAD_DIM, 2 * ML_HEAD_DIM), F32),
                        pltpu.VMEM((ML_HEADS, 1, LANES), F32)],
        compiler_params=_cparams(2, 32),
        name="mlstm",
    )(qkv, mkt, qkv, qkv, fc, gr, ng)


def _post_kernel(x_ref, hf_ref, hm_ref, wo_ref, g1_ref, nf_ref, sc_ref, sh_ref, wr_ref, br_ref, tri_ref,
                 x1_ref, hn_ref, gsel_ref, grp_ref, rank_ref, tot_ref, cnt_sc, *, te, steps_per_chunk):
    @pl.when(pl.program_id(0) % steps_per_chunk == 0)
    def _():
        cnt_sc[...] = jnp.zeros_like(cnt_sc)

    mix = (jnp.dot(hf_ref[...], wo_ref[:FOX_WIDTH, :], preferred_element_type=F32)
           + jnp.dot(hm_ref[...], wo_ref[FOX_WIDTH:, :], preferred_element_type=F32))
    x1 = x_ref[...] + g1_ref[0] * mix
    x1_ref[...] = x1
    ms = jnp.mean(x1 * x1, axis=1, keepdims=True)
    hn = x1 * lax.rsqrt(ms + EPS) * nf_ref[...]
    hn = hn * (1.0 + sc_ref[0]) + sh_ref[0]
    _to_token_tiles(hn_ref, hn)

    hn_hi = hn.astype(BF16)
    hn_lo = (hn - hn_hi.astype(F32)).astype(BF16)
    l2 = jnp.dot(hn_hi, wr_ref[...], preferred_element_type=F32)
    logits = (l2[:, :LANES] + l2[:, LANES:]
              + jnp.dot(hn_lo, wr_ref[:, :LANES], preferred_element_type=F32))
    aff = jax.nn.sigmoid(logits.T[:N_EXPERTS, :])
    sel = aff + br_ref[...]
    selr = [sel[e:e + 1, :] for e in range(N_EXPERTS)]
    affr = [aff[e:e + 1, :] for e in range(N_EXPERTS)]
    keep = [None] * N_EXPERTS
    score = []
    for g in range(N_GROUPS):
        vs = selr[g * EXPERTS_PER_GROUP:(g + 1) * EXPERTS_PER_GROUP]
        sg = jnp.zeros_like(vs[0])
        for i in range(EXPERTS_PER_GROUP):
            beaten = jnp.zeros_like(vs[0])
            for j in range(EXPERTS_PER_GROUP):
                if j != i:
                    b = (vs[j] >= vs[i]) if j < i else (vs[j] > vs[i])
                    beaten = beaten + jnp.where(b, 1.0, 0.0)
            kp = beaten < 2.0
            keep[g * EXPERTS_PER_GROUP + i] = kp
            sg = sg + jnp.where(kp, vs[i], 0.0)
        score.append(sg)
    chosen = []
    for g in range(N_GROUPS):
        lost = jnp.zeros_like(score[0])
        for g2 in range(N_GROUPS):
            if g2 != g:
                b = (score[g2] >= score[g]) if g2 < g else (score[g2] > score[g])
                lost = lost + jnp.where(b, 1.0, 0.0)
        chosen.append(jnp.where(lost < 0.5, 1.0, 0.0))
    wsel = []
    for i in range(EXPERTS_PER_GROUP):
        wi = jnp.zeros_like(score[0])
        for g in range(N_GROUPS):
            e = g * EXPERTS_PER_GROUP + i
            wi = wi + chosen[g] * jnp.where(keep[e], affr[e], 0.0)
        wsel.append(wi)
    wsum = wsel[0] + wsel[1] + wsel[2] + wsel[3]
    wsel = [w / wsum for w in wsel]

    row8 = lax.broadcasted_iota(jnp.int32, (SUBLANES, te), 0)
    gmat = jnp.zeros((SUBLANES, te), F32)
    wmat = jnp.zeros((SUBLANES, te), F32)
    grp = jnp.zeros_like(score[0])
    for g in range(N_GROUPS):
        gmat = jnp.where(row8 == g, chosen[g], gmat)
        wmat = jnp.where(row8 == g, wsel[g], wmat)
        grp = grp + g * chosen[g]
    pref = jnp.dot(gmat.astype(BF16), tri_ref[...], preferred_element_type=F32) + cnt_sc[:, :1]
    rank = jnp.sum(gmat * pref, axis=0, keepdims=True)
    tot = cnt_sc[:, :1] + jnp.sum(gmat, axis=1, keepdims=True)
    cnt_sc[...] = jnp.broadcast_to(tot, cnt_sc.shape)
    rank_ref[0] = rank.astype(jnp.int32)
    grp_ref[0] = grp.astype(jnp.int32)
    tot_ref[0] = jnp.broadcast_to(tot, (SUBLANES, LANES)).astype(jnp.int32)
    wfull = jnp.concatenate([wmat, jnp.zeros((LANES - SUBLANES, te), F32)], axis=0)
    gsel_ref[...] = wfull.T


def _post_call(x, hf, hm, wo, g1, nf, sc, sh, wr, br, tri, *, bsz, seq, chunk, te=512):
    t, d = x.shape
    steps_per_chunk = chunk // te
    per_seq = seq // te
    n_steps = t // te
    n_chunks = t // chunk
    row = lambda i: (i, 0)
    per_b = lambda i: (i // per_seq, 0, 0)
    const2 = lambda i: (0, 0)
    return pl.pallas_call(
        functools.partial(_post_kernel, te=te, steps_per_chunk=steps_per_chunk),
        grid=(n_steps,),
        in_specs=[
            pl.BlockSpec((te, d), row),
            pl.BlockSpec((te, FOX_WIDTH), row),
            pl.BlockSpec((te, ML_WIDTH), row),
            pl.BlockSpec((d, d), const2),
            pl.BlockSpec((1, 1, d), per_b),
            pl.BlockSpec((1, d), const2),
            pl.BlockSpec((1, 1, d), per_b),
            pl.BlockSpec((1, 1, d), per_b),
            pl.BlockSpec((d, 2 * LANES), const2),
            pl.BlockSpec((N_EXPERTS, 1), const2),
            pl.BlockSpec((te, te), const2),
        ],
        out_specs=[
            pl.BlockSpec((te, d), row),
            pl.BlockSpec((te * TOK_ROWS, LANES), row),
            pl.BlockSpec((te, LANES), row),
            pl.BlockSpec((1, 1, te), lambda i: (i, 0, 0)),
            pl.BlockSpec((1, 1, te), lambda i: (i, 0, 0)),
            pl.BlockSpec((1, SUBLANES, LANES), lambda i: (i // steps_per_chunk, 0, 0)),
        ],
        out_shape=[
            jax.ShapeDtypeStruct((t, d), F32),
            jax.ShapeDtypeStruct((t * TOK_ROWS, LANES), F32),
            jax.ShapeDtypeStruct((t, LANES), F32),
            jax.ShapeDtypeStruct((n_steps, 1, te), jnp.int32),
            jax.ShapeDtypeStruct((n_steps, 1, te), jnp.int32),
            jax.ShapeDtypeStruct((n_chunks, SUBLANES, LANES), jnp.int32),
        ],
        scratch_shapes=[pltpu.VMEM((SUBLANES, LANES), F32)],
        compiler_params=_cparams(1, 48),
        name="post_router",
    )(x, hf, hm, wo, g1, nf, sc, sh, wr, br, tri)


def _scatter_kernel(pos_ref, hn_ref, gsel_ref, xb_ref, gs_ref, xs_sc, *, chunk, rows, tm):
    xs_sc[...] = jnp.zeros_like(xs_sc)
    gs_ref[...] = jnp.zeros_like(gs_ref)

    def body(t, carry):
        p = pos_ref[0, 0, t]
        src = pl.multiple_of(t * TOK_ROWS, TOK_ROWS)
        dst = pl.multiple_of(p * TOK_ROWS, TOK_ROWS)
        xs_sc[pl.ds(dst, TOK_ROWS), :] = hn_ref[pl.ds(src, TOK_ROWS), :]
        gs_ref[pl.ds(p, 1), :] = gsel_ref[pl.ds(t, 1), :]
        return carry

    lax.fori_loop(0, chunk, body, 0, unroll=8)
    for j in range(rows // tm):
        for c in range(TOK_ROWS):
            xb_ref[j * tm:(j + 1) * tm, c * LANES:(c + 1) * LANES] = (
                xs_sc[pl.ds(j * tm * TOK_ROWS + c, tm, stride=TOK_ROWS), :].astype(BF16))


def _scatter_call(pos, hn_tt, gsel, *, chunk, rows, tm):
    t = gsel.shape[0]
    n_chunks = t // chunk
    return pl.pallas_call(
        functools.partial(_scatter_kernel, chunk=chunk, rows=rows, tm=tm),
        grid=(n_chunks,),
        in_specs=[
            pl.BlockSpec((1, 1, chunk), lambda c: (c, 0, 0), memory_space=pltpu.SMEM),
            pl.BlockSpec((chunk * TOK_ROWS, LANES), lambda c: (c, 0)),
            pl.BlockSpec((chunk, LANES), lambda c: (c, 0)),
        ],
        out_specs=[
            pl.BlockSpec((rows, D_MODEL), lambda c: (c, 0)),
            pl.BlockSpec((rows, LANES), lambda c: (c, 0)),
        ],
        out_shape=[
            jax.ShapeDtypeStruct((n_chunks * rows, D_MODEL), BF16),
            jax.ShapeDtypeStruct((n_chunks * rows, LANES), F32),
        ],
        scratch_shapes=[pltpu.VMEM((rows * TOK_ROWS, LANES), F32)],
        compiler_params=_cparams(1, 56),
        name="moe_scatter",
    )(pos, hn_tt, gsel)


def _experts_kernel(blk_ref, grp_ref, xb_ref, gs_ref, wg_ref, wu_ref, wd_ref, y_ref, *, tm):
    g = grp_ref[pl.program_id(0)]

    @pl.when(g < N_GROUPS)
    def _():
        x = xb_ref[...]
        gs = gs_ref[...]
        acts = []
        for i in range(EXPERTS_PER_GROUP):
            hg = jnp.dot(x, wg_ref[i], preferred_element_type=F32)
            hu = jnp.dot(x, wu_ref[i], preferred_element_type=F32)
            acts.append((_silu(hg) * hu * gs[:, i:i + 1]).astype(BF16))
        y = jnp.dot(jnp.concatenate(acts, axis=1), wd_ref[0], preferred_element_type=F32)
        _to_token_tiles(y_ref, y)

    @pl.when(g >= N_GROUPS)
    def _():
        y_ref[...] = jnp.zeros_like(y_ref)


def _experts_call(tile_blk, tile_grp, xb, gs, wg, wu, wd, *, tm):
    n_rows = xb.shape[0]
    n_slots = n_rows // tm
    wmap = lambda s, blk, grp: (jnp.minimum(grp[s], N_GROUPS - 1), 0, 0)
    grid_spec = pltpu.PrefetchScalarGridSpec(
        num_scalar_prefetch=2,
        grid=(n_slots,),
        in_specs=[
            pl.BlockSpec((tm, D_MODEL), lambda s, blk, grp: (blk[s], 0)),
            pl.BlockSpec((tm, LANES), lambda s, blk, grp: (blk[s], 0)),
            pl.BlockSpec((EXPERTS_PER_GROUP, D_MODEL, D_FF), wmap),
            pl.BlockSpec((EXPERTS_PER_GROUP, D_MODEL, D_FF), wmap),
            pl.BlockSpec((1, EXPERTS_PER_GROUP * D_FF, D_MODEL), wmap),
        ],
        out_specs=pl.BlockSpec((tm * TOK_ROWS, LANES), lambda s, blk, grp: (blk[s], 0)),
    )
    return pl.pallas_call(
        functools.partial(_experts_kernel, tm=tm),
        grid_spec=grid_spec,
        out_shape=jax.ShapeDtypeStruct((n_rows * TOK_ROWS, LANES), F32),
        compiler_params=_cparams(1, 48),
        name="moe_experts",
    )(tile_blk, tile_grp, xb, gs, wg, wu, wd)


def _gather_kernel(pos_ref, ys_ref, x_ref, g2_ref, nf_ref, o_ref, o_sc, *, part, sub, final):
    base = pl.program_id(1) * part

    def body(t, carry):
        p = pos_ref[0, 0, base + t]
        src = pl.multiple_of(p * TOK_ROWS, TOK_ROWS)
        dst = pl.multiple_of(t * TOK_ROWS, TOK_ROWS)
        o_sc[pl.ds(dst, TOK_ROWS), :] = ys_ref[pl.ds(src, TOK_ROWS), :]
        return carry

    lax.fori_loop(0, part, body, 0, unroll=8)
    for r in range(part // sub):
        rs = slice(r * sub, (r + 1) * sub)
        x = x_ref[rs, :] + g2_ref[0] * _from_token_tiles(o_sc, sub, r * sub)
        if final:
            ms = jnp.mean(x * x, axis=1, keepdims=True)
            x = x * lax.rsqrt(ms + EPS) * nf_ref[...]
        o_ref[rs, :] = x


def _gather_call(pos, ys_tt, x, g2, nf, *, chunk, rows, seq, final):
    n_tok, d = x.shape
    n_chunks = n_tok // chunk
    parts = 2
    part = chunk // parts
    return pl.pallas_call(
        functools.partial(_gather_kernel, part=part, sub=256, final=final),
        grid=(n_chunks, parts),
        in_specs=[
            pl.BlockSpec((1, 1, chunk), lambda c, s: (c, 0, 0), memory_space=pltpu.SMEM),
            pl.BlockSpec((rows * TOK_ROWS, LANES), lambda c, s: (c, 0)),
            pl.BlockSpec((part, d), lambda c, s: (c * parts + s, 0)),
            pl.BlockSpec((1, 1, d), lambda c, s: ((c * chunk + s * part) // seq, 0, 0)),
            pl.BlockSpec((1, d), lambda c, s: (0, 0)),
        ],
        out_specs=pl.BlockSpec((part, d), lambda c, s: (c * parts + s, 0)),
        out_shape=jax.ShapeDtypeStruct((n_tok, d), F32),
        scratch_shapes=[pltpu.VMEM((part * TOK_ROWS, LANES), F32)],
        compiler_params=_cparams(2, 56),
        name="moe_gather",
    )(pos, ys_tt, x, g2, nf)


def _moe_tiles(tot, *, tm, tiles_per_chunk):
    nt = (tot + tm - 1) // tm
    ts = jnp.cumsum(nt, axis=1) - nt
    off = (ts * tm).astype(jnp.int32)
    j = jnp.arange(tiles_per_chunk, dtype=jnp.int32)[None, :, None]
    inside = (j >= ts[:, None, :]) & (j < (ts + nt)[:, None, :])
    key = jnp.where(jnp.any(inside, axis=-1), jnp.argmax(inside, axis=-1), N_GROUPS).reshape(-1)
    order = jnp.argsort(key, stable=True).astype(jnp.int32)
    return off, order, key[order].astype(jnp.int32)


def kernel(x, c, w_in, conv_w, conv_b, fox_f_bias, mlstm_i_bias, mlstm_f_bias, fox_out_norm,
           mlstm_out_norm, w_out, w_ada, b_ada, norm_mix, norm_ffn, w_router, b_router, w_gate,
           w_up, w_down, norm_final):
    bsz, seq, d = x.shape
    depth = w_in.shape[0]
    t = bsz * seq
    assert d == D_MODEL and w_in.shape[-1] == IN_COLS
    chunk = min(2048, seq)
    tm = 272 if chunk == 2048 else 256
    rows = -(-(chunk + N_GROUPS * (tm - 1)) // tm) * tm
    te = min(512, seq)
    assert seq % 512 == 0 and seq % chunk == 0 and chunk % te == 0

    mods = _ada_call(c, w_ada, b_ada).reshape(depth, bsz, N_ADA, 1, d)
    xf = x.reshape(t, d)

    tri = (jnp.arange(te)[:, None] < jnp.arange(te)[None, :]).astype(BF16)
    wr_f = jnp.pad(w_router, ((0, 0), (0, LANES - N_EXPERTS))).astype(F32)
    wr_hi = wr_f.astype(BF16)
    wr = jnp.concatenate([wr_hi, (wr_f - wr_hi.astype(F32)).astype(BF16)], axis=1)
    br = b_router.reshape(N_EXPERTS, 1).astype(F32)

    for l in range(depth):
        sh1, sc1, g1, sh2, sc2, g2 = [mods[l, :, i] for i in range(N_ADA)]
        wl = w_in[l]
        wm = jnp.concatenate([wl[:, FOX_Q:FOX_F], wl[:, ML_Q:ML_I], wl[:, ML_O:IN_COLS]], axis=1).astype(BF16)
        zpad = lambda n: jnp.zeros((d, n), F32)
        wg = jnp.concatenate([
            wl[:, FOX_F:ML_Q], wl[:, ML_F:ML_O], zpad(LANES - FOX_HEADS - ML_HEADS),
            zpad(GATE_ML), wl[:, ML_I:ML_F], zpad(LANES - GATE_ML - ML_HEADS)], axis=1).astype(BF16)
        zb = lambda n: jnp.zeros((n,), F32)
        gb = jnp.concatenate([
            fox_f_bias[l], mlstm_f_bias[l], zb(LANES - FOX_HEADS - ML_HEADS),
            zb(GATE_ML), mlstm_i_bias[l], zb(LANES - GATE_ML - ML_HEADS)]).reshape(1, 2 * LANES)

        qat, ka, fvt, qkv, mkt, fc, gr = _in_proj_call(
            xf, norm_mix[l].reshape(1, d), sc1, sh1, wm, wg, gb,
            conv_w[l], conv_b[l].reshape(1, -1), bsz=bsz, seq=seq)
        hf = _fox_call(qat, ka, fvt, fox_out_norm[l].reshape(1, FOX_WIDTH), bsz=bsz, seq=seq)
        hm = _mlstm_call(qkv, mkt, fc, gr, mlstm_out_norm[l].reshape(1, ML_WIDTH), bsz=bsz, seq=seq)
        xf, hn_tt, gsel, grp, rank, tot = _post_call(
            xf, hf, hm, w_out[l].astype(BF16), g1, norm_ffn[l].reshape(1, d), sc2, sh2, wr, br, tri,
            bsz=bsz, seq=seq, chunk=chunk, te=te)

        n_chunks = t // chunk
        grp = grp.reshape(n_chunks, 1, chunk)
        rank = rank.reshape(n_chunks, 1, chunk)
        off, tile_blk, tile_grp = _moe_tiles(tot[:, :N_GROUPS, 0], tm=tm, tiles_per_chunk=rows // tm)
        pos = rank
        for g in range(N_GROUPS):
            pos = pos + jnp.where(grp == g, off[:, g].reshape(n_chunks, 1, 1), 0)
        xb, gs = _scatter_call(pos, hn_tt, gsel, chunk=chunk, rows=rows, tm=tm)
        ys_tt = _experts_call(tile_blk, tile_grp, xb, gs, w_gate[l].astype(BF16),
                              w_up[l].astype(BF16),
                              w_down[l].astype(BF16).reshape(N_GROUPS, EXPERTS_PER_GROUP * D_FF, d), tm=tm)
        xf = _gather_call(pos, ys_tt, xf, g2, norm_final.reshape(1, d), chunk=chunk, rows=rows,
                          seq=seq, final=(l == depth - 1))

    return xf.reshape(bsz, seq, d)
```

```python
import functools

import numpy as np
import jax
import jax.numpy as jnp
from jax import lax
from jax.experimental import pallas as pl
from jax.experimental.pallas import tpu as pltpu

F32 = jnp.float32
BF16 = jnp.bfloat16

LANES = 128
SUBLANES = 8
VMEM_BYTES_V7X = 64 * 1024 * 1024

D_MODEL = 1024
FOX_HEADS = 8
FOX_HEAD_DIM = 64
FOX_WIDTH = FOX_HEADS * FOX_HEAD_DIM
ML_HEADS = 4
ML_HEAD_DIM = 128
ML_WIDTH = ML_HEADS * ML_HEAD_DIM
CONV_WIDTH = 4
N_EXPERTS = 16
N_GROUPS = 4
EXPERTS_PER_GROUP = 4
D_FF = 512
N_ADA = 6
EPS = 1e-6
NEG = -1e30

FOX_Q = 0
FOX_F = 3 * FOX_WIDTH
ML_Q = FOX_F + FOX_HEADS
ML_I = ML_Q + 3 * ML_WIDTH
ML_F = ML_I + ML_HEADS
ML_O = ML_F + ML_HEADS
IN_COLS = ML_O + ML_WIDTH

MAIN_COLS = 7 * 512
BLK_FQ, BLK_FK, BLK_FV, BLK_MQ, BLK_MK, BLK_MV, BLK_MO = range(7)
REST_COLS = 3 * 512
OUT_MQ, OUT_MV, OUT_MO = range(3)
AUG_COLS = FOX_HEADS * 128
BIAS_TERMS = 3
VT_PAD = 16
LOG2E = 1.4426950408889634
GATE_FOX = 0
GATE_ML = FOX_HEADS
GATE_ROWS = 16

TOK_ROWS = D_MODEL // LANES


def _cparams(n_grid, vmem_mb):
    return pltpu.CompilerParams(
        dimension_semantics=("arbitrary",) * n_grid,
        vmem_limit_bytes=vmem_mb * 1024 * 1024)


def _silu(x):
    return x * jax.nn.sigmoid(x)


def _log_sigmoid(z):
    return jnp.minimum(z, 0.0) - jnp.log1p(jnp.exp(-jnp.abs(z)))


def _cumsum_rows(x):
    n = x.shape[0]
    row = lax.broadcasted_iota(jnp.int32, x.shape, 0)
    s = 1
    while s < n:
        x = x + jnp.where(row >= s, pltpu.roll(x, s, axis=0), 0.0)
        s *= 2
    return x


def _from_token_tiles(ref, n_tok):
    return jnp.concatenate(
        [ref[pl.ds(c, n_tok, stride=TOK_ROWS), :] for c in range(TOK_ROWS)], axis=1)


def _to_token_tiles(ref, val, row0=0):
    n = val.shape[0]
    for c in range(TOK_ROWS):
        ref[pl.ds(row0 * TOK_ROWS + c, n, stride=TOK_ROWS), :] = val[:, c * LANES:(c + 1) * LANES]


def _ada_kernel(c_ref, w_ref, b_ref, o_ref):
    c = c_ref[...]
    o_ref[0] = jnp.dot(_silu(c), w_ref[0], preferred_element_type=F32,
                       precision=lax.Precision.HIGHEST) + b_ref[0]


def _ada_call(c, w_ada, b_ada):
    depth, d, n = w_ada.shape
    bsz = c.shape[0]
    tn = 1536
    return pl.pallas_call(
        _ada_kernel,
        grid=(depth, n // tn),
        in_specs=[
            pl.BlockSpec((bsz, d), lambda l, j: (0, 0)),
            pl.BlockSpec((1, d, tn), lambda l, j: (l, 0, j)),
            pl.BlockSpec((1, 1, tn), lambda l, j: (l, 0, j)),
        ],
        out_specs=pl.BlockSpec((1, bsz, tn), lambda l, j: (l, 0, j)),
        out_shape=jax.ShapeDtypeStruct((depth, bsz, n), F32),
        compiler_params=_cparams(2, 32),
        name="ada_mod",
    )(c, w_ada, b_ada.reshape(depth, 1, n))


def _in_proj_kernel(*refs, tm, combine):
    if combine:
        (x_ref, ott_ref, g2_ref, nm_ref, sc_ref, sh_ref, wm_ref, wg_ref, gb_ref, cw_ref, cb_ref,
         place_ref, ones_ref,
         xn_ref, qa_ref, ka_ref, vt_ref, qkv_ref, kt_ref, fc_ref, gr_ref, fcar, ccar) = refs
    else:
        (x_ref, nm_ref, sc_ref, sh_ref, wm_ref, wg_ref, gb_ref, cw_ref, cb_ref,
         place_ref, ones_ref,
         qa_ref, ka_ref, vt_ref, qkv_ref, kt_ref, fc_ref, gr_ref, fcar, ccar) = refs

    @pl.when(pl.program_id(1) == 0)
    def _():
        fcar[...] = jnp.zeros_like(fcar)
        ccar[...] = jnp.zeros_like(ccar)

    x = x_ref[...]
    if combine:
        x = x + g2_ref[0] * _from_token_tiles(ott_ref, tm)
        xn_ref[...] = x
    ms = jnp.mean(x * x, axis=1, keepdims=True)
    hn = x * lax.rsqrt(ms + EPS) * nm_ref[...]
    hn = hn * (1.0 + sc_ref[0]) + sh_ref[0]
    hb = hn.astype(BF16)

    gp = jnp.dot(hb, wg_ref[...], preferred_element_type=F32) + gb_ref[...]
    lf = _log_sigmoid(gp[:, :LANES])
    fcum = _cumsum_rows(lf) + fcar[...]
    fcar[...] = fcum[tm - 1:tm, :]
    gml = gp[:, LANES:] - fcum
    fc_ref[...] = fcum
    gr_ref[0] = gml.T[:GATE_ROWS, :]

    def mm(j):
        return jnp.dot(hb, wm_ref[:, j * 512:(j + 1) * 512], preferred_element_type=F32)

    def put(j, v):
        qkv_ref[:, j * 512:(j + 1) * 512] = v.astype(BF16)

    lane = lax.broadcasted_iota(jnp.int32, (tm, LANES), 1)
    fs = fcum * LOG2E
    hi = fs.astype(BF16).astype(F32)
    mid = (fs - hi).astype(BF16).astype(F32)
    low = (fs - hi - mid).astype(BF16).astype(F32)
    packed = jnp.where(lane < FOX_HEADS, hi,
                       jnp.where(lane < 2 * FOX_HEADS, pltpu.roll(mid, FOX_HEADS, axis=1),
                                 jnp.where(lane < 3 * FOX_HEADS, pltpu.roll(low, 2 * FOX_HEADS, axis=1), 0.0)))
    bias = jnp.dot(packed.astype(BF16), place_ref[...], preferred_element_type=F32) + ones_ref[...]

    def put_heads(ref, val, col0, transposed):
        lo = lane < FOX_HEAD_DIM
        for p in range(FOX_HEADS // 2):
            slab = val[:, p * LANES:(p + 1) * LANES]
            for h, data in ((2 * p, slab), (2 * p + 1, pltpu.roll(slab, FOX_HEAD_DIM, axis=1))):
                blk = jnp.where(lo, data, 0.0) + bias[:, col0 + h * LANES:col0 + (h + 1) * LANES]
                if transposed:
                    ref[0, 0, h * LANES:(h + 1) * LANES, :] = blk.T.astype(BF16)
                else:
                    ref[:, h * LANES:(h + 1) * LANES] = blk.astype(BF16)

    u = jnp.concatenate([mm(BLK_MQ), mm(BLK_MK)], axis=1)
    prev = ccar[...]
    ccar[...] = u[tm - SUBLANES:tm, :]
    row8 = lax.broadcasted_iota(jnp.int32, prev.shape, 0)
    y = cb_ref[...] + cw_ref[CONV_WIDTH - 1:CONV_WIDTH, :] * u
    for k in range(1, CONV_WIDTH):
        r = pltpu.roll(u, k, axis=0)
        top = jnp.where(row8 < k, pltpu.roll(prev, k, axis=0), r[:SUBLANES])
        shifted = jnp.concatenate([top, r[SUBLANES:]], axis=0)
        y = y + cw_ref[CONV_WIDTH - 1 - k:CONV_WIDTH - k, :] * shifted
    act = _silu(y)
    put(OUT_MQ, act[:, :ML_WIDTH])
    kt_ref[0, 0] = (act[:, ML_WIDTH:] * (ML_HEAD_DIM ** -0.5)).T.astype(BF16)

    put(OUT_MO, jax.nn.sigmoid(mm(BLK_MO)))
    put(OUT_MV, mm(BLK_MV))
    vt_ref[0, 0] = mm(BLK_FV).astype(BF16).T
    put_heads(qa_ref, mm(BLK_FQ) * (FOX_HEAD_DIM ** -0.5 * LOG2E), 0, True)
    put_heads(ka_ref, mm(BLK_FK), AUG_COLS, False)


def _bias_placement():
    place = np.zeros((LANES, 2 * AUG_COLS), np.float32)
    ones = np.zeros((1, 2 * AUG_COLS), np.float32)
    for h in range(FOX_HEADS):
        for term in range(BIAS_TERMS):
            src = term * FOX_HEADS + h
            place[src, h * LANES + FOX_HEAD_DIM + term] = 1.0
            ones[0, h * LANES + FOX_HEAD_DIM + BIAS_TERMS + term] = 1.0
            place[src, AUG_COLS + h * LANES + FOX_HEAD_DIM + BIAS_TERMS + term] = -1.0
            ones[0, AUG_COLS + h * LANES + FOX_HEAD_DIM + term] = 1.0
    return jnp.asarray(place, BF16), jnp.asarray(ones, F32)


def _in_proj_call(x, moe_tt, g2, nm, sc, sh, wm, wg, gb, cw, cb, *, bsz, seq, tm=512):
    t, d = x.shape
    ns = seq // tm
    combine = moe_tt is not None
    row = lambda b, s: (b * ns + s, 0)
    per_b = lambda b, s: (b, 0, 0)
    const2 = lambda b, s: (0, 0)
    in_specs = [pl.BlockSpec((tm, d), row)]
    args = [x]
    if combine:
        in_specs += [pl.BlockSpec((tm * TOK_ROWS, LANES), row), pl.BlockSpec((1, 1, d), per_b)]
        args += [moe_tt, g2]
    in_specs += [
        pl.BlockSpec((1, d), const2),
        pl.BlockSpec((1, 1, d), per_b),
        pl.BlockSpec((1, 1, d), per_b),
        pl.BlockSpec((d, MAIN_COLS), const2),
        pl.BlockSpec((d, 2 * LANES), const2),
        pl.BlockSpec((1, 2 * LANES), const2),
        pl.BlockSpec((CONV_WIDTH, 2 * ML_WIDTH), const2),
        pl.BlockSpec((1, 2 * ML_WIDTH), const2),
        pl.BlockSpec((LANES, 2 * AUG_COLS), const2),
        pl.BlockSpec((1, 2 * AUG_COLS), const2),
    ]
    place, ones = _bias_placement()
    args += [nm, sc, sh, wm, wg, gb, cw, cb, place, ones]
    out_specs = []
    out_shape = []
    if combine:
        out_specs.append(pl.BlockSpec((tm, d), row))
        out_shape.append(jax.ShapeDtypeStruct((t, d), F32))
    tiled = lambda b, s: (b, s, 0, 0)
    out_specs += [
        pl.BlockSpec((1, 1, AUG_COLS, tm), tiled),
        pl.BlockSpec((tm, AUG_COLS), row),
        pl.BlockSpec((1, 1, FOX_WIDTH, tm), tiled),
        pl.BlockSpec((tm, REST_COLS), row),
        pl.BlockSpec((1, 1, ML_WIDTH, tm), tiled),
        pl.BlockSpec((tm, LANES), row),
        pl.BlockSpec((1, GATE_ROWS, tm), lambda b, s: (b, 0, s)),
    ]
    out_shape += [
        jax.ShapeDtypeStruct((bsz, ns, AUG_COLS, tm), BF16),
        jax.ShapeDtypeStruct((t, AUG_COLS), BF16),
        jax.ShapeDtypeStruct((bsz, ns, FOX_WIDTH, tm), BF16),
        jax.ShapeDtypeStruct((t, REST_COLS), BF16),
        jax.ShapeDtypeStruct((bsz, ns, ML_WIDTH, tm), BF16),
        jax.ShapeDtypeStruct((t, LANES), F32),
        jax.ShapeDtypeStruct((bsz, GATE_ROWS, seq), F32),
    ]
    outs = pl.pallas_call(
        functools.partial(_in_proj_kernel, tm=tm, combine=combine),
        grid=(bsz, ns),
        in_specs=in_specs,
        out_specs=out_specs,
        out_shape=out_shape,
        scratch_shapes=[pltpu.VMEM((1, LANES), F32), pltpu.VMEM((SUBLANES, 2 * ML_WIDTH), F32)],
        compiler_params=_cparams(2, 48),
        name="in_proj",
    )(*args)
    if combine:
        return outs
    return [x] + list(outs)


def _fox_kernel(qt_ref, k_ref, vt_ref, ng_ref, cm_ref, o_ref, m_sc, acc_sc, *, tq, tk, nh):
    qi = pl.program_id(2)
    ones_rows = jnp.where(lax.broadcasted_iota(jnp.int32, (VT_PAD, tk), 0) == 0, 1.0, 0.0).astype(BF16)
    for h in range(nh):
        m_sc[h] = jnp.full((SUBLANES, tq), NEG, F32)
        acc_sc[h] = jnp.zeros((FOX_HEAD_DIM + VT_PAD, tq), F32)

    def body(j, carry):
        k0 = pl.multiple_of(j * tk, tk)
        slot = (j == qi).astype(jnp.int32)

        def logits(h):
            hs = slice(h * LANES, (h + 1) * LANES)
            return (jnp.dot(k_ref[pl.ds(k0, tk), hs], qt_ref[0, 0, hs, :], preferred_element_type=F32)
                    + cm_ref[slot])

        ahead = 2
        zs = [logits(h) for h in range(min(ahead, nh))]
        for h in range(nh):
            ds_ = slice(h * FOX_HEAD_DIM, (h + 1) * FOX_HEAD_DIM)
            z = zs[h]
            if h + ahead < nh:
                zs.append(logits(h + ahead))
            m_prev = m_sc[h]
            m_new = jnp.maximum(m_prev, jnp.max(z, axis=0, keepdims=True))
            alpha = jnp.exp2(m_prev - m_new)
            p = jnp.exp2(z - m_new[:1, :])
            va = jnp.concatenate([vt_ref[0, j, ds_, :], ones_rows], axis=0)
            acc_sc[h] = alpha[:1, :] * acc_sc[h] + jnp.dot(va, p.astype(BF16), preferred_element_type=F32)
            m_sc[h] = m_new
        return carry

    lax.fori_loop(0, qi + 1, body, 0)

    for p in range(nh // 2):
        outs = []
        for h in (2 * p, 2 * p + 1):
            acc = acc_sc[h]
            num = acc[:FOX_HEAD_DIM, :]
            l = acc[FOX_HEAD_DIM:FOX_HEAD_DIM + 1, :]
            ms = jnp.mean(num * num, axis=0, keepdims=True)
            outs.append(num * lax.rsqrt(ms + EPS * l * l))
        ps = slice(p * LANES, (p + 1) * LANES)
        o_ref[:, ps] = (jnp.concatenate(outs, axis=0).T * ng_ref[:, ps]).astype(BF16)


def _fox_call(qat, ka, fvt, ng, *, bsz, seq, tq=512, nh=8):
    t = ka.shape[0]
    nq = seq // tq
    ngrp = FOX_HEADS // nh
    vw = nh * FOX_HEAD_DIM
    causal = jnp.where(jnp.arange(tq)[:, None] <= jnp.arange(tq)[None, :], 0.0, NEG).astype(F32)
    cmask = jnp.stack([jnp.zeros_like(causal), causal])
    return pl.pallas_call(
        functools.partial(_fox_kernel, tq=tq, tk=tq, nh=nh),
        grid=(bsz, ngrp, nq),
        in_specs=[
            pl.BlockSpec((1, 1, nh * LANES, tq), lambda b, p, i: (b, i, p, 0)),
            pl.BlockSpec((seq, nh * LANES), lambda b, p, i: (b, p)),
            pl.BlockSpec((1, nq, vw, tq), lambda b, p, i: (b, 0, p, 0)),
            pl.BlockSpec((1, vw), lambda b, p, i: (0, p)),
            pl.BlockSpec((2, tq, tq), lambda b, p, i: (0, 0, 0)),
        ],
        out_specs=pl.BlockSpec((tq, vw), lambda b, p, i: (b * nq + i, p)),
        out_shape=jax.ShapeDtypeStruct((t, FOX_WIDTH), BF16),
        scratch_shapes=[pltpu.VMEM((nh, SUBLANES, tq), F32),
                        pltpu.VMEM((nh, FOX_HEAD_DIM + VT_PAD, tq), F32)],
        compiler_params=_cparams(3, 48),
        name="fox_attn",
    )(qat, ka, fvt, ng, cmask)


def _mlstm_kernel(q_ref, kt_ref, v_ref, og_ref, fc_ref, gr_ref, ng_ref, o_ref, ct_sc, u_sc, *, ch, per_tile):
    @pl.when(pl.program_id(1) == 0)
    def _():
        ct_sc[...] = jnp.zeros_like(ct_sc)
        u_sc[...] = jnp.zeros_like(u_sc)

    causal = (lax.broadcasted_iota(jnp.int32, (ch, ch), 1)
              <= lax.broadcasted_iota(jnp.int32, (ch, ch), 0))
    lane = lax.broadcasted_iota(jnp.int32, (ch, LANES), 1)
    fc = fc_ref[...]
    koff = pl.multiple_of((pl.program_id(1) % per_tile) * ch, ch)
    for h in range(ML_HEADS):
        sl = slice(h * ML_HEAD_DIM, (h + 1) * ML_HEAD_DIM)
        gl = GATE_ML + h
        q = q_ref[:, sl]
        kt = kt_ref[0, 0, sl, pl.ds(koff, ch)]
        vp = jnp.concatenate([v_ref[:, sl], jnp.where(lane == gl, 1.0, 0.0).astype(BF16)], axis=1)
        g_row = gr_ref[0, gl:gl + 1, :]
        u_prev = u_sc[h][:, :1]
        gm = jnp.where(causal, g_row, NEG)
        u_i = jnp.maximum(u_prev, jnp.max(gm, axis=1, keepdims=True))
        dmat = jnp.exp(gm - u_i)
        s = jnp.dot(q, kt, preferred_element_type=F32)
        scores = (s * dmat).astype(BF16)
        inter = jnp.exp(u_prev - u_i)
        ct = ct_sc[h]
        nd = (jnp.dot(scores, vp, preferred_element_type=F32)
              + jnp.dot(q, ct.astype(BF16), preferred_element_type=F32) * inter)
        num = nd[:, :ML_HEAD_DIM]
        den = jnp.maximum(jnp.abs(nd[:, ML_HEAD_DIM:]), jnp.exp(-(jnp.where(lane == gl, fc, 0.0) + u_i)))
        ms = jnp.mean(num * num, axis=1, keepdims=True)
        scale = lax.rsqrt(ms + EPS * den * den)[:, gl:gl + 1]
        y = num * scale * ng_ref[:, sl] * og_ref[:, sl].astype(F32)
        o_ref[:, sl] = y.astype(BF16)
        u_new = jnp.maximum(u_prev, jnp.max(g_row, axis=1, keepdims=True))
        ktw = (kt.astype(F32) * jnp.exp(g_row - u_new)).astype(BF16)
        ct_sc[h] = jnp.exp(u_prev - u_new) * ct + jnp.dot(ktw, vp, preferred_element_type=F32)
        u_sc[h] = jnp.broadcast_to(u_new, (1, LANES))


def _mlstm_call(qkv, mkt, fc, gr, ng, *, bsz, seq, ch=256):
    t = qkv.shape[0]
    nc = seq // ch
    ktile = mkt.shape[-1]
    per_tile = ktile // ch
    row = lambda b, c: (b * nc + c, 0)
    return pl.pallas_call(
        functools.partial(_mlstm_kernel, ch=ch, per_tile=per_tile),
        grid=(bsz, nc),
        in_specs=[
            pl.BlockSpec((ch, ML_WIDTH), lambda b, c: (b * nc + c, OUT_MQ)),
            pl.BlockSpec((1, 1, ML_WIDTH, ktile), lambda b, c: (b, c // per_tile, 0, 0)),
            pl.BlockSpec((ch, ML_WIDTH), lambda b, c: (b * nc + c, OUT_MV)),
            pl.BlockSpec((ch, ML_WIDTH), lambda b, c: (b * nc + c, OUT_MO)),
            pl.BlockSpec((ch, LANES), row),
            pl.BlockSpec((1, GATE_ROWS, ch), lambda b, c: (b, 0, c)),
            pl.BlockSpec((1, ML_WIDTH), lambda b, c: (0, 0)),
        ],
        out_specs=pl.BlockSpec((ch, ML_WIDTH), row),
        out_shape=jax.ShapeDtypeStruct((t, ML_WIDTH), BF16),
        scratch_shapes=[pltpu.VMEM((ML_HEADS, ML_HEAD_DIM, 2 * ML_HEAD_DIM), F32),
                        pltpu.VMEM((ML_HEADS, 1, LANES), F32)],
        compiler_params=_cparams(2, 32),
        name="mlstm",
    )(qkv, mkt, qkv, qkv, fc, gr, ng)


def _post_kernel(x_ref, hf_ref, hm_ref, wo_ref, g1_ref, nf_ref, sc_ref, sh_ref, wr_ref, br_ref, tri_ref,
                 x1_ref, hn_ref, gsel_ref, grp_ref, rank_ref, tot_ref, cnt_sc, *, te, steps_per_chunk):
    @pl.when(pl.program_id(0) % steps_per_chunk == 0)
    def _():
        cnt_sc[...] = jnp.zeros_like(cnt_sc)

    mix = (jnp.dot(hf_ref[...], wo_ref[:FOX_WIDTH, :], preferred_element_type=F32)
           + jnp.dot(hm_ref[...], wo_ref[FOX_WIDTH:, :], preferred_element_type=F32))
    x1 = x_ref[...] + g1_ref[0] * mix
    x1_ref[...] = x1
    ms = jnp.mean(x1 * x1, axis=1, keepdims=True)
    hn = x1 * lax.rsqrt(ms + EPS) * nf_ref[...]
    hn = hn * (1.0 + sc_ref[0]) + sh_ref[0]
    _to_token_tiles(hn_ref, hn)

    hn_hi = hn.astype(BF16)
    hn_lo = (hn - hn_hi.astype(F32)).astype(BF16)
    l2 = jnp.dot(hn_hi, wr_ref[...], preferred_element_type=F32)
    logits = (l2[:, :LANES] + l2[:, LANES:]
              + jnp.dot(hn_lo, wr_ref[:, :LANES], preferred_element_type=F32))
    aff = jax.nn.sigmoid(logits.T[:N_EXPERTS, :])
    sel = aff + br_ref[...]
    selr = [sel[e:e + 1, :] for e in range(N_EXPERTS)]
    affr = [aff[e:e + 1, :] for e in range(N_EXPERTS)]
    keep = [None] * N_EXPERTS
    score = []
    for g in range(N_GROUPS):
        vs = selr[g * EXPERTS_PER_GROUP:(g + 1) * EXPERTS_PER_GROUP]
        sg = jnp.zeros_like(vs[0])
        for i in range(EXPERTS_PER_GROUP):
            beaten = jnp.zeros_like(vs[0])
            for j in range(EXPERTS_PER_GROUP):
                if j != i:
                    b = (vs[j] >= vs[i]) if j < i else (vs[j] > vs[i])
                    beaten = beaten + jnp.where(b, 1.0, 0.0)
            kp = beaten < 2.0
            keep[g * EXPERTS_PER_GROUP + i] = kp
            sg = sg + jnp.where(kp, vs[i], 0.0)
        score.append(sg)
    chosen = []
    for g in range(N_GROUPS):
        lost = jnp.zeros_like(score[0])
        for g2 in range(N_GROUPS):
            if g2 != g:
                b = (score[g2] >= score[g]) if g2 < g else (score[g2] > score[g])
                lost = lost + jnp.where(b, 1.0, 0.0)
        chosen.append(jnp.where(lost < 0.5, 1.0, 0.0))
    wsel = []
    for i in range(EXPERTS_PER_GROUP):
        wi = jnp.zeros_like(score[0])
        for g in range(N_GROUPS):
            e = g * EXPERTS_PER_GROUP + i
            wi = wi + chosen[g] * jnp.where(keep[e], affr[e], 0.0)
        wsel.append(wi)
    wsum = wsel[0] + wsel[1] + wsel[2] + wsel[3]
    wsel = [w / wsum for w in wsel]

    row8 = lax.broadcasted_iota(jnp.int32, (SUBLANES, te), 0)
    gmat = jnp.zeros((SUBLANES, te), F32)
    wmat = jnp.zeros((SUBLANES, te), F32)
    grp = jnp.zeros_like(score[0])
    for g in range(N_GROUPS):
        gmat = jnp.where(row8 == g, chosen[g], gmat)
        wmat = jnp.where(row8 == g, wsel[g], wmat)
        grp = grp + g * chosen[g]
    pref = jnp.dot(gmat.astype(BF16), tri_ref[...], preferred_element_type=F32) + cnt_sc[:, :1]
    rank = jnp.sum(gmat * pref, axis=0, keepdims=True)
    tot = cnt_sc[:, :1] + jnp.sum(gmat, axis=1, keepdims=True)
    cnt_sc[...] = jnp.broadcast_to(tot, cnt_sc.shape)
    rank_ref[0] = rank.astype(jnp.int32)
    grp_ref[0] = grp.astype(jnp.int32)
    tot_ref[0] = jnp.broadcast_to(tot, (SUBLANES, LANES)).astype(jnp.int32)
    wfull = jnp.concatenate([wmat, jnp.zeros((LANES - SUBLANES, te), F32)], axis=0)
    gsel_ref[...] = wfull.T


def _post_call(x, hf, hm, wo, g1, nf, sc, sh, wr, br, tri, *, bsz, seq, chunk, te=512):
    t, d = x.shape
    steps_per_chunk = chunk // te
    per_seq = seq // te
    n_steps = t // te
    n_chunks = t // chunk
    row = lambda i: (i, 0)
    per_b = lambda i: (i // per_seq, 0, 0)
    const2 = lambda i: (0, 0)
    return pl.pallas_call(
        functools.partial(_post_kernel, te=te, steps_per_chunk=steps_per_chunk),
        grid=(n_steps,),
        in_specs=[
            pl.BlockSpec((te, d), row),
            pl.BlockSpec((te, FOX_WIDTH), row),
            pl.BlockSpec((te, ML_WIDTH), row),
            pl.BlockSpec((d, d), const2),
            pl.BlockSpec((1, 1, d), per_b),
            pl.BlockSpec((1, d), const2),
            pl.BlockSpec((1, 1, d), per_b),
            pl.BlockSpec((1, 1, d), per_b),
            pl.BlockSpec((d, 2 * LANES), const2),
            pl.BlockSpec((N_EXPERTS, 1), const2),
            pl.BlockSpec((te, te), const2),
        ],
        out_specs=[
            pl.BlockSpec((te, d), row),
            pl.BlockSpec((te * TOK_ROWS, LANES), row),
            pl.BlockSpec((te, LANES), row),
            pl.BlockSpec((1, 1, te), lambda i: (i, 0, 0)),
            pl.BlockSpec((1, 1, te), lambda i: (i, 0, 0)),
            pl.BlockSpec((1, SUBLANES, LANES), lambda i: (i // steps_per_chunk, 0, 0)),
        ],
        out_shape=[
            jax.ShapeDtypeStruct((t, d), F32),
            jax.ShapeDtypeStruct((t * TOK_ROWS, LANES), F32),
            jax.ShapeDtypeStruct((t, LANES), F32),
            jax.ShapeDtypeStruct((n_steps, 1, te), jnp.int32),
            jax.ShapeDtypeStruct((n_steps, 1, te), jnp.int32),
            jax.ShapeDtypeStruct((n_chunks, SUBLANES, LANES), jnp.int32),
        ],
        scratch_shapes=[pltpu.VMEM((SUBLANES, LANES), F32)],
        compiler_params=_cparams(1, 48),
        name="post_router",
    )(x, hf, hm, wo, g1, nf, sc, sh, wr, br, tri)


def _scatter_kernel(pos_ref, hn_ref, gsel_ref, xb_ref, gs_ref, xs_sc, *, chunk, rows, tm):
    xs_sc[...] = jnp.zeros_like(xs_sc)
    gs_ref[...] = jnp.zeros_like(gs_ref)

    def body(t, carry):
        p = pos_ref[0, 0, t]
        src = pl.multiple_of(t * TOK_ROWS, TOK_ROWS)
        dst = pl.multiple_of(p * TOK_ROWS, TOK_ROWS)
        xs_sc[pl.ds(dst, TOK_ROWS), :] = hn_ref[pl.ds(src, TOK_ROWS), :]
        gs_ref[pl.ds(p, 1), :] = gsel_ref[pl.ds(t, 1), :]
        return carry

    lax.fori_loop(0, chunk, body, 0, unroll=8)
    for j in range(rows // tm):
        for c in range(TOK_ROWS):
            xb_ref[j * tm:(j + 1) * tm, c * LANES:(c + 1) * LANES] = (
                xs_sc[pl.ds(j * tm * TOK_ROWS + c, tm, stride=TOK_ROWS), :].astype(BF16))


def _scatter_call(pos, hn_tt, gsel, *, chunk, rows, tm):
    t = gsel.shape[0]
    n_chunks = t // chunk
    return pl.pallas_call(
        functools.partial(_scatter_kernel, chunk=chunk, rows=rows, tm=tm),
        grid=(n_chunks,),
        in_specs=[
            pl.BlockSpec((1, 1, chunk), lambda c: (c, 0, 0), memory_space=pltpu.SMEM),
            pl.BlockSpec((chunk * TOK_ROWS, LANES), lambda c: (c, 0)),
            pl.BlockSpec((chunk, LANES), lambda c: (c, 0)),
        ],
        out_specs=[
            pl.BlockSpec((rows, D_MODEL), lambda c: (c, 0)),
            pl.BlockSpec((rows, LANES), lambda c: (c, 0)),
        ],
        out_shape=[
            jax.ShapeDtypeStruct((n_chunks * rows, D_MODEL), BF16),
            jax.ShapeDtypeStruct((n_chunks * rows, LANES), F32),
        ],
        scratch_shapes=[pltpu.VMEM((rows * TOK_ROWS, LANES), F32)],
        compiler_params=_cparams(1, 56),
        name="moe_scatter",
    )(pos, hn_tt, gsel)


def _experts_kernel(blk_ref, grp_ref, xb_ref, gs_ref, wg_ref, wu_ref, wd_ref, y_ref, *, tm):
    g = grp_ref[pl.program_id(0)]

    @pl.when(g < N_GROUPS)
    def _():
        x = xb_ref[...]
        gs = gs_ref[...]
        acts = []
        for i in range(EXPERTS_PER_GROUP):
            hg = jnp.dot(x, wg_ref[i], preferred_element_type=F32)
            hu = jnp.dot(x, wu_ref[i], preferred_element_type=F32)
            acts.append((_silu(hg) * hu * gs[:, i:i + 1]).astype(BF16))
        y = jnp.dot(jnp.concatenate(acts, axis=1), wd_ref[0], preferred_element_type=F32)
        _to_token_tiles(y_ref, y)

    @pl.when(g >= N_GROUPS)
    def _():
        y_ref[...] = jnp.zeros_like(y_ref)


def _experts_call(tile_blk, tile_grp, xb, gs, wg, wu, wd, *, tm):
    n_rows = xb.shape[0]
    n_slots = n_rows // tm
    wmap = lambda s, blk, grp: (jnp.minimum(grp[s], N_GROUPS - 1), 0, 0)
    grid_spec = pltpu.PrefetchScalarGridSpec(
        num_scalar_prefetch=2,
        grid=(n_slots,),
        in_specs=[
            pl.BlockSpec((tm, D_MODEL), lambda s, blk, grp: (blk[s], 0)),
            pl.BlockSpec((tm, LANES), lambda s, blk, grp: (blk[s], 0)),
            pl.BlockSpec((EXPERTS_PER_GROUP, D_MODEL, D_FF), wmap),
            pl.BlockSpec((EXPERTS_PER_GROUP, D_MODEL, D_FF), wmap),
            pl.BlockSpec((1, EXPERTS_PER_GROUP * D_FF, D_MODEL), wmap),
        ],
        out_specs=pl.BlockSpec((tm * TOK_ROWS, LANES), lambda s, blk, grp: (blk[s], 0)),
    )
    return pl.pallas_call(
        functools.partial(_experts_kernel, tm=tm),
        grid_spec=grid_spec,
        out_shape=jax.ShapeDtypeStruct((n_rows * TOK_ROWS, LANES), F32),
        compiler_params=_cparams(1, 48),
        name="moe_experts",
    )(tile_blk, tile_grp, xb, gs, wg, wu, wd)


def _gather_kernel(pos_ref, ys_ref, o_ref, *, chunk):
    def body(t, carry):
        p = pos_ref[0, 0, t]
        src = pl.multiple_of(p * TOK_ROWS, TOK_ROWS)
        dst = pl.multiple_of(t * TOK_ROWS, TOK_ROWS)
        o_ref[pl.ds(dst, TOK_ROWS), :] = ys_ref[pl.ds(src, TOK_ROWS), :]
        return carry

    lax.fori_loop(0, chunk, body, 0, unroll=8)


def _gather_call(pos, ys_tt, *, chunk, rows, n_tok):
    n_chunks = n_tok // chunk
    return pl.pallas_call(
        functools.partial(_gather_kernel, chunk=chunk),
        grid=(n_chunks,),
        in_specs=[
            pl.BlockSpec((1, 1, chunk), lambda c: (c, 0, 0), memory_space=pltpu.SMEM),
            pl.BlockSpec((rows * TOK_ROWS, LANES), lambda c: (c, 0)),
        ],
        out_specs=pl.BlockSpec((chunk * TOK_ROWS, LANES), lambda c: (c, 0)),
        out_shape=jax.ShapeDtypeStruct((n_tok * TOK_ROWS, LANES), F32),
        compiler_params=_cparams(1, 56),
        name="moe_gather",
    )(pos, ys_tt)


def _final_kernel(x_ref, ott_ref, g2_ref, nf_ref, o_ref, *, tm):
    x = x_ref[...] + g2_ref[0] * _from_token_tiles(ott_ref, tm)
    ms = jnp.mean(x * x, axis=1, keepdims=True)
    o_ref[...] = x * lax.rsqrt(ms + EPS) * nf_ref[...]


def _final_call(x, moe_tt, g2, nf, *, seq, tm=512):
    t, d = x.shape
    per_seq = seq // tm
    return pl.pallas_call(
        functools.partial(_final_kernel, tm=tm),
        grid=(t // tm,),
        in_specs=[
            pl.BlockSpec((tm, d), lambda i: (i, 0)),
            pl.BlockSpec((tm * TOK_ROWS, LANES), lambda i: (i, 0)),
            pl.BlockSpec((1, 1, d), lambda i: (i // per_seq, 0, 0)),
            pl.BlockSpec((1, d), lambda i: (0, 0)),
        ],
        out_specs=pl.BlockSpec((tm, d), lambda i: (i, 0)),
        out_shape=jax.ShapeDtypeStruct((t, d), F32),
        compiler_params=_cparams(1, 32),
        name="final_norm",
    )(x, moe_tt, g2, nf)


def _moe_tiles(tot, *, tm, tiles_per_chunk):
    nt = (tot + tm - 1) // tm
    ts = jnp.cumsum(nt, axis=1) - nt
    off = (ts * tm).astype(jnp.int32)
    j = jnp.arange(tiles_per_chunk, dtype=jnp.int32)[None, :, None]
    inside = (j >= ts[:, None, :]) & (j < (ts + nt)[:, None, :])
    key = jnp.where(jnp.any(inside, axis=-1), jnp.argmax(inside, axis=-1), N_GROUPS).reshape(-1)
    order = jnp.argsort(key, stable=True).astype(jnp.int32)
    return off, order, key[order].astype(jnp.int32)


def kernel(x, c, w_in, conv_w, conv_b, fox_f_bias, mlstm_i_bias, mlstm_f_bias, fox_out_norm,
           mlstm_out_norm, w_out, w_ada, b_ada, norm_mix, norm_ffn, w_router, b_router, w_gate,
           w_up, w_down, norm_final):
    bsz, seq, d = x.shape
    depth = w_in.shape[0]
    t = bsz * seq
    assert d == D_MODEL and w_in.shape[-1] == IN_COLS
    chunk = min(2048, seq)
    tm = 256
    rows = -(-(chunk + N_GROUPS * (tm - 1)) // tm) * tm
    te = min(512, seq)
    assert seq % 512 == 0 and seq % chunk == 0 and chunk % te == 0

    mods = _ada_call(c, w_ada, b_ada).reshape(depth, bsz, N_ADA, 1, d)
    xf = x.reshape(t, d)

    tri = (jnp.arange(te)[:, None] < jnp.arange(te)[None, :]).astype(BF16)
    wr_f = jnp.pad(w_router, ((0, 0), (0, LANES - N_EXPERTS))).astype(F32)
    wr_hi = wr_f.astype(BF16)
    wr = jnp.concatenate([wr_hi, (wr_f - wr_hi.astype(F32)).astype(BF16)], axis=1)
    br = b_router.reshape(N_EXPERTS, 1).astype(F32)

    moe_tt = None
    g2_prev = None
    for l in range(depth):
        sh1, sc1, g1, sh2, sc2, g2 = [mods[l, :, i] for i in range(N_ADA)]
        wl = w_in[l]
        wm = jnp.concatenate([wl[:, FOX_Q:FOX_F], wl[:, ML_Q:ML_I], wl[:, ML_O:IN_COLS]], axis=1).astype(BF16)
        zpad = lambda n: jnp.zeros((d, n), F32)
        wg = jnp.concatenate([
            wl[:, FOX_F:ML_Q], wl[:, ML_F:ML_O], zpad(LANES - FOX_HEADS - ML_HEADS),
            zpad(GATE_ML), wl[:, ML_I:ML_F], zpad(LANES - GATE_ML - ML_HEADS)], axis=1).astype(BF16)
        zb = lambda n: jnp.zeros((n,), F32)
        gb = jnp.concatenate([
            fox_f_bias[l], mlstm_f_bias[l], zb(LANES - FOX_HEADS - ML_HEADS),
            zb(GATE_ML), mlstm_i_bias[l], zb(LANES - GATE_ML - ML_HEADS)]).reshape(1, 2 * LANES)

        xf, qat, ka, fvt, qkv, mkt, fc, gr = _in_proj_call(
            xf, moe_tt, g2_prev, norm_mix[l].reshape(1, d), sc1, sh1, wm, wg, gb,
            conv_w[l], conv_b[l].reshape(1, -1), bsz=bsz, seq=seq)
        hf = _fox_call(qat, ka, fvt, fox_out_norm[l].reshape(1, FOX_WIDTH), bsz=bsz, seq=seq)
        hm = _mlstm_call(qkv, mkt, fc, gr, mlstm_out_norm[l].reshape(1, ML_WIDTH), bsz=bsz, seq=seq)
        xf, hn_tt, gsel, grp, rank, tot = _post_call(
            xf, hf, hm, w_out[l].astype(BF16), g1, norm_ffn[l].reshape(1, d), sc2, sh2, wr, br, tri,
            bsz=bsz, seq=seq, chunk=chunk, te=te)

        n_chunks = t // chunk
        grp = grp.reshape(n_chunks, 1, chunk)
        rank = rank.reshape(n_chunks, 1, chunk)
        off, tile_blk, tile_grp = _moe_tiles(tot[:, :N_GROUPS, 0], tm=tm, tiles_per_chunk=rows // tm)
        pos = rank
        for g in range(N_GROUPS):
            pos = pos + jnp.where(grp == g, off[:, g].reshape(n_chunks, 1, 1), 0)
        xb, gs = _scatter_call(pos, hn_tt, gsel, chunk=chunk, rows=rows, tm=tm)
        ys_tt = _experts_call(tile_blk, tile_grp, xb, gs, w_gate[l].astype(BF16),
                              w_up[l].astype(BF16),
                              w_down[l].astype(BF16).reshape(N_GROUPS, EXPERTS_PER_GROUP * D_FF, d), tm=tm)
        moe_tt = _gather_call(pos, ys_tt, chunk=chunk, rows=rows, n_tok=t)
        g2_prev = g2

    out = _final_call(xf, moe_tt, g2_prev, norm_final.reshape(1, d), seq=seq)
    return out.reshape(bsz, seq, d)
```

```python
import functools

import numpy as np
import jax
import jax.numpy as jnp
from jax import lax
from jax.experimental import pallas as pl
from jax.experimental.pallas import tpu as pltpu

F32 = jnp.float32
BF16 = jnp.bfloat16

LANES = 128
SUBLANES = 8
VMEM_BYTES_V7X = 64 * 1024 * 1024

D_MODEL = 1024
FOX_HEADS = 8
FOX_HEAD_DIM = 64
FOX_WIDTH = FOX_HEADS * FOX_HEAD_DIM
ML_HEADS = 4
ML_HEAD_DIM = 128
ML_WIDTH = ML_HEADS * ML_HEAD_DIM
CONV_WIDTH = 4
N_EXPERTS = 16
N_GROUPS = 4
EXPERTS_PER_GROUP = 4
D_FF = 512
N_ADA = 6
EPS = 1e-6
NEG = -1e30

FOX_Q = 0
FOX_F = 3 * FOX_WIDTH
ML_Q = FOX_F + FOX_HEADS
ML_I = ML_Q + 3 * ML_WIDTH
ML_F = ML_I + ML_HEADS
ML_O = ML_F + ML_HEADS
IN_COLS = ML_O + ML_WIDTH

MAIN_COLS = 7 * 512
BLK_FQ, BLK_FK, BLK_FV, BLK_MQ, BLK_MK, BLK_MV, BLK_MO = range(7)
REST_COLS = 3 * 512
OUT_MQ, OUT_MV, OUT_MO = range(3)
AUG_COLS = FOX_HEADS * 128
BIAS_TERMS = 3
VT_PAD = 16
LOG2E = 1.4426950408889634
GATE_FOX = 0
GATE_ML = FOX_HEADS
GATE_ROWS = 16

TOK_ROWS = D_MODEL // LANES


def _cparams(n_grid, vmem_mb):
    return pltpu.CompilerParams(
        dimension_semantics=("arbitrary",) * n_grid,
        vmem_limit_bytes=vmem_mb * 1024 * 1024)


def _silu(x):
    return x * jax.nn.sigmoid(x)


def _log_sigmoid(z):
    return jnp.minimum(z, 0.0) - jnp.log1p(jnp.exp(-jnp.abs(z)))


def _cumsum_rows(x):
    n = x.shape[0]
    row = lax.broadcasted_iota(jnp.int32, x.shape, 0)
    s = 1
    while s < n:
        x = x + jnp.where(row >= s, pltpu.roll(x, s, axis=0), 0.0)
        s *= 2
    return x


def _from_token_tiles(ref, n_tok):
    return jnp.concatenate(
        [ref[pl.ds(c, n_tok, stride=TOK_ROWS), :] for c in range(TOK_ROWS)], axis=1)


def _to_token_tiles(ref, val, row0=0):
    n = val.shape[0]
    for c in range(TOK_ROWS):
        ref[pl.ds(row0 * TOK_ROWS + c, n, stride=TOK_ROWS), :] = val[:, c * LANES:(c + 1) * LANES]


def _ada_kernel(c_ref, w_ref, b_ref, o_ref):
    c = c_ref[...]
    o_ref[0] = jnp.dot(_silu(c), w_ref[0], preferred_element_type=F32,
                       precision=lax.Precision.HIGHEST) + b_ref[0]


def _ada_call(c, w_ada, b_ada):
    depth, d, n = w_ada.shape
    bsz = c.shape[0]
    tn = 1536
    return pl.pallas_call(
        _ada_kernel,
        grid=(depth, n // tn),
        in_specs=[
            pl.BlockSpec((bsz, d), lambda l, j: (0, 0)),
            pl.BlockSpec((1, d, tn), lambda l, j: (l, 0, j)),
            pl.BlockSpec((1, 1, tn), lambda l, j: (l, 0, j)),
        ],
        out_specs=pl.BlockSpec((1, bsz, tn), lambda l, j: (l, 0, j)),
        out_shape=jax.ShapeDtypeStruct((depth, bsz, n), F32),
        compiler_params=_cparams(2, 32),
        name="ada_mod",
    )(c, w_ada, b_ada.reshape(depth, 1, n))


def _in_proj_kernel(*refs, tm, combine):
    if combine:
        (x_ref, ott_ref, g2_ref, nm_ref, sc_ref, sh_ref, wm_ref, wg_ref, gb_ref, cw_ref, cb_ref,
         place_ref, ones_ref,
         xn_ref, qa_ref, ka_ref, vt_ref, qkv_ref, kt_ref, fc_ref, gr_ref, fcar, ccar) = refs
    else:
        (x_ref, nm_ref, sc_ref, sh_ref, wm_ref, wg_ref, gb_ref, cw_ref, cb_ref,
         place_ref, ones_ref,
         qa_ref, ka_ref, vt_ref, qkv_ref, kt_ref, fc_ref, gr_ref, fcar, ccar) = refs

    @pl.when(pl.program_id(1) == 0)
    def _():
        fcar[...] = jnp.zeros_like(fcar)
        ccar[...] = jnp.zeros_like(ccar)

    x = x_ref[...]
    if combine:
        x = x + g2_ref[0] * _from_token_tiles(ott_ref, tm)
        xn_ref[...] = x
    ms = jnp.mean(x * x, axis=1, keepdims=True)
    hn = x * lax.rsqrt(ms + EPS) * nm_ref[...]
    hn = hn * (1.0 + sc_ref[0]) + sh_ref[0]
    hb = hn.astype(BF16)

    gp = jnp.dot(hb, wg_ref[...], preferred_element_type=F32) + gb_ref[...]
    lf = _log_sigmoid(gp[:, :LANES])
    fcum = _cumsum_rows(lf) + fcar[...]
    fcar[...] = fcum[tm - 1:tm, :]
    gml = gp[:, LANES:] - fcum
    fc_ref[...] = fcum
    gr_ref[0] = gml.T[:GATE_ROWS, :]

    def mm(j):
        return jnp.dot(hb, wm_ref[:, j * 512:(j + 1) * 512], preferred_element_type=F32)

    def put(j, v):
        qkv_ref[:, j * 512:(j + 1) * 512] = v.astype(BF16)

    lane = lax.broadcasted_iota(jnp.int32, (tm, LANES), 1)
    fs = fcum * LOG2E
    hi = fs.astype(BF16).astype(F32)
    mid = (fs - hi).astype(BF16).astype(F32)
    low = (fs - hi - mid).astype(BF16).astype(F32)
    packed = jnp.where(lane < FOX_HEADS, hi,
                       jnp.where(lane < 2 * FOX_HEADS, pltpu.roll(mid, FOX_HEADS, axis=1),
                                 jnp.where(lane < 3 * FOX_HEADS, pltpu.roll(low, 2 * FOX_HEADS, axis=1), 0.0)))
    bias = jnp.dot(packed.astype(BF16), place_ref[...], preferred_element_type=F32) + ones_ref[...]

    def put_heads(ref, val, col0, transposed):
        lo = lane < FOX_HEAD_DIM
        for p in range(FOX_HEADS // 2):
            slab = val[:, p * LANES:(p + 1) * LANES]
            for h, data in ((2 * p, slab), (2 * p + 1, pltpu.roll(slab, FOX_HEAD_DIM, axis=1))):
                blk = jnp.where(lo, data, 0.0) + bias[:, col0 + h * LANES:col0 + (h + 1) * LANES]
                if transposed:
                    ref[0, 0, h * LANES:(h + 1) * LANES, :] = blk.T.astype(BF16)
                else:
                    ref[:, h * LANES:(h + 1) * LANES] = blk.astype(BF16)

    u = jnp.concatenate([mm(BLK_MQ), mm(BLK_MK)], axis=1)
    prev = ccar[...]
    ccar[...] = u[tm - SUBLANES:tm, :]
    row8 = lax.broadcasted_iota(jnp.int32, prev.shape, 0)
    y = cb_ref[...] + cw_ref[CONV_WIDTH - 1:CONV_WIDTH, :] * u
    for k in range(1, CONV_WIDTH):
        r = pltpu.roll(u, k, axis=0)
        top = jnp.where(row8 < k, pltpu.roll(prev, k, axis=0), r[:SUBLANES])
        shifted = jnp.concatenate([top, r[SUBLANES:]], axis=0)
        y = y + cw_ref[CONV_WIDTH - 1 - k:CONV_WIDTH - k, :] * shifted
    act = _silu(y)
    put(OUT_MQ, act[:, :ML_WIDTH])
    kt_ref[0, 0] = (act[:, ML_WIDTH:] * (ML_HEAD_DIM ** -0.5)).T.astype(BF16)

    put(OUT_MO, jax.nn.sigmoid(mm(BLK_MO)))
    put(OUT_MV, mm(BLK_MV))
    vt_ref[0, 0] = mm(BLK_FV).astype(BF16).T
    put_heads(qa_ref, mm(BLK_FQ) * (FOX_HEAD_DIM ** -0.5 * LOG2E), 0, True)
    put_heads(ka_ref, mm(BLK_FK), AUG_COLS, False)


def _bias_placement():
    place = np.zeros((LANES, 2 * AUG_COLS), np.float32)
    ones = np.zeros((1, 2 * AUG_COLS), np.float32)
    for h in range(FOX_HEADS):
        for term in range(BIAS_TERMS):
            src = term * FOX_HEADS + h
            place[src, h * LANES + FOX_HEAD_DIM + term] = 1.0
            ones[0, h * LANES + FOX_HEAD_DIM + BIAS_TERMS + term] = 1.0
            place[src, AUG_COLS + h * LANES + FOX_HEAD_DIM + BIAS_TERMS + term] = -1.0
            ones[0, AUG_COLS + h * LANES + FOX_HEAD_DIM + term] = 1.0
    return jnp.asarray(place, BF16), jnp.asarray(ones, F32)


def _in_proj_call(x, moe_tt, g2, nm, sc, sh, wm, wg, gb, cw, cb, *, bsz, seq, tm=512):
    t, d = x.shape
    ns = seq // tm
    combine = moe_tt is not None
    row = lambda b, s: (b * ns + s, 0)
    per_b = lambda b, s: (b, 0, 0)
    const2 = lambda b, s: (0, 0)
    in_specs = [pl.BlockSpec((tm, d), row)]
    args = [x]
    if combine:
        in_specs += [pl.BlockSpec((tm * TOK_ROWS, LANES), row), pl.BlockSpec((1, 1, d), per_b)]
        args += [moe_tt, g2]
    in_specs += [
        pl.BlockSpec((1, d), const2),
        pl.BlockSpec((1, 1, d), per_b),
        pl.BlockSpec((1, 1, d), per_b),
        pl.BlockSpec((d, MAIN_COLS), const2),
        pl.BlockSpec((d, 2 * LANES), const2),
        pl.BlockSpec((1, 2 * LANES), const2),
        pl.BlockSpec((CONV_WIDTH, 2 * ML_WIDTH), const2),
        pl.BlockSpec((1, 2 * ML_WIDTH), const2),
        pl.BlockSpec((LANES, 2 * AUG_COLS), const2),
        pl.BlockSpec((1, 2 * AUG_COLS), const2),
    ]
    place, ones = _bias_placement()
    args += [nm, sc, sh, wm, wg, gb, cw, cb, place, ones]
    out_specs = []
    out_shape = []
    if combine:
        out_specs.append(pl.BlockSpec((tm, d), row))
        out_shape.append(jax.ShapeDtypeStruct((t, d), F32))
    tiled = lambda b, s: (b, s, 0, 0)
    out_specs += [
        pl.BlockSpec((1, 1, AUG_COLS, tm), tiled),
        pl.BlockSpec((tm, AUG_COLS), row),
        pl.BlockSpec((1, 1, FOX_WIDTH, tm), tiled),
        pl.BlockSpec((tm, REST_COLS), row),
        pl.BlockSpec((1, 1, ML_WIDTH, tm), tiled),
        pl.BlockSpec((tm, LANES), row),
        pl.BlockSpec((1, GATE_ROWS, tm), lambda b, s: (b, 0, s)),
    ]
    out_shape += [
        jax.ShapeDtypeStruct((bsz, ns, AUG_COLS, tm), BF16),
        jax.ShapeDtypeStruct((t, AUG_COLS), BF16),
        jax.ShapeDtypeStruct((bsz, ns, FOX_WIDTH, tm), BF16),
        jax.ShapeDtypeStruct((t, REST_COLS), BF16),
        jax.ShapeDtypeStruct((bsz, ns, ML_WIDTH, tm), BF16),
        jax.ShapeDtypeStruct((t, LANES), F32),
        jax.ShapeDtypeStruct((bsz, GATE_ROWS, seq), F32),
    ]
    outs = pl.pallas_call(
        functools.partial(_in_proj_kernel, tm=tm, combine=combine),
        grid=(bsz, ns),
        in_specs=in_specs,
        out_specs=out_specs,
        out_shape=out_shape,
        scratch_shapes=[pltpu.VMEM((1, LANES), F32), pltpu.VMEM((SUBLANES, 2 * ML_WIDTH), F32)],
        compiler_params=_cparams(2, 48),
        name="in_proj",
    )(*args)
    if combine:
        return outs
    return [x] + list(outs)


def _fox_kernel(qt_ref, k_ref, vt_ref, ng_ref, cm_ref, o_ref, m_sc, acc_sc, *, tq, tk, nh):
    qi = pl.program_id(2)
    ones_rows = jnp.where(lax.broadcasted_iota(jnp.int32, (VT_PAD, tk), 0) == 0, 1.0, 0.0).astype(BF16)
    for h in range(nh):
        m_sc[h] = jnp.full((SUBLANES, tq), NEG, F32)
        acc_sc[h] = jnp.zeros((FOX_HEAD_DIM + VT_PAD, tq), F32)

    def body(j, carry, diagonal):
        k0 = pl.multiple_of(j * tk, tk)

        def logits(h):
            hs = slice(h * LANES, (h + 1) * LANES)
            z = jnp.dot(k_ref[pl.ds(k0, tk), hs], qt_ref[0, 0, hs, :], preferred_element_type=F32)
            return z + cm_ref[...] if diagonal else z

        ahead = 2
        zs = [logits(h) for h in range(min(ahead, nh))]
        for h in range(nh):
            ds_ = slice(h * FOX_HEAD_DIM, (h + 1) * FOX_HEAD_DIM)
            z = zs[h]
            if h + ahead < nh:
                zs.append(logits(h + ahead))
            m_prev = m_sc[h]
            m_new = jnp.maximum(m_prev, jnp.max(z, axis=0, keepdims=True))
            alpha = jnp.exp2(m_prev - m_new)
            p = jnp.exp2(z - m_new[:1, :])
            va = jnp.concatenate([vt_ref[0, j, ds_, :], ones_rows], axis=0)
            acc_sc[h] = alpha[:1, :] * acc_sc[h] + jnp.dot(va, p.astype(BF16), preferred_element_type=F32)
            m_sc[h] = m_new
        return carry

    lax.fori_loop(0, qi, functools.partial(body, diagonal=False), 0)
    lax.fori_loop(qi, qi + 1, functools.partial(body, diagonal=True), 0)

    for p in range(nh // 2):
        outs = []
        for h in (2 * p, 2 * p + 1):
            acc = acc_sc[h]
            num = acc[:FOX_HEAD_DIM, :]
            l = acc[FOX_HEAD_DIM:FOX_HEAD_DIM + 1, :]
            ms = jnp.mean(num * num, axis=0, keepdims=True)
            outs.append(num * lax.rsqrt(ms + EPS * l * l))
        ps = slice(p * LANES, (p + 1) * LANES)
        o_ref[:, ps] = (jnp.concatenate(outs, axis=0).T * ng_ref[:, ps]).astype(BF16)


def _fox_call(qat, ka, fvt, ng, *, bsz, seq, tq=512, nh=8):
    t = ka.shape[0]
    nq = seq // tq
    ngrp = FOX_HEADS // nh
    vw = nh * FOX_HEAD_DIM
    cmask = jnp.where(jnp.arange(tq)[:, None] <= jnp.arange(tq)[None, :], 0.0, NEG).astype(F32)
    return pl.pallas_call(
        functools.partial(_fox_kernel, tq=tq, tk=tq, nh=nh),
        grid=(bsz, ngrp, nq),
        in_specs=[
            pl.BlockSpec((1, 1, nh * LANES, tq), lambda b, p, i: (b, i, p, 0)),
            pl.BlockSpec((seq, nh * LANES), lambda b, p, i: (b, p)),
            pl.BlockSpec((1, nq, vw, tq), lambda b, p, i: (b, 0, p, 0)),
            pl.BlockSpec((1, vw), lambda b, p, i: (0, p)),
            pl.BlockSpec((tq, tq), lambda b, p, i: (0, 0)),
        ],
        out_specs=pl.BlockSpec((tq, vw), lambda b, p, i: (b * nq + i, p)),
        out_shape=jax.ShapeDtypeStruct((t, FOX_WIDTH), BF16),
        scratch_shapes=[pltpu.VMEM((nh, SUBLANES, tq), F32),
                        pltpu.VMEM((nh, FOX_HEAD_DIM + VT_PAD, tq), F32)],
        compiler_params=_cparams(3, 48),
        name="fox_attn",
    )(qat, ka, fvt, ng, cmask)


def _mlstm_kernel(q_ref, kt_ref, v_ref, og_ref, fc_ref, gr_ref, ng_ref, o_ref, ct_sc, u_sc, *, ch, per_tile):
    @pl.when(pl.program_id(1) == 0)
    def _():
        ct_sc[...] = jnp.zeros_like(ct_sc)
        u_sc[...] = jnp.zeros_like(u_sc)

    causal = (lax.broadcasted_iota(jnp.int32, (ch, ch), 1)
              <= lax.broadcasted_iota(jnp.int32, (ch, ch), 0))
    lane = lax.broadcasted_iota(jnp.int32, (ch, LANES), 1)
    fc = fc_ref[...]
    koff = pl.multiple_of((pl.program_id(1) % per_tile) * ch, ch)
    for h in range(ML_HEADS):
        sl = slice(h * ML_HEAD_DIM, (h + 1) * ML_HEAD_DIM)
        gl = GATE_ML + h
        q = q_ref[:, sl]
        kt = kt_ref[0, 0, sl, pl.ds(koff, ch)]
        vp = jnp.concatenate([v_ref[:, sl], jnp.where(lane == gl, 1.0, 0.0).astype(BF16)], axis=1)
        g_row = gr_ref[0, gl:gl + 1, :]
        u_prev = u_sc[h][:, :1]
        gm = jnp.where(causal, g_row, NEG)
        u_i = jnp.maximum(u_prev, jnp.max(gm, axis=1, keepdims=True))
        dmat = jnp.exp(gm - u_i)
        s = jnp.dot(q, kt, preferred_element_type=F32)
        scores = (s * dmat).astype(BF16)
        inter = jnp.exp(u_prev - u_i)
        ct = ct_sc[h]
        nd = (jnp.dot(scores, vp, preferred_element_type=F32)
              + jnp.dot(q, ct.astype(BF16), preferred_element_type=F32) * inter)
        num = nd[:, :ML_HEAD_DIM]
        den = jnp.maximum(jnp.abs(nd[:, ML_HEAD_DIM:]), jnp.exp(-(jnp.where(lane == gl, fc, 0.0) + u_i)))
        ms = jnp.mean(num * num, axis=1, keepdims=True)
        scale = lax.rsqrt(ms + EPS * den * den)[:, gl:gl + 1]
        y = num * scale * ng_ref[:, sl] * og_ref[:, sl].astype(F32)
        o_ref[:, sl] = y.astype(BF16)
        u_new = jnp.maximum(u_prev, jnp.max(g_row, axis=1, keepdims=True))
        ktw = (kt.astype(F32) * jnp.exp(g_row - u_new)).astype(BF16)
        ct_sc[h] = jnp.exp(u_prev - u_new) * ct + jnp.dot(ktw, vp, preferred_element_type=F32)
        u_sc[h] = jnp.broadcast_to(u_new, (1, LANES))


def _mlstm_call(qkv, mkt, fc, gr, ng, *, bsz, seq, ch=256):
    t = qkv.shape[0]
    nc = seq // ch
    ktile = mkt.shape[-1]
    per_tile = ktile // ch
    row = lambda b, c: (b * nc + c, 0)
    return pl.pallas_call(
        functools.partial(_mlstm_kernel, ch=ch, per_tile=per_tile),
        grid=(bsz, nc),
        in_specs=[
            pl.BlockSpec((ch, ML_WIDTH), lambda b, c: (b * nc + c, OUT_MQ)),
            pl.BlockSpec((1, 1, ML_WIDTH, ktile), lambda b, c: (b, c // per_tile, 0, 0)),
            pl.BlockSpec((ch, ML_WIDTH), lambda b, c: (b * nc + c, OUT_MV)),
            pl.BlockSpec((ch, ML_WIDTH), lambda b, c: (b * nc + c, OUT_MO)),
            pl.BlockSpec((ch, LANES), row),
            pl.BlockSpec((1, GATE_ROWS, ch), lambda b, c: (b, 0, c)),
            pl.BlockSpec((1, ML_WIDTH), lambda b, c: (0, 0)),
        ],
        out_specs=pl.BlockSpec((ch, ML_WIDTH), row),
        out_shape=jax.ShapeDtypeStruct((t, ML_WIDTH), BF16),
        scratch_shapes=[pltpu.VMEM((ML_HEADS, ML_HEAD_DIM, 2 * ML_HEAD_DIM), F32),
                        pltpu.VMEM((ML_HEADS, 1, LANES), F32)],
        compiler_params=_cparams(2, 32),
        name="mlstm",
    )(qkv, mkt, qkv, qkv, fc, gr, ng)


def _post_kernel(x_ref, hf_ref, hm_ref, wo_ref, g1_ref, nf_ref, sc_ref, sh_ref, wr_ref, br_ref, tri_ref,
                 x1_ref, hn_ref, gsel_ref, grp_ref, rank_ref, tot_ref, cnt_sc, *, te, steps_per_chunk):
    @pl.when(pl.program_id(0) % steps_per_chunk == 0)
    def _():
        cnt_sc[...] = jnp.zeros_like(cnt_sc)

    mix = (jnp.dot(hf_ref[...], wo_ref[:FOX_WIDTH, :], preferred_element_type=F32)
           + jnp.dot(hm_ref[...], wo_ref[FOX_WIDTH:, :], preferred_element_type=F32))
    x1 = x_ref[...] + g1_ref[0] * mix
    x1_ref[...] = x1
    ms = jnp.mean(x1 * x1, axis=1, keepdims=True)
    hn = x1 * lax.rsqrt(ms + EPS) * nf_ref[...]
    hn = hn * (1.0 + sc_ref[0]) + sh_ref[0]
    _to_token_tiles(hn_ref, hn)

    hn_hi = hn.astype(BF16)
    hn_lo = (hn - hn_hi.astype(F32)).astype(BF16)
    l2 = jnp.dot(hn_hi, wr_ref[...], preferred_element_type=F32)
    logits = (l2[:, :LANES] + l2[:, LANES:]
              + jnp.dot(hn_lo, wr_ref[:, :LANES], preferred_element_type=F32))
    aff = jax.nn.sigmoid(logits.T[:N_EXPERTS, :])
    sel = aff + br_ref[...]
    selr = [sel[e:e + 1, :] for e in range(N_EXPERTS)]
    affr = [aff[e:e + 1, :] for e in range(N_EXPERTS)]
    keep = [None] * N_EXPERTS
    score = []
    for g in range(N_GROUPS):
        vs = selr[g * EXPERTS_PER_GROUP:(g + 1) * EXPERTS_PER_GROUP]
        sg = jnp.zeros_like(vs[0])
        for i in range(EXPERTS_PER_GROUP):
            beaten = jnp.zeros_like(vs[0])
            for j in range(EXPERTS_PER_GROUP):
                if j != i:
                    b = (vs[j] >= vs[i]) if j < i else (vs[j] > vs[i])
                    beaten = beaten + jnp.where(b, 1.0, 0.0)
            kp = beaten < 2.0
            keep[g * EXPERTS_PER_GROUP + i] = kp
            sg = sg + jnp.where(kp, vs[i], 0.0)
        score.append(sg)
    chosen = []
    for g in range(N_GROUPS):
        lost = jnp.zeros_like(score[0])
        for g2 in range(N_GROUPS):
            if g2 != g:
                b = (score[g2] >= score[g]) if g2 < g else (score[g2] > score[g])
                lost = lost + jnp.where(b, 1.0, 0.0)
        chosen.append(jnp.where(lost < 0.5, 1.0, 0.0))
    wsel = []
    for i in range(EXPERTS_PER_GROUP):
        wi = jnp.zeros_like(score[0])
        for g in range(N_GROUPS):
            e = g * EXPERTS_PER_GROUP + i
            wi = wi + chosen[g] * jnp.where(keep[e], affr[e], 0.0)
        wsel.append(wi)
    wsum = wsel[0] + wsel[1] + wsel[2] + wsel[3]
    wsel = [w / wsum for w in wsel]

    row8 = lax.broadcasted_iota(jnp.int32, (SUBLANES, te), 0)
    gmat = jnp.zeros((SUBLANES, te), F32)
    wmat = jnp.zeros((SUBLANES, te), F32)
    grp = jnp.zeros_like(score[0])
    for g in range(N_GROUPS):
        gmat = jnp.where(row8 == g, chosen[g], gmat)
        wmat = jnp.where(row8 == g, wsel[g], wmat)
        grp = grp + g * chosen[g]
    pref = jnp.dot(gmat.astype(BF16), tri_ref[...], preferred_element_type=F32) + cnt_sc[:, :1]
    rank = jnp.sum(gmat * pref, axis=0, keepdims=True)
    tot = cnt_sc[:, :1] + jnp.sum(gmat, axis=1, keepdims=True)
    cnt_sc[...] = jnp.broadcast_to(tot, cnt_sc.shape)
    rank_ref[0] = rank.astype(jnp.int32)
    grp_ref[0] = grp.astype(jnp.int32)
    tot_ref[0] = jnp.broadcast_to(tot, (SUBLANES, LANES)).astype(jnp.int32)
    wfull = jnp.concatenate([wmat, jnp.zeros((LANES - SUBLANES, te), F32)], axis=0)
    gsel_ref[...] = wfull.T


def _post_call(x, hf, hm, wo, g1, nf, sc, sh, wr, br, tri, *, bsz, seq, chunk, te=512):
    t, d = x.shape
    steps_per_chunk = chunk // te
    per_seq = seq // te
    n_steps = t // te
    n_chunks = t // chunk
    row = lambda i: (i, 0)
    per_b = lambda i: (i // per_seq, 0, 0)
    const2 = lambda i: (0, 0)
    return pl.pallas_call(
        functools.partial(_post_kernel, te=te, steps_per_chunk=steps_per_chunk),
        grid=(n_steps,),
        in_specs=[
            pl.BlockSpec((te, d), row),
            pl.BlockSpec((te, FOX_WIDTH), row),
            pl.BlockSpec((te, ML_WIDTH), row),
            pl.BlockSpec((d, d), const2),
            pl.BlockSpec((1, 1, d), per_b),
            pl.BlockSpec((1, d), const2),
            pl.BlockSpec((1, 1, d), per_b),
            pl.BlockSpec((1, 1, d), per_b),
            pl.BlockSpec((d, 2 * LANES), const2),
            pl.BlockSpec((N_EXPERTS, 1), const2),
            pl.BlockSpec((te, te), const2),
        ],
        out_specs=[
            pl.BlockSpec((te, d), row),
            pl.BlockSpec((te * TOK_ROWS, LANES), row),
            pl.BlockSpec((te, LANES), row),
            pl.BlockSpec((1, 1, te), lambda i: (i, 0, 0)),
            pl.BlockSpec((1, 1, te), lambda i: (i, 0, 0)),
            pl.BlockSpec((1, SUBLANES, LANES), lambda i: (i // steps_per_chunk, 0, 0)),
        ],
        out_shape=[
            jax.ShapeDtypeStruct((t, d), F32),
            jax.ShapeDtypeStruct((t * TOK_ROWS, LANES), F32),
            jax.ShapeDtypeStruct((t, LANES), F32),
            jax.ShapeDtypeStruct((n_steps, 1, te), jnp.int32),
            jax.ShapeDtypeStruct((n_steps, 1, te), jnp.int32),
            jax.ShapeDtypeStruct((n_chunks, SUBLANES, LANES), jnp.int32),
        ],
        scratch_shapes=[pltpu.VMEM((SUBLANES, LANES), F32)],
        compiler_params=_cparams(1, 48),
        name="post_router",
    )(x, hf, hm, wo, g1, nf, sc, sh, wr, br, tri)


def _scatter_kernel(pos_ref, hn_ref, gsel_ref, xb_ref, gs_ref, xs_sc, *, chunk, rows, tm):
    xs_sc[...] = jnp.zeros_like(xs_sc)
    gs_ref[...] = jnp.zeros_like(gs_ref)

    def body(t, carry):
        p = pos_ref[0, 0, t]
        src = pl.multiple_of(t * TOK_ROWS, TOK_ROWS)
        dst = pl.multiple_of(p * TOK_ROWS, TOK_ROWS)
        xs_sc[pl.ds(dst, TOK_ROWS), :] = hn_ref[pl.ds(src, TOK_ROWS), :]
        gs_ref[pl.ds(p, 1), :] = gsel_ref[pl.ds(t, 1), :]
        return carry

    lax.fori_loop(0, chunk, body, 0, unroll=8)
    for j in range(rows // tm):
        for c in range(TOK_ROWS):
            xb_ref[j * tm:(j + 1) * tm, c * LANES:(c + 1) * LANES] = (
                xs_sc[pl.ds(j * tm * TOK_ROWS + c, tm, stride=TOK_ROWS), :].astype(BF16))


def _scatter_call(pos, hn_tt, gsel, *, chunk, rows, tm):
    t = gsel.shape[0]
    n_chunks = t // chunk
    return pl.pallas_call(
        functools.partial(_scatter_kernel, chunk=chunk, rows=rows, tm=tm),
        grid=(n_chunks,),
        in_specs=[
            pl.BlockSpec((1, 1, chunk), lambda c: (c, 0, 0), memory_space=pltpu.SMEM),
            pl.BlockSpec((chunk * TOK_ROWS, LANES), lambda c: (c, 0)),
            pl.BlockSpec((chunk, LANES), lambda c: (c, 0)),
        ],
        out_specs=[
            pl.BlockSpec((rows, D_MODEL), lambda c: (c, 0)),
            pl.BlockSpec((rows, LANES), lambda c: (c, 0)),
        ],
        out_shape=[
            jax.ShapeDtypeStruct((n_chunks * rows, D_MODEL), BF16),
            jax.ShapeDtypeStruct((n_chunks * rows, LANES), F32),
        ],
        scratch_shapes=[pltpu.VMEM((rows * TOK_ROWS, LANES), F32)],
        compiler_params=_cparams(1, 56),
        name="moe_scatter",
    )(pos, hn_tt, gsel)


def _experts_kernel(blk_ref, grp_ref, xb_ref, gs_ref, wg_ref, wu_ref, wd_ref, y_ref, *, tm):
    g = grp_ref[pl.program_id(0)]

    @pl.when(g < N_GROUPS)
    def _():
        x = xb_ref[...]
        gs = gs_ref[...]
        acts = []
        for i in range(EXPERTS_PER_GROUP):
            hg = jnp.dot(x, wg_ref[i], preferred_element_type=F32)
            hu = jnp.dot(x, wu_ref[i], preferred_element_type=F32)
            acts.append((_silu(hg) * hu * gs[:, i:i + 1]).astype(BF16))
        y = jnp.dot(jnp.concatenate(acts, axis=1), wd_ref[0], preferred_element_type=F32)
        _to_token_tiles(y_ref, y)

    @pl.when(g >= N_GROUPS)
    def _():
        y_ref[...] = jnp.zeros_like(y_ref)


def _experts_call(tile_blk, tile_grp, xb, gs, wg, wu, wd, *, tm):
    n_rows = xb.shape[0]
    n_slots = n_rows // tm
    wmap = lambda s, blk, grp: (jnp.minimum(grp[s], N_GROUPS - 1), 0, 0)
    grid_spec = pltpu.PrefetchScalarGridSpec(
        num_scalar_prefetch=2,
        grid=(n_slots,),
        in_specs=[
            pl.BlockSpec((tm, D_MODEL), lambda s, blk, grp: (blk[s], 0)),
            pl.BlockSpec((tm, LANES), lambda s, blk, grp: (blk[s], 0)),
            pl.BlockSpec((EXPERTS_PER_GROUP, D_MODEL, D_FF), wmap),
            pl.BlockSpec((EXPERTS_PER_GROUP, D_MODEL, D_FF), wmap),
            pl.BlockSpec((1, EXPERTS_PER_GROUP * D_FF, D_MODEL), wmap),
        ],
        out_specs=pl.BlockSpec((tm * TOK_ROWS, LANES), lambda s, blk, grp: (blk[s], 0)),
    )
    return pl.pallas_call(
        functools.partial(_experts_kernel, tm=tm),
        grid_spec=grid_spec,
        out_shape=jax.ShapeDtypeStruct((n_rows * TOK_ROWS, LANES), F32),
        compiler_params=_cparams(1, 48),
        name="moe_experts",
    )(tile_blk, tile_grp, xb, gs, wg, wu, wd)


def _gather_kernel(pos_ref, ys_ref, o_ref, *, chunk):
    def body(t, carry):
        p = pos_ref[0, 0, t]
        src = pl.multiple_of(p * TOK_ROWS, TOK_ROWS)
        dst = pl.multiple_of(t * TOK_ROWS, TOK_ROWS)
        o_ref[pl.ds(dst, TOK_ROWS), :] = ys_ref[pl.ds(src, TOK_ROWS), :]
        return carry

    lax.fori_loop(0, chunk, body, 0, unroll=8)


def _gather_call(pos, ys_tt, *, chunk, rows, n_tok):
    n_chunks = n_tok // chunk
    return pl.pallas_call(
        functools.partial(_gather_kernel, chunk=chunk),
        grid=(n_chunks,),
        in_specs=[
            pl.BlockSpec((1, 1, chunk), lambda c: (c, 0, 0), memory_space=pltpu.SMEM),
            pl.BlockSpec((rows * TOK_ROWS, LANES), lambda c: (c, 0)),
        ],
        out_specs=pl.BlockSpec((chunk * TOK_ROWS, LANES), lambda c: (c, 0)),
        out_shape=jax.ShapeDtypeStruct((n_tok * TOK_ROWS, LANES), F32),
        compiler_params=_cparams(1, 56),
        name="moe_gather",
    )(pos, ys_tt)


def _final_kernel(x_ref, ott_ref, g2_ref, nf_ref, o_ref, *, tm):
    x = x_ref[...] + g2_ref[0] * _from_token_tiles(ott_ref, tm)
    ms = jnp.mean(x * x, axis=1, keepdims=True)
    o_ref[...] = x * lax.rsqrt(ms + EPS) * nf_ref[...]


def _final_call(x, moe_tt, g2, nf, *, seq, tm=512):
    t, d = x.shape
    per_seq = seq // tm
    return pl.pallas_call(
        functools.partial(_final_kernel, tm=tm),
        grid=(t // tm,),
        in_specs=[
            pl.BlockSpec((tm, d), lambda i: (i, 0)),
            pl.BlockSpec((tm * TOK_ROWS, LANES), lambda i: (i, 0)),
            pl.BlockSpec((1, 1, d), lambda i: (i // per_seq, 0, 0)),
            pl.BlockSpec((1, d), lambda i: (0, 0)),
        ],
        out_specs=pl.BlockSpec((tm, d), lambda i: (i, 0)),
        out_shape=jax.ShapeDtypeStruct((t, d), F32),
        compiler_params=_cparams(1, 32),
        name="final_norm",
    )(x, moe_tt, g2, nf)


def _moe_tiles(tot, *, tm, tiles_per_chunk):
    nt = (tot + tm - 1) // tm
    ts = jnp.cumsum(nt, axis=1) - nt
    off = (ts * tm).astype(jnp.int32)
    j = jnp.arange(tiles_per_chunk, dtype=jnp.int32)[None, :, None]
    inside = (j >= ts[:, None, :]) & (j < (ts + nt)[:, None, :])
    key = jnp.where(jnp.any(inside, axis=-1), jnp.argmax(inside, axis=-1), N_GROUPS).reshape(-1)
    order = jnp.argsort(key, stable=True).astype(jnp.int32)
    return off, order, key[order].astype(jnp.int32)


def kernel(x, c, w_in, conv_w, conv_b, fox_f_bias, mlstm_i_bias, mlstm_f_bias, fox_out_norm,
           mlstm_out_norm, w_out, w_ada, b_ada, norm_mix, norm_ffn, w_router, b_router, w_gate,
           w_up, w_down, norm_final):
    bsz, seq, d = x.shape
    depth = w_in.shape[0]
    t = bsz * seq
    assert d == D_MODEL and w_in.shape[-1] == IN_COLS
    chunk = min(2048, seq)
    tm = 256
    rows = -(-(chunk + N_GROUPS * (tm - 1)) // tm) * tm
    te = min(512, seq)
    assert seq % 512 == 0 and seq % chunk == 0 and chunk % te == 0

    mods = _ada_call(c, w_ada, b_ada).reshape(depth, bsz, N_ADA, 1, d)
    xf = x.reshape(t, d)

    tri = (jnp.arange(te)[:, None] < jnp.arange(te)[None, :]).astype(BF16)
    wr_f = jnp.pad(w_router, ((0, 0), (0, LANES - N_EXPERTS))).astype(F32)
    wr_hi = wr_f.astype(BF16)
    wr = jnp.concatenate([wr_hi, (wr_f - wr_hi.astype(F32)).astype(BF16)], axis=1)
    br = b_router.reshape(N_EXPERTS, 1).astype(F32)

    moe_tt = None
    g2_prev = None
    for l in range(depth):
        sh1, sc1, g1, sh2, sc2, g2 = [mods[l, :, i] for i in range(N_ADA)]
        wl = w_in[l]
        wm = jnp.concatenate([wl[:, FOX_Q:FOX_F], wl[:, ML_Q:ML_I], wl[:, ML_O:IN_COLS]], axis=1).astype(BF16)
        zpad = lambda n: jnp.zeros((d, n), F32)
        wg = jnp.concatenate([
            wl[:, FOX_F:ML_Q], wl[:, ML_F:ML_O], zpad(LANES - FOX_HEADS - ML_HEADS),
            zpad(GATE_ML), wl[:, ML_I:ML_F], zpad(LANES - GATE_ML - ML_HEADS)], axis=1).astype(BF16)
        zb = lambda n: jnp.zeros((n,), F32)
        gb = jnp.concatenate([
            fox_f_bias[l], mlstm_f_bias[l], zb(LANES - FOX_HEADS - ML_HEADS),
            zb(GATE_ML), mlstm_i_bias[l], zb(LANES - GATE_ML - ML_HEADS)]).reshape(1, 2 * LANES)

        xf, qat, ka, fvt, qkv, mkt, fc, gr = _in_proj_call(
            xf, moe_tt, g2_prev, norm_mix[l].reshape(1, d), sc1, sh1, wm, wg, gb,
            conv_w[l], conv_b[l].reshape(1, -1), bsz=bsz, seq=seq)
        hf = _fox_call(qat, ka, fvt, fox_out_norm[l].reshape(1, FOX_WIDTH), bsz=bsz, seq=seq)
        hm = _mlstm_call(qkv, mkt, fc, gr, mlstm_out_norm[l].reshape(1, ML_WIDTH), bsz=bsz, seq=seq)
        xf, hn_tt, gsel, grp, rank, tot = _post_call(
            xf, hf, hm, w_out[l].astype(BF16), g1, norm_ffn[l].reshape(1, d), sc2, sh2, wr, br, tri,
            bsz=bsz, seq=seq, chunk=chunk, te=te)

        n_chunks = t // chunk
        grp = grp.reshape(n_chunks, 1, chunk)
        rank = rank.reshape(n_chunks, 1, chunk)
        off, tile_blk, tile_grp = _moe_tiles(tot[:, :N_GROUPS, 0], tm=tm, tiles_per_chunk=rows // tm)
        pos = rank
        for g in range(N_GROUPS):
            pos = pos + jnp.where(grp == g, off[:, g].reshape(n_chunks, 1, 1), 0)
        xb, gs = _scatter_call(pos, hn_tt, gsel, chunk=chunk, rows=rows, tm=tm)
        ys_tt = _experts_call(tile_blk, tile_grp, xb, gs, w_gate[l].astype(BF16),
                              w_up[l].astype(BF16),
                              w_down[l].astype(BF16).reshape(N_GROUPS, EXPERTS_PER_GROUP * D_FF, d), tm=tm)
        moe_tt = _gather_call(pos, ys_tt, chunk=chunk, rows=rows, n_tok=t)
        g2_prev = g2

    out = _final_call(xf, moe_tt, g2_prev, norm_final.reshape(1, d), seq=seq)
    return out.reshape(bsz, seq, d)
```

```python
import functools

import numpy as np
import jax
import jax.numpy as jnp
from jax import lax
from jax.experimental import pallas as pl
from jax.experimental.pallas import tpu as pltpu

F32 = jnp.float32
BF16 = jnp.bfloat16

LANES = 128
SUBLANES = 8
VMEM_BYTES_V7X = 64 * 1024 * 1024

D_MODEL = 1024
FOX_HEADS = 8
FOX_HEAD_DIM = 64
FOX_WIDTH = FOX_HEADS * FOX_HEAD_DIM
ML_HEADS = 4
ML_HEAD_DIM = 128
ML_WIDTH = ML_HEADS * ML_HEAD_DIM
CONV_WIDTH = 4
N_EXPERTS = 16
N_GROUPS = 4
EXPERTS_PER_GROUP = 4
D_FF = 512
N_ADA = 6
EPS = 1e-6
NEG = -1e30

FOX_Q = 0
FOX_F = 3 * FOX_WIDTH
ML_Q = FOX_F + FOX_HEADS
ML_I = ML_Q + 3 * ML_WIDTH
ML_F = ML_I + ML_HEADS
ML_O = ML_F + ML_HEADS
IN_COLS = ML_O + ML_WIDTH

MAIN_COLS = 7 * 512
BLK_FQ, BLK_FK, BLK_FV, BLK_MQ, BLK_MK, BLK_MV, BLK_MO = range(7)
REST_COLS = 3 * 512
OUT_MQ, OUT_MV, OUT_MO = range(3)
AUG_COLS = FOX_HEADS * 128
BIAS_TERMS = 3
VT_PAD = 16
LOG2E = 1.4426950408889634
GATE_FOX = 0
GATE_ML = FOX_HEADS
GATE_ROWS = 16

TOK_ROWS = D_MODEL // LANES


def _cparams(n_grid, vmem_mb):
    return pltpu.CompilerParams(
        dimension_semantics=("arbitrary",) * n_grid,
        vmem_limit_bytes=vmem_mb * 1024 * 1024)


def _silu(x):
    return x * jax.nn.sigmoid(x)


def _log_sigmoid(z):
    return jnp.minimum(z, 0.0) - jnp.log1p(jnp.exp(-jnp.abs(z)))


def _cumsum_rows(x):
    n = x.shape[0]
    row = lax.broadcasted_iota(jnp.int32, x.shape, 0)
    s = 1
    while s < n:
        x = x + jnp.where(row >= s, pltpu.roll(x, s, axis=0), 0.0)
        s *= 2
    return x


def _from_token_tiles(ref, n_tok):
    return jnp.concatenate(
        [ref[pl.ds(c, n_tok, stride=TOK_ROWS), :] for c in range(TOK_ROWS)], axis=1)


def _to_token_tiles(ref, val, row0=0):
    n = val.shape[0]
    for c in range(TOK_ROWS):
        ref[pl.ds(row0 * TOK_ROWS + c, n, stride=TOK_ROWS), :] = val[:, c * LANES:(c + 1) * LANES]


def _ada_kernel(c_ref, w_ref, b_ref, o_ref):
    c = c_ref[...]
    o_ref[0] = jnp.dot(_silu(c), w_ref[0], preferred_element_type=F32,
                       precision=lax.Precision.HIGHEST) + b_ref[0]


def _ada_call(c, w_ada, b_ada):
    depth, d, n = w_ada.shape
    bsz = c.shape[0]
    tn = 1536
    return pl.pallas_call(
        _ada_kernel,
        grid=(depth, n // tn),
        in_specs=[
            pl.BlockSpec((bsz, d), lambda l, j: (0, 0)),
            pl.BlockSpec((1, d, tn), lambda l, j: (l, 0, j)),
            pl.BlockSpec((1, 1, tn), lambda l, j: (l, 0, j)),
        ],
        out_specs=pl.BlockSpec((1, bsz, tn), lambda l, j: (l, 0, j)),
        out_shape=jax.ShapeDtypeStruct((depth, bsz, n), F32),
        compiler_params=_cparams(2, 32),
        name="ada_mod",
    )(c, w_ada, b_ada.reshape(depth, 1, n))


def _in_proj_kernel(*refs, tm, combine):
    if combine:
        (x_ref, ott_ref, g2_ref, nm_ref, sc_ref, sh_ref, wm_ref, wg_ref, gb_ref, cw_ref, cb_ref,
         place_ref, ones_ref,
         xn_ref, qa_ref, ka_ref, vt_ref, qkv_ref, kt_ref, fc_ref, gr_ref, fcar, ccar) = refs
    else:
        (x_ref, nm_ref, sc_ref, sh_ref, wm_ref, wg_ref, gb_ref, cw_ref, cb_ref,
         place_ref, ones_ref,
         qa_ref, ka_ref, vt_ref, qkv_ref, kt_ref, fc_ref, gr_ref, fcar, ccar) = refs

    @pl.when(pl.program_id(1) == 0)
    def _():
        fcar[...] = jnp.zeros_like(fcar)
        ccar[...] = jnp.zeros_like(ccar)

    x = x_ref[...]
    if combine:
        x = x + g2_ref[0] * _from_token_tiles(ott_ref, tm)
        xn_ref[...] = x
    ms = jnp.mean(x * x, axis=1, keepdims=True)
    hn = x * lax.rsqrt(ms + EPS) * nm_ref[...]
    hn = hn * (1.0 + sc_ref[0]) + sh_ref[0]
    hb = hn.astype(BF16)

    gp = jnp.dot(hb, wg_ref[...], preferred_element_type=F32) + gb_ref[...]
    lf = _log_sigmoid(gp[:, :LANES])
    fcum = _cumsum_rows(lf) + fcar[...]
    fcar[...] = fcum[tm - 1:tm, :]
    gml = gp[:, LANES:] - fcum
    fc_ref[...] = fcum
    gr_ref[0] = gml.T[:GATE_ROWS, :]

    def mm(j):
        return jnp.dot(hb, wm_ref[:, j * 512:(j + 1) * 512], preferred_element_type=F32)

    def put(j, v):
        qkv_ref[:, j * 512:(j + 1) * 512] = v.astype(BF16)

    lane = lax.broadcasted_iota(jnp.int32, (tm, LANES), 1)
    fs = fcum * LOG2E
    hi = fs.astype(BF16).astype(F32)
    mid = (fs - hi).astype(BF16).astype(F32)
    low = (fs - hi - mid).astype(BF16).astype(F32)
    packed = jnp.where(lane < FOX_HEADS, hi,
                       jnp.where(lane < 2 * FOX_HEADS, pltpu.roll(mid, FOX_HEADS, axis=1),
                                 jnp.where(lane < 3 * FOX_HEADS, pltpu.roll(low, 2 * FOX_HEADS, axis=1), 0.0)))
    bias = jnp.dot(packed.astype(BF16), place_ref[...], preferred_element_type=F32) + ones_ref[...]

    def put_heads(ref, val, col0, transposed):
        lo = lane < FOX_HEAD_DIM
        for p in range(FOX_HEADS // 2):
            slab = val[:, p * LANES:(p + 1) * LANES]
            for h, data in ((2 * p, slab), (2 * p + 1, pltpu.roll(slab, FOX_HEAD_DIM, axis=1))):
                blk = jnp.where(lo, data, 0.0) + bias[:, col0 + h * LANES:col0 + (h + 1) * LANES]
                if transposed:
                    ref[0, 0, h * LANES:(h + 1) * LANES, :] = blk.T.astype(BF16)
                else:
                    ref[:, h * LANES:(h + 1) * LANES] = blk.astype(BF16)

    u = jnp.concatenate([mm(BLK_MQ), mm(BLK_MK)], axis=1)
    prev = ccar[...]
    ccar[...] = u[tm - SUBLANES:tm, :]
    row8 = lax.broadcasted_iota(jnp.int32, prev.shape, 0)
    y = cb_ref[...] + cw_ref[CONV_WIDTH - 1:CONV_WIDTH, :] * u
    for k in range(1, CONV_WIDTH):
        r = pltpu.roll(u, k, axis=0)
        top = jnp.where(row8 < k, pltpu.roll(prev, k, axis=0), r[:SUBLANES])
        shifted = jnp.concatenate([top, r[SUBLANES:]], axis=0)
        y = y + cw_ref[CONV_WIDTH - 1 - k:CONV_WIDTH - k, :] * shifted
    act = _silu(y)
    put(OUT_MQ, act[:, :ML_WIDTH])
    kt_ref[0, 0] = (act[:, ML_WIDTH:] * (ML_HEAD_DIM ** -0.5)).T.astype(BF16)

    put(OUT_MO, jax.nn.sigmoid(mm(BLK_MO)))
    put(OUT_MV, mm(BLK_MV))
    vt_ref[0, 0] = mm(BLK_FV).astype(BF16).T
    put_heads(qa_ref, mm(BLK_FQ) * (FOX_HEAD_DIM ** -0.5 * LOG2E), 0, True)
    put_heads(ka_ref, mm(BLK_FK), AUG_COLS, False)


def _bias_placement():
    place = np.zeros((LANES, 2 * AUG_COLS), np.float32)
    ones = np.zeros((1, 2 * AUG_COLS), np.float32)
    for h in range(FOX_HEADS):
        for term in range(BIAS_TERMS):
            src = term * FOX_HEADS + h
            place[src, h * LANES + FOX_HEAD_DIM + term] = 1.0
            ones[0, h * LANES + FOX_HEAD_DIM + BIAS_TERMS + term] = 1.0
            place[src, AUG_COLS + h * LANES + FOX_HEAD_DIM + BIAS_TERMS + term] = -1.0
            ones[0, AUG_COLS + h * LANES + FOX_HEAD_DIM + term] = 1.0
    return jnp.asarray(place, BF16), jnp.asarray(ones, F32)


def _in_proj_call(x, moe_tt, g2, nm, sc, sh, wm, wg, gb, cw, cb, *, bsz, seq, tm=512):
    t, d = x.shape
    ns = seq // tm
    combine = moe_tt is not None
    row = lambda b, s: (b * ns + s, 0)
    per_b = lambda b, s: (b, 0, 0)
    const2 = lambda b, s: (0, 0)
    in_specs = [pl.BlockSpec((tm, d), row)]
    args = [x]
    if combine:
        in_specs += [pl.BlockSpec((tm * TOK_ROWS, LANES), row), pl.BlockSpec((1, 1, d), per_b)]
        args += [moe_tt, g2]
    in_specs += [
        pl.BlockSpec((1, d), const2),
        pl.BlockSpec((1, 1, d), per_b),
        pl.BlockSpec((1, 1, d), per_b),
        pl.BlockSpec((d, MAIN_COLS), const2),
        pl.BlockSpec((d, 2 * LANES), const2),
        pl.BlockSpec((1, 2 * LANES), const2),
        pl.BlockSpec((CONV_WIDTH, 2 * ML_WIDTH), const2),
        pl.BlockSpec((1, 2 * ML_WIDTH), const2),
        pl.BlockSpec((LANES, 2 * AUG_COLS), const2),
        pl.BlockSpec((1, 2 * AUG_COLS), const2),
    ]
    place, ones = _bias_placement()
    args += [nm, sc, sh, wm, wg, gb, cw, cb, place, ones]
    out_specs = []
    out_shape = []
    if combine:
        out_specs.append(pl.BlockSpec((tm, d), row))
        out_shape.append(jax.ShapeDtypeStruct((t, d), F32))
    tiled = lambda b, s: (b, s, 0, 0)
    out_specs += [
        pl.BlockSpec((1, 1, AUG_COLS, tm), tiled),
        pl.BlockSpec((tm, AUG_COLS), row),
        pl.BlockSpec((1, 1, FOX_WIDTH, tm), tiled),
        pl.BlockSpec((tm, REST_COLS), row),
        pl.BlockSpec((1, 1, ML_WIDTH, tm), tiled),
        pl.BlockSpec((tm, LANES), row),
        pl.BlockSpec((1, GATE_ROWS, tm), lambda b, s: (b, 0, s)),
    ]
    out_shape += [
        jax.ShapeDtypeStruct((bsz, ns, AUG_COLS, tm), BF16),
        jax.ShapeDtypeStruct((t, AUG_COLS), BF16),
        jax.ShapeDtypeStruct((bsz, ns, FOX_WIDTH, tm), BF16),
        jax.ShapeDtypeStruct((t, REST_COLS), BF16),
        jax.ShapeDtypeStruct((bsz, ns, ML_WIDTH, tm), BF16),
        jax.ShapeDtypeStruct((t, LANES), F32),
        jax.ShapeDtypeStruct((bsz, GATE_ROWS, seq), F32),
    ]
    outs = pl.pallas_call(
        functools.partial(_in_proj_kernel, tm=tm, combine=combine),
        grid=(bsz, ns),
        in_specs=in_specs,
        out_specs=out_specs,
        out_shape=out_shape,
        scratch_shapes=[pltpu.VMEM((1, LANES), F32), pltpu.VMEM((SUBLANES, 2 * ML_WIDTH), F32)],
        compiler_params=_cparams(2, 48),
        name="in_proj",
    )(*args)
    if combine:
        return outs
    return [x] + list(outs)


def _fox_kernel(qt_ref, k_ref, vt_ref, ng_ref, cm_ref, o_ref, m_sc, acc_sc, *, tq, tk, nh):
    qi = pl.program_id(2)
    ones_rows = jnp.where(lax.broadcasted_iota(jnp.int32, (VT_PAD, tk), 0) == 0, 1.0, 0.0).astype(BF16)
    for h in range(nh):
        m_sc[h] = jnp.full((SUBLANES, tq), NEG, F32)
        acc_sc[h] = jnp.zeros((FOX_HEAD_DIM + VT_PAD, tq), F32)

    def body(j, carry, diagonal):
        k0 = pl.multiple_of(j * tk, tk)

        def logits(h):
            hs = slice(h * LANES, (h + 1) * LANES)
            z = jnp.dot(k_ref[pl.ds(k0, tk), hs], qt_ref[0, 0, hs, :], preferred_element_type=F32)
            return z + cm_ref[...] if diagonal else z

        ahead = 2
        zs = [logits(h) for h in range(min(ahead, nh))]
        for h in range(nh):
            ds_ = slice(h * FOX_HEAD_DIM, (h + 1) * FOX_HEAD_DIM)
            z = zs[h]
            if h + ahead < nh:
                zs.append(logits(h + ahead))
            m_prev = m_sc[h]
            m_new = jnp.maximum(m_prev, jnp.max(z, axis=0, keepdims=True))
            alpha = jnp.exp2(m_prev - m_new)
            p = jnp.exp2(z - m_new[:1, :])
            va = jnp.concatenate([vt_ref[0, j, ds_, :], ones_rows], axis=0)
            acc_sc[h] = alpha[:1, :] * acc_sc[h] + jnp.dot(va, p.astype(BF16), preferred_element_type=F32)
            m_sc[h] = m_new
        return carry

    lax.fori_loop(0, qi, functools.partial(body, diagonal=False), 0)
    lax.fori_loop(qi, qi + 1, functools.partial(body, diagonal=True), 0)

    for p in range(nh // 2):
        outs = []
        for h in (2 * p, 2 * p + 1):
            acc = acc_sc[h]
            num = acc[:FOX_HEAD_DIM, :]
            l = acc[FOX_HEAD_DIM:FOX_HEAD_DIM + 1, :]
            ms = jnp.mean(num * num, axis=0, keepdims=True)
            outs.append(num * lax.rsqrt(ms + EPS * l * l))
        ps = slice(p * LANES, (p + 1) * LANES)
        o_ref[:, ps] = (jnp.concatenate(outs, axis=0).T * ng_ref[:, ps]).astype(BF16)


def _fox_call(qat, ka, fvt, ng, *, bsz, seq, tq=512, nh=8):
    t = ka.shape[0]
    nq = seq // tq
    ngrp = FOX_HEADS // nh
    vw = nh * FOX_HEAD_DIM
    cmask = jnp.where(jnp.arange(tq)[:, None] <= jnp.arange(tq)[None, :], 0.0, NEG).astype(F32)
    return pl.pallas_call(
        functools.partial(_fox_kernel, tq=tq, tk=tq, nh=nh),
        grid=(bsz, ngrp, nq),
        in_specs=[
            pl.BlockSpec((1, 1, nh * LANES, tq), lambda b, p, i: (b, i, p, 0)),
            pl.BlockSpec((seq, nh * LANES), lambda b, p, i: (b, p)),
            pl.BlockSpec((1, nq, vw, tq), lambda b, p, i: (b, 0, p, 0)),
            pl.BlockSpec((1, vw), lambda b, p, i: (0, p)),
            pl.BlockSpec((tq, tq), lambda b, p, i: (0, 0)),
        ],
        out_specs=pl.BlockSpec((tq, vw), lambda b, p, i: (b * nq + i, p)),
        out_shape=jax.ShapeDtypeStruct((t, FOX_WIDTH), BF16),
        scratch_shapes=[pltpu.VMEM((nh, SUBLANES, tq), F32),
                        pltpu.VMEM((nh, FOX_HEAD_DIM + VT_PAD, tq), F32)],
        compiler_params=_cparams(3, 48),
        name="fox_attn",
    )(qat, ka, fvt, ng, cmask)


def _mlstm_kernel(q_ref, kt_ref, v_ref, og_ref, fc_ref, gr_ref, ng_ref, o_ref, ct_sc, u_sc, *, ch, per_tile):
    @pl.when(pl.program_id(1) == 0)
    def _():
        ct_sc[...] = jnp.zeros_like(ct_sc)
        u_sc[...] = jnp.zeros_like(u_sc)

    causal = (lax.broadcasted_iota(jnp.int32, (ch, ch), 1)
              <= lax.broadcasted_iota(jnp.int32, (ch, ch), 0))
    lane = lax.broadcasted_iota(jnp.int32, (ch, LANES), 1)
    fc = fc_ref[...]
    koff = pl.multiple_of((pl.program_id(1) % per_tile) * ch, ch)
    for h in range(ML_HEADS):
        sl = slice(h * ML_HEAD_DIM, (h + 1) * ML_HEAD_DIM)
        gl = GATE_ML + h
        q = q_ref[:, sl]
        kt = kt_ref[0, 0, sl, pl.ds(koff, ch)]
        vp = jnp.concatenate([v_ref[:, sl], jnp.where(lane == gl, 1.0, 0.0).astype(BF16)], axis=1)
        g_row = gr_ref[0, gl:gl + 1, :]
        u_prev = u_sc[h][:, :1]
        gm = jnp.where(causal, g_row, NEG)
        u_i = jnp.maximum(u_prev, jnp.max(gm, axis=1, keepdims=True))
        dmat = jnp.exp(gm - u_i)
        s = jnp.dot(q, kt, preferred_element_type=F32)
        scores = (s * dmat).astype(BF16)
        inter = jnp.exp(u_prev - u_i)
        ct = ct_sc[h]
        nd = (jnp.dot(scores, vp, preferred_element_type=F32)
              + jnp.dot(q, ct.astype(BF16), preferred_element_type=F32) * inter)
        num = nd[:, :ML_HEAD_DIM]
        den = jnp.maximum(jnp.abs(nd[:, ML_HEAD_DIM:]), jnp.exp(-(jnp.where(lane == gl, fc, 0.0) + u_i)))
        ms = jnp.mean(num * num, axis=1, keepdims=True)
        scale = lax.rsqrt(ms + EPS * den * den)[:, gl:gl + 1]
        y = num * scale * ng_ref[:, sl] * og_ref[:, sl].astype(F32)
        o_ref[:, sl] = y.astype(BF16)
        u_new = jnp.maximum(u_prev, jnp.max(g_row, axis=1, keepdims=True))
        ktw = (kt.astype(F32) * jnp.exp(g_row - u_new)).astype(BF16)
        ct_sc[h] = jnp.exp(u_prev - u_new) * ct + jnp.dot(ktw, vp, preferred_element_type=F32)
        u_sc[h] = jnp.broadcast_to(u_new, (1, LANES))


def _mlstm_call(qkv, mkt, fc, gr, ng, *, bsz, seq, ch=512):
    t = qkv.shape[0]
    nc = seq // ch
    ktile = mkt.shape[-1]
    per_tile = ktile // ch
    row = lambda b, c: (b * nc + c, 0)
    return pl.pallas_call(
        functools.partial(_mlstm_kernel, ch=ch, per_tile=per_tile),
        grid=(bsz, nc),
        in_specs=[
            pl.BlockSpec((ch, ML_WIDTH), lambda b, c: (b * nc + c, OUT_MQ)),
            pl.BlockSpec((1, 1, ML_WIDTH, ktile), lambda b, c: (b, c // per_tile, 0, 0)),
            pl.BlockSpec((ch, ML_WIDTH), lambda b, c: (b * nc + c, OUT_MV)),
            pl.BlockSpec((ch, ML_WIDTH), lambda b, c: (b * nc + c, OUT_MO)),
            pl.BlockSpec((ch, LANES), row),
            pl.BlockSpec((1, GATE_ROWS, ch), lambda b, c: (b, 0, c)),
            pl.BlockSpec((1, ML_WIDTH), lambda b, c: (0, 0)),
        ],
        out_specs=pl.BlockSpec((ch, ML_WIDTH), row),
        out_shape=jax.ShapeDtypeStruct((t, ML_WIDTH), BF16),
        scratch_shapes=[pltpu.VMEM((ML_HEADS, ML_HEAD_DIM, 2 * ML_HEAD_DIM), F32),
                        pltpu.VMEM((ML_HEADS, 1, LANES), F32)],
        compiler_params=_cparams(2, 32),
        name="mlstm",
    )(qkv, mkt, qkv, qkv, fc, gr, ng)


def _post_kernel(x_ref, hf_ref, hm_ref, wo_ref, g1_ref, nf_ref, sc_ref, sh_ref, wr_ref, br_ref, tri_ref,
                 x1_ref, hn_ref, gsel_ref, grp_ref, rank_ref, tot_ref, cnt_sc, *, te, steps_per_chunk):
    @pl.when(pl.program_id(0) % steps_per_chunk == 0)
    def _():
        cnt_sc[...] = jnp.zeros_like(cnt_sc)

    mix = (jnp.dot(hf_ref[...], wo_ref[:FOX_WIDTH, :], preferred_element_type=F32)
           + jnp.dot(hm_ref[...], wo_ref[FOX_WIDTH:, :], preferred_element_type=F32))
    x1 = x_ref[...] + g1_ref[0] * mix
    x1_ref[...] = x1
    ms = jnp.mean(x1 * x1, axis=1, keepdims=True)
    hn = x1 * lax.rsqrt(ms + EPS) * nf_ref[...]
    hn = hn * (1.0 + sc_ref[0]) + sh_ref[0]
    _to_token_tiles(hn_ref, hn)

    hn_hi = hn.astype(BF16)
    hn_lo = (hn - hn_hi.astype(F32)).astype(BF16)
    l2 = jnp.dot(hn_hi, wr_ref[...], preferred_element_type=F32)
    logits = (l2[:, :LANES] + l2[:, LANES:]
              + jnp.dot(hn_lo, wr_ref[:, :LANES], preferred_element_type=F32))
    aff = jax.nn.sigmoid(logits.T[:N_EXPERTS, :])
    sel = aff + br_ref[...]
    selr = [sel[e:e + 1, :] for e in range(N_EXPERTS)]
    affr = [aff[e:e + 1, :] for e in range(N_EXPERTS)]
    keep = [None] * N_EXPERTS
    score = []
    for g in range(N_GROUPS):
        vs = selr[g * EXPERTS_PER_GROUP:(g + 1) * EXPERTS_PER_GROUP]
        sg = jnp.zeros_like(vs[0])
        for i in range(EXPERTS_PER_GROUP):
            beaten = jnp.zeros_like(vs[0])
            for j in range(EXPERTS_PER_GROUP):
                if j != i:
                    b = (vs[j] >= vs[i]) if j < i else (vs[j] > vs[i])
                    beaten = beaten + jnp.where(b, 1.0, 0.0)
            kp = beaten < 2.0
            keep[g * EXPERTS_PER_GROUP + i] = kp
            sg = sg + jnp.where(kp, vs[i], 0.0)
        score.append(sg)
    chosen = []
    for g in range(N_GROUPS):
        lost = jnp.zeros_like(score[0])
        for g2 in range(N_GROUPS):
            if g2 != g:
                b = (score[g2] >= score[g]) if g2 < g else (score[g2] > score[g])
                lost = lost + jnp.where(b, 1.0, 0.0)
        chosen.append(jnp.where(lost < 0.5, 1.0, 0.0))
    wsel = []
    for i in range(EXPERTS_PER_GROUP):
        wi = jnp.zeros_like(score[0])
        for g in range(N_GROUPS):
            e = g * EXPERTS_PER_GROUP + i
            wi = wi + chosen[g] * jnp.where(keep[e], affr[e], 0.0)
        wsel.append(wi)
    wsum = wsel[0] + wsel[1] + wsel[2] + wsel[3]
    wsel = [w / wsum for w in wsel]

    row8 = lax.broadcasted_iota(jnp.int32, (SUBLANES, te), 0)
    gmat = jnp.zeros((SUBLANES, te), F32)
    wmat = jnp.zeros((SUBLANES, te), F32)
    grp = jnp.zeros_like(score[0])
    for g in range(N_GROUPS):
        gmat = jnp.where(row8 == g, chosen[g], gmat)
        wmat = jnp.where(row8 == g, wsel[g], wmat)
        grp = grp + g * chosen[g]
    pref = jnp.dot(gmat.astype(BF16), tri_ref[...], preferred_element_type=F32) + cnt_sc[:, :1]
    rank = jnp.sum(gmat * pref, axis=0, keepdims=True)
    tot = cnt_sc[:, :1] + jnp.sum(gmat, axis=1, keepdims=True)
    cnt_sc[...] = jnp.broadcast_to(tot, cnt_sc.shape)
    rank_ref[0] = rank.astype(jnp.int32)
    grp_ref[0] = grp.astype(jnp.int32)
    tot_ref[0] = jnp.broadcast_to(tot, (SUBLANES, LANES)).astype(jnp.int32)
    wfull = jnp.concatenate([wmat, jnp.zeros((LANES - SUBLANES, te), F32)], axis=0)
    gsel_ref[...] = wfull.T


def _post_call(x, hf, hm, wo, g1, nf, sc, sh, wr, br, tri, *, bsz, seq, chunk, te=512):
    t, d = x.shape
    steps_per_chunk = chunk // te
    per_seq = seq // te
    n_steps = t // te
    n_chunks = t // chunk
    row = lambda i: (i, 0)
    per_b = lambda i: (i // per_seq, 0, 0)
    const2 = lambda i: (0, 0)
    return pl.pallas_call(
        functools.partial(_post_kernel, te=te, steps_per_chunk=steps_per_chunk),
        grid=(n_steps,),
        in_specs=[
            pl.BlockSpec((te, d), row),
            pl.BlockSpec((te, FOX_WIDTH), row),
            pl.BlockSpec((te, ML_WIDTH), row),
            pl.BlockSpec((d, d), const2),
            pl.BlockSpec((1, 1, d), per_b),
            pl.BlockSpec((1, d), const2),
            pl.BlockSpec((1, 1, d), per_b),
            pl.BlockSpec((1, 1, d), per_b),
            pl.BlockSpec((d, 2 * LANES), const2),
            pl.BlockSpec((N_EXPERTS, 1), const2),
            pl.BlockSpec((te, te), const2),
        ],
        out_specs=[
            pl.BlockSpec((te, d), row),
            pl.BlockSpec((te * TOK_ROWS, LANES), row),
            pl.BlockSpec((te, LANES), row),
            pl.BlockSpec((1, 1, te), lambda i: (i, 0, 0)),
            pl.BlockSpec((1, 1, te), lambda i: (i, 0, 0)),
            pl.BlockSpec((1, SUBLANES, LANES), lambda i: (i // steps_per_chunk, 0, 0)),
        ],
        out_shape=[
            jax.ShapeDtypeStruct((t, d), F32),
            jax.ShapeDtypeStruct((t * TOK_ROWS, LANES), F32),
            jax.ShapeDtypeStruct((t, LANES), F32),
            jax.ShapeDtypeStruct((n_steps, 1, te), jnp.int32),
            jax.ShapeDtypeStruct((n_steps, 1, te), jnp.int32),
            jax.ShapeDtypeStruct((n_chunks, SUBLANES, LANES), jnp.int32),
        ],
        scratch_shapes=[pltpu.VMEM((SUBLANES, LANES), F32)],
        compiler_params=_cparams(1, 48),
        name="post_router",
    )(x, hf, hm, wo, g1, nf, sc, sh, wr, br, tri)


def _scatter_kernel(pos_ref, hn_ref, gsel_ref, xb_ref, gs_ref, xs_sc, *, chunk, rows, tm):
    xs_sc[...] = jnp.zeros_like(xs_sc)
    gs_ref[...] = jnp.zeros_like(gs_ref)

    def body(t, carry):
        p = pos_ref[0, 0, t]
        src = pl.multiple_of(t * TOK_ROWS, TOK_ROWS)
        dst = pl.multiple_of(p * TOK_ROWS, TOK_ROWS)
        xs_sc[pl.ds(dst, TOK_ROWS), :] = hn_ref[pl.ds(src, TOK_ROWS), :]
        gs_ref[pl.ds(p, 1), :] = gsel_ref[pl.ds(t, 1), :]
        return carry

    lax.fori_loop(0, chunk, body, 0, unroll=8)
    for j in range(rows // tm):
        for c in range(TOK_ROWS):
            xb_ref[j * tm:(j + 1) * tm, c * LANES:(c + 1) * LANES] = (
                xs_sc[pl.ds(j * tm * TOK_ROWS + c, tm, stride=TOK_ROWS), :].astype(BF16))


def _scatter_call(pos, hn_tt, gsel, *, chunk, rows, tm):
    t = gsel.shape[0]
    n_chunks = t // chunk
    return pl.pallas_call(
        functools.partial(_scatter_kernel, chunk=chunk, rows=rows, tm=tm),
        grid=(n_chunks,),
        in_specs=[
            pl.BlockSpec((1, 1, chunk), lambda c: (c, 0, 0), memory_space=pltpu.SMEM),
            pl.BlockSpec((chunk * TOK_ROWS, LANES), lambda c: (c, 0)),
            pl.BlockSpec((chunk, LANES), lambda c: (c, 0)),
        ],
        out_specs=[
            pl.BlockSpec((rows, D_MODEL), lambda c: (c, 0)),
            pl.BlockSpec((rows, LANES), lambda c: (c, 0)),
        ],
        out_shape=[
            jax.ShapeDtypeStruct((n_chunks * rows, D_MODEL), BF16),
            jax.ShapeDtypeStruct((n_chunks * rows, LANES), F32),
        ],
        scratch_shapes=[pltpu.VMEM((rows * TOK_ROWS, LANES), F32)],
        compiler_params=_cparams(1, 56),
        name="moe_scatter",
    )(pos, hn_tt, gsel)


def _experts_kernel(blk_ref, grp_ref, xb_ref, gs_ref, wg_ref, wu_ref, wd_ref, y_ref, *, tm):
    g = grp_ref[pl.program_id(0)]

    @pl.when(g < N_GROUPS)
    def _():
        x = xb_ref[...]
        gs = gs_ref[...]
        acts = []
        for i in range(EXPERTS_PER_GROUP):
            hg = jnp.dot(x, wg_ref[i], preferred_element_type=F32)
            hu = jnp.dot(x, wu_ref[i], preferred_element_type=F32)
            acts.append((_silu(hg) * hu * gs[:, i:i + 1]).astype(BF16))
        y = jnp.dot(jnp.concatenate(acts, axis=1), wd_ref[0], preferred_element_type=F32)
        _to_token_tiles(y_ref, y)

    @pl.when(g >= N_GROUPS)
    def _():
        y_ref[...] = jnp.zeros_like(y_ref)


def _experts_call(tile_blk, tile_grp, xb, gs, wg, wu, wd, *, tm):
    n_rows = xb.shape[0]
    n_slots = n_rows // tm
    wmap = lambda s, blk, grp: (jnp.minimum(grp[s], N_GROUPS - 1), 0, 0)
    grid_spec = pltpu.PrefetchScalarGridSpec(
        num_scalar_prefetch=2,
        grid=(n_slots,),
        in_specs=[
            pl.BlockSpec((tm, D_MODEL), lambda s, blk, grp: (blk[s], 0)),
            pl.BlockSpec((tm, LANES), lambda s, blk, grp: (blk[s], 0)),
            pl.BlockSpec((EXPERTS_PER_GROUP, D_MODEL, D_FF), wmap),
            pl.BlockSpec((EXPERTS_PER_GROUP, D_MODEL, D_FF), wmap),
            pl.BlockSpec((1, EXPERTS_PER_GROUP * D_FF, D_MODEL), wmap),
        ],
        out_specs=pl.BlockSpec((tm * TOK_ROWS, LANES), lambda s, blk, grp: (blk[s], 0)),
    )
    return pl.pallas_call(
        functools.partial(_experts_kernel, tm=tm),
        grid_spec=grid_spec,
        out_shape=jax.ShapeDtypeStruct((n_rows * TOK_ROWS, LANES), F32),
        compiler_params=_cparams(1, 48),
        name="moe_experts",
    )(tile_blk, tile_grp, xb, gs, wg, wu, wd)


def _gather_kernel(pos_ref, ys_ref, o_ref, *, chunk):
    def body(t, carry):
        p = pos_ref[0, 0, t]
        src = pl.multiple_of(p * TOK_ROWS, TOK_ROWS)
        dst = pl.multiple_of(t * TOK_ROWS, TOK_ROWS)
        o_ref[pl.ds(dst, TOK_ROWS), :] = ys_ref[pl.ds(src, TOK_ROWS), :]
        return carry

    lax.fori_loop(0, chunk, body, 0, unroll=8)


def _gather_call(pos, ys_tt, *, chunk, rows, n_tok):
    n_chunks = n_tok // chunk
    return pl.pallas_call(
        functools.partial(_gather_kernel, chunk=chunk),
        grid=(n_chunks,),
        in_specs=[
            pl.BlockSpec((1, 1, chunk), lambda c: (c, 0, 0), memory_space=pltpu.SMEM),
            pl.BlockSpec((rows * TOK_ROWS, LANES), lambda c: (c, 0)),
        ],
        out_specs=pl.BlockSpec((chunk * TOK_ROWS, LANES), lambda c: (c, 0)),
        out_shape=jax.ShapeDtypeStruct((n_tok * TOK_ROWS, LANES), F32),
        compiler_params=_cparams(1, 56),
        name="moe_gather",
    )(pos, ys_tt)


def _final_kernel(x_ref, ott_ref, g2_ref, nf_ref, o_ref, *, tm):
    x = x_ref[...] + g2_ref[0] * _from_token_tiles(ott_ref, tm)
    ms = jnp.mean(x * x, axis=1, keepdims=True)
    o_ref[...] = x * lax.rsqrt(ms + EPS) * nf_ref[...]


def _final_call(x, moe_tt, g2, nf, *, seq, tm=512):
    t, d = x.shape
    per_seq = seq // tm
    return pl.pallas_call(
        functools.partial(_final_kernel, tm=tm),
        grid=(t // tm,),
        in_specs=[
            pl.BlockSpec((tm, d), lambda i: (i, 0)),
            pl.BlockSpec((tm * TOK_ROWS, LANES), lambda i: (i, 0)),
            pl.BlockSpec((1, 1, d), lambda i: (i // per_seq, 0, 0)),
            pl.BlockSpec((1, d), lambda i: (0, 0)),
        ],
        out_specs=pl.BlockSpec((tm, d), lambda i: (i, 0)),
        out_shape=jax.ShapeDtypeStruct((t, d), F32),
        compiler_params=_cparams(1, 32),
        name="final_norm",
    )(x, moe_tt, g2, nf)


def _moe_tiles(tot, *, tm, tiles_per_chunk):
    nt = (tot + tm - 1) // tm
    ts = jnp.cumsum(nt, axis=1) - nt
    off = (ts * tm).astype(jnp.int32)
    j = jnp.arange(tiles_per_chunk, dtype=jnp.int32)[None, :, None]
    inside = (j >= ts[:, None, :]) & (j < (ts + nt)[:, None, :])
    key = jnp.where(jnp.any(inside, axis=-1), jnp.argmax(inside, axis=-1), N_GROUPS).reshape(-1)
    order = jnp.argsort(key, stable=True).astype(jnp.int32)
    return off, order, key[order].astype(jnp.int32)


def kernel(x, c, w_in, conv_w, conv_b, fox_f_bias, mlstm_i_bias, mlstm_f_bias, fox_out_norm,
           mlstm_out_norm, w_out, w_ada, b_ada, norm_mix, norm_ffn, w_router, b_router, w_gate,
           w_up, w_down, norm_final):
    bsz, seq, d = x.shape
    depth = w_in.shape[0]
    t = bsz * seq
    assert d == D_MODEL and w_in.shape[-1] == IN_COLS
    chunk = min(2048, seq)
    tm = 256
    rows = -(-(chunk + N_GROUPS * (tm - 1)) // tm) * tm
    te = min(512, seq)
    assert seq % 512 == 0 and seq % chunk == 0 and chunk % te == 0

    mods = _ada_call(c, w_ada, b_ada).reshape(depth, bsz, N_ADA, 1, d)
    xf = x.reshape(t, d)

    tri = (jnp.arange(te)[:, None] < jnp.arange(te)[None, :]).astype(BF16)
    wr_f = jnp.pad(w_router, ((0, 0), (0, LANES - N_EXPERTS))).astype(F32)
    wr_hi = wr_f.astype(BF16)
    wr = jnp.concatenate([wr_hi, (wr_f - wr_hi.astype(F32)).astype(BF16)], axis=1)
    br = b_router.reshape(N_EXPERTS, 1).astype(F32)

    moe_tt = None
    g2_prev = None
    for l in range(depth):
        sh1, sc1, g1, sh2, sc2, g2 = [mods[l, :, i] for i in range(N_ADA)]
        wl = w_in[l]
        wm = jnp.concatenate([wl[:, FOX_Q:FOX_F], wl[:, ML_Q:ML_I], wl[:, ML_O:IN_COLS]], axis=1).astype(BF16)
        zpad = lambda n: jnp.zeros((d, n), F32)
        wg = jnp.concatenate([
            wl[:, FOX_F:ML_Q], wl[:, ML_F:ML_O], zpad(LANES - FOX_HEADS - ML_HEADS),
            zpad(GATE_ML), wl[:, ML_I:ML_F], zpad(LANES - GATE_ML - ML_HEADS)], axis=1).astype(BF16)
        zb = lambda n: jnp.zeros((n,), F32)
        gb = jnp.concatenate([
            fox_f_bias[l], mlstm_f_bias[l], zb(LANES - FOX_HEADS - ML_HEADS),
            zb(GATE_ML), mlstm_i_bias[l], zb(LANES - GATE_ML - ML_HEADS)]).reshape(1, 2 * LANES)

        xf, qat, ka, fvt, qkv, mkt, fc, gr = _in_proj_call(
            xf, moe_tt, g2_prev, norm_mix[l].reshape(1, d), sc1, sh1, wm, wg, gb,
            conv_w[l], conv_b[l].reshape(1, -1), bsz=bsz, seq=seq)
        hf = _fox_call(qat, ka, fvt, fox_out_norm[l].reshape(1, FOX_WIDTH), bsz=bsz, seq=seq)
        hm = _mlstm_call(qkv, mkt, fc, gr, mlstm_out_norm[l].reshape(1, ML_WIDTH), bsz=bsz, seq=seq)
        xf, hn_tt, gsel, grp, rank, tot = _post_call(
            xf, hf, hm, w_out[l].astype(BF16), g1, norm_ffn[l].reshape(1, d), sc2, sh2, wr, br, tri,
            bsz=bsz, seq=seq, chunk=chunk, te=te)

        n_chunks = t // chunk
        grp = grp.reshape(n_chunks, 1, chunk)
        rank = rank.reshape(n_chunks, 1, chunk)
        off, tile_blk, tile_grp = _moe_tiles(tot[:, :N_GROUPS, 0], tm=tm, tiles_per_chunk=rows // tm)
        pos = rank
        for g in range(N_GROUPS):
            pos = pos + jnp.where(grp == g, off[:, g].reshape(n_chunks, 1, 1), 0)
        xb, gs = _scatter_call(pos, hn_tt, gsel, chunk=chunk, rows=rows, tm=tm)
        ys_tt = _experts_call(tile_blk, tile_grp, xb, gs, w_gate[l].astype(BF16),
                              w_up[l].astype(BF16),
                              w_down[l].astype(BF16).reshape(N_GROUPS, EXPERTS_PER_GROUP * D_FF, d), tm=tm)
        moe_tt = _gather_call(pos, ys_tt, chunk=chunk, rows=rows, n_tok=t)
        g2_prev = g2

    out = _final_call(xf, moe_tt, g2_prev, norm_final.reshape(1, d), seq=seq)
    return out.reshape(bsz, seq, d)
```

```python
import functools

import numpy as np
import jax
import jax.numpy as jnp
from jax import lax
from jax.experimental import pallas as pl
from jax.experimental.pallas import tpu as pltpu

F32 = jnp.float32
BF16 = jnp.bfloat16

LANES = 128
SUBLANES = 8
VMEM_BYTES_V7X = 64 * 1024 * 1024

D_MODEL = 1024
FOX_HEADS = 8
FOX_HEAD_DIM = 64
FOX_WIDTH = FOX_HEADS * FOX_HEAD_DIM
ML_HEADS = 4
ML_HEAD_DIM = 128
ML_WIDTH = ML_HEADS * ML_HEAD_DIM
CONV_WIDTH = 4
N_EXPERTS = 16
N_GROUPS = 4
EXPERTS_PER_GROUP = 4
D_FF = 512
N_ADA = 6
EPS = 1e-6
NEG = -1e30

FOX_Q = 0
FOX_F = 3 * FOX_WIDTH
ML_Q = FOX_F + FOX_HEADS
ML_I = ML_Q + 3 * ML_WIDTH
ML_F = ML_I + ML_HEADS
ML_O = ML_F + ML_HEADS
IN_COLS = ML_O + ML_WIDTH

MAIN_COLS = 7 * 512
BLK_FQ, BLK_FK, BLK_FV, BLK_MQ, BLK_MK, BLK_MV, BLK_MO = range(7)
REST_COLS = 3 * 512
OUT_MQ, OUT_MV, OUT_MO = range(3)
AUG_COLS = FOX_HEADS * 128
BIAS_TERMS = 3
VT_PAD = 16
LOG2E = 1.4426950408889634
GATE_FOX = 0
GATE_ML = FOX_HEADS
GATE_ROWS = 16

TOK_ROWS = D_MODEL // LANES


def _cparams(n_grid, vmem_mb):
    return pltpu.CompilerParams(
        dimension_semantics=("arbitrary",) * n_grid,
        vmem_limit_bytes=vmem_mb * 1024 * 1024)


def _silu(x):
    return x * jax.nn.sigmoid(x)


def _log_sigmoid(z):
    return jnp.minimum(z, 0.0) - jnp.log1p(jnp.exp(-jnp.abs(z)))


def _cumsum_rows(x):
    n = x.shape[0]
    row = lax.broadcasted_iota(jnp.int32, x.shape, 0)
    s = 1
    while s < n:
        x = x + jnp.where(row >= s, pltpu.roll(x, s, axis=0), 0.0)
        s *= 2
    return x


def _from_token_tiles(ref, n_tok):
    return jnp.concatenate(
        [ref[pl.ds(c, n_tok, stride=TOK_ROWS), :] for c in range(TOK_ROWS)], axis=1)


def _to_token_tiles(ref, val, row0=0):
    n = val.shape[0]
    for c in range(TOK_ROWS):
        ref[pl.ds(row0 * TOK_ROWS + c, n, stride=TOK_ROWS), :] = val[:, c * LANES:(c + 1) * LANES]


def _ada_kernel(c_ref, w_ref, b_ref, o_ref):
    c = c_ref[...]
    o_ref[0] = jnp.dot(_silu(c), w_ref[0], preferred_element_type=F32,
                       precision=lax.Precision.HIGHEST) + b_ref[0]


def _ada_call(c, w_ada, b_ada):
    depth, d, n = w_ada.shape
    bsz = c.shape[0]
    tn = 1536
    return pl.pallas_call(
        _ada_kernel,
        grid=(depth, n // tn),
        in_specs=[
            pl.BlockSpec((bsz, d), lambda l, j: (0, 0)),
            pl.BlockSpec((1, d, tn), lambda l, j: (l, 0, j)),
            pl.BlockSpec((1, 1, tn), lambda l, j: (l, 0, j)),
        ],
        out_specs=pl.BlockSpec((1, bsz, tn), lambda l, j: (l, 0, j)),
        out_shape=jax.ShapeDtypeStruct((depth, bsz, n), F32),
        compiler_params=_cparams(2, 32),
        name="ada_mod",
    )(c, w_ada, b_ada.reshape(depth, 1, n))


def _in_proj_kernel(*refs, tm, combine):
    if combine:
        (x_ref, ott_ref, g2_ref, nm_ref, sc_ref, sh_ref, wm_ref, wg_ref, gb_ref, cw_ref, cb_ref,
         place_ref, ones_ref,
         xn_ref, qa_ref, ka_ref, vt_ref, qkv_ref, kt_ref, fc_ref, gr_ref, fcar, ccar) = refs
    else:
        (x_ref, nm_ref, sc_ref, sh_ref, wm_ref, wg_ref, gb_ref, cw_ref, cb_ref,
         place_ref, ones_ref,
         qa_ref, ka_ref, vt_ref, qkv_ref, kt_ref, fc_ref, gr_ref, fcar, ccar) = refs

    @pl.when(pl.program_id(1) == 0)
    def _():
        fcar[...] = jnp.zeros_like(fcar)
        ccar[...] = jnp.zeros_like(ccar)

    x = x_ref[...]
    if combine:
        x = x + g2_ref[0] * _from_token_tiles(ott_ref, tm)
        xn_ref[...] = x
    ms = jnp.mean(x * x, axis=1, keepdims=True)
    hn = x * lax.rsqrt(ms + EPS) * nm_ref[...]
    hn = hn * (1.0 + sc_ref[0]) + sh_ref[0]
    hb = hn.astype(BF16)

    gp = jnp.dot(hb, wg_ref[...], preferred_element_type=F32) + gb_ref[...]
    lf = _log_sigmoid(gp[:, :LANES])
    fcum = _cumsum_rows(lf) + fcar[...]
    fcar[...] = fcum[tm - 1:tm, :]
    gml = gp[:, LANES:] - fcum
    fc_ref[...] = fcum
    gr_ref[0] = gml.T[:GATE_ROWS, :]

    def mm(j):
        return jnp.dot(hb, wm_ref[:, j * 512:(j + 1) * 512], preferred_element_type=F32)

    def put(j, v):
        qkv_ref[:, j * 512:(j + 1) * 512] = v.astype(BF16)

    lane = lax.broadcasted_iota(jnp.int32, (tm, LANES), 1)
    fs = fcum * LOG2E
    hi = fs.astype(BF16).astype(F32)
    mid = (fs - hi).astype(BF16).astype(F32)
    low = (fs - hi - mid).astype(BF16).astype(F32)
    packed = jnp.where(lane < FOX_HEADS, hi,
                       jnp.where(lane < 2 * FOX_HEADS, pltpu.roll(mid, FOX_HEADS, axis=1),
                                 jnp.where(lane < 3 * FOX_HEADS, pltpu.roll(low, 2 * FOX_HEADS, axis=1), 0.0)))
    bias = jnp.dot(packed.astype(BF16), place_ref[...], preferred_element_type=F32) + ones_ref[...]

    def put_heads(ref, val, col0, transposed):
        lo = lane < FOX_HEAD_DIM
        for p in range(FOX_HEADS // 2):
            slab = val[:, p * LANES:(p + 1) * LANES]
            for h, data in ((2 * p, slab), (2 * p + 1, pltpu.roll(slab, FOX_HEAD_DIM, axis=1))):
                blk = jnp.where(lo, data, 0.0) + bias[:, col0 + h * LANES:col0 + (h + 1) * LANES]
                if transposed:
                    ref[0, 0, h * LANES:(h + 1) * LANES, :] = blk.T.astype(BF16)
                else:
                    ref[:, h * LANES:(h + 1) * LANES] = blk.astype(BF16)

    u = jnp.concatenate([mm(BLK_MQ), mm(BLK_MK)], axis=1)
    prev = ccar[...]
    ccar[...] = u[tm - SUBLANES:tm, :]
    row8 = lax.broadcasted_iota(jnp.int32, prev.shape, 0)
    y = cb_ref[...] + cw_ref[CONV_WIDTH - 1:CONV_WIDTH, :] * u
    for k in range(1, CONV_WIDTH):
        r = pltpu.roll(u, k, axis=0)
        top = jnp.where(row8 < k, pltpu.roll(prev, k, axis=0), r[:SUBLANES])
        shifted = jnp.concatenate([top, r[SUBLANES:]], axis=0)
        y = y + cw_ref[CONV_WIDTH - 1 - k:CONV_WIDTH - k, :] * shifted
    act = _silu(y)
    put(OUT_MQ, act[:, :ML_WIDTH])
    kt_ref[0, 0] = (act[:, ML_WIDTH:] * (ML_HEAD_DIM ** -0.5)).T.astype(BF16)

    put(OUT_MO, jax.nn.sigmoid(mm(BLK_MO)))
    put(OUT_MV, mm(BLK_MV))
    vt_ref[0, 0] = mm(BLK_FV).astype(BF16).T
    put_heads(qa_ref, mm(BLK_FQ) * (FOX_HEAD_DIM ** -0.5 * LOG2E), 0, True)
    put_heads(ka_ref, mm(BLK_FK), AUG_COLS, False)


def _bias_placement():
    place = np.zeros((LANES, 2 * AUG_COLS), np.float32)
    ones = np.zeros((1, 2 * AUG_COLS), np.float32)
    for h in range(FOX_HEADS):
        for term in range(BIAS_TERMS):
            src = term * FOX_HEADS + h
            place[src, h * LANES + FOX_HEAD_DIM + term] = 1.0
            ones[0, h * LANES + FOX_HEAD_DIM + BIAS_TERMS + term] = 1.0
            place[src, AUG_COLS + h * LANES + FOX_HEAD_DIM + BIAS_TERMS + term] = -1.0
            ones[0, AUG_COLS + h * LANES + FOX_HEAD_DIM + term] = 1.0
    return jnp.asarray(place, BF16), jnp.asarray(ones, F32)


def _in_proj_call(x, moe_tt, g2, nm, sc, sh, wm, wg, gb, cw, cb, *, bsz, seq, tm=512):
    t, d = x.shape
    ns = seq // tm
    combine = moe_tt is not None
    row = lambda b, s: (b * ns + s, 0)
    per_b = lambda b, s: (b, 0, 0)
    const2 = lambda b, s: (0, 0)
    in_specs = [pl.BlockSpec((tm, d), row)]
    args = [x]
    if combine:
        in_specs += [pl.BlockSpec((tm * TOK_ROWS, LANES), row), pl.BlockSpec((1, 1, d), per_b)]
        args += [moe_tt, g2]
    in_specs += [
        pl.BlockSpec((1, d), const2),
        pl.BlockSpec((1, 1, d), per_b),
        pl.BlockSpec((1, 1, d), per_b),
        pl.BlockSpec((d, MAIN_COLS), const2),
        pl.BlockSpec((d, 2 * LANES), const2),
        pl.BlockSpec((1, 2 * LANES), const2),
        pl.BlockSpec((CONV_WIDTH, 2 * ML_WIDTH), const2),
        pl.BlockSpec((1, 2 * ML_WIDTH), const2),
        pl.BlockSpec((LANES, 2 * AUG_COLS), const2),
        pl.BlockSpec((1, 2 * AUG_COLS), const2),
    ]
    place, ones = _bias_placement()
    args += [nm, sc, sh, wm, wg, gb, cw, cb, place, ones]
    out_specs = []
    out_shape = []
    if combine:
        out_specs.append(pl.BlockSpec((tm, d), row))
        out_shape.append(jax.ShapeDtypeStruct((t, d), F32))
    tiled = lambda b, s: (b, s, 0, 0)
    out_specs += [
        pl.BlockSpec((1, 1, AUG_COLS, tm), tiled),
        pl.BlockSpec((tm, AUG_COLS), row),
        pl.BlockSpec((1, 1, FOX_WIDTH, tm), tiled),
        pl.BlockSpec((tm, REST_COLS), row),
        pl.BlockSpec((1, 1, ML_WIDTH, tm), tiled),
        pl.BlockSpec((tm, LANES), row),
        pl.BlockSpec((1, GATE_ROWS, tm), lambda b, s: (b, 0, s)),
    ]
    out_shape += [
        jax.ShapeDtypeStruct((bsz, ns, AUG_COLS, tm), BF16),
        jax.ShapeDtypeStruct((t, AUG_COLS), BF16),
        jax.ShapeDtypeStruct((bsz, ns, FOX_WIDTH, tm), BF16),
        jax.ShapeDtypeStruct((t, REST_COLS), BF16),
        jax.ShapeDtypeStruct((bsz, ns, ML_WIDTH, tm), BF16),
        jax.ShapeDtypeStruct((t, LANES), F32),
        jax.ShapeDtypeStruct((bsz, GATE_ROWS, seq), F32),
    ]
    outs = pl.pallas_call(
        functools.partial(_in_proj_kernel, tm=tm, combine=combine),
        grid=(bsz, ns),
        in_specs=in_specs,
        out_specs=out_specs,
        out_shape=out_shape,
        scratch_shapes=[pltpu.VMEM((1, LANES), F32), pltpu.VMEM((SUBLANES, 2 * ML_WIDTH), F32)],
        compiler_params=_cparams(2, 48),
        name="in_proj",
    )(*args)
    if combine:
        return outs
    return [x] + list(outs)


def _fox_kernel(qt_ref, k_ref, vt_ref, ng_ref, cm_ref, o_ref, m_sc, acc_sc, *, tq, tk, nh):
    qi = pl.program_id(2)
    ones_rows = jnp.where(lax.broadcasted_iota(jnp.int32, (VT_PAD, tk), 0) == 0, 1.0, 0.0).astype(BF16)
    for h in range(nh):
        m_sc[h] = jnp.full((SUBLANES, tq), NEG, F32)
        acc_sc[h] = jnp.zeros((FOX_HEAD_DIM + VT_PAD, tq), F32)

    def body(j, carry, diagonal):
        k0 = pl.multiple_of(j * tk, tk)

        def logits(h):
            hs = slice(h * LANES, (h + 1) * LANES)
            z = jnp.dot(k_ref[pl.ds(k0, tk), hs], qt_ref[0, 0, hs, :], preferred_element_type=F32)
            return z + cm_ref[...] if diagonal else z

        ahead = 2
        zs = [logits(h) for h in range(min(ahead, nh))]
        for h in range(nh):
            ds_ = slice(h * FOX_HEAD_DIM, (h + 1) * FOX_HEAD_DIM)
            z = zs[h]
            if h + ahead < nh:
                zs.append(logits(h + ahead))
            m_prev = m_sc[h]
            m_new = jnp.maximum(m_prev, jnp.max(z, axis=0, keepdims=True))
            alpha = jnp.exp2(m_prev - m_new)
            p = jnp.exp2(z - m_new[:1, :])
            va = jnp.concatenate([vt_ref[0, j, ds_, :], ones_rows], axis=0)
            acc_sc[h] = alpha[:1, :] * acc_sc[h] + jnp.dot(va, p.astype(BF16), preferred_element_type=F32)
            m_sc[h] = m_new
        return carry

    lax.fori_loop(0, qi, functools.partial(body, diagonal=False), 0)
    lax.fori_loop(qi, qi + 1, functools.partial(body, diagonal=True), 0)

    for p in range(nh // 2):
        outs = []
        for h in (2 * p, 2 * p + 1):
            acc = acc_sc[h]
            num = acc[:FOX_HEAD_DIM, :]
            l = acc[FOX_HEAD_DIM:FOX_HEAD_DIM + 1, :]
            ms = jnp.mean(num * num, axis=0, keepdims=True)
            outs.append(num * lax.rsqrt(ms + EPS * l * l))
        ps = slice(p * LANES, (p + 1) * LANES)
        o_ref[:, ps] = (jnp.concatenate(outs, axis=0).T * ng_ref[:, ps]).astype(BF16)


def _fox_call(qat, ka, fvt, ng, *, bsz, seq, tq=512, nh=8):
    t = ka.shape[0]
    nq = seq // tq
    ngrp = FOX_HEADS // nh
    vw = nh * FOX_HEAD_DIM
    cmask = jnp.where(jnp.arange(tq)[:, None] <= jnp.arange(tq)[None, :], 0.0, NEG).astype(F32)
    return pl.pallas_call(
        functools.partial(_fox_kernel, tq=tq, tk=tq, nh=nh),
        grid=(bsz, ngrp, nq),
        in_specs=[
            pl.BlockSpec((1, 1, nh * LANES, tq), lambda b, p, i: (b, i, p, 0)),
            pl.BlockSpec((seq, nh * LANES), lambda b, p, i: (b, p)),
            pl.BlockSpec((1, nq, vw, tq), lambda b, p, i: (b, 0, p, 0)),
            pl.BlockSpec((1, vw), lambda b, p, i: (0, p)),
            pl.BlockSpec((tq, tq), lambda b, p, i: (0, 0)),
        ],
        out_specs=pl.BlockSpec((tq, vw), lambda b, p, i: (b * nq + i, p)),
        out_shape=jax.ShapeDtypeStruct((t, FOX_WIDTH), BF16),
        scratch_shapes=[pltpu.VMEM((nh, SUBLANES, tq), F32),
                        pltpu.VMEM((nh, FOX_HEAD_DIM + VT_PAD, tq), F32)],
        compiler_params=_cparams(3, 48),
        name="fox_attn",
    )(qat, ka, fvt, ng, cmask)


def _mlstm_kernel(q_ref, kt_ref, v_ref, og_ref, fc_ref, gr_ref, ng_ref, o_ref, ct_sc, u_sc, *, ch, per_tile):
    @pl.when(pl.program_id(1) == 0)
    def _():
        ct_sc[...] = jnp.zeros_like(ct_sc)
        u_sc[...] = jnp.zeros_like(u_sc)

    causal = (lax.broadcasted_iota(jnp.int32, (ch, ch), 1)
              <= lax.broadcasted_iota(jnp.int32, (ch, ch), 0))
    lane = lax.broadcasted_iota(jnp.int32, (ch, LANES), 1)
    fc = fc_ref[...]
    koff = pl.multiple_of((pl.program_id(1) % per_tile) * ch, ch)
    for h in range(ML_HEADS):
        sl = slice(h * ML_HEAD_DIM, (h + 1) * ML_HEAD_DIM)
        gl = GATE_ML + h
        q = q_ref[:, sl]
        kt = kt_ref[0, 0, sl, pl.ds(koff, ch)]
        vp = jnp.concatenate([v_ref[:, sl], jnp.where(lane == gl, 1.0, 0.0).astype(BF16)], axis=1)
        g_row = gr_ref[0, gl:gl + 1, :]
        u_prev = u_sc[h][:, :1]
        gm = jnp.where(causal, g_row, NEG)
        u_i = jnp.maximum(u_prev, jnp.max(gm, axis=1, keepdims=True))
        dmat = jnp.exp(gm - u_i)
        s = jnp.dot(q, kt, preferred_element_type=F32)
        scores = (s * dmat).astype(BF16)
        inter = jnp.exp(u_prev - u_i)
        ct = ct_sc[h]
        nd = (jnp.dot(scores, vp, preferred_element_type=F32)
              + jnp.dot(q, ct.astype(BF16), preferred_element_type=F32) * inter)
        num = nd[:, :ML_HEAD_DIM]
        den = jnp.maximum(jnp.abs(nd[:, ML_HEAD_DIM:]), jnp.exp(-(jnp.where(lane == gl, fc, 0.0) + u_i)))
        ms = jnp.mean(num * num, axis=1, keepdims=True)
        scale = lax.rsqrt(ms + EPS * den * den)[:, gl:gl + 1]
        y = num * scale * ng_ref[:, sl] * og_ref[:, sl].astype(F32)
        o_ref[:, sl] = y.astype(BF16)
        u_new = jnp.maximum(u_prev, jnp.max(g_row, axis=1, keepdims=True))
        ktw = (kt.astype(F32) * jnp.exp(g_row - u_new)).astype(BF16)
        ct_sc[h] = jnp.exp(u_prev - u_new) * ct + jnp.dot(ktw, vp, preferred_element_type=F32)
        u_sc[h] = jnp.broadcast_to(u_new, (1, LANES))


def _mlstm_call(qkv, mkt, fc, gr, ng, *, bsz, seq, ch=512):
    t = qkv.shape[0]
    nc = seq // ch
    ktile = mkt.shape[-1]
    per_tile = ktile // ch
    row = lambda b, c: (b * nc + c, 0)
    return pl.pallas_call(
        functools.partial(_mlstm_kernel, ch=ch, per_tile=per_tile),
        grid=(bsz, nc),
        in_specs=[
            pl.BlockSpec((ch, ML_WIDTH), lambda b, c: (b * nc + c, OUT_MQ)),
            pl.BlockSpec((1, 1, ML_WIDTH, ktile), lambda b, c: (b, c // per_tile, 0, 0)),
            pl.BlockSpec((ch, ML_WIDTH), lambda b, c: (b * nc + c, OUT_MV)),
            pl.BlockSpec((ch, ML_WIDTH), lambda b, c: (b * nc + c, OUT_MO)),
            pl.BlockSpec((ch, LANES), row),
            pl.BlockSpec((1, GATE_ROWS, ch), lambda b, c: (b, 0, c)),
            pl.BlockSpec((1, ML_WIDTH), lambda b, c: (0, 0)),
        ],
        out_specs=pl.BlockSpec((ch, ML_WIDTH), row),
        out_shape=jax.ShapeDtypeStruct((t, ML_WIDTH), BF16),
        scratch_shapes=[pltpu.VMEM((ML_HEADS, ML_HEAD_DIM, 2 * ML_HEAD_DIM), F32),
                        pltpu.VMEM((ML_HEADS, 1, LANES), F32)],
        compiler_params=_cparams(2, 32),
        name="mlstm",
    )(qkv, mkt, qkv, qkv, fc, gr, ng)


def _post_kernel(x_ref, hf_ref, hm_ref, wo_ref, g1_ref, nf_ref, sc_ref, sh_ref, wr_ref, br_ref, tri_ref,
                 x1_ref, hn_ref, gsel_ref, grp_ref, rank_ref, tot_ref, cnt_sc, *, te, steps_per_chunk):
    @pl.when(pl.program_id(0) % steps_per_chunk == 0)
    def _():
        cnt_sc[...] = jnp.zeros_like(cnt_sc)

    mix = (jnp.dot(hf_ref[...], wo_ref[:FOX_WIDTH, :], preferred_element_type=F32)
           + jnp.dot(hm_ref[...], wo_ref[FOX_WIDTH:, :], preferred_element_type=F32))
    x1 = x_ref[...] + g1_ref[0] * mix
    x1_ref[...] = x1
    ms = jnp.mean(x1 * x1, axis=1, keepdims=True)
    hn = x1 * lax.rsqrt(ms + EPS) * nf_ref[...]
    hn = hn * (1.0 + sc_ref[0]) + sh_ref[0]
    _to_token_tiles(hn_ref, hn)

    hn_hi = hn.astype(BF16)
    hn_lo = (hn - hn_hi.astype(F32)).astype(BF16)
    l2 = jnp.dot(hn_hi, wr_ref[...], preferred_element_type=F32)
    logits = (l2[:, :LANES] + l2[:, LANES:]
              + jnp.dot(hn_lo, wr_ref[:, :LANES], preferred_element_type=F32))
    aff = jax.nn.sigmoid(logits.T[:N_EXPERTS, :])
    sel = aff + br_ref[...]
    selr = [sel[e:e + 1, :] for e in range(N_EXPERTS)]
    affr = [aff[e:e + 1, :] for e in range(N_EXPERTS)]
    keep = [None] * N_EXPERTS
    score = []
    for g in range(N_GROUPS):
        vs = selr[g * EXPERTS_PER_GROUP:(g + 1) * EXPERTS_PER_GROUP]
        sg = jnp.zeros_like(vs[0])
        for i in range(EXPERTS_PER_GROUP):
            beaten = jnp.zeros_like(vs[0])
            for j in range(EXPERTS_PER_GROUP):
                if j != i:
                    b = (vs[j] >= vs[i]) if j < i else (vs[j] > vs[i])
                    beaten = beaten + jnp.where(b, 1.0, 0.0)
            kp = beaten < 2.0
            keep[g * EXPERTS_PER_GROUP + i] = kp
            sg = sg + jnp.where(kp, vs[i], 0.0)
        score.append(sg)
    chosen = []
    for g in range(N_GROUPS):
        lost = jnp.zeros_like(score[0])
        for g2 in range(N_GROUPS):
            if g2 != g:
                b = (score[g2] >= score[g]) if g2 < g else (score[g2] > score[g])
                lost = lost + jnp.where(b, 1.0, 0.0)
        chosen.append(jnp.where(lost < 0.5, 1.0, 0.0))
    wsel = []
    for i in range(EXPERTS_PER_GROUP):
        wi = jnp.zeros_like(score[0])
        for g in range(N_GROUPS):
            e = g * EXPERTS_PER_GROUP + i
            wi = wi + chosen[g] * jnp.where(keep[e], affr[e], 0.0)
        wsel.append(wi)
    wsum = wsel[0] + wsel[1] + wsel[2] + wsel[3]
    wsel = [w / wsum for w in wsel]

    row8 = lax.broadcasted_iota(jnp.int32, (SUBLANES, te), 0)
    gmat = jnp.zeros((SUBLANES, te), F32)
    wmat = jnp.zeros((SUBLANES, te), F32)
    grp = jnp.zeros_like(score[0])
    for g in range(N_GROUPS):
        gmat = jnp.where(row8 == g, chosen[g], gmat)
        wmat = jnp.where(row8 == g, wsel[g], wmat)
        grp = grp + g * chosen[g]
    pref = jnp.dot(gmat.astype(BF16), tri_ref[...], preferred_element_type=F32) + cnt_sc[:, :1]
    rank = jnp.sum(gmat * pref, axis=0, keepdims=True)
    tot = cnt_sc[:, :1] + jnp.sum(gmat, axis=1, keepdims=True)
    cnt_sc[...] = jnp.broadcast_to(tot, cnt_sc.shape)
    rank_ref[0] = rank.astype(jnp.int32)
    grp_ref[0] = grp.astype(jnp.int32)
    tot_ref[0] = jnp.broadcast_to(tot, (SUBLANES, LANES)).astype(jnp.int32)
    wfull = jnp.concatenate([wmat, jnp.zeros((LANES - SUBLANES, te), F32)], axis=0)
    gsel_ref[...] = wfull.T


def _post_call(x, hf, hm, wo, g1, nf, sc, sh, wr, br, tri, *, bsz, seq, chunk, te=512):
    t, d = x.shape
    steps_per_chunk = chunk // te
    per_seq = seq // te
    n_steps = t // te
    n_chunks = t // chunk
    row = lambda i: (i, 0)
    per_b = lambda i: (i // per_seq, 0, 0)
    const2 = lambda i: (0, 0)
    return pl.pallas_call(
        functools.partial(_post_kernel, te=te, steps_per_chunk=steps_per_chunk),
        grid=(n_steps,),
        in_specs=[
            pl.BlockSpec((te, d), row),
            pl.BlockSpec((te, FOX_WIDTH), row),
            pl.BlockSpec((te, ML_WIDTH), row),
            pl.BlockSpec((d, d), const2),
            pl.BlockSpec((1, 1, d), per_b),
            pl.BlockSpec((1, d), const2),
            pl.BlockSpec((1, 1, d), per_b),
            pl.BlockSpec((1, 1, d), per_b),
            pl.BlockSpec((d, 2 * LANES), const2),
            pl.BlockSpec((N_EXPERTS, 1), const2),
            pl.BlockSpec((te, te), const2),
        ],
        out_specs=[
            pl.BlockSpec((te, d), row),
            pl.BlockSpec((te * TOK_ROWS, LANES), row),
            pl.BlockSpec((te, LANES), row),
            pl.BlockSpec((1, 1, te), lambda i: (i, 0, 0)),
            pl.BlockSpec((1, 1, te), lambda i: (i, 0, 0)),
            pl.BlockSpec((1, SUBLANES, LANES), lambda i: (i // steps_per_chunk, 0, 0)),
        ],
        out_shape=[
            jax.ShapeDtypeStruct((t, d), F32),
            jax.ShapeDtypeStruct((t * TOK_ROWS, LANES), F32),
            jax.ShapeDtypeStruct((t, LANES), F32),
            jax.ShapeDtypeStruct((n_steps, 1, te), jnp.int32),
            jax.ShapeDtypeStruct((n_steps, 1, te), jnp.int32),
            jax.ShapeDtypeStruct((n_chunks, SUBLANES, LANES), jnp.int32),
        ],
        scratch_shapes=[pltpu.VMEM((SUBLANES, LANES), F32)],
        compiler_params=_cparams(1, 48),
        name="post_router",
    )(x, hf, hm, wo, g1, nf, sc, sh, wr, br, tri)


def _scatter_kernel(pos_ref, hn_ref, gsel_ref, xb_ref, gs_ref, xs_sc, *, chunk, rows, tm):
    xs_sc[...] = jnp.zeros_like(xs_sc)
    gs_ref[...] = jnp.zeros_like(gs_ref)

    def body(t, carry):
        p = pos_ref[0, 0, t]
        src = pl.multiple_of(t * TOK_ROWS, TOK_ROWS)
        dst = pl.multiple_of(p * TOK_ROWS, TOK_ROWS)
        xs_sc[pl.ds(dst, TOK_ROWS), :] = hn_ref[pl.ds(src, TOK_ROWS), :]
        gs_ref[pl.ds(p, 1), :] = gsel_ref[pl.ds(t, 1), :]
        return carry

    lax.fori_loop(0, chunk, body, 0, unroll=8)
    for j in range(rows // tm):
        for c in range(TOK_ROWS):
            xb_ref[j * tm:(j + 1) * tm, c * LANES:(c + 1) * LANES] = (
                xs_sc[pl.ds(j * tm * TOK_ROWS + c, tm, stride=TOK_ROWS), :].astype(BF16))


def _scatter_call(pos, hn_tt, gsel, *, chunk, rows, tm):
    t = gsel.shape[0]
    n_chunks = t // chunk
    return pl.pallas_call(
        functools.partial(_scatter_kernel, chunk=chunk, rows=rows, tm=tm),
        grid=(n_chunks,),
        in_specs=[
            pl.BlockSpec((1, 1, chunk), lambda c: (c, 0, 0), memory_space=pltpu.SMEM),
            pl.BlockSpec((chunk * TOK_ROWS, LANES), lambda c: (c, 0)),
            pl.BlockSpec((chunk, LANES), lambda c: (c, 0)),
        ],
        out_specs=[
            pl.BlockSpec((rows, D_MODEL), lambda c: (c, 0)),
            pl.BlockSpec((rows, LANES), lambda c: (c, 0)),
        ],
        out_shape=[
            jax.ShapeDtypeStruct((n_chunks * rows, D_MODEL), BF16),
            jax.ShapeDtypeStruct((n_chunks * rows, LANES), F32),
        ],
        scratch_shapes=[pltpu.VMEM((rows * TOK_ROWS, LANES), F32)],
        compiler_params=_cparams(1, 56),
        name="moe_scatter",
    )(pos, hn_tt, gsel)


def _experts_kernel(blk_ref, grp_ref, xb_ref, gs_ref, wg_ref, wu_ref, wd_ref, y_ref, *, tm):
    g = grp_ref[pl.program_id(0)]

    @pl.when(g < N_GROUPS)
    def _():
        x = xb_ref[...]
        gs = gs_ref[...]
        acts = []
        for i in range(EXPERTS_PER_GROUP):
            hg = jnp.dot(x, wg_ref[i], preferred_element_type=F32)
            hu = jnp.dot(x, wu_ref[i], preferred_element_type=F32)
            acts.append((_silu(hg) * hu * gs[:, i:i + 1]).astype(BF16))
        y = jnp.dot(jnp.concatenate(acts, axis=1), wd_ref[0], preferred_element_type=F32)
        _to_token_tiles(y_ref, y)

    @pl.when(g >= N_GROUPS)
    def _():
        y_ref[...] = jnp.zeros_like(y_ref)


def _experts_call(tile_blk, tile_grp, xb, gs, wg, wu, wd, *, tm):
    n_rows = xb.shape[0]
    n_slots = n_rows // tm
    wmap = lambda s, blk, grp: (jnp.minimum(grp[s], N_GROUPS - 1), 0, 0)
    grid_spec = pltpu.PrefetchScalarGridSpec(
        num_scalar_prefetch=2,
        grid=(n_slots,),
        in_specs=[
            pl.BlockSpec((tm, D_MODEL), lambda s, blk, grp: (blk[s], 0)),
            pl.BlockSpec((tm, LANES), lambda s, blk, grp: (blk[s], 0)),
            pl.BlockSpec((EXPERTS_PER_GROUP, D_MODEL, D_FF), wmap),
            pl.BlockSpec((EXPERTS_PER_GROUP, D_MODEL, D_FF), wmap),
            pl.BlockSpec((1, EXPERTS_PER_GROUP * D_FF, D_MODEL), wmap),
        ],
        out_specs=pl.BlockSpec((tm * TOK_ROWS, LANES), lambda s, blk, grp: (blk[s], 0)),
    )
    return pl.pallas_call(
        functools.partial(_experts_kernel, tm=tm),
        grid_spec=grid_spec,
        out_shape=jax.ShapeDtypeStruct((n_rows * TOK_ROWS, LANES), F32),
        compiler_params=_cparams(1, 48),
        name="moe_experts",
    )(tile_blk, tile_grp, xb, gs, wg, wu, wd)


def _gather_kernel(pos_ref, ys_ref, o_ref, *, chunk):
    def body(t, carry):
        p = pos_ref[0, 0, t]
        src = pl.multiple_of(p * TOK_ROWS, TOK_ROWS)
        dst = pl.multiple_of(t * TOK_ROWS, TOK_ROWS)
        o_ref[pl.ds(dst, TOK_ROWS), :] = ys_ref[pl.ds(src, TOK_ROWS), :]
        return carry

    lax.fori_loop(0, chunk, body, 0, unroll=8)


def _gather_call(pos, ys_tt, *, chunk, rows, n_tok):
    n_chunks = n_tok // chunk
    return pl.pallas_call(
        functools.partial(_gather_kernel, chunk=chunk),
        grid=(n_chunks,),
        in_specs=[
            pl.BlockSpec((1, 1, chunk), lambda c: (c, 0, 0), memory_space=pltpu.SMEM),
            pl.BlockSpec((rows * TOK_ROWS, LANES), lambda c: (c, 0)),
        ],
        out_specs=pl.BlockSpec((chunk * TOK_ROWS, LANES), lambda c: (c, 0)),
        out_shape=jax.ShapeDtypeStruct((n_tok * TOK_ROWS, LANES), F32),
        compiler_params=_cparams(1, 56),
        name="moe_gather",
    )(pos, ys_tt)


def _final_kernel(x_ref, ott_ref, g2_ref, nf_ref, o_ref, *, tm):
    x = x_ref[...] + g2_ref[0] * _from_token_tiles(ott_ref, tm)
    ms = jnp.mean(x * x, axis=1, keepdims=True)
    o_ref[...] = x * lax.rsqrt(ms + EPS) * nf_ref[...]


def _final_call(x, moe_tt, g2, nf, *, seq, tm=512):
    t, d = x.shape
    per_seq = seq // tm
    return pl.pallas_call(
        functools.partial(_final_kernel, tm=tm),
        grid=(t // tm,),
        in_specs=[
            pl.BlockSpec((tm, d), lambda i: (i, 0)),
            pl.BlockSpec((tm * TOK_ROWS, LANES), lambda i: (i, 0)),
            pl.BlockSpec((1, 1, d), lambda i: (i // per_seq, 0, 0)),
            pl.BlockSpec((1, d), lambda i: (0, 0)),
        ],
        out_specs=pl.BlockSpec((tm, d), lambda i: (i, 0)),
        out_shape=jax.ShapeDtypeStruct((t, d), F32),
        compiler_params=_cparams(1, 32),
        name="final_norm",
    )(x, moe_tt, g2, nf)


def _moe_tiles(tot, *, tm, tiles_per_chunk):
    nt = (tot + tm - 1) // tm
    ts = jnp.cumsum(nt, axis=1) - nt
    off = (ts * tm).astype(jnp.int32)
    j = jnp.arange(tiles_per_chunk, dtype=jnp.int32)[None, :, None]
    inside = (j >= ts[:, None, :]) & (j < (ts + nt)[:, None, :])
    key = jnp.where(jnp.any(inside, axis=-1), jnp.argmax(inside, axis=-1), N_GROUPS).reshape(-1)
    order = jnp.argsort(key, stable=True).astype(jnp.int32)
    return off, order, key[order].astype(jnp.int32)


def kernel(x, c, w_in, conv_w, conv_b, fox_f_bias, mlstm_i_bias, mlstm_f_bias, fox_out_norm,
           mlstm_out_norm, w_out, w_ada, b_ada, norm_mix, norm_ffn, w_router, b_router, w_gate,
           w_up, w_down, norm_final):
    bsz, seq, d = x.shape
    depth = w_in.shape[0]
    t = bsz * seq
    assert d == D_MODEL and w_in.shape[-1] == IN_COLS
    chunk = min(2048, seq)
    tm = 256
    rows = -(-(chunk + N_GROUPS * (tm - 1)) // tm) * tm
    te = min(1024, seq)
    assert seq % 512 == 0 and seq % chunk == 0 and chunk % te == 0

    mods = _ada_call(c, w_ada, b_ada).reshape(depth, bsz, N_ADA, 1, d)
    xf = x.reshape(t, d)

    tri = (jnp.arange(te)[:, None] < jnp.arange(te)[None, :]).astype(BF16)
    wr_f = jnp.pad(w_router, ((0, 0), (0, LANES - N_EXPERTS))).astype(F32)
    wr_hi = wr_f.astype(BF16)
    wr = jnp.concatenate([wr_hi, (wr_f - wr_hi.astype(F32)).astype(BF16)], axis=1)
    br = b_router.reshape(N_EXPERTS, 1).astype(F32)

    moe_tt = None
    g2_prev = None
    for l in range(depth):
        sh1, sc1, g1, sh2, sc2, g2 = [mods[l, :, i] for i in range(N_ADA)]
        wl = w_in[l]
        wm = jnp.concatenate([wl[:, FOX_Q:FOX_F], wl[:, ML_Q:ML_I], wl[:, ML_O:IN_COLS]], axis=1).astype(BF16)
        zpad = lambda n: jnp.zeros((d, n), F32)
        wg = jnp.concatenate([
            wl[:, FOX_F:ML_Q], wl[:, ML_F:ML_O], zpad(LANES - FOX_HEADS - ML_HEADS),
            zpad(GATE_ML), wl[:, ML_I:ML_F], zpad(LANES - GATE_ML - ML_HEADS)], axis=1).astype(BF16)
        zb = lambda n: jnp.zeros((n,), F32)
        gb = jnp.concatenate([
            fox_f_bias[l], mlstm_f_bias[l], zb(LANES - FOX_HEADS - ML_HEADS),
            zb(GATE_ML), mlstm_i_bias[l], zb(LANES - GATE_ML - ML_HEADS)]).reshape(1, 2 * LANES)

        xf, qat, ka, fvt, qkv, mkt, fc, gr = _in_proj_call(
            xf, moe_tt, g2_prev, norm_mix[l].reshape(1, d), sc1, sh1, wm, wg, gb,
            conv_w[l], conv_b[l].reshape(1, -1), bsz=bsz, seq=seq)
        hf = _fox_call(qat, ka, fvt, fox_out_norm[l].reshape(1, FOX_WIDTH), bsz=bsz, seq=seq)
        hm = _mlstm_call(qkv, mkt, fc, gr, mlstm_out_norm[l].reshape(1, ML_WIDTH), bsz=bsz, seq=seq)
        xf, hn_tt, gsel, grp, rank, tot = _post_call(
            xf, hf, hm, w_out[l].astype(BF16), g1, norm_ffn[l].reshape(1, d), sc2, sh2, wr, br, tri,
            bsz=bsz, seq=seq, chunk=chunk, te=te)

        n_chunks = t // chunk
        grp = grp.reshape(n_chunks, 1, chunk)
        rank = rank.reshape(n_chunks, 1, chunk)
        off, tile_blk, tile_grp = _moe_tiles(tot[:, :N_GROUPS, 0], tm=tm, tiles_per_chunk=rows // tm)
        pos = rank
        for g in range(N_GROUPS):
            pos = pos + jnp.where(grp == g, off[:, g].reshape(n_chunks, 1, 1), 0)
        xb, gs = _scatter_call(pos, hn_tt, gsel, chunk=chunk, rows=rows, tm=tm)
        ys_tt = _experts_call(tile_blk, tile_grp, xb, gs, w_gate[l].astype(BF16),
                              w_up[l].astype(BF16),
                              w_down[l].astype(BF16).reshape(N_GROUPS, EXPERTS_PER_GROUP * D_FF, d), tm=tm)
        moe_tt = _gather_call(pos, ys_tt, chunk=chunk, rows=rows, n_tok=t)
        g2_prev = g2

    out = _final_call(xf, moe_tt, g2_prev, norm_final.reshape(1, d), seq=seq)
    return out.reshape(bsz, seq, d)
```

```python
import functools

import numpy as np
import jax
import jax.numpy as jnp
from jax import lax
from jax.experimental import pallas as pl
from jax.experimental.pallas import tpu as pltpu

F32 = jnp.float32
BF16 = jnp.bfloat16

LANES = 128
SUBLANES = 8
VMEM_BYTES_V7X = 64 * 1024 * 1024

D_MODEL = 1024
FOX_HEADS = 8
FOX_HEAD_DIM = 64
FOX_WIDTH = FOX_HEADS * FOX_HEAD_DIM
ML_HEADS = 4
ML_HEAD_DIM = 128
ML_WIDTH = ML_HEADS * ML_HEAD_DIM
CONV_WIDTH = 4
N_EXPERTS = 16
N_GROUPS = 4
EXPERTS_PER_GROUP = 4
D_FF = 512
N_ADA = 6
EPS = 1e-6
NEG = -1e30

FOX_Q = 0
FOX_F = 3 * FOX_WIDTH
ML_Q = FOX_F + FOX_HEADS
ML_I = ML_Q + 3 * ML_WIDTH
ML_F = ML_I + ML_HEADS
ML_O = ML_F + ML_HEADS
IN_COLS = ML_O + ML_WIDTH

MAIN_COLS = 7 * 512
BLK_FQ, BLK_FK, BLK_FV, BLK_MQ, BLK_MK, BLK_MV, BLK_MO = range(7)
REST_COLS = 3 * 512
OUT_MQ, OUT_MV, OUT_MO = range(3)
AUG_COLS = FOX_HEADS * 128
BIAS_TERMS = 3
VT_PAD = 16
LOG2E = 1.4426950408889634
GATE_FOX = 0
GATE_ML = FOX_HEADS
GATE_ROWS = 16

TOK_ROWS = D_MODEL // LANES


def _cparams(n_grid, vmem_mb):
    return pltpu.CompilerParams(
        dimension_semantics=("arbitrary",) * n_grid,
        vmem_limit_bytes=vmem_mb * 1024 * 1024)


def _silu(x):
    return x * jax.nn.sigmoid(x)


def _log_sigmoid(z):
    return jnp.minimum(z, 0.0) - jnp.log1p(jnp.exp(-jnp.abs(z)))


def _cumsum_rows(x):
    n = x.shape[0]
    row = lax.broadcasted_iota(jnp.int32, x.shape, 0)
    s = 1
    while s < n:
        x = x + jnp.where(row >= s, pltpu.roll(x, s, axis=0), 0.0)
        s *= 2
    return x


def _from_token_tiles(ref, n_tok):
    return jnp.concatenate(
        [ref[pl.ds(c, n_tok, stride=TOK_ROWS), :] for c in range(TOK_ROWS)], axis=1)


def _to_token_tiles(ref, val, row0=0):
    n = val.shape[0]
    for c in range(TOK_ROWS):
        ref[pl.ds(row0 * TOK_ROWS + c, n, stride=TOK_ROWS), :] = val[:, c * LANES:(c + 1) * LANES]


def _ada_kernel(c_ref, w_ref, b_ref, o_ref):
    c = c_ref[...]
    o_ref[0] = jnp.dot(_silu(c), w_ref[0], preferred_element_type=F32,
                       precision=lax.Precision.HIGHEST) + b_ref[0]


def _ada_call(c, w_ada, b_ada):
    depth, d, n = w_ada.shape
    bsz = c.shape[0]
    tn = 1536
    return pl.pallas_call(
        _ada_kernel,
        grid=(depth, n // tn),
        in_specs=[
            pl.BlockSpec((bsz, d), lambda l, j: (0, 0)),
            pl.BlockSpec((1, d, tn), lambda l, j: (l, 0, j)),
            pl.BlockSpec((1, 1, tn), lambda l, j: (l, 0, j)),
        ],
        out_specs=pl.BlockSpec((1, bsz, tn), lambda l, j: (l, 0, j)),
        out_shape=jax.ShapeDtypeStruct((depth, bsz, n), F32),
        compiler_params=_cparams(2, 32),
        name="ada_mod",
    )(c, w_ada, b_ada.reshape(depth, 1, n))


def _in_proj_kernel(*refs, tm, combine):
    if combine:
        (x_ref, ott_ref, g2_ref, nm_ref, sc_ref, sh_ref, wm_ref, wg_ref, gb_ref, cw_ref, cb_ref,
         place_ref, ones_ref,
         xn_ref, qa_ref, ka_ref, vt_ref, qkv_ref, kt_ref, fc_ref, gr_ref, fcar, ccar) = refs
    else:
        (x_ref, nm_ref, sc_ref, sh_ref, wm_ref, wg_ref, gb_ref, cw_ref, cb_ref,
         place_ref, ones_ref,
         qa_ref, ka_ref, vt_ref, qkv_ref, kt_ref, fc_ref, gr_ref, fcar, ccar) = refs

    @pl.when(pl.program_id(1) == 0)
    def _():
        fcar[...] = jnp.zeros_like(fcar)
        ccar[...] = jnp.zeros_like(ccar)

    x = x_ref[...]
    if combine:
        x = x + g2_ref[0] * _from_token_tiles(ott_ref, tm)
        xn_ref[...] = x
    ms = jnp.mean(x * x, axis=1, keepdims=True)
    hn = x * lax.rsqrt(ms + EPS) * nm_ref[...]
    hn = hn * (1.0 + sc_ref[0]) + sh_ref[0]
    hb = hn.astype(BF16)

    gp = jnp.dot(hb, wg_ref[...], preferred_element_type=F32) + gb_ref[...]
    lf = _log_sigmoid(gp[:, :LANES])
    fcum = _cumsum_rows(lf) + fcar[...]
    fcar[...] = fcum[tm - 1:tm, :]
    gml = gp[:, LANES:] - fcum
    fc_ref[...] = fcum
    gr_ref[0] = gml.T[:GATE_ROWS, :]

    def mm(j):
        return jnp.dot(hb, wm_ref[:, j * 512:(j + 1) * 512], preferred_element_type=F32)

    def put(j, v):
        qkv_ref[:, j * 512:(j + 1) * 512] = v.astype(BF16)

    lane = lax.broadcasted_iota(jnp.int32, (tm, LANES), 1)
    fs = fcum * LOG2E
    hi = fs.astype(BF16).astype(F32)
    mid = (fs - hi).astype(BF16).astype(F32)
    low = (fs - hi - mid).astype(BF16).astype(F32)
    packed = jnp.where(lane < FOX_HEADS, hi,
                       jnp.where(lane < 2 * FOX_HEADS, pltpu.roll(mid, FOX_HEADS, axis=1),
                                 jnp.where(lane < 3 * FOX_HEADS, pltpu.roll(low, 2 * FOX_HEADS, axis=1), 0.0)))
    bias = jnp.dot(packed.astype(BF16), place_ref[...], preferred_element_type=F32) + ones_ref[...]

    def put_heads(ref, val, col0, transposed):
        lo = lane < FOX_HEAD_DIM
        for p in range(FOX_HEADS // 2):
            slab = val[:, p * LANES:(p + 1) * LANES]
            for h, data in ((2 * p, slab), (2 * p + 1, pltpu.roll(slab, FOX_HEAD_DIM, axis=1))):
                blk = jnp.where(lo, data, 0.0) + bias[:, col0 + h * LANES:col0 + (h + 1) * LANES]
                if transposed:
                    ref[0, 0, h * LANES:(h + 1) * LANES, :] = blk.T.astype(BF16)
                else:
                    ref[:, h * LANES:(h + 1) * LANES] = blk.astype(BF16)

    u = jnp.concatenate([mm(BLK_MQ), mm(BLK_MK)], axis=1)
    prev = ccar[...]
    ccar[...] = u[tm - SUBLANES:tm, :]
    row8 = lax.broadcasted_iota(jnp.int32, prev.shape, 0)
    y = cb_ref[...] + cw_ref[CONV_WIDTH - 1:CONV_WIDTH, :] * u
    for k in range(1, CONV_WIDTH):
        r = pltpu.roll(u, k, axis=0)
        top = jnp.where(row8 < k, pltpu.roll(prev, k, axis=0), r[:SUBLANES])
        shifted = jnp.concatenate([top, r[SUBLANES:]], axis=0)
        y = y + cw_ref[CONV_WIDTH - 1 - k:CONV_WIDTH - k, :] * shifted
    act = _silu(y)
    put(OUT_MQ, act[:, :ML_WIDTH])
    kt_ref[0, 0] = (act[:, ML_WIDTH:] * (ML_HEAD_DIM ** -0.5)).T.astype(BF16)

    put(OUT_MO, jax.nn.sigmoid(mm(BLK_MO)))
    put(OUT_MV, mm(BLK_MV))
    vt_ref[0, 0] = mm(BLK_FV).astype(BF16).T
    put_heads(qa_ref, mm(BLK_FQ) * (FOX_HEAD_DIM ** -0.5 * LOG2E), 0, True)
    put_heads(ka_ref, mm(BLK_FK), AUG_COLS, False)


def _bias_placement():
    place = np.zeros((LANES, 2 * AUG_COLS), np.float32)
    ones = np.zeros((1, 2 * AUG_COLS), np.float32)
    for h in range(FOX_HEADS):
        for term in range(BIAS_TERMS):
            src = term * FOX_HEADS + h
            place[src, h * LANES + FOX_HEAD_DIM + term] = 1.0
            ones[0, h * LANES + FOX_HEAD_DIM + BIAS_TERMS + term] = 1.0
            place[src, AUG_COLS + h * LANES + FOX_HEAD_DIM + BIAS_TERMS + term] = -1.0
            ones[0, AUG_COLS + h * LANES + FOX_HEAD_DIM + term] = 1.0
    return jnp.asarray(place, BF16), jnp.asarray(ones, F32)


def _in_proj_call(x, moe_tt, g2, nm, sc, sh, wm, wg, gb, cw, cb, *, bsz, seq, tm=512):
    t, d = x.shape
    ns = seq // tm
    combine = moe_tt is not None
    row = lambda b, s: (b * ns + s, 0)
    per_b = lambda b, s: (b, 0, 0)
    const2 = lambda b, s: (0, 0)
    in_specs = [pl.BlockSpec((tm, d), row)]
    args = [x]
    if combine:
        in_specs += [pl.BlockSpec((tm * TOK_ROWS, LANES), row), pl.BlockSpec((1, 1, d), per_b)]
        args += [moe_tt, g2]
    in_specs += [
        pl.BlockSpec((1, d), const2),
        pl.BlockSpec((1, 1, d), per_b),
        pl.BlockSpec((1, 1, d), per_b),
        pl.BlockSpec((d, MAIN_COLS), const2),
        pl.BlockSpec((d, 2 * LANES), const2),
        pl.BlockSpec((1, 2 * LANES), const2),
        pl.BlockSpec((CONV_WIDTH, 2 * ML_WIDTH), const2),
        pl.BlockSpec((1, 2 * ML_WIDTH), const2),
        pl.BlockSpec((LANES, 2 * AUG_COLS), const2),
        pl.BlockSpec((1, 2 * AUG_COLS), const2),
    ]
    place, ones = _bias_placement()
    args += [nm, sc, sh, wm, wg, gb, cw, cb, place, ones]
    out_specs = []
    out_shape = []
    if combine:
        out_specs.append(pl.BlockSpec((tm, d), row))
        out_shape.append(jax.ShapeDtypeStruct((t, d), F32))
    tiled = lambda b, s: (b, s, 0, 0)
    out_specs += [
        pl.BlockSpec((1, 1, AUG_COLS, tm), tiled),
        pl.BlockSpec((tm, AUG_COLS), row),
        pl.BlockSpec((1, 1, FOX_WIDTH, tm), tiled),
        pl.BlockSpec((tm, REST_COLS), row),
        pl.BlockSpec((1, 1, ML_WIDTH, tm), tiled),
        pl.BlockSpec((tm, LANES), row),
        pl.BlockSpec((1, GATE_ROWS, tm), lambda b, s: (b, 0, s)),
    ]
    out_shape += [
        jax.ShapeDtypeStruct((bsz, ns, AUG_COLS, tm), BF16),
        jax.ShapeDtypeStruct((t, AUG_COLS), BF16),
        jax.ShapeDtypeStruct((bsz, ns, FOX_WIDTH, tm), BF16),
        jax.ShapeDtypeStruct((t, REST_COLS), BF16),
        jax.ShapeDtypeStruct((bsz, ns, ML_WIDTH, tm), BF16),
        jax.ShapeDtypeStruct((t, LANES), F32),
        jax.ShapeDtypeStruct((bsz, GATE_ROWS, seq), F32),
    ]
    outs = pl.pallas_call(
        functools.partial(_in_proj_kernel, tm=tm, combine=combine),
        grid=(bsz, ns),
        in_specs=in_specs,
        out_specs=out_specs,
        out_shape=out_shape,
        scratch_shapes=[pltpu.VMEM((1, LANES), F32), pltpu.VMEM((SUBLANES, 2 * ML_WIDTH), F32)],
        compiler_params=_cparams(2, 48),
        name="in_proj",
    )(*args)
    if combine:
        return outs
    return [x] + list(outs)


def _fox_kernel(qt_ref, k_ref, vt_ref, ng_ref, cm_ref, o_ref, m_sc, acc_sc, *, tq, tk, nh):
    qi = pl.program_id(2)
    ones_rows = jnp.where(lax.broadcasted_iota(jnp.int32, (VT_PAD, tk), 0) == 0, 1.0, 0.0).astype(BF16)
    for h in range(nh):
        m_sc[h] = jnp.full((SUBLANES, tq), NEG, F32)
        acc_sc[h] = jnp.zeros((FOX_HEAD_DIM + VT_PAD, tq), F32)

    def body(j, carry, diagonal):
        k0 = pl.multiple_of(j * tk, tk)

        def logits(h):
            hs = slice(h * LANES, (h + 1) * LANES)
            z = jnp.dot(k_ref[pl.ds(k0, tk), hs], qt_ref[0, 0, hs, :], preferred_element_type=F32)
            return z + cm_ref[...] if diagonal else z

        ahead = 2
        zs = [logits(h) for h in range(min(ahead, nh))]
        for h in range(nh):
            ds_ = slice(h * FOX_HEAD_DIM, (h + 1) * FOX_HEAD_DIM)
            z = zs[h]
            if h + ahead < nh:
                zs.append(logits(h + ahead))
            m_prev = m_sc[h]
            m_new = jnp.maximum(m_prev, jnp.max(z, axis=0, keepdims=True))
            alpha = jnp.exp2(m_prev - m_new)
            p = jnp.exp2(z - m_new[:1, :])
            va = jnp.concatenate([vt_ref[0, j, ds_, :], ones_rows], axis=0)
            acc_sc[h] = alpha[:1, :] * acc_sc[h] + jnp.dot(va, p.astype(BF16), preferred_element_type=F32)
            m_sc[h] = m_new
        return carry

    lax.fori_loop(0, qi, functools.partial(body, diagonal=False), 0)
    lax.fori_loop(qi, qi + 1, functools.partial(body, diagonal=True), 0)

    for p in range(nh // 2):
        outs = []
        for h in (2 * p, 2 * p + 1):
            acc = acc_sc[h]
            num = acc[:FOX_HEAD_DIM, :]
            l = acc[FOX_HEAD_DIM:FOX_HEAD_DIM + 1, :]
            ms = jnp.mean(num * num, axis=0, keepdims=True)
            outs.append(num * lax.rsqrt(ms + EPS * l * l))
        ps = slice(p * LANES, (p + 1) * LANES)
        o_ref[:, ps] = (jnp.concatenate(outs, axis=0).T * ng_ref[:, ps]).astype(BF16)


def _fox_call(qat, ka, fvt, ng, *, bsz, seq, tq=512, nh=8):
    t = ka.shape[0]
    nq = seq // tq
    ngrp = FOX_HEADS // nh
    vw = nh * FOX_HEAD_DIM
    cmask = jnp.where(jnp.arange(tq)[:, None] <= jnp.arange(tq)[None, :], 0.0, NEG).astype(F32)
    return pl.pallas_call(
        functools.partial(_fox_kernel, tq=tq, tk=tq, nh=nh),
        grid=(bsz, ngrp, nq),
        in_specs=[
            pl.BlockSpec((1, 1, nh * LANES, tq), lambda b, p, i: (b, i, p, 0)),
            pl.BlockSpec((seq, nh * LANES), lambda b, p, i: (b, p)),
            pl.BlockSpec((1, nq, vw, tq), lambda b, p, i: (b, 0, p, 0)),
            pl.BlockSpec((1, vw), lambda b, p, i: (0, p)),
            pl.BlockSpec((tq, tq), lambda b, p, i: (0, 0)),
        ],
        out_specs=pl.BlockSpec((tq, vw), lambda b, p, i: (b * nq + i, p)),
        out_shape=jax.ShapeDtypeStruct((t, FOX_WIDTH), BF16),
        scratch_shapes=[pltpu.VMEM((nh, SUBLANES, tq), F32),
                        pltpu.VMEM((nh, FOX_HEAD_DIM + VT_PAD, tq), F32)],
        compiler_params=_cparams(3, 48),
        name="fox_attn",
    )(qat, ka, fvt, ng, cmask)


def _mlstm_kernel(q_ref, kt_ref, v_ref, og_ref, fc_ref, gr_ref, ng_ref, o_ref, ct_sc, u_sc, *, ch, per_tile):
    @pl.when(pl.program_id(1) == 0)
    def _():
        ct_sc[...] = jnp.zeros_like(ct_sc)
        u_sc[...] = jnp.zeros_like(u_sc)

    causal = (lax.broadcasted_iota(jnp.int32, (ch, ch), 1)
              <= lax.broadcasted_iota(jnp.int32, (ch, ch), 0))
    lane = lax.broadcasted_iota(jnp.int32, (ch, LANES), 1)
    fc = fc_ref[...]
    koff = pl.multiple_of((pl.program_id(1) % per_tile) * ch, ch)
    for h in range(ML_HEADS):
        sl = slice(h * ML_HEAD_DIM, (h + 1) * ML_HEAD_DIM)
        gl = GATE_ML + h
        q = q_ref[:, sl]
        kt = kt_ref[0, 0, sl, pl.ds(koff, ch)]
        vp = jnp.concatenate([v_ref[:, sl], jnp.where(lane == gl, 1.0, 0.0).astype(BF16)], axis=1)
        g_row = gr_ref[0, gl:gl + 1, :]
        u_prev = u_sc[h][:, :1]
        gm = jnp.where(causal, g_row, NEG)
        u_i = jnp.maximum(u_prev, jnp.max(gm, axis=1, keepdims=True))
        dmat = jnp.exp(gm - u_i)
        s = jnp.dot(q, kt, preferred_element_type=F32)
        scores = (s * dmat).astype(BF16)
        inter = jnp.exp(u_prev - u_i)
        ct = ct_sc[h]
        nd = (jnp.dot(scores, vp, preferred_element_type=F32)
              + jnp.dot(q, ct.astype(BF16), preferred_element_type=F32) * inter)
        num = nd[:, :ML_HEAD_DIM]
        den = jnp.maximum(jnp.abs(nd[:, ML_HEAD_DIM:]), jnp.exp(-(jnp.where(lane == gl, fc, 0.0) + u_i)))
        ms = jnp.mean(num * num, axis=1, keepdims=True)
        scale = lax.rsqrt(ms + EPS * den * den)[:, gl:gl + 1]
        y = num * scale * ng_ref[:, sl] * og_ref[:, sl].astype(F32)
        o_ref[:, sl] = y.astype(BF16)
        u_new = jnp.maximum(u_prev, jnp.max(g_row, axis=1, keepdims=True))
        ktw = (kt.astype(F32) * jnp.exp(g_row - u_new)).astype(BF16)
        ct_sc[h] = jnp.exp(u_prev - u_new) * ct + jnp.dot(ktw, vp, preferred_element_type=F32)
        u_sc[h] = jnp.broadcast_to(u_new, (1, LANES))


def _mlstm_call(qkv, mkt, fc, gr, ng, *, bsz, seq, ch=512):
    t = qkv.shape[0]
    nc = seq // ch
    ktile = mkt.shape[-1]
    per_tile = ktile // ch
    row = lambda b, c: (b * nc + c, 0)
    return pl.pallas_call(
        functools.partial(_mlstm_kernel, ch=ch, per_tile=per_tile),
        grid=(bsz, nc),
        in_specs=[
            pl.BlockSpec((ch, ML_WIDTH), lambda b, c: (b * nc + c, OUT_MQ)),
            pl.BlockSpec((1, 1, ML_WIDTH, ktile), lambda b, c: (b, c // per_tile, 0, 0)),
            pl.BlockSpec((ch, ML_WIDTH), lambda b, c: (b * nc + c, OUT_MV)),
            pl.BlockSpec((ch, ML_WIDTH), lambda b, c: (b * nc + c, OUT_MO)),
            pl.BlockSpec((ch, LANES), row),
            pl.BlockSpec((1, GATE_ROWS, ch), lambda b, c: (b, 0, c)),
            pl.BlockSpec((1, ML_WIDTH), lambda b, c: (0, 0)),
        ],
        out_specs=pl.BlockSpec((ch, ML_WIDTH), row),
        out_shape=jax.ShapeDtypeStruct((t, ML_WIDTH), BF16),
        scratch_shapes=[pltpu.VMEM((ML_HEADS, ML_HEAD_DIM, 2 * ML_HEAD_DIM), F32),
                        pltpu.VMEM((ML_HEADS, 1, LANES), F32)],
        compiler_params=_cparams(2, 32),
        name="mlstm",
    )(qkv, mkt, qkv, qkv, fc, gr, ng)


def _post_kernel(x_ref, hf_ref, hm_ref, wo_ref, g1_ref, nf_ref, sc_ref, sh_ref, wr_ref, br_ref, tri_ref,
                 x1_ref, hn_ref, gsel_ref, grp_ref, rank_ref, tot_ref, cnt_sc, *, te, steps_per_chunk):
    @pl.when(pl.program_id(0) % steps_per_chunk == 0)
    def _():
        cnt_sc[...] = jnp.zeros_like(cnt_sc)

    mix = (jnp.dot(hf_ref[...], wo_ref[:FOX_WIDTH, :], preferred_element_type=F32)
           + jnp.dot(hm_ref[...], wo_ref[FOX_WIDTH:, :], preferred_element_type=F32))
    x1 = x_ref[...] + g1_ref[0] * mix
    x1_ref[...] = x1
    ms = jnp.mean(x1 * x1, axis=1, keepdims=True)
    hn = x1 * lax.rsqrt(ms + EPS) * nf_ref[...]
    hn = hn * (1.0 + sc_ref[0]) + sh_ref[0]
    _to_token_tiles(hn_ref, hn)

    hn_hi = hn.astype(BF16)
    hn_lo = (hn - hn_hi.astype(F32)).astype(BF16)
    l2 = jnp.dot(hn_hi, wr_ref[...], preferred_element_type=F32)
    logits = (l2[:, :LANES] + l2[:, LANES:]
              + jnp.dot(hn_lo, wr_ref[:, :LANES], preferred_element_type=F32))
    aff = jax.nn.sigmoid(logits.T[:N_EXPERTS, :])
    sel = aff + br_ref[...]
    selr = [sel[e:e + 1, :] for e in range(N_EXPERTS)]
    affr = [aff[e:e + 1, :] for e in range(N_EXPERTS)]
    keep = [None] * N_EXPERTS
    score = []
    for g in range(N_GROUPS):
        vs = selr[g * EXPERTS_PER_GROUP:(g + 1) * EXPERTS_PER_GROUP]
        sg = jnp.zeros_like(vs[0])
        for i in range(EXPERTS_PER_GROUP):
            beaten = jnp.zeros_like(vs[0])
            for j in range(EXPERTS_PER_GROUP):
                if j != i:
                    b = (vs[j] >= vs[i]) if j < i else (vs[j] > vs[i])
                    beaten = beaten + jnp.where(b, 1.0, 0.0)
            kp = beaten < 2.0
            keep[g * EXPERTS_PER_GROUP + i] = kp
            sg = sg + jnp.where(kp, vs[i], 0.0)
        score.append(sg)
    chosen = []
    for g in range(N_GROUPS):
        lost = jnp.zeros_like(score[0])
        for g2 in range(N_GROUPS):
            if g2 != g:
                b = (score[g2] >= score[g]) if g2 < g else (score[g2] > score[g])
                lost = lost + jnp.where(b, 1.0, 0.0)
        chosen.append(jnp.where(lost < 0.5, 1.0, 0.0))
    wsel = []
    for i in range(EXPERTS_PER_GROUP):
        wi = jnp.zeros_like(score[0])
        for g in range(N_GROUPS):
            e = g * EXPERTS_PER_GROUP + i
            wi = wi + chosen[g] * jnp.where(keep[e], affr[e], 0.0)
        wsel.append(wi)
    wsum = wsel[0] + wsel[1] + wsel[2] + wsel[3]
    wsel = [w / wsum for w in wsel]

    row8 = lax.broadcasted_iota(jnp.int32, (SUBLANES, te), 0)
    gmat = jnp.zeros((SUBLANES, te), F32)
    wmat = jnp.zeros((SUBLANES, te), F32)
    grp = jnp.zeros_like(score[0])
    for g in range(N_GROUPS):
        gmat = jnp.where(row8 == g, chosen[g], gmat)
        wmat = jnp.where(row8 == g, wsel[g], wmat)
        grp = grp + g * chosen[g]
    pref = jnp.dot(gmat.astype(BF16), tri_ref[...], preferred_element_type=F32) + cnt_sc[:, :1]
    rank = jnp.sum(gmat * pref, axis=0, keepdims=True)
    tot = cnt_sc[:, :1] + jnp.sum(gmat, axis=1, keepdims=True)
    cnt_sc[...] = jnp.broadcast_to(tot, cnt_sc.shape)
    rank_ref[0] = rank.astype(jnp.int32)
    grp_ref[0] = grp.astype(jnp.int32)
    tot_ref[0] = jnp.broadcast_to(tot, (SUBLANES, LANES)).astype(jnp.int32)
    wfull = jnp.concatenate([wmat, jnp.zeros((LANES - SUBLANES, te), F32)], axis=0)
    gsel_ref[...] = wfull.T


def _post_call(x, hf, hm, wo, g1, nf, sc, sh, wr, br, tri, *, bsz, seq, chunk, te=512):
    t, d = x.shape
    steps_per_chunk = chunk // te
    per_seq = seq // te
    n_steps = t // te
    n_chunks = t // chunk
    row = lambda i: (i, 0)
    per_b = lambda i: (i // per_seq, 0, 0)
    const2 = lambda i: (0, 0)
    return pl.pallas_call(
        functools.partial(_post_kernel, te=te, steps_per_chunk=steps_per_chunk),
        grid=(n_steps,),
        in_specs=[
            pl.BlockSpec((te, d), row),
            pl.BlockSpec((te, FOX_WIDTH), row),
            pl.BlockSpec((te, ML_WIDTH), row),
            pl.BlockSpec((d, d), const2),
            pl.BlockSpec((1, 1, d), per_b),
            pl.BlockSpec((1, d), const2),
            pl.BlockSpec((1, 1, d), per_b),
            pl.BlockSpec((1, 1, d), per_b),
            pl.BlockSpec((d, 2 * LANES), const2),
            pl.BlockSpec((N_EXPERTS, 1), const2),
            pl.BlockSpec((te, te), const2),
        ],
        out_specs=[
            pl.BlockSpec((te, d), row),
            pl.BlockSpec((te * TOK_ROWS, LANES), row),
            pl.BlockSpec((te, LANES), row),
            pl.BlockSpec((1, 1, te), lambda i: (i, 0, 0)),
            pl.BlockSpec((1, 1, te), lambda i: (i, 0, 0)),
            pl.BlockSpec((1, SUBLANES, LANES), lambda i: (i // steps_per_chunk, 0, 0)),
        ],
        out_shape=[
            jax.ShapeDtypeStruct((t, d), F32),
            jax.ShapeDtypeStruct((t * TOK_ROWS, LANES), F32),
            jax.ShapeDtypeStruct((t, LANES), F32),
            jax.ShapeDtypeStruct((n_steps, 1, te), jnp.int32),
            jax.ShapeDtypeStruct((n_steps, 1, te), jnp.int32),
            jax.ShapeDtypeStruct((n_chunks, SUBLANES, LANES), jnp.int32),
        ],
        scratch_shapes=[pltpu.VMEM((SUBLANES, LANES), F32)],
        compiler_params=_cparams(1, 48),
        name="post_router",
    )(x, hf, hm, wo, g1, nf, sc, sh, wr, br, tri)


def _scatter_kernel(pos_ref, cnt_ref, off_ref, hn_ref, gsel_ref, xb_ref, gs_ref, xs_sc, *, chunk, rows, tm):
    gs_ref[...] = jnp.zeros_like(gs_ref)
    zero_tile = jnp.zeros((TOK_ROWS, LANES), F32)

    def zero_row(r, carry):
        xs_sc[pl.ds(pl.multiple_of(r * TOK_ROWS, TOK_ROWS), TOK_ROWS), :] = zero_tile
        return carry

    used_rows = 0
    for g in range(N_GROUPS):
        first_pad = off_ref[0, 0, g] + cnt_ref[0, 0, g]
        used_rows = off_ref[0, 0, g] + lax.shift_right_logical(cnt_ref[0, 0, g] + (tm - 1), tm.bit_length() - 1) * tm
        lax.fori_loop(first_pad, used_rows, zero_row, 0)
    used_tiles = lax.shift_right_logical(used_rows, tm.bit_length() - 1)

    def body(t, carry):
        p = pos_ref[0, 0, t]
        src = pl.multiple_of(t * TOK_ROWS, TOK_ROWS)
        dst = pl.multiple_of(p * TOK_ROWS, TOK_ROWS)
        xs_sc[pl.ds(dst, TOK_ROWS), :] = hn_ref[pl.ds(src, TOK_ROWS), :]
        gs_ref[pl.ds(p, 1), :] = gsel_ref[pl.ds(t, 1), :]
        return carry

    lax.fori_loop(0, chunk, body, 0, unroll=8)
    for j in range(rows // tm):
        @pl.when(j < used_tiles)
        def _():
            for c in range(TOK_ROWS):
                xb_ref[j * tm:(j + 1) * tm, c * LANES:(c + 1) * LANES] = (
                    xs_sc[pl.ds(j * tm * TOK_ROWS + c, tm, stride=TOK_ROWS), :].astype(BF16))

        @pl.when(j >= used_tiles)
        def _():
            xb_ref[j * tm:(j + 1) * tm, :] = jnp.zeros((tm, D_MODEL), BF16)


def _scatter_call(pos, cnt, off, hn_tt, gsel, *, chunk, rows, tm):
    t = gsel.shape[0]
    n_chunks = t // chunk
    assert tm & (tm - 1) == 0
    smem = lambda n: pl.BlockSpec((1, 1, n), lambda c: (c, 0, 0), memory_space=pltpu.SMEM)
    return pl.pallas_call(
        functools.partial(_scatter_kernel, chunk=chunk, rows=rows, tm=tm),
        grid=(n_chunks,),
        in_specs=[
            smem(chunk),
            smem(N_GROUPS),
            smem(N_GROUPS),
            pl.BlockSpec((chunk * TOK_ROWS, LANES), lambda c: (c, 0)),
            pl.BlockSpec((chunk, LANES), lambda c: (c, 0)),
        ],
        out_specs=[
            pl.BlockSpec((rows, D_MODEL), lambda c: (c, 0)),
            pl.BlockSpec((rows, LANES), lambda c: (c, 0)),
        ],
        out_shape=[
            jax.ShapeDtypeStruct((n_chunks * rows, D_MODEL), BF16),
            jax.ShapeDtypeStruct((n_chunks * rows, LANES), F32),
        ],
        scratch_shapes=[pltpu.VMEM((rows * TOK_ROWS, LANES), F32)],
        compiler_params=_cparams(1, 56),
        name="moe_scatter",
    )(pos, cnt, off, hn_tt, gsel)


def _experts_kernel(blk_ref, grp_ref, xb_ref, gs_ref, wg_ref, wu_ref, wd_ref, y_ref, *, tm):
    g = grp_ref[pl.program_id(0)]

    @pl.when(g < N_GROUPS)
    def _():
        x = xb_ref[...]
        gs = gs_ref[...]
        acts = []
        for i in range(EXPERTS_PER_GROUP):
            hg = jnp.dot(x, wg_ref[i], preferred_element_type=F32)
            hu = jnp.dot(x, wu_ref[i], preferred_element_type=F32)
            acts.append((_silu(hg) * hu * gs[:, i:i + 1]).astype(BF16))
        y = jnp.dot(jnp.concatenate(acts, axis=1), wd_ref[0], preferred_element_type=F32)
        _to_token_tiles(y_ref, y)

    @pl.when(g >= N_GROUPS)
    def _():
        y_ref[...] = jnp.zeros_like(y_ref)


def _experts_call(tile_blk, tile_grp, xb, gs, wg, wu, wd, *, tm):
    n_rows = xb.shape[0]
    n_slots = n_rows // tm
    wmap = lambda s, blk, grp: (jnp.minimum(grp[s], N_GROUPS - 1), 0, 0)
    grid_spec = pltpu.PrefetchScalarGridSpec(
        num_scalar_prefetch=2,
        grid=(n_slots,),
        in_specs=[
            pl.BlockSpec((tm, D_MODEL), lambda s, blk, grp: (blk[s], 0)),
            pl.BlockSpec((tm, LANES), lambda s, blk, grp: (blk[s], 0)),
            pl.BlockSpec((EXPERTS_PER_GROUP, D_MODEL, D_FF), wmap),
            pl.BlockSpec((EXPERTS_PER_GROUP, D_MODEL, D_FF), wmap),
            pl.BlockSpec((1, EXPERTS_PER_GROUP * D_FF, D_MODEL), wmap),
        ],
        out_specs=pl.BlockSpec((tm * TOK_ROWS, LANES), lambda s, blk, grp: (blk[s], 0)),
    )
    return pl.pallas_call(
        functools.partial(_experts_kernel, tm=tm),
        grid_spec=grid_spec,
        out_shape=jax.ShapeDtypeStruct((n_rows * TOK_ROWS, LANES), F32),
        compiler_params=_cparams(1, 48),
        name="moe_experts",
    )(tile_blk, tile_grp, xb, gs, wg, wu, wd)


def _gather_kernel(pos_ref, ys_ref, o_ref, *, chunk):
    def body(t, carry):
        p = pos_ref[0, 0, t]
        src = pl.multiple_of(p * TOK_ROWS, TOK_ROWS)
        dst = pl.multiple_of(t * TOK_ROWS, TOK_ROWS)
        o_ref[pl.ds(dst, TOK_ROWS), :] = ys_ref[pl.ds(src, TOK_ROWS), :]
        return carry

    lax.fori_loop(0, chunk, body, 0, unroll=8)


def _gather_call(pos, ys_tt, *, chunk, rows, n_tok):
    n_chunks = n_tok // chunk
    return pl.pallas_call(
        functools.partial(_gather_kernel, chunk=chunk),
        grid=(n_chunks,),
        in_specs=[
            pl.BlockSpec((1, 1, chunk), lambda c: (c, 0, 0), memory_space=pltpu.SMEM),
            pl.BlockSpec((rows * TOK_ROWS, LANES), lambda c: (c, 0)),
        ],
        out_specs=pl.BlockSpec((chunk * TOK_ROWS, LANES), lambda c: (c, 0)),
        out_shape=jax.ShapeDtypeStruct((n_tok * TOK_ROWS, LANES), F32),
        compiler_params=_cparams(1, 56),
        name="moe_gather",
    )(pos, ys_tt)


def _gather_final_kernel(pos_ref, ys_ref, x_ref, g2_ref, nf_ref, o_ref, o_sc, *, part, sub):
    base = pl.program_id(1) * part

    def body(t, carry):
        p = pos_ref[0, 0, base + t]
        src = pl.multiple_of(p * TOK_ROWS, TOK_ROWS)
        dst = pl.multiple_of(t * TOK_ROWS, TOK_ROWS)
        o_sc[pl.ds(dst, TOK_ROWS), :] = ys_ref[pl.ds(src, TOK_ROWS), :]
        return carry

    lax.fori_loop(0, part, body, 0, unroll=8)
    for r in range(part // sub):
        rs = slice(r * sub, (r + 1) * sub)
        moe = jnp.concatenate(
            [o_sc[pl.ds(r * sub * TOK_ROWS + c, sub, stride=TOK_ROWS), :] for c in range(TOK_ROWS)], axis=1)
        x = x_ref[rs, :] + g2_ref[0] * moe
        ms = jnp.mean(x * x, axis=1, keepdims=True)
        o_ref[rs, :] = x * lax.rsqrt(ms + EPS) * nf_ref[...]


def _gather_final_call(pos, ys_tt, x, g2, nf, *, chunk, rows, seq):
    n_tok, d = x.shape
    n_chunks = n_tok // chunk
    parts = 2
    part = chunk // parts
    return pl.pallas_call(
        functools.partial(_gather_final_kernel, part=part, sub=256),
        grid=(n_chunks, parts),
        in_specs=[
            pl.BlockSpec((1, 1, chunk), lambda c, s: (c, 0, 0), memory_space=pltpu.SMEM),
            pl.BlockSpec((rows * TOK_ROWS, LANES), lambda c, s: (c, 0)),
            pl.BlockSpec((part, d), lambda c, s: (c * parts + s, 0)),
            pl.BlockSpec((1, 1, d), lambda c, s: ((c * chunk + s * part) // seq, 0, 0)),
            pl.BlockSpec((1, d), lambda c, s: (0, 0)),
        ],
        out_specs=pl.BlockSpec((part, d), lambda c, s: (c * parts + s, 0)),
        out_shape=jax.ShapeDtypeStruct((n_tok, d), F32),
        scratch_shapes=[pltpu.VMEM((part * TOK_ROWS, LANES), F32)],
        compiler_params=_cparams(2, 56),
        name="moe_gather_final",
    )(pos, ys_tt, x, g2, nf)


def _moe_tiles(tot, *, tm, tiles_per_chunk):
    nt = (tot + tm - 1) // tm
    ts = jnp.cumsum(nt, axis=1) - nt
    off = (ts * tm).astype(jnp.int32)
    j = jnp.arange(tiles_per_chunk, dtype=jnp.int32)[None, :, None]
    inside = (j >= ts[:, None, :]) & (j < (ts + nt)[:, None, :])
    key = jnp.where(jnp.any(inside, axis=-1), jnp.argmax(inside, axis=-1), N_GROUPS).reshape(-1)
    order = jnp.argsort(key, stable=True).astype(jnp.int32)
    return off, order, key[order].astype(jnp.int32)


def kernel(x, c, w_in, conv_w, conv_b, fox_f_bias, mlstm_i_bias, mlstm_f_bias, fox_out_norm,
           mlstm_out_norm, w_out, w_ada, b_ada, norm_mix, norm_ffn, w_router, b_router, w_gate,
           w_up, w_down, norm_final):
    bsz, seq, d = x.shape
    depth = w_in.shape[0]
    t = bsz * seq
    assert d == D_MODEL and w_in.shape[-1] == IN_COLS
    chunk = min(2048, seq)
    tm = 256
    rows = -(-(chunk + N_GROUPS * (tm - 1)) // tm) * tm
    te = min(1024, seq)
    assert seq % 512 == 0 and seq % chunk == 0 and chunk % te == 0

    mods = _ada_call(c, w_ada, b_ada).reshape(depth, bsz, N_ADA, 1, d)
    xf = x.reshape(t, d)

    tri = (jnp.arange(te)[:, None] < jnp.arange(te)[None, :]).astype(BF16)
    wr_f = jnp.pad(w_router, ((0, 0), (0, LANES - N_EXPERTS))).astype(F32)
    wr_hi = wr_f.astype(BF16)
    wr = jnp.concatenate([wr_hi, (wr_f - wr_hi.astype(F32)).astype(BF16)], axis=1)
    br = b_router.reshape(N_EXPERTS, 1).astype(F32)

    moe_tt = None
    g2_prev = None
    for l in range(depth):
        sh1, sc1, g1, sh2, sc2, g2 = [mods[l, :, i] for i in range(N_ADA)]
        wl = w_in[l]
        wm = jnp.concatenate([wl[:, FOX_Q:FOX_F], wl[:, ML_Q:ML_I], wl[:, ML_O:IN_COLS]], axis=1).astype(BF16)
        zpad = lambda n: jnp.zeros((d, n), F32)
        wg = jnp.concatenate([
            wl[:, FOX_F:ML_Q], wl[:, ML_F:ML_O], zpad(LANES - FOX_HEADS - ML_HEADS),
            zpad(GATE_ML), wl[:, ML_I:ML_F], zpad(LANES - GATE_ML - ML_HEADS)], axis=1).astype(BF16)
        zb = lambda n: jnp.zeros((n,), F32)
        gb = jnp.concatenate([
            fox_f_bias[l], mlstm_f_bias[l], zb(LANES - FOX_HEADS - ML_HEADS),
            zb(GATE_ML), mlstm_i_bias[l], zb(LANES - GATE_ML - ML_HEADS)]).reshape(1, 2 * LANES)

        xf, qat, ka, fvt, qkv, mkt, fc, gr = _in_proj_call(
            xf, moe_tt, g2_prev, norm_mix[l].reshape(1, d), sc1, sh1, wm, wg, gb,
            conv_w[l], conv_b[l].reshape(1, -1), bsz=bsz, seq=seq)
        hf = _fox_call(qat, ka, fvt, fox_out_norm[l].reshape(1, FOX_WIDTH), bsz=bsz, seq=seq)
        hm = _mlstm_call(qkv, mkt, fc, gr, mlstm_out_norm[l].reshape(1, ML_WIDTH), bsz=bsz, seq=seq)
        xf, hn_tt, gsel, grp, rank, tot = _post_call(
            xf, hf, hm, w_out[l].astype(BF16), g1, norm_ffn[l].reshape(1, d), sc2, sh2, wr, br, tri,
            bsz=bsz, seq=seq, chunk=chunk, te=te)

        n_chunks = t // chunk
        grp = grp.reshape(n_chunks, 1, chunk)
        rank = rank.reshape(n_chunks, 1, chunk)
        cnt = tot[:, :N_GROUPS, 0]
        off, tile_blk, tile_grp = _moe_tiles(cnt, tm=tm, tiles_per_chunk=rows // tm)
        pos = rank
        for g in range(N_GROUPS):
            pos = pos + jnp.where(grp == g, off[:, g].reshape(n_chunks, 1, 1), 0)
        xb, gs = _scatter_call(pos, cnt.reshape(n_chunks, 1, N_GROUPS), off.reshape(n_chunks, 1, N_GROUPS),
                               hn_tt, gsel, chunk=chunk, rows=rows, tm=tm)
        ys_tt = _experts_call(tile_blk, tile_grp, xb, gs, w_gate[l].astype(BF16),
                              w_up[l].astype(BF16),
                              w_down[l].astype(BF16).reshape(N_GROUPS, EXPERTS_PER_GROUP * D_FF, d), tm=tm)
        if l < depth - 1:
            moe_tt = _gather_call(pos, ys_tt, chunk=chunk, rows=rows, n_tok=t)
            g2_prev = g2
        else:
            out = _gather_final_call(pos, ys_tt, xf, g2, norm_final.reshape(1, d),
                                     chunk=chunk, rows=rows, seq=seq)
    return out.reshape(bsz, seq, d)
```

```python
import functools

import numpy as np
import jax
import jax.numpy as jnp
from jax import lax
from jax.experimental import pallas as pl
from jax.experimental.pallas import tpu as pltpu

F32 = jnp.float32
BF16 = jnp.bfloat16

LANES = 128
SUBLANES = 8
VMEM_BYTES_V7X = 64 * 1024 * 1024

D_MODEL = 1024
FOX_HEADS = 8
FOX_HEAD_DIM = 64
FOX_WIDTH = FOX_HEADS * FOX_HEAD_DIM
ML_HEADS = 4
ML_HEAD_DIM = 128
ML_WIDTH = ML_HEADS * ML_HEAD_DIM
CONV_WIDTH = 4
N_EXPERTS = 16
N_GROUPS = 4
EXPERTS_PER_GROUP = 4
D_FF = 512
N_ADA = 6
EPS = 1e-6
NEG = -1e30

FOX_Q = 0
FOX_F = 3 * FOX_WIDTH
ML_Q = FOX_F + FOX_HEADS
ML_I = ML_Q + 3 * ML_WIDTH
ML_F = ML_I + ML_HEADS
ML_O = ML_F + ML_HEADS
IN_COLS = ML_O + ML_WIDTH

MAIN_COLS = 7 * 512
BLK_FQ, BLK_FK, BLK_FV, BLK_MQ, BLK_MK, BLK_MV, BLK_MO = range(7)
REST_COLS = 3 * 512
OUT_MQ, OUT_MV, OUT_MO = range(3)
AUG_COLS = FOX_HEADS * 128
BIAS_TERMS = 3
VT_PAD = 16
LOG2E = 1.4426950408889634
GATE_FOX = 0
GATE_ML = FOX_HEADS
GATE_ROWS = 16

TOK_ROWS = D_MODEL // LANES


def _cparams(n_grid, vmem_mb):
    return pltpu.CompilerParams(
        dimension_semantics=("arbitrary",) * n_grid,
        vmem_limit_bytes=vmem_mb * 1024 * 1024)


def _silu(x):
    return x * jax.nn.sigmoid(x)


def _log_sigmoid(z):
    return jnp.minimum(z, 0.0) - jnp.log1p(jnp.exp(-jnp.abs(z)))


def _cumsum_rows(x):
    n = x.shape[0]
    row = lax.broadcasted_iota(jnp.int32, x.shape, 0)
    s = 1
    while s < n:
        x = x + jnp.where(row >= s, pltpu.roll(x, s, axis=0), 0.0)
        s *= 2
    return x


def _from_token_tiles(ref, n_tok):
    return jnp.concatenate(
        [ref[pl.ds(c, n_tok, stride=TOK_ROWS), :] for c in range(TOK_ROWS)], axis=1)


def _to_token_tiles(ref, val, row0=0):
    n = val.shape[0]
    for c in range(TOK_ROWS):
        ref[pl.ds(row0 * TOK_ROWS + c, n, stride=TOK_ROWS), :] = val[:, c * LANES:(c + 1) * LANES]


def _ada_kernel(c_ref, w_ref, b_ref, o_ref):
    c = c_ref[...]
    o_ref[0] = jnp.dot(_silu(c), w_ref[0], preferred_element_type=F32,
                       precision=lax.Precision.HIGHEST) + b_ref[0]


def _ada_call(c, w_ada, b_ada):
    depth, d, n = w_ada.shape
    bsz = c.shape[0]
    tn = 1536
    return pl.pallas_call(
        _ada_kernel,
        grid=(depth, n // tn),
        in_specs=[
            pl.BlockSpec((bsz, d), lambda l, j: (0, 0)),
            pl.BlockSpec((1, d, tn), lambda l, j: (l, 0, j)),
            pl.BlockSpec((1, 1, tn), lambda l, j: (l, 0, j)),
        ],
        out_specs=pl.BlockSpec((1, bsz, tn), lambda l, j: (l, 0, j)),
        out_shape=jax.ShapeDtypeStruct((depth, bsz, n), F32),
        compiler_params=_cparams(2, 32),
        name="ada_mod",
    )(c, w_ada, b_ada.reshape(depth, 1, n))


def _in_proj_kernel(*refs, tm, combine):
    if combine:
        (x_ref, ott_ref, g2_ref, nm_ref, sc_ref, sh_ref, wm_ref, wg_ref, gb_ref, cw_ref, cb_ref,
         place_ref, ones_ref,
         xn_ref, qa_ref, ka_ref, vt_ref, qkv_ref, kt_ref, fc_ref, gr_ref, fcar, ccar) = refs
    else:
        (x_ref, nm_ref, sc_ref, sh_ref, wm_ref, wg_ref, gb_ref, cw_ref, cb_ref,
         place_ref, ones_ref,
         qa_ref, ka_ref, vt_ref, qkv_ref, kt_ref, fc_ref, gr_ref, fcar, ccar) = refs

    @pl.when(pl.program_id(1) == 0)
    def _():
        fcar[...] = jnp.zeros_like(fcar)
        ccar[...] = jnp.zeros_like(ccar)

    x = x_ref[...]
    if combine:
        x = x + g2_ref[0] * _from_token_tiles(ott_ref, tm)
        xn_ref[...] = x
    ms = jnp.mean(x * x, axis=1, keepdims=True)
    hn = x * lax.rsqrt(ms + EPS) * nm_ref[...]
    hn = hn * (1.0 + sc_ref[0]) + sh_ref[0]
    hb = hn.astype(BF16)

    gp = jnp.dot(hb, wg_ref[0], preferred_element_type=F32) + gb_ref[...]
    lf = _log_sigmoid(gp[:, :LANES])
    fcum = _cumsum_rows(lf) + fcar[...]
    fcar[...] = fcum[tm - 1:tm, :]
    gml = gp[:, LANES:] - fcum
    fc_ref[...] = fcum
    gr_ref[0] = gml.T[:GATE_ROWS, :]

    def mm(j):
        return jnp.dot(hb, wm_ref[0, :, j * 512:(j + 1) * 512], preferred_element_type=F32)

    def put(j, v):
        qkv_ref[:, j * 512:(j + 1) * 512] = v.astype(BF16)

    lane = lax.broadcasted_iota(jnp.int32, (tm, LANES), 1)
    fs = fcum * LOG2E
    hi = fs.astype(BF16).astype(F32)
    mid = (fs - hi).astype(BF16).astype(F32)
    low = (fs - hi - mid).astype(BF16).astype(F32)
    packed = jnp.where(lane < FOX_HEADS, hi,
                       jnp.where(lane < 2 * FOX_HEADS, pltpu.roll(mid, FOX_HEADS, axis=1),
                                 jnp.where(lane < 3 * FOX_HEADS, pltpu.roll(low, 2 * FOX_HEADS, axis=1), 0.0)))
    bias = jnp.dot(packed.astype(BF16), place_ref[...], preferred_element_type=F32) + ones_ref[...]

    def put_heads(ref, val, col0, transposed):
        lo = lane < FOX_HEAD_DIM
        for p in range(FOX_HEADS // 2):
            slab = val[:, p * LANES:(p + 1) * LANES]
            for h, data in ((2 * p, slab), (2 * p + 1, pltpu.roll(slab, FOX_HEAD_DIM, axis=1))):
                blk = jnp.where(lo, data, 0.0) + bias[:, col0 + h * LANES:col0 + (h + 1) * LANES]
                if transposed:
                    ref[0, 0, h * LANES:(h + 1) * LANES, :] = blk.T.astype(BF16)
                else:
                    ref[:, h * LANES:(h + 1) * LANES] = blk.astype(BF16)

    u = jnp.concatenate([mm(BLK_MQ), mm(BLK_MK)], axis=1)
    prev = ccar[...]
    ccar[...] = u[tm - SUBLANES:tm, :]
    row8 = lax.broadcasted_iota(jnp.int32, prev.shape, 0)
    y = cb_ref[...] + cw_ref[CONV_WIDTH - 1:CONV_WIDTH, :] * u
    for k in range(1, CONV_WIDTH):
        r = pltpu.roll(u, k, axis=0)
        top = jnp.where(row8 < k, pltpu.roll(prev, k, axis=0), r[:SUBLANES])
        shifted = jnp.concatenate([top, r[SUBLANES:]], axis=0)
        y = y + cw_ref[CONV_WIDTH - 1 - k:CONV_WIDTH - k, :] * shifted
    act = _silu(y)
    put(OUT_MQ, act[:, :ML_WIDTH])
    kt_ref[0, 0] = (act[:, ML_WIDTH:] * (ML_HEAD_DIM ** -0.5)).T.astype(BF16)

    put(OUT_MO, jax.nn.sigmoid(mm(BLK_MO)))
    put(OUT_MV, mm(BLK_MV))
    vt_ref[0, 0] = mm(BLK_FV).astype(BF16).T
    put_heads(qa_ref, mm(BLK_FQ) * (FOX_HEAD_DIM ** -0.5 * LOG2E), 0, True)
    put_heads(ka_ref, mm(BLK_FK), AUG_COLS, False)


def _bias_placement():
    place = np.zeros((LANES, 2 * AUG_COLS), np.float32)
    ones = np.zeros((1, 2 * AUG_COLS), np.float32)
    for h in range(FOX_HEADS):
        for term in range(BIAS_TERMS):
            src = term * FOX_HEADS + h
            place[src, h * LANES + FOX_HEAD_DIM + term] = 1.0
            ones[0, h * LANES + FOX_HEAD_DIM + BIAS_TERMS + term] = 1.0
            place[src, AUG_COLS + h * LANES + FOX_HEAD_DIM + BIAS_TERMS + term] = -1.0
            ones[0, AUG_COLS + h * LANES + FOX_HEAD_DIM + term] = 1.0
    return jnp.asarray(place, BF16), jnp.asarray(ones, F32)


def _in_proj_call(x, moe_tt, g2, nm, sc, sh, wm, wg, gb, cw, cb, *, layer, bsz, seq, tm=512):
    t, d = x.shape
    ns = seq // tm
    combine = moe_tt is not None
    row = lambda b, s: (b * ns + s, 0)
    per_b = lambda b, s: (b, 0, 0)
    const2 = lambda b, s: (0, 0)
    in_specs = [pl.BlockSpec((tm, d), row)]
    args = [x]
    if combine:
        in_specs += [pl.BlockSpec((tm * TOK_ROWS, LANES), row), pl.BlockSpec((1, 1, d), per_b)]
        args += [moe_tt, g2]
    in_specs += [
        pl.BlockSpec((1, d), const2),
        pl.BlockSpec((1, 1, d), per_b),
        pl.BlockSpec((1, 1, d), per_b),
        pl.BlockSpec((1, d, MAIN_COLS), lambda b, s: (layer, 0, 0)),
        pl.BlockSpec((1, d, 2 * LANES), lambda b, s: (layer, 0, 0)),
        pl.BlockSpec((1, 2 * LANES), const2),
        pl.BlockSpec((CONV_WIDTH, 2 * ML_WIDTH), const2),
        pl.BlockSpec((1, 2 * ML_WIDTH), const2),
        pl.BlockSpec((LANES, 2 * AUG_COLS), const2),
        pl.BlockSpec((1, 2 * AUG_COLS), const2),
    ]
    place, ones = _bias_placement()
    args += [nm, sc, sh, wm, wg, gb, cw, cb, place, ones]
    out_specs = []
    out_shape = []
    if combine:
        out_specs.append(pl.BlockSpec((tm, d), row))
        out_shape.append(jax.ShapeDtypeStruct((t, d), F32))
    tiled = lambda b, s: (b, s, 0, 0)
    out_specs += [
        pl.BlockSpec((1, 1, AUG_COLS, tm), tiled),
        pl.BlockSpec((tm, AUG_COLS), row),
        pl.BlockSpec((1, 1, FOX_WIDTH, tm), tiled),
        pl.BlockSpec((tm, REST_COLS), row),
        pl.BlockSpec((1, 1, ML_WIDTH, tm), tiled),
        pl.BlockSpec((tm, LANES), row),
        pl.BlockSpec((1, GATE_ROWS, tm), lambda b, s: (b, 0, s)),
    ]
    out_shape += [
        jax.ShapeDtypeStruct((bsz, ns, AUG_COLS, tm), BF16),
        jax.ShapeDtypeStruct((t, AUG_COLS), BF16),
        jax.ShapeDtypeStruct((bsz, ns, FOX_WIDTH, tm), BF16),
        jax.ShapeDtypeStruct((t, REST_COLS), BF16),
        jax.ShapeDtypeStruct((bsz, ns, ML_WIDTH, tm), BF16),
        jax.ShapeDtypeStruct((t, LANES), F32),
        jax.ShapeDtypeStruct((bsz, GATE_ROWS, seq), F32),
    ]
    outs = pl.pallas_call(
        functools.partial(_in_proj_kernel, tm=tm, combine=combine),
        grid=(bsz, ns),
        in_specs=in_specs,
        out_specs=out_specs,
        out_shape=out_shape,
        scratch_shapes=[pltpu.VMEM((1, LANES), F32), pltpu.VMEM((SUBLANES, 2 * ML_WIDTH), F32)],
        compiler_params=_cparams(2, 48),
        name="in_proj",
    )(*args)
    if combine:
        return outs
    return [x] + list(outs)


def _fox_kernel(qt_ref, k_ref, vt_ref, ng_ref, cm_ref, o_ref, m_sc, acc_sc, *, tq, tk, nh):
    qi = pl.program_id(2)
    ones_rows = jnp.where(lax.broadcasted_iota(jnp.int32, (VT_PAD, tk), 0) == 0, 1.0, 0.0).astype(BF16)
    for h in range(nh):
        m_sc[h] = jnp.full((SUBLANES, tq), NEG, F32)
        acc_sc[h] = jnp.zeros((FOX_HEAD_DIM + VT_PAD, tq), F32)

    def body(j, carry, diagonal):
        k0 = pl.multiple_of(j * tk, tk)

        def logits(h):
            hs = slice(h * LANES, (h + 1) * LANES)
            z = jnp.dot(k_ref[pl.ds(k0, tk), hs], qt_ref[0, 0, hs, :], preferred_element_type=F32)
            return z + cm_ref[...] if diagonal else z

        ahead = 2
        zs = [logits(h) for h in range(min(ahead, nh))]
        for h in range(nh):
            ds_ = slice(h * FOX_HEAD_DIM, (h + 1) * FOX_HEAD_DIM)
            z = zs[h]
            if h + ahead < nh:
                zs.append(logits(h + ahead))
            m_prev = m_sc[h]
            m_new = jnp.maximum(m_prev, jnp.max(z, axis=0, keepdims=True))
            alpha = jnp.exp2(m_prev - m_new)
            p = jnp.exp2(z - m_new[:1, :])
            va = jnp.concatenate([vt_ref[0, j, ds_, :], ones_rows], axis=0)
            acc_sc[h] = alpha[:1, :] * acc_sc[h] + jnp.dot(va, p.astype(BF16), preferred_element_type=F32)
            m_sc[h] = m_new
        return carry

    lax.fori_loop(0, qi, functools.partial(body, diagonal=False), 0)
    lax.fori_loop(qi, qi + 1, functools.partial(body, diagonal=True), 0)

    for p in range(nh // 2):
        outs = []
        for h in (2 * p, 2 * p + 1):
            acc = acc_sc[h]
            num = acc[:FOX_HEAD_DIM, :]
            l = acc[FOX_HEAD_DIM:FOX_HEAD_DIM + 1, :]
            ms = jnp.mean(num * num, axis=0, keepdims=True)
            outs.append(num * lax.rsqrt(ms + EPS * l * l))
        ps = slice(p * LANES, (p + 1) * LANES)
        o_ref[:, ps] = (jnp.concatenate(outs, axis=0).T * ng_ref[:, ps]).astype(BF16)


def _fox_call(qat, ka, fvt, ng, *, bsz, seq, tq=512, nh=8):
    t = ka.shape[0]
    nq = seq // tq
    ngrp = FOX_HEADS // nh
    vw = nh * FOX_HEAD_DIM
    cmask = jnp.where(jnp.arange(tq)[:, None] <= jnp.arange(tq)[None, :], 0.0, NEG).astype(F32)
    return pl.pallas_call(
        functools.partial(_fox_kernel, tq=tq, tk=tq, nh=nh),
        grid=(bsz, ngrp, nq),
        in_specs=[
            pl.BlockSpec((1, 1, nh * LANES, tq), lambda b, p, i: (b, i, p, 0)),
            pl.BlockSpec((seq, nh * LANES), lambda b, p, i: (b, p)),
            pl.BlockSpec((1, nq, vw, tq), lambda b, p, i: (b, 0, p, 0)),
            pl.BlockSpec((1, vw), lambda b, p, i: (0, p)),
            pl.BlockSpec((tq, tq), lambda b, p, i: (0, 0)),
        ],
        out_specs=pl.BlockSpec((tq, vw), lambda b, p, i: (b * nq + i, p)),
        out_shape=jax.ShapeDtypeStruct((t, FOX_WIDTH), BF16),
        scratch_shapes=[pltpu.VMEM((nh, SUBLANES, tq), F32),
                        pltpu.VMEM((nh, FOX_HEAD_DIM + VT_PAD, tq), F32)],
        compiler_params=_cparams(3, 48),
        name="fox_attn",
    )(qat, ka, fvt, ng, cmask)


def _mlstm_kernel(q_ref, kt_ref, v_ref, og_ref, fc_ref, gr_ref, ng_ref, o_ref, ct_sc, u_sc, *, ch, per_tile):
    @pl.when(pl.program_id(1) == 0)
    def _():
        ct_sc[...] = jnp.zeros_like(ct_sc)
        u_sc[...] = jnp.zeros_like(u_sc)

    causal = (lax.broadcasted_iota(jnp.int32, (ch, ch), 1)
              <= lax.broadcasted_iota(jnp.int32, (ch, ch), 0))
    lane = lax.broadcasted_iota(jnp.int32, (ch, LANES), 1)
    fc = fc_ref[...]
    koff = pl.multiple_of((pl.program_id(1) % per_tile) * ch, ch)
    for h in range(ML_HEADS):
        sl = slice(h * ML_HEAD_DIM, (h + 1) * ML_HEAD_DIM)
        gl = GATE_ML + h
        q = q_ref[:, sl]
        kt = kt_ref[0, 0, sl, pl.ds(koff, ch)]
        vp = jnp.concatenate([v_ref[:, sl], jnp.where(lane == gl, 1.0, 0.0).astype(BF16)], axis=1)
        g_row = gr_ref[0, gl:gl + 1, :]
        u_prev = u_sc[h][:, :1]
        gm = jnp.where(causal, g_row, NEG)
        u_i = jnp.maximum(u_prev, jnp.max(gm, axis=1, keepdims=True))
        dmat = jnp.exp(gm - u_i)
        s = jnp.dot(q, kt, preferred_element_type=F32)
        scores = (s * dmat).astype(BF16)
        inter = jnp.exp(u_prev - u_i)
        ct = ct_sc[h]
        nd = (jnp.dot(scores, vp, preferred_element_type=F32)
              + jnp.dot(q, ct.astype(BF16), preferred_element_type=F32) * inter)
        num = nd[:, :ML_HEAD_DIM]
        den = jnp.maximum(jnp.abs(nd[:, ML_HEAD_DIM:]), jnp.exp(-(jnp.where(lane == gl, fc, 0.0) + u_i)))
        ms = jnp.mean(num * num, axis=1, keepdims=True)
        scale = lax.rsqrt(ms + EPS * den * den)[:, gl:gl + 1]
        y = num * scale * ng_ref[:, sl] * og_ref[:, sl].astype(F32)
        o_ref[:, sl] = y.astype(BF16)
        u_new = jnp.maximum(u_prev, jnp.max(g_row, axis=1, keepdims=True))
        ktw = (kt.astype(F32) * jnp.exp(g_row - u_new)).astype(BF16)
        ct_sc[h] = jnp.exp(u_prev - u_new) * ct + jnp.dot(ktw, vp, preferred_element_type=F32)
        u_sc[h] = jnp.broadcast_to(u_new, (1, LANES))


def _mlstm_call(qkv, mkt, fc, gr, ng, *, bsz, seq, ch=512):
    t = qkv.shape[0]
    nc = seq // ch
    ktile = mkt.shape[-1]
    per_tile = ktile // ch
    row = lambda b, c: (b * nc + c, 0)
    return pl.pallas_call(
        functools.partial(_mlstm_kernel, ch=ch, per_tile=per_tile),
        grid=(bsz, nc),
        in_specs=[
            pl.BlockSpec((ch, ML_WIDTH), lambda b, c: (b * nc + c, OUT_MQ)),
            pl.BlockSpec((1, 1, ML_WIDTH, ktile), lambda b, c: (b, c // per_tile, 0, 0)),
            pl.BlockSpec((ch, ML_WIDTH), lambda b, c: (b * nc + c, OUT_MV)),
            pl.BlockSpec((ch, ML_WIDTH), lambda b, c: (b * nc + c, OUT_MO)),
            pl.BlockSpec((ch, LANES), row),
            pl.BlockSpec((1, GATE_ROWS, ch), lambda b, c: (b, 0, c)),
            pl.BlockSpec((1, ML_WIDTH), lambda b, c: (0, 0)),
        ],
        out_specs=pl.BlockSpec((ch, ML_WIDTH), row),
        out_shape=jax.ShapeDtypeStruct((t, ML_WIDTH), BF16),
        scratch_shapes=[pltpu.VMEM((ML_HEADS, ML_HEAD_DIM, 2 * ML_HEAD_DIM), F32),
                        pltpu.VMEM((ML_HEADS, 1, LANES), F32)],
        compiler_params=_cparams(2, 32),
        name="mlstm",
    )(qkv, mkt, qkv, qkv, fc, gr, ng)


def _post_kernel(x_ref, hf_ref, hm_ref, wo_ref, g1_ref, nf_ref, sc_ref, sh_ref, wr_ref, br_ref, tri_ref,
                 x1_ref, hn_ref, gsel_ref, grp_ref, rank_ref, tot_ref, cnt_sc, *, te, steps_per_chunk):
    @pl.when(pl.program_id(0) % steps_per_chunk == 0)
    def _():
        cnt_sc[...] = jnp.zeros_like(cnt_sc)

    mix = (jnp.dot(hf_ref[...], wo_ref[0, :FOX_WIDTH, :], preferred_element_type=F32)
           + jnp.dot(hm_ref[...], wo_ref[0, FOX_WIDTH:, :], preferred_element_type=F32))
    x1 = x_ref[...] + g1_ref[0] * mix
    x1_ref[...] = x1
    ms = jnp.mean(x1 * x1, axis=1, keepdims=True)
    hn = x1 * lax.rsqrt(ms + EPS) * nf_ref[...]
    hn = hn * (1.0 + sc_ref[0]) + sh_ref[0]
    _to_token_tiles(hn_ref, hn)

    hn_hi = hn.astype(BF16)
    hn_lo = (hn - hn_hi.astype(F32)).astype(BF16)
    l2 = jnp.dot(hn_hi, wr_ref[...], preferred_element_type=F32)
    logits = (l2[:, :LANES] + l2[:, LANES:]
              + jnp.dot(hn_lo, wr_ref[:, :LANES], preferred_element_type=F32))
    aff = jax.nn.sigmoid(logits.T[:N_EXPERTS, :])
    sel = aff + br_ref[...]
    selr = [sel[e:e + 1, :] for e in range(N_EXPERTS)]
    affr = [aff[e:e + 1, :] for e in range(N_EXPERTS)]
    keep = [None] * N_EXPERTS
    score = []
    for g in range(N_GROUPS):
        vs = selr[g * EXPERTS_PER_GROUP:(g + 1) * EXPERTS_PER_GROUP]
        sg = jnp.zeros_like(vs[0])
        for i in range(EXPERTS_PER_GROUP):
            beaten = jnp.zeros_like(vs[0])
            for j in range(EXPERTS_PER_GROUP):
                if j != i:
                    b = (vs[j] >= vs[i]) if j < i else (vs[j] > vs[i])
                    beaten = beaten + jnp.where(b, 1.0, 0.0)
            kp = beaten < 2.0
            keep[g * EXPERTS_PER_GROUP + i] = kp
            sg = sg + jnp.where(kp, vs[i], 0.0)
        score.append(sg)
    chosen = []
    for g in range(N_GROUPS):
        lost = jnp.zeros_like(score[0])
        for g2 in range(N_GROUPS):
            if g2 != g:
                b = (score[g2] >= score[g]) if g2 < g else (score[g2] > score[g])
                lost = lost + jnp.where(b, 1.0, 0.0)
        chosen.append(jnp.where(lost < 0.5, 1.0, 0.0))
    wsel = []
    for i in range(EXPERTS_PER_GROUP):
        wi = jnp.zeros_like(score[0])
        for g in range(N_GROUPS):
            e = g * EXPERTS_PER_GROUP + i
            wi = wi + chosen[g] * jnp.where(keep[e], affr[e], 0.0)
        wsel.append(wi)
    wsum = wsel[0] + wsel[1] + wsel[2] + wsel[3]
    wsel = [w / wsum for w in wsel]

    row8 = lax.broadcasted_iota(jnp.int32, (SUBLANES, te), 0)
    gmat = jnp.zeros((SUBLANES, te), F32)
    wmat = jnp.zeros((SUBLANES, te), F32)
    grp = jnp.zeros_like(score[0])
    for g in range(N_GROUPS):
        gmat = jnp.where(row8 == g, chosen[g], gmat)
        wmat = jnp.where(row8 == g, wsel[g], wmat)
        grp = grp + g * chosen[g]
    pref = jnp.dot(gmat.astype(BF16), tri_ref[...], preferred_element_type=F32) + cnt_sc[:, :1]
    rank = jnp.sum(gmat * pref, axis=0, keepdims=True)
    tot = cnt_sc[:, :1] + jnp.sum(gmat, axis=1, keepdims=True)
    cnt_sc[...] = jnp.broadcast_to(tot, cnt_sc.shape)
    rank_ref[0] = rank.astype(jnp.int32)
    grp_ref[0] = grp.astype(jnp.int32)
    tot_ref[0] = jnp.broadcast_to(tot, (SUBLANES, LANES)).astype(jnp.int32)
    wfull = jnp.concatenate([wmat, jnp.zeros((LANES - SUBLANES, te), F32)], axis=0)
    gsel_ref[...] = wfull.T


def _post_call(x, hf, hm, wo, g1, nf, sc, sh, wr, br, tri, *, layer, bsz, seq, chunk, te=512):
    t, d = x.shape
    steps_per_chunk = chunk // te
    per_seq = seq // te
    n_steps = t // te
    n_chunks = t // chunk
    row = lambda i: (i, 0)
    per_b = lambda i: (i // per_seq, 0, 0)
    const2 = lambda i: (0, 0)
    return pl.pallas_call(
        functools.partial(_post_kernel, te=te, steps_per_chunk=steps_per_chunk),
        grid=(n_steps,),
        in_specs=[
            pl.BlockSpec((te, d), row),
            pl.BlockSpec((te, FOX_WIDTH), row),
            pl.BlockSpec((te, ML_WIDTH), row),
            pl.BlockSpec((1, d, d), lambda i: (layer, 0, 0)),
            pl.BlockSpec((1, 1, d), per_b),
            pl.BlockSpec((1, d), const2),
            pl.BlockSpec((1, 1, d), per_b),
            pl.BlockSpec((1, 1, d), per_b),
            pl.BlockSpec((d, 2 * LANES), const2),
            pl.BlockSpec((N_EXPERTS, 1), const2),
            pl.BlockSpec((te, te), const2),
        ],
        out_specs=[
            pl.BlockSpec((te, d), row),
            pl.BlockSpec((te * TOK_ROWS, LANES), row),
            pl.BlockSpec((te, LANES), row),
            pl.BlockSpec((1, 1, te), lambda i: (i, 0, 0)),
            pl.BlockSpec((1, 1, te), lambda i: (i, 0, 0)),
            pl.BlockSpec((1, SUBLANES, LANES), lambda i: (i // steps_per_chunk, 0, 0)),
        ],
        out_shape=[
            jax.ShapeDtypeStruct((t, d), F32),
            jax.ShapeDtypeStruct((t * TOK_ROWS, LANES), F32),
            jax.ShapeDtypeStruct((t, LANES), F32),
            jax.ShapeDtypeStruct((n_steps, 1, te), jnp.int32),
            jax.ShapeDtypeStruct((n_steps, 1, te), jnp.int32),
            jax.ShapeDtypeStruct((n_chunks, SUBLANES, LANES), jnp.int32),
        ],
        scratch_shapes=[pltpu.VMEM((SUBLANES, LANES), F32)],
        compiler_params=_cparams(1, 48),
        name="post_router",
    )(x, hf, hm, wo, g1, nf, sc, sh, wr, br, tri)


def _scatter_kernel(pos_ref, hn_ref, gsel_ref, xb_ref, gs_ref, xs_sc, *, chunk, rows, tm):
    xs_sc[...] = jnp.zeros_like(xs_sc)
    gs_ref[...] = jnp.zeros_like(gs_ref)

    def body(t, carry):
        p = pos_ref[0, 0, t]
        src = pl.multiple_of(t * TOK_ROWS, TOK_ROWS)
        dst = pl.multiple_of(p * TOK_ROWS, TOK_ROWS)
        xs_sc[pl.ds(dst, TOK_ROWS), :] = hn_ref[pl.ds(src, TOK_ROWS), :]
        gs_ref[pl.ds(p, 1), :] = gsel_ref[pl.ds(t, 1), :]
        return carry

    lax.fori_loop(0, chunk, body, 0, unroll=8)
    for j in range(rows // tm):
        for c in range(TOK_ROWS):
            xb_ref[j * tm:(j + 1) * tm, c * LANES:(c + 1) * LANES] = (
                xs_sc[pl.ds(j * tm * TOK_ROWS + c, tm, stride=TOK_ROWS), :].astype(BF16))


def _scatter_call(pos, hn_tt, gsel, *, chunk, rows, tm):
    t = gsel.shape[0]
    n_chunks = t // chunk
    return pl.pallas_call(
        functools.partial(_scatter_kernel, chunk=chunk, rows=rows, tm=tm),
        grid=(n_chunks,),
        in_specs=[
            pl.BlockSpec((1, 1, chunk), lambda c: (c, 0, 0), memory_space=pltpu.SMEM),
            pl.BlockSpec((chunk * TOK_ROWS, LANES), lambda c: (c, 0)),
            pl.BlockSpec((chunk, LANES), lambda c: (c, 0)),
        ],
        out_specs=[
            pl.BlockSpec((rows, D_MODEL), lambda c: (c, 0)),
            pl.BlockSpec((rows, LANES), lambda c: (c, 0)),
        ],
        out_shape=[
            jax.ShapeDtypeStruct((n_chunks * rows, D_MODEL), BF16),
            jax.ShapeDtypeStruct((n_chunks * rows, LANES), F32),
        ],
        scratch_shapes=[pltpu.VMEM((rows * TOK_ROWS, LANES), F32)],
        compiler_params=_cparams(1, 56),
        name="moe_scatter",
    )(pos, hn_tt, gsel)


def _experts_kernel(blk_ref, grp_ref, xb_ref, gs_ref, wg_ref, wu_ref, wd_ref, y_ref, *, tm):
    g = grp_ref[pl.program_id(0)]

    @pl.when(g < N_GROUPS)
    def _():
        x = xb_ref[...]
        gs = gs_ref[...]
        acts = []
        for i in range(EXPERTS_PER_GROUP):
            hg = jnp.dot(x, wg_ref[i], preferred_element_type=F32)
            hu = jnp.dot(x, wu_ref[i], preferred_element_type=F32)
            acts.append((_silu(hg) * hu * gs[:, i:i + 1]).astype(BF16))
        y = jnp.dot(jnp.concatenate(acts, axis=1), wd_ref[0], preferred_element_type=F32)
        _to_token_tiles(y_ref, y)

    @pl.when(g >= N_GROUPS)
    def _():
        y_ref[...] = jnp.zeros_like(y_ref)


def _experts_call(tile_blk, tile_grp, xb, gs, wg, wu, wd, *, layer, tm):
    n_rows = xb.shape[0]
    n_slots = n_rows // tm
    wmap = lambda s, blk, grp: (layer * N_GROUPS + jnp.minimum(grp[s], N_GROUPS - 1), 0, 0)
    grid_spec = pltpu.PrefetchScalarGridSpec(
        num_scalar_prefetch=2,
        grid=(n_slots,),
        in_specs=[
            pl.BlockSpec((tm, D_MODEL), lambda s, blk, grp: (blk[s], 0)),
            pl.BlockSpec((tm, LANES), lambda s, blk, grp: (blk[s], 0)),
            pl.BlockSpec((EXPERTS_PER_GROUP, D_MODEL, D_FF), wmap),
            pl.BlockSpec((EXPERTS_PER_GROUP, D_MODEL, D_FF), wmap),
            pl.BlockSpec((1, EXPERTS_PER_GROUP * D_FF, D_MODEL), wmap),
        ],
        out_specs=pl.BlockSpec((tm * TOK_ROWS, LANES), lambda s, blk, grp: (blk[s], 0)),
    )
    return pl.pallas_call(
        functools.partial(_experts_kernel, tm=tm),
        grid_spec=grid_spec,
        out_shape=jax.ShapeDtypeStruct((n_rows * TOK_ROWS, LANES), F32),
        compiler_params=_cparams(1, 48),
        name="moe_experts",
    )(tile_blk, tile_grp, xb, gs, wg, wu, wd)


def _gather_kernel(pos_ref, ys_ref, o_ref, *, chunk):
    def body(t, carry):
        p = pos_ref[0, 0, t]
        src = pl.multiple_of(p * TOK_ROWS, TOK_ROWS)
        dst = pl.multiple_of(t * TOK_ROWS, TOK_ROWS)
        o_ref[pl.ds(dst, TOK_ROWS), :] = ys_ref[pl.ds(src, TOK_ROWS), :]
        return carry

    lax.fori_loop(0, chunk, body, 0, unroll=8)


def _gather_call(pos, ys_tt, *, chunk, rows, n_tok):
    n_chunks = n_tok // chunk
    return pl.pallas_call(
        functools.partial(_gather_kernel, chunk=chunk),
        grid=(n_chunks,),
        in_specs=[
            pl.BlockSpec((1, 1, chunk), lambda c: (c, 0, 0), memory_space=pltpu.SMEM),
            pl.BlockSpec((rows * TOK_ROWS, LANES), lambda c: (c, 0)),
        ],
        out_specs=pl.BlockSpec((chunk * TOK_ROWS, LANES), lambda c: (c, 0)),
        out_shape=jax.ShapeDtypeStruct((n_tok * TOK_ROWS, LANES), F32),
        compiler_params=_cparams(1, 56),
        name="moe_gather",
    )(pos, ys_tt)


def _gather_final_kernel(pos_ref, ys_ref, x_ref, g2_ref, nf_ref, o_ref, o_sc, *, part, sub):
    base = pl.program_id(1) * part

    def body(t, carry):
        p = pos_ref[0, 0, base + t]
        src = pl.multiple_of(p * TOK_ROWS, TOK_ROWS)
        dst = pl.multiple_of(t * TOK_ROWS, TOK_ROWS)
        o_sc[pl.ds(dst, TOK_ROWS), :] = ys_ref[pl.ds(src, TOK_ROWS), :]
        return carry

    lax.fori_loop(0, part, body, 0, unroll=8)
    for r in range(part // sub):
        rs = slice(r * sub, (r + 1) * sub)
        moe = jnp.concatenate(
            [o_sc[pl.ds(r * sub * TOK_ROWS + c, sub, stride=TOK_ROWS), :] for c in range(TOK_ROWS)], axis=1)
        x = x_ref[rs, :] + g2_ref[0] * moe
        ms = jnp.mean(x * x, axis=1, keepdims=True)
        o_ref[rs, :] = x * lax.rsqrt(ms + EPS) * nf_ref[...]


def _gather_final_call(pos, ys_tt, x, g2, nf, *, chunk, rows, seq):
    n_tok, d = x.shape
    n_chunks = n_tok // chunk
    parts = 2
    part = chunk // parts
    return pl.pallas_call(
        functools.partial(_gather_final_kernel, part=part, sub=256),
        grid=(n_chunks, parts),
        in_specs=[
            pl.BlockSpec((1, 1, chunk), lambda c, s: (c, 0, 0), memory_space=pltpu.SMEM),
            pl.BlockSpec((rows * TOK_ROWS, LANES), lambda c, s: (c, 0)),
            pl.BlockSpec((part, d), lambda c, s: (c * parts + s, 0)),
            pl.BlockSpec((1, 1, d), lambda c, s: ((c * chunk + s * part) // seq, 0, 0)),
            pl.BlockSpec((1, d), lambda c, s: (0, 0)),
        ],
        out_specs=pl.BlockSpec((part, d), lambda c, s: (c * parts + s, 0)),
        out_shape=jax.ShapeDtypeStruct((n_tok, d), F32),
        scratch_shapes=[pltpu.VMEM((part * TOK_ROWS, LANES), F32)],
        compiler_params=_cparams(2, 56),
        name="moe_gather_final",
    )(pos, ys_tt, x, g2, nf)


def _moe_tiles(tot, *, tm, tiles_per_chunk):
    nt = (tot + tm - 1) // tm
    ts = jnp.cumsum(nt, axis=1) - nt
    off = (ts * tm).astype(jnp.int32)
    j = jnp.arange(tiles_per_chunk, dtype=jnp.int32)[None, :, None]
    inside = (j >= ts[:, None, :]) & (j < (ts + nt)[:, None, :])
    key = jnp.where(jnp.any(inside, axis=-1), jnp.argmax(inside, axis=-1), N_GROUPS).reshape(-1)
    order = jnp.argsort(key, stable=True).astype(jnp.int32)
    return off, order, key[order].astype(jnp.int32)


def kernel(x, c, w_in, conv_w, conv_b, fox_f_bias, mlstm_i_bias, mlstm_f_bias, fox_out_norm,
           mlstm_out_norm, w_out, w_ada, b_ada, norm_mix, norm_ffn, w_router, b_router, w_gate,
           w_up, w_down, norm_final):
    bsz, seq, d = x.shape
    depth = w_in.shape[0]
    t = bsz * seq
    assert d == D_MODEL and w_in.shape[-1] == IN_COLS
    chunk = min(2048, seq)
    tm = 256
    rows = -(-(chunk + N_GROUPS * (tm - 1)) // tm) * tm
    te = min(1024, seq)
    assert seq % 512 == 0 and seq % chunk == 0 and chunk % te == 0

    mods = _ada_call(c, w_ada, b_ada).reshape(depth, bsz, N_ADA, 1, d)
    wm_all = jnp.concatenate([w_in[:, :, FOX_Q:FOX_F], w_in[:, :, ML_Q:ML_I], w_in[:, :, ML_O:IN_COLS]],
                             axis=2).astype(BF16)
    zpad = lambda n: jnp.zeros((depth, d, n), F32)
    wg_all = jnp.concatenate([
        w_in[:, :, FOX_F:ML_Q], w_in[:, :, ML_F:ML_O], zpad(LANES - FOX_HEADS - ML_HEADS),
        zpad(GATE_ML), w_in[:, :, ML_I:ML_F], zpad(LANES - GATE_ML - ML_HEADS)], axis=2).astype(BF16)
    wo_all = w_out.astype(BF16)
    wge_all = w_gate.astype(BF16).reshape(depth * N_EXPERTS, d, D_FF)
    wue_all = w_up.astype(BF16).reshape(depth * N_EXPERTS, d, D_FF)
    wde_all = w_down.astype(BF16).reshape(depth * N_GROUPS, EXPERTS_PER_GROUP * D_FF, d)
    xf = x.reshape(t, d)

    tri = (jnp.arange(te)[:, None] < jnp.arange(te)[None, :]).astype(BF16)
    wr_f = jnp.pad(w_router, ((0, 0), (0, LANES - N_EXPERTS))).astype(F32)
    wr_hi = wr_f.astype(BF16)
    wr = jnp.concatenate([wr_hi, (wr_f - wr_hi.astype(F32)).astype(BF16)], axis=1)
    br = b_router.reshape(N_EXPERTS, 1).astype(F32)

    moe_tt = None
    g2_prev = None
    for l in range(depth):
        sh1, sc1, g1, sh2, sc2, g2 = [mods[l, :, i] for i in range(N_ADA)]
        zb = lambda n: jnp.zeros((n,), F32)
        gb = jnp.concatenate([
            fox_f_bias[l], mlstm_f_bias[l], zb(LANES - FOX_HEADS - ML_HEADS),
            zb(GATE_ML), mlstm_i_bias[l], zb(LANES - GATE_ML - ML_HEADS)]).reshape(1, 2 * LANES)

        xf, qat, ka, fvt, qkv, mkt, fc, gr = _in_proj_call(
            xf, moe_tt, g2_prev, norm_mix[l].reshape(1, d), sc1, sh1, wm_all, wg_all, gb,
            conv_w[l], conv_b[l].reshape(1, -1), layer=l, bsz=bsz, seq=seq)
        hf = _fox_call(qat, ka, fvt, fox_out_norm[l].reshape(1, FOX_WIDTH), bsz=bsz, seq=seq)
        hm = _mlstm_call(qkv, mkt, fc, gr, mlstm_out_norm[l].reshape(1, ML_WIDTH), bsz=bsz, seq=seq)
        xf, hn_tt, gsel, grp, rank, tot = _post_call(
            xf, hf, hm, wo_all, g1, norm_ffn[l].reshape(1, d), sc2, sh2, wr, br, tri,
            layer=l, bsz=bsz, seq=seq, chunk=chunk, te=te)

        n_chunks = t // chunk
        grp = grp.reshape(n_chunks, 1, chunk)
        rank = rank.reshape(n_chunks, 1, chunk)
        off, tile_blk, tile_grp = _moe_tiles(tot[:, :N_GROUPS, 0], tm=tm, tiles_per_chunk=rows // tm)
        pos = rank
        for g in range(N_GROUPS):
            pos = pos + jnp.where(grp == g, off[:, g].reshape(n_chunks, 1, 1), 0)
        xb, gs = _scatter_call(pos, hn_tt, gsel, chunk=chunk, rows=rows, tm=tm)
        ys_tt = _experts_call(tile_blk, tile_grp, xb, gs, wge_all, wue_all, wde_all, layer=l, tm=tm)
        if l < depth - 1:
            moe_tt = _gather_call(pos, ys_tt, chunk=chunk, rows=rows, n_tok=t)
            g2_prev = g2
        else:
            out = _gather_final_call(pos, ys_tt, xf, g2, norm_final.reshape(1, d),
                                     chunk=chunk, rows=rows, seq=seq)
    return out.reshape(bsz, seq, d)
```

```python
import functools

import numpy as np
import jax
import jax.numpy as jnp
from jax import lax
from jax.experimental import pallas as pl
from jax.experimental.pallas import tpu as pltpu

F32 = jnp.float32
BF16 = jnp.bfloat16

LANES = 128
SUBLANES = 8
VMEM_BYTES_V7X = 64 * 1024 * 1024

D_MODEL = 1024
FOX_HEADS = 8
FOX_HEAD_DIM = 64
FOX_WIDTH = FOX_HEADS * FOX_HEAD_DIM
ML_HEADS = 4
ML_HEAD_DIM = 128
ML_WIDTH = ML_HEADS * ML_HEAD_DIM
CONV_WIDTH = 4
N_EXPERTS = 16
N_GROUPS = 4
EXPERTS_PER_GROUP = 4
D_FF = 512
N_ADA = 6
EPS = 1e-6
NEG = -1e30

FOX_Q = 0
FOX_F = 3 * FOX_WIDTH
ML_Q = FOX_F + FOX_HEADS
ML_I = ML_Q + 3 * ML_WIDTH
ML_F = ML_I + ML_HEADS
ML_O = ML_F + ML_HEADS
IN_COLS = ML_O + ML_WIDTH

MAIN_COLS = 7 * 512
BLK_FQ, BLK_FK, BLK_FV, BLK_MQ, BLK_MK, BLK_MV, BLK_MO = range(7)
REST_COLS = 3 * 512
OUT_MQ, OUT_MV, OUT_MO = range(3)
AUG_COLS = FOX_HEADS * 128
BIAS_TERMS = 3
VT_PAD = 16
LOG2E = 1.4426950408889634
GATE_FOX = 0
GATE_ML = FOX_HEADS
GATE_ROWS = 16

TOK_ROWS = D_MODEL // LANES


def _cparams(n_grid, vmem_mb):
    return pltpu.CompilerParams(
        dimension_semantics=("arbitrary",) * n_grid,
        vmem_limit_bytes=vmem_mb * 1024 * 1024)


def _silu(x):
    return x * jax.nn.sigmoid(x)


def _log_sigmoid(z):
    return jnp.minimum(z, 0.0) - jnp.log1p(jnp.exp(-jnp.abs(z)))


def _cumsum_rows(x):
    n = x.shape[0]
    row = lax.broadcasted_iota(jnp.int32, x.shape, 0)
    s = 1
    while s < n:
        x = x + jnp.where(row >= s, pltpu.roll(x, s, axis=0), 0.0)
        s *= 2
    return x


def _from_token_tiles(ref, n_tok):
    return jnp.concatenate(
        [ref[pl.ds(c, n_tok, stride=TOK_ROWS), :] for c in range(TOK_ROWS)], axis=1)


def _to_token_tiles(ref, val, row0=0):
    n = val.shape[0]
    for c in range(TOK_ROWS):
        ref[pl.ds(row0 * TOK_ROWS + c, n, stride=TOK_ROWS), :] = val[:, c * LANES:(c + 1) * LANES]


def _ada_kernel(c_ref, w_ref, b_ref, o_ref):
    c = c_ref[...]
    o_ref[0] = jnp.dot(_silu(c), w_ref[0], preferred_element_type=F32,
                       precision=lax.Precision.HIGHEST) + b_ref[0]


def _ada_call(c, w_ada, b_ada):
    depth, d, n = w_ada.shape
    bsz = c.shape[0]
    tn = 1536
    return pl.pallas_call(
        _ada_kernel,
        grid=(depth, n // tn),
        in_specs=[
            pl.BlockSpec((bsz, d), lambda l, j: (0, 0)),
            pl.BlockSpec((1, d, tn), lambda l, j: (l, 0, j)),
            pl.BlockSpec((1, 1, tn), lambda l, j: (l, 0, j)),
        ],
        out_specs=pl.BlockSpec((1, bsz, tn), lambda l, j: (l, 0, j)),
        out_shape=jax.ShapeDtypeStruct((depth, bsz, n), F32),
        compiler_params=_cparams(2, 32),
        name="ada_mod",
    )(c, w_ada, b_ada.reshape(depth, 1, n))


def _in_proj_kernel(*refs, tm, combine):
    if combine:
        (x_ref, ott_ref, g2_ref, nm_ref, sc_ref, sh_ref, wm_ref, wg_ref, gb_ref, cw_ref, cb_ref,
         place_ref, ones_ref,
         xn_ref, qa_ref, ka_ref, vt_ref, qkv_ref, kt_ref, fc_ref, gr_ref, fcar, ccar) = refs
    else:
        (x_ref, nm_ref, sc_ref, sh_ref, wm_ref, wg_ref, gb_ref, cw_ref, cb_ref,
         place_ref, ones_ref,
         qa_ref, ka_ref, vt_ref, qkv_ref, kt_ref, fc_ref, gr_ref, fcar, ccar) = refs

    @pl.when(pl.program_id(1) == 0)
    def _():
        fcar[...] = jnp.zeros_like(fcar)
        ccar[...] = jnp.zeros_like(ccar)

    x = x_ref[...]
    if combine:
        x = x + g2_ref[0] * _from_token_tiles(ott_ref, tm)
        xn_ref[...] = x
    ms = jnp.mean(x * x, axis=1, keepdims=True)
    hn = x * lax.rsqrt(ms + EPS) * nm_ref[...]
    hn = hn * (1.0 + sc_ref[0]) + sh_ref[0]
    hb = hn.astype(BF16)

    gp = jnp.dot(hb, wg_ref[0], preferred_element_type=F32) + gb_ref[...]
    lf = _log_sigmoid(gp[:, :LANES])
    fcum = _cumsum_rows(lf) + fcar[...]
    fcar[...] = fcum[tm - 1:tm, :]
    gml = gp[:, LANES:] - fcum
    fc_ref[...] = fcum
    gr_ref[0] = gml.T[:GATE_ROWS, :]

    def mm(j):
        return jnp.dot(hb, wm_ref[0, :, j * 512:(j + 1) * 512], preferred_element_type=F32)

    def put(j, v):
        qkv_ref[:, j * 512:(j + 1) * 512] = v.astype(BF16)

    lane = lax.broadcasted_iota(jnp.int32, (tm, LANES), 1)
    fs = fcum * LOG2E
    hi = fs.astype(BF16).astype(F32)
    mid = (fs - hi).astype(BF16).astype(F32)
    low = (fs - hi - mid).astype(BF16).astype(F32)
    packed = jnp.where(lane < FOX_HEADS, hi,
                       jnp.where(lane < 2 * FOX_HEADS, pltpu.roll(mid, FOX_HEADS, axis=1),
                                 jnp.where(lane < 3 * FOX_HEADS, pltpu.roll(low, 2 * FOX_HEADS, axis=1), 0.0)))
    bias = jnp.dot(packed.astype(BF16), place_ref[...], preferred_element_type=F32) + ones_ref[...]

    def put_heads(ref, val, col0, transposed):
        lo = lane < FOX_HEAD_DIM
        for p in range(FOX_HEADS // 2):
            slab = val[:, p * LANES:(p + 1) * LANES]
            for h, data in ((2 * p, slab), (2 * p + 1, pltpu.roll(slab, FOX_HEAD_DIM, axis=1))):
                blk = jnp.where(lo, data, 0.0) + bias[:, col0 + h * LANES:col0 + (h + 1) * LANES]
                if transposed:
                    ref[0, 0, h * LANES:(h + 1) * LANES, :] = blk.T.astype(BF16)
                else:
                    ref[:, h * LANES:(h + 1) * LANES] = blk.astype(BF16)

    u = jnp.concatenate([mm(BLK_MQ), mm(BLK_MK)], axis=1)
    prev = ccar[...]
    ccar[...] = u[tm - SUBLANES:tm, :]
    row8 = lax.broadcasted_iota(jnp.int32, prev.shape, 0)
    y = cb_ref[...] + cw_ref[CONV_WIDTH - 1:CONV_WIDTH, :] * u
    for k in range(1, CONV_WIDTH):
        r = pltpu.roll(u, k, axis=0)
        top = jnp.where(row8 < k, pltpu.roll(prev, k, axis=0), r[:SUBLANES])
        shifted = jnp.concatenate([top, r[SUBLANES:]], axis=0)
        y = y + cw_ref[CONV_WIDTH - 1 - k:CONV_WIDTH - k, :] * shifted
    act = _silu(y)
    put(OUT_MQ, act[:, :ML_WIDTH])
    kt_ref[0, 0] = (act[:, ML_WIDTH:] * (ML_HEAD_DIM ** -0.5)).T.astype(BF16)

    put(OUT_MO, jax.nn.sigmoid(mm(BLK_MO)))
    put(OUT_MV, mm(BLK_MV))
    vt_ref[0, 0] = mm(BLK_FV).astype(BF16).T
    put_heads(qa_ref, mm(BLK_FQ) * (FOX_HEAD_DIM ** -0.5 * LOG2E), 0, True)
    put_heads(ka_ref, mm(BLK_FK), AUG_COLS, False)


def _bias_placement():
    place = np.zeros((LANES, 2 * AUG_COLS), np.float32)
    ones = np.zeros((1, 2 * AUG_COLS), np.float32)
    for h in range(FOX_HEADS):
        for term in range(BIAS_TERMS):
            src = term * FOX_HEADS + h
            place[src, h * LANES + FOX_HEAD_DIM + term] = 1.0
            ones[0, h * LANES + FOX_HEAD_DIM + BIAS_TERMS + term] = 1.0
            place[src, AUG_COLS + h * LANES + FOX_HEAD_DIM + BIAS_TERMS + term] = -1.0
            ones[0, AUG_COLS + h * LANES + FOX_HEAD_DIM + term] = 1.0
    return jnp.asarray(place, BF16), jnp.asarray(ones, F32)


def _in_proj_call(x, moe_tt, g2, nm, sc, sh, wm, wg, gb, cw, cb, *, layer, bsz, seq, tm=512):
    t, d = x.shape
    ns = seq // tm
    combine = moe_tt is not None
    row = lambda b, s: (b * ns + s, 0)
    per_b = lambda b, s: (b, 0, 0)
    const2 = lambda b, s: (0, 0)
    in_specs = [pl.BlockSpec((tm, d), row)]
    args = [x]
    if combine:
        in_specs += [pl.BlockSpec((tm * TOK_ROWS, LANES), row), pl.BlockSpec((1, 1, d), per_b)]
        args += [moe_tt, g2]
    in_specs += [
        pl.BlockSpec((1, d), const2),
        pl.BlockSpec((1, 1, d), per_b),
        pl.BlockSpec((1, 1, d), per_b),
        pl.BlockSpec((1, d, MAIN_COLS), lambda b, s: (layer, 0, 0)),
        pl.BlockSpec((1, d, 2 * LANES), lambda b, s: (layer, 0, 0)),
        pl.BlockSpec((1, 2 * LANES), const2),
        pl.BlockSpec((CONV_WIDTH, 2 * ML_WIDTH), const2),
        pl.BlockSpec((1, 2 * ML_WIDTH), const2),
        pl.BlockSpec((LANES, 2 * AUG_COLS), const2),
        pl.BlockSpec((1, 2 * AUG_COLS), const2),
    ]
    place, ones = _bias_placement()
    args += [nm, sc, sh, wm, wg, gb, cw, cb, place, ones]
    out_specs = []
    out_shape = []
    if combine:
        out_specs.append(pl.BlockSpec((tm, d), row))
        out_shape.append(jax.ShapeDtypeStruct((t, d), F32))
    tiled = lambda b, s: (b, s, 0, 0)
    out_specs += [
        pl.BlockSpec((1, 1, AUG_COLS, tm), tiled),
        pl.BlockSpec((tm, AUG_COLS), row),
        pl.BlockSpec((1, 1, FOX_WIDTH, tm), tiled),
        pl.BlockSpec((tm, REST_COLS), row),
        pl.BlockSpec((1, 1, ML_WIDTH, tm), tiled),
        pl.BlockSpec((tm, LANES), row),
        pl.BlockSpec((1, GATE_ROWS, tm), lambda b, s: (b, 0, s)),
    ]
    out_shape += [
        jax.ShapeDtypeStruct((bsz, ns, AUG_COLS, tm), BF16),
        jax.ShapeDtypeStruct((t, AUG_COLS), BF16),
        jax.ShapeDtypeStruct((bsz, ns, FOX_WIDTH, tm), BF16),
        jax.ShapeDtypeStruct((t, REST_COLS), BF16),
        jax.ShapeDtypeStruct((bsz, ns, ML_WIDTH, tm), BF16),
        jax.ShapeDtypeStruct((t, LANES), F32),
        jax.ShapeDtypeStruct((bsz, GATE_ROWS, seq), F32),
    ]
    outs = pl.pallas_call(
        functools.partial(_in_proj_kernel, tm=tm, combine=combine),
        grid=(bsz, ns),
        in_specs=in_specs,
        out_specs=out_specs,
        out_shape=out_shape,
        scratch_shapes=[pltpu.VMEM((1, LANES), F32), pltpu.VMEM((SUBLANES, 2 * ML_WIDTH), F32)],
        compiler_params=_cparams(2, 48),
        name="in_proj",
    )(*args)
    if combine:
        return outs
    return [x] + list(outs)


def _fox_kernel(qt_ref, k_ref, vt_ref, ng_ref, cm_ref, o_ref, m_sc, acc_sc, *, tq, tk, nh):
    qi = pl.program_id(2)
    ones_rows = jnp.where(lax.broadcasted_iota(jnp.int32, (VT_PAD, tk), 0) == 0, 1.0, 0.0).astype(BF16)
    for h in range(nh):
        m_sc[h] = jnp.full((SUBLANES, tq), NEG, F32)
        acc_sc[h] = jnp.zeros((FOX_HEAD_DIM + VT_PAD, tq), F32)

    def body(j, carry, diagonal):
        k0 = pl.multiple_of(j * tk, tk)

        def logits(h):
            hs = slice(h * LANES, (h + 1) * LANES)
            z = jnp.dot(k_ref[pl.ds(k0, tk), hs], qt_ref[0, 0, hs, :], preferred_element_type=F32)
            return z + cm_ref[...] if diagonal else z

        ahead = 2
        zs = [logits(h) for h in range(min(ahead, nh))]
        for h in range(nh):
            ds_ = slice(h * FOX_HEAD_DIM, (h + 1) * FOX_HEAD_DIM)
            z = zs[h]
            if h + ahead < nh:
                zs.append(logits(h + ahead))
            m_prev = m_sc[h]
            m_new = jnp.maximum(m_prev, jnp.max(z, axis=0, keepdims=True))
            alpha = jnp.exp2(m_prev - m_new)
            p = jnp.exp2(z - m_new[:1, :])
            va = jnp.concatenate([vt_ref[0, j, ds_, :], ones_rows], axis=0)
            acc_sc[h] = alpha[:1, :] * acc_sc[h] + jnp.dot(va, p.astype(BF16), preferred_element_type=F32)
            m_sc[h] = m_new
        return carry

    lax.fori_loop(0, qi, functools.partial(body, diagonal=False), 0)
    lax.fori_loop(qi, qi + 1, functools.partial(body, diagonal=True), 0)

    for p in range(nh // 2):
        outs = []
        for h in (2 * p, 2 * p + 1):
            acc = acc_sc[h]
            num = acc[:FOX_HEAD_DIM, :]
            l = acc[FOX_HEAD_DIM:FOX_HEAD_DIM + 1, :]
            ms = jnp.mean(num * num, axis=0, keepdims=True)
            outs.append(num * lax.rsqrt(ms + EPS * l * l))
        ps = slice(p * LANES, (p + 1) * LANES)
        o_ref[:, ps] = (jnp.concatenate(outs, axis=0).T * ng_ref[:, ps]).astype(BF16)


def _fox_call(qat, ka, fvt, ng, *, bsz, seq, tq=512, nh=8):
    t = ka.shape[0]
    nq = seq // tq
    ngrp = FOX_HEADS // nh
    vw = nh * FOX_HEAD_DIM
    cmask = jnp.where(jnp.arange(tq)[:, None] <= jnp.arange(tq)[None, :], 0.0, NEG).astype(F32)
    return pl.pallas_call(
        functools.partial(_fox_kernel, tq=tq, tk=tq, nh=nh),
        grid=(bsz, ngrp, nq),
        in_specs=[
            pl.BlockSpec((1, 1, nh * LANES, tq), lambda b, p, i: (b, i, p, 0)),
            pl.BlockSpec((seq, nh * LANES), lambda b, p, i: (b, p)),
            pl.BlockSpec((1, nq, vw, tq), lambda b, p, i: (b, 0, p, 0)),
            pl.BlockSpec((1, vw), lambda b, p, i: (0, p)),
            pl.BlockSpec((tq, tq), lambda b, p, i: (0, 0)),
        ],
        out_specs=pl.BlockSpec((tq, vw), lambda b, p, i: (b * nq + i, p)),
        out_shape=jax.ShapeDtypeStruct((t, FOX_WIDTH), BF16),
        scratch_shapes=[pltpu.VMEM((nh, SUBLANES, tq), F32),
                        pltpu.VMEM((nh, FOX_HEAD_DIM + VT_PAD, tq), F32)],
        compiler_params=_cparams(3, 48),
        name="fox_attn",
    )(qat, ka, fvt, ng, cmask)


def _mlstm_kernel(q_ref, kt_ref, v_ref, og_ref, fc_ref, gr_ref, ng_ref, o_ref, ct_sc, u_sc, *, ch, per_tile):
    @pl.when(pl.program_id(1) == 0)
    def _():
        ct_sc[...] = jnp.zeros_like(ct_sc)
        u_sc[...] = jnp.zeros_like(u_sc)

    causal = (lax.broadcasted_iota(jnp.int32, (ch, ch), 1)
              <= lax.broadcasted_iota(jnp.int32, (ch, ch), 0))
    lane = lax.broadcasted_iota(jnp.int32, (ch, LANES), 1)
    fc = fc_ref[...]
    koff = pl.multiple_of((pl.program_id(1) % per_tile) * ch, ch)
    for h in range(ML_HEADS):
        sl = slice(h * ML_HEAD_DIM, (h + 1) * ML_HEAD_DIM)
        gl = GATE_ML + h
        q = q_ref[:, sl]
        kt = kt_ref[0, 0, sl, pl.ds(koff, ch)]
        vp = jnp.concatenate([v_ref[:, sl], jnp.where(lane == gl, 1.0, 0.0).astype(BF16)], axis=1)
        g_row = gr_ref[0, gl:gl + 1, :]
        u_prev = u_sc[h][:, :1]
        gm = jnp.where(causal, g_row, NEG)
        u_i = jnp.maximum(u_prev, jnp.max(gm, axis=1, keepdims=True))
        dmat = jnp.exp(gm - u_i)
        s = jnp.dot(q, kt, preferred_element_type=F32)
        scores = (s * dmat).astype(BF16)
        inter = jnp.exp(u_prev - u_i)
        ct = ct_sc[h]
        nd = (jnp.dot(scores, vp, preferred_element_type=F32)
              + jnp.dot(q, ct.astype(BF16), preferred_element_type=F32) * inter)
        num = nd[:, :ML_HEAD_DIM]
        den = jnp.maximum(jnp.abs(nd[:, ML_HEAD_DIM:]), jnp.exp(-(jnp.where(lane == gl, fc, 0.0) + u_i)))
        ms = jnp.mean(num * num, axis=1, keepdims=True)
        scale = lax.rsqrt(ms + EPS * den * den)[:, gl:gl + 1]
        y = num * scale * ng_ref[:, sl] * og_ref[:, sl].astype(F32)
        o_ref[:, sl] = y.astype(BF16)
        u_new = jnp.maximum(u_prev, jnp.max(g_row, axis=1, keepdims=True))
        ktw = (kt.astype(F32) * jnp.exp(g_row - u_new)).astype(BF16)
        ct_sc[h] = jnp.exp(u_prev - u_new) * ct + jnp.dot(ktw, vp, preferred_element_type=F32)
        u_sc[h] = jnp.broadcast_to(u_new, (1, LANES))


def _mlstm_call(qkv, mkt, fc, gr, ng, *, bsz, seq, ch=512):
    t = qkv.shape[0]
    nc = seq // ch
    ktile = mkt.shape[-1]
    per_tile = ktile // ch
    row = lambda b, c: (b * nc + c, 0)
    return pl.pallas_call(
        functools.partial(_mlstm_kernel, ch=ch, per_tile=per_tile),
        grid=(bsz, nc),
        in_specs=[
            pl.BlockSpec((ch, ML_WIDTH), lambda b, c: (b * nc + c, OUT_MQ)),
            pl.BlockSpec((1, 1, ML_WIDTH, ktile), lambda b, c: (b, c // per_tile, 0, 0)),
            pl.BlockSpec((ch, ML_WIDTH), lambda b, c: (b * nc + c, OUT_MV)),
            pl.BlockSpec((ch, ML_WIDTH), lambda b, c: (b * nc + c, OUT_MO)),
            pl.BlockSpec((ch, LANES), row),
            pl.BlockSpec((1, GATE_ROWS, ch), lambda b, c: (b, 0, c)),
            pl.BlockSpec((1, ML_WIDTH), lambda b, c: (0, 0)),
        ],
        out_specs=pl.BlockSpec((ch, ML_WIDTH), row),
        out_shape=jax.ShapeDtypeStruct((t, ML_WIDTH), BF16),
        scratch_shapes=[pltpu.VMEM((ML_HEADS, ML_HEAD_DIM, 2 * ML_HEAD_DIM), F32),
                        pltpu.VMEM((ML_HEADS, 1, LANES), F32)],
        compiler_params=_cparams(2, 32),
        name="mlstm",
    )(qkv, mkt, qkv, qkv, fc, gr, ng)


def _post_kernel(x_ref, hf_ref, hm_ref, wo_ref, g1_ref, nf_ref, sc_ref, sh_ref, wr_ref, br_ref, tri_ref,
                 x1_ref, hn_ref, gsel_ref, grp_ref, rank_ref, tot_ref, cnt_sc, *, te, steps_per_chunk):
    @pl.when(pl.program_id(0) % steps_per_chunk == 0)
    def _():
        cnt_sc[...] = jnp.zeros_like(cnt_sc)

    mix = (jnp.dot(hf_ref[...], wo_ref[0, :FOX_WIDTH, :], preferred_element_type=F32)
           + jnp.dot(hm_ref[...], wo_ref[0, FOX_WIDTH:, :], preferred_element_type=F32))
    x1 = x_ref[...] + g1_ref[0] * mix
    x1_ref[...] = x1
    ms = jnp.mean(x1 * x1, axis=1, keepdims=True)
    hn = x1 * lax.rsqrt(ms + EPS) * nf_ref[...]
    hn = hn * (1.0 + sc_ref[0]) + sh_ref[0]
    _to_token_tiles(hn_ref, hn)

    hn_hi = hn.astype(BF16)
    hn_lo = (hn - hn_hi.astype(F32)).astype(BF16)
    l2 = jnp.dot(hn_hi, wr_ref[...], preferred_element_type=F32)
    logits = (l2[:, :LANES] + l2[:, LANES:]
              + jnp.dot(hn_lo, wr_ref[:, :LANES], preferred_element_type=F32))
    aff = jax.nn.sigmoid(logits.T[:N_EXPERTS, :])
    sel = aff + br_ref[...]
    selr = [sel[e:e + 1, :] for e in range(N_EXPERTS)]
    affr = [aff[e:e + 1, :] for e in range(N_EXPERTS)]
    keep = [None] * N_EXPERTS
    score = []
    for g in range(N_GROUPS):
        vs = selr[g * EXPERTS_PER_GROUP:(g + 1) * EXPERTS_PER_GROUP]
        sg = jnp.zeros_like(vs[0])
        for i in range(EXPERTS_PER_GROUP):
            beaten = jnp.zeros_like(vs[0])
            for j in range(EXPERTS_PER_GROUP):
                if j != i:
                    b = (vs[j] >= vs[i]) if j < i else (vs[j] > vs[i])
                    beaten = beaten + jnp.where(b, 1.0, 0.0)
            kp = beaten < 2.0
            keep[g * EXPERTS_PER_GROUP + i] = kp
            sg = sg + jnp.where(kp, vs[i], 0.0)
        score.append(sg)
    chosen = []
    for g in range(N_GROUPS):
        lost = jnp.zeros_like(score[0])
        for g2 in range(N_GROUPS):
            if g2 != g:
                b = (score[g2] >= score[g]) if g2 < g else (score[g2] > score[g])
                lost = lost + jnp.where(b, 1.0, 0.0)
        chosen.append(jnp.where(lost < 0.5, 1.0, 0.0))
    wsel = []
    for i in range(EXPERTS_PER_GROUP):
        wi = jnp.zeros_like(score[0])
        for g in range(N_GROUPS):
            e = g * EXPERTS_PER_GROUP + i
            wi = wi + chosen[g] * jnp.where(keep[e], affr[e], 0.0)
        wsel.append(wi)
    wsum = wsel[0] + wsel[1] + wsel[2] + wsel[3]
    wsel = [w / wsum for w in wsel]

    row8 = lax.broadcasted_iota(jnp.int32, (SUBLANES, te), 0)
    gmat = jnp.zeros((SUBLANES, te), F32)
    wmat = jnp.zeros((SUBLANES, te), F32)
    grp = jnp.zeros_like(score[0])
    for g in range(N_GROUPS):
        gmat = jnp.where(row8 == g, chosen[g], gmat)
        wmat = jnp.where(row8 == g, wsel[g], wmat)
        grp = grp + g * chosen[g]
    pref = jnp.dot(gmat.astype(BF16), tri_ref[...], preferred_element_type=F32) + cnt_sc[:, :1]
    rank = jnp.sum(gmat * pref, axis=0, keepdims=True)
    tot = cnt_sc[:, :1] + jnp.sum(gmat, axis=1, keepdims=True)
    cnt_sc[...] = jnp.broadcast_to(tot, cnt_sc.shape)
    rank_ref[0] = rank.astype(jnp.int32)
    grp_ref[0] = grp.astype(jnp.int32)
    tot_ref[0] = jnp.broadcast_to(tot, (SUBLANES, LANES)).astype(jnp.int32)
    wfull = jnp.concatenate([wmat, jnp.zeros((LANES - SUBLANES, te), F32)], axis=0)
    gsel_ref[...] = wfull.T


def _post_call(x, hf, hm, wo, g1, nf, sc, sh, wr, br, tri, *, layer, bsz, seq, chunk, te=512):
    t, d = x.shape
    steps_per_chunk = chunk // te
    per_seq = seq // te
    n_steps = t // te
    n_chunks = t // chunk
    row = lambda i: (i, 0)
    per_b = lambda i: (i // per_seq, 0, 0)
    const2 = lambda i: (0, 0)
    return pl.pallas_call(
        functools.partial(_post_kernel, te=te, steps_per_chunk=steps_per_chunk),
        grid=(n_steps,),
        in_specs=[
            pl.BlockSpec((te, d), row),
            pl.BlockSpec((te, FOX_WIDTH), row),
            pl.BlockSpec((te, ML_WIDTH), row),
            pl.BlockSpec((1, d, d), lambda i: (layer, 0, 0)),
            pl.BlockSpec((1, 1, d), per_b),
            pl.BlockSpec((1, d), const2),
            pl.BlockSpec((1, 1, d), per_b),
            pl.BlockSpec((1, 1, d), per_b),
            pl.BlockSpec((d, 2 * LANES), const2),
            pl.BlockSpec((N_EXPERTS, 1), const2),
            pl.BlockSpec((te, te), const2),
        ],
        out_specs=[
            pl.BlockSpec((te, d), row),
            pl.BlockSpec((te * TOK_ROWS, LANES), row),
            pl.BlockSpec((te, LANES), row),
            pl.BlockSpec((1, 1, te), lambda i: (i, 0, 0)),
            pl.BlockSpec((1, 1, te), lambda i: (i, 0, 0)),
            pl.BlockSpec((1, SUBLANES, LANES), lambda i: (i // steps_per_chunk, 0, 0)),
        ],
        out_shape=[
            jax.ShapeDtypeStruct((t, d), F32),
            jax.ShapeDtypeStruct((t * TOK_ROWS, LANES), F32),
            jax.ShapeDtypeStruct((t, LANES), F32),
            jax.ShapeDtypeStruct((n_steps, 1, te), jnp.int32),
            jax.ShapeDtypeStruct((n_steps, 1, te), jnp.int32),
            jax.ShapeDtypeStruct((n_chunks, SUBLANES, LANES), jnp.int32),
        ],
        scratch_shapes=[pltpu.VMEM((SUBLANES, LANES), F32)],
        compiler_params=_cparams(1, 48),
        name="post_router",
    )(x, hf, hm, wo, g1, nf, sc, sh, wr, br, tri)


def _scatter_kernel(pos_ref, hn_ref, gsel_ref, xb_ref, gs_ref, xs_sc, *, chunk, rows, tm):
    xs_sc[...] = jnp.zeros_like(xs_sc)
    gs_ref[...] = jnp.zeros_like(gs_ref)

    def body(t, carry):
        p = pos_ref[0, 0, t]
        src = pl.multiple_of(t * TOK_ROWS, TOK_ROWS)
        dst = pl.multiple_of(p * TOK_ROWS, TOK_ROWS)
        xs_sc[pl.ds(dst, TOK_ROWS), :] = hn_ref[pl.ds(src, TOK_ROWS), :]
        gs_ref[pl.ds(p, 1), :] = gsel_ref[pl.ds(t, 1), :]
        return carry

    lax.fori_loop(0, chunk, body, 0, unroll=8)
    for j in range(rows // tm):
        for c in range(TOK_ROWS):
            xb_ref[j * tm:(j + 1) * tm, c * LANES:(c + 1) * LANES] = (
                xs_sc[pl.ds(j * tm * TOK_ROWS + c, tm, stride=TOK_ROWS), :].astype(BF16))


def _scatter_call(pos, hn_tt, gsel, *, chunk, rows, tm):
    t = gsel.shape[0]
    n_chunks = t // chunk
    return pl.pallas_call(
        functools.partial(_scatter_kernel, chunk=chunk, rows=rows, tm=tm),
        grid=(n_chunks,),
        in_specs=[
            pl.BlockSpec((1, 1, chunk), lambda c: (c, 0, 0), memory_space=pltpu.SMEM),
            pl.BlockSpec((chunk * TOK_ROWS, LANES), lambda c: (c, 0)),
            pl.BlockSpec((chunk, LANES), lambda c: (c, 0)),
        ],
        out_specs=[
            pl.BlockSpec((rows, D_MODEL), lambda c: (c, 0)),
            pl.BlockSpec((rows, LANES), lambda c: (c, 0)),
        ],
        out_shape=[
            jax.ShapeDtypeStruct((n_chunks * rows, D_MODEL), BF16),
            jax.ShapeDtypeStruct((n_chunks * rows, LANES), F32),
        ],
        scratch_shapes=[pltpu.VMEM((rows * TOK_ROWS, LANES), F32)],
        compiler_params=_cparams(1, 56),
        name="moe_scatter",
    )(pos, hn_tt, gsel)


def _experts_kernel(blk_ref, grp_ref, xb_ref, gs_ref, wg_ref, wu_ref, wd_ref, y_ref, *, tm):
    g = grp_ref[pl.program_id(0)]

    @pl.when(g < N_GROUPS)
    def _():
        x = xb_ref[...]
        gs = gs_ref[...]
        acts = []
        for i in range(EXPERTS_PER_GROUP):
            hg = jnp.dot(x, wg_ref[i], preferred_element_type=F32)
            hu = jnp.dot(x, wu_ref[i], preferred_element_type=F32)
            acts.append((_silu(hg) * hu * gs[:, i:i + 1]).astype(BF16))
        y = jnp.dot(jnp.concatenate(acts, axis=1), wd_ref[0], preferred_element_type=F32)
        _to_token_tiles(y_ref, y)

    @pl.when(g >= N_GROUPS)
    def _():
        y_ref[...] = jnp.zeros_like(y_ref)


def _experts_call(tile_blk, tile_grp, xb, gs, wg, wu, wd, *, layer, tm):
    n_rows = xb.shape[0]
    n_slots = n_rows // tm
    wmap = lambda s, blk, grp: (layer * N_GROUPS + jnp.minimum(grp[s], N_GROUPS - 1), 0, 0)
    grid_spec = pltpu.PrefetchScalarGridSpec(
        num_scalar_prefetch=2,
        grid=(n_slots,),
        in_specs=[
            pl.BlockSpec((tm, D_MODEL), lambda s, blk, grp: (blk[s], 0)),
            pl.BlockSpec((tm, LANES), lambda s, blk, grp: (blk[s], 0)),
            pl.BlockSpec((EXPERTS_PER_GROUP, D_MODEL, D_FF), wmap),
            pl.BlockSpec((EXPERTS_PER_GROUP, D_MODEL, D_FF), wmap),
            pl.BlockSpec((1, EXPERTS_PER_GROUP * D_FF, D_MODEL), wmap),
        ],
        out_specs=pl.BlockSpec((tm * TOK_ROWS, LANES), lambda s, blk, grp: (blk[s], 0)),
    )
    return pl.pallas_call(
        functools.partial(_experts_kernel, tm=tm),
        grid_spec=grid_spec,
        out_shape=jax.ShapeDtypeStruct((n_rows * TOK_ROWS, LANES), F32),
        compiler_params=_cparams(1, 48),
        name="moe_experts",
    )(tile_blk, tile_grp, xb, gs, wg, wu, wd)


def _gather_kernel(pos_ref, ys_ref, o_ref, *, chunk):
    def body(t, carry):
        p = pos_ref[0, 0, t]
        src = pl.multiple_of(p * TOK_ROWS, TOK_ROWS)
        dst = pl.multiple_of(t * TOK_ROWS, TOK_ROWS)
        o_ref[pl.ds(dst, TOK_ROWS), :] = ys_ref[pl.ds(src, TOK_ROWS), :]
        return carry

    lax.fori_loop(0, chunk, body, 0, unroll=8)


def _gather_call(pos, ys_tt, *, chunk, rows, n_tok):
    n_chunks = n_tok // chunk
    return pl.pallas_call(
        functools.partial(_gather_kernel, chunk=chunk),
        grid=(n_chunks,),
        in_specs=[
            pl.BlockSpec((1, 1, chunk), lambda c: (c, 0, 0), memory_space=pltpu.SMEM),
            pl.BlockSpec((rows * TOK_ROWS, LANES), lambda c: (c, 0)),
        ],
        out_specs=pl.BlockSpec((chunk * TOK_ROWS, LANES), lambda c: (c, 0)),
        out_shape=jax.ShapeDtypeStruct((n_tok * TOK_ROWS, LANES), F32),
        compiler_params=_cparams(1, 56),
        name="moe_gather",
    )(pos, ys_tt)


def _gather_final_kernel(pos_ref, ys_ref, x_ref, g2_ref, nf_ref, o_ref, o_sc, *, part, sub):
    base = pl.program_id(1) * part

    def body(t, carry):
        p = pos_ref[0, 0, base + t]
        src = pl.multiple_of(p * TOK_ROWS, TOK_ROWS)
        dst = pl.multiple_of(t * TOK_ROWS, TOK_ROWS)
        o_sc[pl.ds(dst, TOK_ROWS), :] = ys_ref[pl.ds(src, TOK_ROWS), :]
        return carry

    lax.fori_loop(0, part, body, 0, unroll=8)
    for r in range(part // sub):
        rs = slice(r * sub, (r + 1) * sub)
        moe = jnp.concatenate(
            [o_sc[pl.ds(r * sub * TOK_ROWS + c, sub, stride=TOK_ROWS), :] for c in range(TOK_ROWS)], axis=1)
        x = x_ref[rs, :] + g2_ref[0] * moe
        ms = jnp.mean(x * x, axis=1, keepdims=True)
        o_ref[rs, :] = x * lax.rsqrt(ms + EPS) * nf_ref[...]


def _gather_final_call(pos, ys_tt, x, g2, nf, *, chunk, rows, seq):
    n_tok, d = x.shape
    n_chunks = n_tok // chunk
    parts = 2
    part = chunk // parts
    return pl.pallas_call(
        functools.partial(_gather_final_kernel, part=part, sub=256),
        grid=(n_chunks, parts),
        in_specs=[
            pl.BlockSpec((1, 1, chunk), lambda c, s: (c, 0, 0), memory_space=pltpu.SMEM),
            pl.BlockSpec((rows * TOK_ROWS, LANES), lambda c, s: (c, 0)),
            pl.BlockSpec((part, d), lambda c, s: (c * parts + s, 0)),
            pl.BlockSpec((1, 1, d), lambda c, s: ((c * chunk + s * part) // seq, 0, 0)),
            pl.BlockSpec((1, d), lambda c, s: (0, 0)),
        ],
        out_specs=pl.BlockSpec((part, d), lambda c, s: (c * parts + s, 0)),
        out_shape=jax.ShapeDtypeStruct((n_tok, d), F32),
        scratch_shapes=[pltpu.VMEM((part * TOK_ROWS, LANES), F32)],
        compiler_params=_cparams(2, 56),
        name="moe_gather_final",
    )(pos, ys_tt, x, g2, nf)


def _moe_tiles(tot, *, tm, tiles_per_chunk):
    nt = (tot + tm - 1) // tm
    ts = jnp.cumsum(nt, axis=1) - nt
    off = (ts * tm).astype(jnp.int32)
    j = jnp.arange(tiles_per_chunk, dtype=jnp.int32)[None, :, None]
    inside = (j >= ts[:, None, :]) & (j < (ts + nt)[:, None, :])
    key = jnp.where(jnp.any(inside, axis=-1), jnp.argmax(inside, axis=-1), N_GROUPS).reshape(-1)
    order = jnp.argsort(key, stable=True).astype(jnp.int32)
    return off, order, key[order].astype(jnp.int32)


def kernel(x, c, w_in, conv_w, conv_b, fox_f_bias, mlstm_i_bias, mlstm_f_bias, fox_out_norm,
           mlstm_out_norm, w_out, w_ada, b_ada, norm_mix, norm_ffn, w_router, b_router, w_gate,
           w_up, w_down, norm_final):
    bsz, seq, d = x.shape
    depth = w_in.shape[0]
    t = bsz * seq
    assert d == D_MODEL and w_in.shape[-1] == IN_COLS
    chunk = min(2048, seq)
    tm = 256
    assert chunk % tm == 0
    rows = chunk + (N_GROUPS - 1) * tm
    te = min(1024, seq)
    assert seq % 512 == 0 and seq % chunk == 0 and chunk % te == 0

    mods = _ada_call(c, w_ada, b_ada).reshape(depth, bsz, N_ADA, 1, d)
    wm_all = jnp.concatenate([w_in[:, :, FOX_Q:FOX_F], w_in[:, :, ML_Q:ML_I], w_in[:, :, ML_O:IN_COLS]],
                             axis=2).astype(BF16)
    zpad = lambda n: jnp.zeros((depth, d, n), F32)
    wg_all = jnp.concatenate([
        w_in[:, :, FOX_F:ML_Q], w_in[:, :, ML_F:ML_O], zpad(LANES - FOX_HEADS - ML_HEADS),
        zpad(GATE_ML), w_in[:, :, ML_I:ML_F], zpad(LANES - GATE_ML - ML_HEADS)], axis=2).astype(BF16)
    wo_all = w_out.astype(BF16)
    wge_all = w_gate.astype(BF16).reshape(depth * N_EXPERTS, d, D_FF)
    wue_all = w_up.astype(BF16).reshape(depth * N_EXPERTS, d, D_FF)
    wde_all = w_down.astype(BF16).reshape(depth * N_GROUPS, EXPERTS_PER_GROUP * D_FF, d)
    xf = x.reshape(t, d)

    tri = (jnp.arange(te)[:, None] < jnp.arange(te)[None, :]).astype(BF16)
    wr_f = jnp.pad(w_router, ((0, 0), (0, LANES - N_EXPERTS))).astype(F32)
    wr_hi = wr_f.astype(BF16)
    wr = jnp.concatenate([wr_hi, (wr_f - wr_hi.astype(F32)).astype(BF16)], axis=1)
    br = b_router.reshape(N_EXPERTS, 1).astype(F32)

    moe_tt = None
    g2_prev = None
    for l in range(depth):
        sh1, sc1, g1, sh2, sc2, g2 = [mods[l, :, i] for i in range(N_ADA)]
        zb = lambda n: jnp.zeros((n,), F32)
        gb = jnp.concatenate([
            fox_f_bias[l], mlstm_f_bias[l], zb(LANES - FOX_HEADS - ML_HEADS),
            zb(GATE_ML), mlstm_i_bias[l], zb(LANES - GATE_ML - ML_HEADS)]).reshape(1, 2 * LANES)

        xf, qat, ka, fvt, qkv, mkt, fc, gr = _in_proj_call(
            xf, moe_tt, g2_prev, norm_mix[l].reshape(1, d), sc1, sh1, wm_all, wg_all, gb,
            conv_w[l], conv_b[l].reshape(1, -1), layer=l, bsz=bsz, seq=seq)
        hf = _fox_call(qat, ka, fvt, fox_out_norm[l].reshape(1, FOX_WIDTH), bsz=bsz, seq=seq)
        hm = _mlstm_call(qkv, mkt, fc, gr, mlstm_out_norm[l].reshape(1, ML_WIDTH), bsz=bsz, seq=seq)
        xf, hn_tt, gsel, grp, rank, tot = _post_call(
            xf, hf, hm, wo_all, g1, norm_ffn[l].reshape(1, d), sc2, sh2, wr, br, tri,
            layer=l, bsz=bsz, seq=seq, chunk=chunk, te=te)

        n_chunks = t // chunk
        grp = grp.reshape(n_chunks, 1, chunk)
        rank = rank.reshape(n_chunks, 1, chunk)
        off, tile_blk, tile_grp = _moe_tiles(tot[:, :N_GROUPS, 0], tm=tm, tiles_per_chunk=rows // tm)
        pos = rank
        for g in range(N_GROUPS):
            pos = pos + jnp.where(grp == g, off[:, g].reshape(n_chunks, 1, 1), 0)
        xb, gs = _scatter_call(pos, hn_tt, gsel, chunk=chunk, rows=rows, tm=tm)
        ys_tt = _experts_call(tile_blk, tile_grp, xb, gs, wge_all, wue_all, wde_all, layer=l, tm=tm)
        if l < depth - 1:
            moe_tt = _gather_call(pos, ys_tt, chunk=chunk, rows=rows, n_tok=t)
            g2_prev = g2
        else:
            out = _gather_final_call(pos, ys_tt, xf, g2, norm_final.reshape(1, d),
                                     chunk=chunk, rows=rows, seq=seq)
    return out.reshape(bsz, seq, d)
```

```python
import functools

import numpy as np
import jax
import jax.numpy as jnp
from jax import lax
from jax.experimental import pallas as pl
from jax.experimental.pallas import tpu as pltpu

F32 = jnp.float32
BF16 = jnp.bfloat16

LANES = 128
SUBLANES = 8
VMEM_BYTES_V7X = 64 * 1024 * 1024

D_MODEL = 1024
FOX_HEADS = 8
FOX_HEAD_DIM = 64
FOX_WIDTH = FOX_HEADS * FOX_HEAD_DIM
ML_HEADS = 4
ML_HEAD_DIM = 128
ML_WIDTH = ML_HEADS * ML_HEAD_DIM
CONV_WIDTH = 4
N_EXPERTS = 16
N_GROUPS = 4
EXPERTS_PER_GROUP = 4
D_FF = 512
N_ADA = 6
EPS = 1e-6
NEG = -1e30

FOX_Q = 0
FOX_F = 3 * FOX_WIDTH
ML_Q = FOX_F + FOX_HEADS
ML_I = ML_Q + 3 * ML_WIDTH
ML_F = ML_I + ML_HEADS
ML_O = ML_F + ML_HEADS
IN_COLS = ML_O + ML_WIDTH

MAIN_COLS = 7 * 512
BLK_FQ, BLK_FK, BLK_FV, BLK_MQ, BLK_MK, BLK_MV, BLK_MO = range(7)
REST_COLS = 3 * 512
OUT_MQ, OUT_MV, OUT_MO = range(3)
AUG_COLS = FOX_HEADS * 128
BIAS_TERMS = 3
VT_PAD = 16
LOG2E = 1.4426950408889634
GATE_FOX = 0
GATE_ML = FOX_HEADS
GATE_ROWS = 16

TOK_ROWS = D_MODEL // LANES


def _cparams(n_grid, vmem_mb):
    return pltpu.CompilerParams(
        dimension_semantics=("arbitrary",) * n_grid,
        vmem_limit_bytes=vmem_mb * 1024 * 1024)


def _silu(x):
    return x * jax.nn.sigmoid(x)


def _log_sigmoid(z):
    return jnp.minimum(z, 0.0) - jnp.log1p(jnp.exp(-jnp.abs(z)))


def _cumsum_rows(x):
    n = x.shape[0]
    row = lax.broadcasted_iota(jnp.int32, x.shape, 0)
    s = 1
    while s < n:
        x = x + jnp.where(row >= s, pltpu.roll(x, s, axis=0), 0.0)
        s *= 2
    return x


def _from_token_tiles(ref, n_tok):
    return jnp.concatenate(
        [ref[pl.ds(c, n_tok, stride=TOK_ROWS), :] for c in range(TOK_ROWS)], axis=1)


def _to_token_tiles(ref, val, row0=0):
    n = val.shape[0]
    for c in range(TOK_ROWS):
        ref[pl.ds(row0 * TOK_ROWS + c, n, stride=TOK_ROWS), :] = val[:, c * LANES:(c + 1) * LANES]


def _ada_kernel(c_ref, w_ref, b_ref, o_ref):
    c = c_ref[...]
    o_ref[0] = jnp.dot(_silu(c), w_ref[0], preferred_element_type=F32,
                       precision=lax.Precision.HIGHEST) + b_ref[0]


def _ada_call(c, w_ada, b_ada):
    depth, d, n = w_ada.shape
    bsz = c.shape[0]
    tn = 1536
    return pl.pallas_call(
        _ada_kernel,
        grid=(depth, n // tn),
        in_specs=[
            pl.BlockSpec((bsz, d), lambda l, j: (0, 0)),
            pl.BlockSpec((1, d, tn), lambda l, j: (l, 0, j)),
            pl.BlockSpec((1, 1, tn), lambda l, j: (l, 0, j)),
        ],
        out_specs=pl.BlockSpec((1, bsz, tn), lambda l, j: (l, 0, j)),
        out_shape=jax.ShapeDtypeStruct((depth, bsz, n), F32),
        compiler_params=_cparams(2, 32),
        name="ada_mod",
    )(c, w_ada, b_ada.reshape(depth, 1, n))


def _in_proj_kernel(*refs, tm, combine):
    if combine:
        (x_ref, ott_ref, g2_ref, nm_ref, sc_ref, sh_ref, wm_ref, wg_ref, gb_ref, cw_ref, cb_ref,
         place_ref, ones_ref,
         xn_ref, qa_ref, ka_ref, vt_ref, qkv_ref, kt_ref, fc_ref, gr_ref, fcar, ccar) = refs
    else:
        (x_ref, nm_ref, sc_ref, sh_ref, wm_ref, wg_ref, gb_ref, cw_ref, cb_ref,
         place_ref, ones_ref,
         qa_ref, ka_ref, vt_ref, qkv_ref, kt_ref, fc_ref, gr_ref, fcar, ccar) = refs

    @pl.when(pl.program_id(1) == 0)
    def _():
        fcar[...] = jnp.zeros_like(fcar)
        ccar[...] = jnp.zeros_like(ccar)

    x = x_ref[...]
    if combine:
        x = x + g2_ref[0] * _from_token_tiles(ott_ref, tm)
        xn_ref[...] = x
    ms = jnp.mean(x * x, axis=1, keepdims=True)
    hn = x * lax.rsqrt(ms + EPS) * nm_ref[...]
    hn = hn * (1.0 + sc_ref[0]) + sh_ref[0]
    hb = hn.astype(BF16)

    gp = jnp.dot(hb, wg_ref[0], preferred_element_type=F32) + gb_ref[...]
    lf = _log_sigmoid(gp[:, :LANES])
    fcum = _cumsum_rows(lf) + fcar[...]
    fcar[...] = fcum[tm - 1:tm, :]
    gml = gp[:, LANES:] - fcum
    fc_ref[...] = fcum
    gr_ref[0] = gml.T[:GATE_ROWS, :]

    def mm(j):
        return jnp.dot(hb, wm_ref[0, :, j * 512:(j + 1) * 512], preferred_element_type=F32)

    def put(j, v):
        qkv_ref[:, j * 512:(j + 1) * 512] = v.astype(BF16)

    lane = lax.broadcasted_iota(jnp.int32, (tm, LANES), 1)
    fs = fcum * LOG2E
    hi = fs.astype(BF16).astype(F32)
    mid = (fs - hi).astype(BF16).astype(F32)
    low = (fs - hi - mid).astype(BF16).astype(F32)
    packed = jnp.where(lane < FOX_HEADS, hi,
                       jnp.where(lane < 2 * FOX_HEADS, pltpu.roll(mid, FOX_HEADS, axis=1),
                                 jnp.where(lane < 3 * FOX_HEADS, pltpu.roll(low, 2 * FOX_HEADS, axis=1), 0.0)))
    bias = jnp.dot(packed.astype(BF16), place_ref[...], preferred_element_type=F32) + ones_ref[...]

    def put_heads(ref, val, col0, transposed):
        lo = lane < FOX_HEAD_DIM
        for p in range(FOX_HEADS // 2):
            slab = val[:, p * LANES:(p + 1) * LANES]
            for h, data in ((2 * p, slab), (2 * p + 1, pltpu.roll(slab, FOX_HEAD_DIM, axis=1))):
                blk = jnp.where(lo, data, 0.0) + bias[:, col0 + h * LANES:col0 + (h + 1) * LANES]
                if transposed:
                    ref[0, 0, h * LANES:(h + 1) * LANES, :] = blk.T.astype(BF16)
                else:
                    ref[:, h * LANES:(h + 1) * LANES] = blk.astype(BF16)

    u = jnp.concatenate([mm(BLK_MQ), mm(BLK_MK)], axis=1)
    prev = ccar[...]
    ccar[...] = u[tm - SUBLANES:tm, :]
    row8 = lax.broadcasted_iota(jnp.int32, prev.shape, 0)
    y = cb_ref[...] + cw_ref[CONV_WIDTH - 1:CONV_WIDTH, :] * u
    for k in range(1, CONV_WIDTH):
        r = pltpu.roll(u, k, axis=0)
        top = jnp.where(row8 < k, pltpu.roll(prev, k, axis=0), r[:SUBLANES])
        shifted = jnp.concatenate([top, r[SUBLANES:]], axis=0)
        y = y + cw_ref[CONV_WIDTH - 1 - k:CONV_WIDTH - k, :] * shifted
    act = _silu(y)
    put(OUT_MQ, act[:, :ML_WIDTH])
    kt_ref[0, 0] = (act[:, ML_WIDTH:] * (ML_HEAD_DIM ** -0.5)).T.astype(BF16)

    put(OUT_MO, jax.nn.sigmoid(mm(BLK_MO)))
    put(OUT_MV, mm(BLK_MV))
    vt_ref[0, 0] = mm(BLK_FV).astype(BF16).T
    put_heads(qa_ref, mm(BLK_FQ) * (FOX_HEAD_DIM ** -0.5 * LOG2E), 0, True)
    put_heads(ka_ref, mm(BLK_FK), AUG_COLS, False)


def _bias_placement():
    place = np.zeros((LANES, 2 * AUG_COLS), np.float32)
    ones = np.zeros((1, 2 * AUG_COLS), np.float32)
    for h in range(FOX_HEADS):
        for term in range(BIAS_TERMS):
            src = term * FOX_HEADS + h
            place[src, h * LANES + FOX_HEAD_DIM + term] = 1.0
            ones[0, h * LANES + FOX_HEAD_DIM + BIAS_TERMS + term] = 1.0
            place[src, AUG_COLS + h * LANES + FOX_HEAD_DIM + BIAS_TERMS + term] = -1.0
            ones[0, AUG_COLS + h * LANES + FOX_HEAD_DIM + term] = 1.0
    return jnp.asarray(place, BF16), jnp.asarray(ones, F32)


def _in_proj_call(x, moe_tt, g2, nm, sc, sh, wm, wg, gb, cw, cb, *, layer, bsz, seq, tm=512):
    t, d = x.shape
    ns = seq // tm
    combine = moe_tt is not None
    row = lambda b, s: (b * ns + s, 0)
    per_b = lambda b, s: (b, 0, 0)
    const2 = lambda b, s: (0, 0)
    in_specs = [pl.BlockSpec((tm, d), row)]
    args = [x]
    if combine:
        in_specs += [pl.BlockSpec((tm * TOK_ROWS, LANES), row), pl.BlockSpec((1, 1, d), per_b)]
        args += [moe_tt, g2]
    in_specs += [
        pl.BlockSpec((1, d), const2),
        pl.BlockSpec((1, 1, d), per_b),
        pl.BlockSpec((1, 1, d), per_b),
        pl.BlockSpec((1, d, MAIN_COLS), lambda b, s: (layer, 0, 0)),
        pl.BlockSpec((1, d, 2 * LANES), lambda b, s: (layer, 0, 0)),
        pl.BlockSpec((1, 2 * LANES), const2),
        pl.BlockSpec((CONV_WIDTH, 2 * ML_WIDTH), const2),
        pl.BlockSpec((1, 2 * ML_WIDTH), const2),
        pl.BlockSpec((LANES, 2 * AUG_COLS), const2),
        pl.BlockSpec((1, 2 * AUG_COLS), const2),
    ]
    place, ones = _bias_placement()
    args += [nm, sc, sh, wm, wg, gb, cw, cb, place, ones]
    out_specs = []
    out_shape = []
    if combine:
        out_specs.append(pl.BlockSpec((tm, d), row))
        out_shape.append(jax.ShapeDtypeStruct((t, d), F32))
    tiled = lambda b, s: (b, s, 0, 0)
    out_specs += [
        pl.BlockSpec((1, 1, AUG_COLS, tm), tiled),
        pl.BlockSpec((tm, AUG_COLS), row),
        pl.BlockSpec((1, 1, FOX_WIDTH, tm), tiled),
        pl.BlockSpec((tm, REST_COLS), row),
        pl.BlockSpec((1, 1, ML_WIDTH, tm), tiled),
        pl.BlockSpec((tm, LANES), row),
        pl.BlockSpec((1, GATE_ROWS, tm), lambda b, s: (b, 0, s)),
    ]
    out_shape += [
        jax.ShapeDtypeStruct((bsz, ns, AUG_COLS, tm), BF16),
        jax.ShapeDtypeStruct((t, AUG_COLS), BF16),
        jax.ShapeDtypeStruct((bsz, ns, FOX_WIDTH, tm), BF16),
        jax.ShapeDtypeStruct((t, REST_COLS), BF16),
        jax.ShapeDtypeStruct((bsz, ns, ML_WIDTH, tm), BF16),
        jax.ShapeDtypeStruct((t, LANES), F32),
        jax.ShapeDtypeStruct((bsz, GATE_ROWS, seq), F32),
    ]
    outs = pl.pallas_call(
        functools.partial(_in_proj_kernel, tm=tm, combine=combine),
        grid=(bsz, ns),
        in_specs=in_specs,
        out_specs=out_specs,
        out_shape=out_shape,
        scratch_shapes=[pltpu.VMEM((1, LANES), F32), pltpu.VMEM((SUBLANES, 2 * ML_WIDTH), F32)],
        compiler_params=_cparams(2, 48),
        name="in_proj",
    )(*args)
    if combine:
        return outs
    return [x] + list(outs)


def _fox_kernel(qt_ref, k_ref, vt_ref, ng_ref, cm_ref, o_ref, m_sc, acc_sc, *, tq, tk, nh):
    qi = pl.program_id(2)
    ones_rows = jnp.where(lax.broadcasted_iota(jnp.int32, (VT_PAD, tk), 0) == 0, 1.0, 0.0).astype(BF16)
    for h in range(nh):
        m_sc[h] = jnp.full((SUBLANES, tq), NEG, F32)
        acc_sc[h] = jnp.zeros((FOX_HEAD_DIM + VT_PAD, tq), F32)

    def body(j, carry, diagonal):
        k0 = pl.multiple_of(j * tk, tk)

        def logits(h):
            hs = slice(h * LANES, (h + 1) * LANES)
            z = jnp.dot(k_ref[pl.ds(k0, tk), hs], qt_ref[0, 0, hs, :], preferred_element_type=F32)
            return z + cm_ref[...] if diagonal else z

        ahead = 2
        zs = [logits(h) for h in range(min(ahead, nh))]
        for h in range(nh):
            ds_ = slice(h * FOX_HEAD_DIM, (h + 1) * FOX_HEAD_DIM)
            z = zs[h]
            if h + ahead < nh:
                zs.append(logits(h + ahead))
            m_prev = m_sc[h]
            m_new = jnp.maximum(m_prev, jnp.max(z, axis=0, keepdims=True))
            alpha = jnp.exp2(m_prev - m_new)
            p = jnp.exp2(z - m_new[:1, :])
            va = jnp.concatenate([vt_ref[0, j, ds_, :], ones_rows], axis=0)
            acc_sc[h] = alpha[:1, :] * acc_sc[h] + jnp.dot(va, p.astype(BF16), preferred_element_type=F32)
            m_sc[h] = m_new
        return carry

    lax.fori_loop(0, qi, functools.partial(body, diagonal=False), 0)
    lax.fori_loop(qi, qi + 1, functools.partial(body, diagonal=True), 0)

    for p in range(nh // 2):
        outs = []
        for h in (2 * p, 2 * p + 1):
            acc = acc_sc[h]
            num = acc[:FOX_HEAD_DIM, :]
            l = acc[FOX_HEAD_DIM:FOX_HEAD_DIM + 1, :]
            ms = jnp.mean(num * num, axis=0, keepdims=True)
            outs.append(num * lax.rsqrt(ms + EPS * l * l))
        ps = slice(p * LANES, (p + 1) * LANES)
        o_ref[:, ps] = (jnp.concatenate(outs, axis=0).T * ng_ref[:, ps]).astype(BF16)


def _fox_call(qat, ka, fvt, ng, *, bsz, seq, tq=512, nh=8):
    t = ka.shape[0]
    nq = seq // tq
    ngrp = FOX_HEADS // nh
    vw = nh * FOX_HEAD_DIM
    cmask = jnp.where(jnp.arange(tq)[:, None] <= jnp.arange(tq)[None, :], 0.0, NEG).astype(F32)
    return pl.pallas_call(
        functools.partial(_fox_kernel, tq=tq, tk=tq, nh=nh),
        grid=(bsz, ngrp, nq),
        in_specs=[
            pl.BlockSpec((1, 1, nh * LANES, tq), lambda b, p, i: (b, i, p, 0)),
            pl.BlockSpec((seq, nh * LANES), lambda b, p, i: (b, p)),
            pl.BlockSpec((1, nq, vw, tq), lambda b, p, i: (b, 0, p, 0)),
            pl.BlockSpec((1, vw), lambda b, p, i: (0, p)),
            pl.BlockSpec((tq, tq), lambda b, p, i: (0, 0)),
        ],
        out_specs=pl.BlockSpec((tq, vw), lambda b, p, i: (b * nq + i, p)),
        out_shape=jax.ShapeDtypeStruct((t, FOX_WIDTH), BF16),
        scratch_shapes=[pltpu.VMEM((nh, SUBLANES, tq), F32),
                        pltpu.VMEM((nh, FOX_HEAD_DIM + VT_PAD, tq), F32)],
        compiler_params=_cparams(3, 48),
        name="fox_attn",
    )(qat, ka, fvt, ng, cmask)


def _mlstm_kernel(q_ref, kt_ref, v_ref, og_ref, fc_ref, gr_ref, ng_ref, o_ref, ct_sc, u_sc, *, ch, per_tile):
    @pl.when(pl.program_id(1) == 0)
    def _():
        ct_sc[...] = jnp.zeros_like(ct_sc)
        u_sc[...] = jnp.zeros_like(u_sc)

    causal = (lax.broadcasted_iota(jnp.int32, (ch, ch), 1)
              <= lax.broadcasted_iota(jnp.int32, (ch, ch), 0))
    lane = lax.broadcasted_iota(jnp.int32, (ch, LANES), 1)
    fc = fc_ref[...]
    koff = pl.multiple_of((pl.program_id(1) % per_tile) * ch, ch)
    for h in range(ML_HEADS):
        sl = slice(h * ML_HEAD_DIM, (h + 1) * ML_HEAD_DIM)
        gl = GATE_ML + h
        q = q_ref[:, sl]
        kt = kt_ref[0, 0, sl, pl.ds(koff, ch)]
        vp = jnp.concatenate([v_ref[:, sl], jnp.where(lane == gl, 1.0, 0.0).astype(BF16)], axis=1)
        g_row = gr_ref[0, gl:gl + 1, :]
        u_prev = u_sc[h][:, :1]
        gm = jnp.where(causal, g_row, NEG)
        u_i = jnp.maximum(u_prev, jnp.max(gm, axis=1, keepdims=True))
        dmat = jnp.exp(gm - u_i)
        s = jnp.dot(q, kt, preferred_element_type=F32)
        scores = (s * dmat).astype(BF16)
        inter = jnp.exp(u_prev - u_i)
        ct = ct_sc[h]
        nd = (jnp.dot(scores, vp, preferred_element_type=F32)
              + jnp.dot(q, ct.astype(BF16), preferred_element_type=F32) * inter)
        num = nd[:, :ML_HEAD_DIM]
        den = jnp.maximum(jnp.abs(nd[:, ML_HEAD_DIM:]), jnp.exp(-(jnp.where(lane == gl, fc, 0.0) + u_i)))
        ms = jnp.mean(num * num, axis=1, keepdims=True)
        scale = lax.rsqrt(ms + EPS * den * den)[:, gl:gl + 1]
        y = num * scale * ng_ref[:, sl] * og_ref[:, sl].astype(F32)
        o_ref[:, sl] = y.astype(BF16)
        u_new = jnp.maximum(u_prev, jnp.max(g_row, axis=1, keepdims=True))
        ktw = (kt.astype(F32) * jnp.exp(g_row - u_new)).astype(BF16)
        ct_sc[h] = jnp.exp(u_prev - u_new) * ct + jnp.dot(ktw, vp, preferred_element_type=F32)
        u_sc[h] = jnp.broadcast_to(u_new, (1, LANES))


def _mlstm_call(qkv, mkt, fc, gr, ng, *, bsz, seq, ch=512):
    t = qkv.shape[0]
    nc = seq // ch
    ktile = mkt.shape[-1]
    per_tile = ktile // ch
    row = lambda b, c: (b * nc + c, 0)
    return pl.pallas_call(
        functools.partial(_mlstm_kernel, ch=ch, per_tile=per_tile),
        grid=(bsz, nc),
        in_specs=[
            pl.BlockSpec((ch, ML_WIDTH), lambda b, c: (b * nc + c, OUT_MQ)),
            pl.BlockSpec((1, 1, ML_WIDTH, ktile), lambda b, c: (b, c // per_tile, 0, 0)),
            pl.BlockSpec((ch, ML_WIDTH), lambda b, c: (b * nc + c, OUT_MV)),
            pl.BlockSpec((ch, ML_WIDTH), lambda b, c: (b * nc + c, OUT_MO)),
            pl.BlockSpec((ch, LANES), row),
            pl.BlockSpec((1, GATE_ROWS, ch), lambda b, c: (b, 0, c)),
            pl.BlockSpec((1, ML_WIDTH), lambda b, c: (0, 0)),
        ],
        out_specs=pl.BlockSpec((ch, ML_WIDTH), row),
        out_shape=jax.ShapeDtypeStruct((t, ML_WIDTH), BF16),
        scratch_shapes=[pltpu.VMEM((ML_HEADS, ML_HEAD_DIM, 2 * ML_HEAD_DIM), F32),
                        pltpu.VMEM((ML_HEADS, 1, LANES), F32)],
        compiler_params=_cparams(2, 32),
        name="mlstm",
    )(qkv, mkt, qkv, qkv, fc, gr, ng)


def _post_kernel(x_ref, hf_ref, hm_ref, wo_ref, g1_ref, nf_ref, sc_ref, sh_ref, wr_ref, br_ref, tri_ref,
                 x1_ref, hn_ref, gsel_ref, grp_ref, rank_ref, tot_ref, cnt_sc, *, te, steps_per_chunk):
    @pl.when(pl.program_id(0) % steps_per_chunk == 0)
    def _():
        cnt_sc[...] = jnp.zeros_like(cnt_sc)

    mix = (jnp.dot(hf_ref[...], wo_ref[0, :FOX_WIDTH, :], preferred_element_type=F32)
           + jnp.dot(hm_ref[...], wo_ref[0, FOX_WIDTH:, :], preferred_element_type=F32))
    x1 = x_ref[...] + g1_ref[0] * mix
    x1_ref[...] = x1
    ms = jnp.mean(x1 * x1, axis=1, keepdims=True)
    hn = x1 * lax.rsqrt(ms + EPS) * nf_ref[...]
    hn = hn * (1.0 + sc_ref[0]) + sh_ref[0]
    _to_token_tiles(hn_ref, hn)

    hn_hi = hn.astype(BF16)
    hn_lo = (hn - hn_hi.astype(F32)).astype(BF16)
    l2 = jnp.dot(hn_hi, wr_ref[...], preferred_element_type=F32)
    logits = (l2[:, :LANES] + l2[:, LANES:]
              + jnp.dot(hn_lo, wr_ref[:, :LANES], preferred_element_type=F32))
    aff = jax.nn.sigmoid(logits.T[:N_EXPERTS, :])
    sel = aff + br_ref[...]
    selr = [sel[e:e + 1, :] for e in range(N_EXPERTS)]
    affr = [aff[e:e + 1, :] for e in range(N_EXPERTS)]
    keep = [None] * N_EXPERTS
    score = []
    for g in range(N_GROUPS):
        vs = selr[g * EXPERTS_PER_GROUP:(g + 1) * EXPERTS_PER_GROUP]
        sg = jnp.zeros_like(vs[0])
        for i in range(EXPERTS_PER_GROUP):
            beaten = jnp.zeros_like(vs[0])
            for j in range(EXPERTS_PER_GROUP):
                if j != i:
                    b = (vs[j] >= vs[i]) if j < i else (vs[j] > vs[i])
                    beaten = beaten + jnp.where(b, 1.0, 0.0)
            kp = beaten < 2.0
            keep[g * EXPERTS_PER_GROUP + i] = kp
            sg = sg + jnp.where(kp, vs[i], 0.0)
        score.append(sg)
    chosen = []
    for g in range(N_GROUPS):
        lost = jnp.zeros_like(score[0])
        for g2 in range(N_GROUPS):
            if g2 != g:
                b = (score[g2] >= score[g]) if g2 < g else (score[g2] > score[g])
                lost = lost + jnp.where(b, 1.0, 0.0)
        chosen.append(jnp.where(lost < 0.5, 1.0, 0.0))
    wsel = []
    for i in range(EXPERTS_PER_GROUP):
        wi = jnp.zeros_like(score[0])
        for g in range(N_GROUPS):
            e = g * EXPERTS_PER_GROUP + i
            wi = wi + chosen[g] * jnp.where(keep[e], affr[e], 0.0)
        wsel.append(wi)
    wsum = wsel[0] + wsel[1] + wsel[2] + wsel[3]
    wsel = [w / wsum for w in wsel]

    row8 = lax.broadcasted_iota(jnp.int32, (SUBLANES, te), 0)
    gmat = jnp.zeros((SUBLANES, te), F32)
    wmat = jnp.zeros((SUBLANES, te), F32)
    grp = jnp.zeros_like(score[0])
    for g in range(N_GROUPS):
        gmat = jnp.where(row8 == g, chosen[g], gmat)
        wmat = jnp.where(row8 == g, wsel[g], wmat)
        grp = grp + g * chosen[g]
    pref = jnp.dot(gmat.astype(BF16), tri_ref[...], preferred_element_type=F32) + cnt_sc[:, :1]
    rank = jnp.sum(gmat * pref, axis=0, keepdims=True)
    tot = cnt_sc[:, :1] + jnp.sum(gmat, axis=1, keepdims=True)
    cnt_sc[...] = jnp.broadcast_to(tot, cnt_sc.shape)
    rank_ref[0] = rank.astype(jnp.int32)
    grp_ref[0] = grp.astype(jnp.int32)
    tot_ref[0] = jnp.broadcast_to(tot, (SUBLANES, LANES)).astype(jnp.int32)
    wfull = jnp.concatenate([wmat, jnp.zeros((LANES - SUBLANES, te), F32)], axis=0)
    gsel_ref[...] = wfull.T


def _post_call(x, hf, hm, wo, g1, nf, sc, sh, wr, br, tri, *, layer, bsz, seq, chunk, te=512):
    t, d = x.shape
    steps_per_chunk = chunk // te
    per_seq = seq // te
    n_steps = t // te
    n_chunks = t // chunk
    row = lambda i: (i, 0)
    per_b = lambda i: (i // per_seq, 0, 0)
    const2 = lambda i: (0, 0)
    return pl.pallas_call(
        functools.partial(_post_kernel, te=te, steps_per_chunk=steps_per_chunk),
        grid=(n_steps,),
        in_specs=[
            pl.BlockSpec((te, d), row),
            pl.BlockSpec((te, FOX_WIDTH), row),
            pl.BlockSpec((te, ML_WIDTH), row),
            pl.BlockSpec((1, d, d), lambda i: (layer, 0, 0)),
            pl.BlockSpec((1, 1, d), per_b),
            pl.BlockSpec((1, d), const2),
            pl.BlockSpec((1, 1, d), per_b),
            pl.BlockSpec((1, 1, d), per_b),
            pl.BlockSpec((d, 2 * LANES), const2),
            pl.BlockSpec((N_EXPERTS, 1), const2),
            pl.BlockSpec((te, te), const2),
        ],
        out_specs=[
            pl.BlockSpec((te, d), row),
            pl.BlockSpec((te * TOK_ROWS, LANES), row),
            pl.BlockSpec((te, LANES), row),
            pl.BlockSpec((1, 1, te), lambda i: (i, 0, 0)),
            pl.BlockSpec((1, 1, te), lambda i: (i, 0, 0)),
            pl.BlockSpec((1, SUBLANES, LANES), lambda i: (i // steps_per_chunk, 0, 0)),
        ],
        out_shape=[
            jax.ShapeDtypeStruct((t, d), F32),
            jax.ShapeDtypeStruct((t * TOK_ROWS, LANES), F32),
            jax.ShapeDtypeStruct((t, LANES), F32),
            jax.ShapeDtypeStruct((n_steps, 1, te), jnp.int32),
            jax.ShapeDtypeStruct((n_steps, 1, te), jnp.int32),
            jax.ShapeDtypeStruct((n_chunks, SUBLANES, LANES), jnp.int32),
        ],
        scratch_shapes=[pltpu.VMEM((SUBLANES, LANES), F32)],
        compiler_params=_cparams(1, 48),
        name="post_router",
    )(x, hf, hm, wo, g1, nf, sc, sh, wr, br, tri)


def _scatter_kernel(pos_ref, hn_ref, gsel_ref, xb_ref, gs_ref, xs_sc, *, chunk, rows, tm):
    xs_sc[...] = jnp.zeros_like(xs_sc)
    gs_ref[...] = jnp.zeros_like(gs_ref)

    def body(t, carry):
        p = pos_ref[0, 0, t]
        src = pl.multiple_of(t * TOK_ROWS, TOK_ROWS)
        dst = pl.multiple_of(p * TOK_ROWS, TOK_ROWS)
        xs_sc[pl.ds(dst, TOK_ROWS), :] = hn_ref[pl.ds(src, TOK_ROWS), :]
        gs_ref[pl.ds(p, 1), :] = gsel_ref[pl.ds(t, 1), :]
        return carry

    lax.fori_loop(0, chunk, body, 0, unroll=8)
    for j in range(rows // tm):
        for c in range(TOK_ROWS):
            xb_ref[j * tm:(j + 1) * tm, c * LANES:(c + 1) * LANES] = (
                xs_sc[pl.ds(j * tm * TOK_ROWS + c, tm, stride=TOK_ROWS), :].astype(BF16))


def _scatter_call(pos, hn_tt, gsel, *, chunk, rows, tm):
    t = gsel.shape[0]
    n_chunks = t // chunk
    return pl.pallas_call(
        functools.partial(_scatter_kernel, chunk=chunk, rows=rows, tm=tm),
        grid=(n_chunks,),
        in_specs=[
            pl.BlockSpec((1, 1, chunk), lambda c: (c, 0, 0), memory_space=pltpu.SMEM),
            pl.BlockSpec((chunk * TOK_ROWS, LANES), lambda c: (c, 0)),
            pl.BlockSpec((chunk, LANES), lambda c: (c, 0)),
        ],
        out_specs=[
            pl.BlockSpec((rows, D_MODEL), lambda c: (c, 0)),
            pl.BlockSpec((rows, LANES), lambda c: (c, 0)),
        ],
        out_shape=[
            jax.ShapeDtypeStruct((n_chunks * rows, D_MODEL), BF16),
            jax.ShapeDtypeStruct((n_chunks * rows, LANES), F32),
        ],
        scratch_shapes=[pltpu.VMEM((rows * TOK_ROWS, LANES), F32)],
        compiler_params=_cparams(1, 56),
        name="moe_scatter",
    )(pos, hn_tt, gsel)


def _experts_kernel(blk_ref, grp_ref, xb_ref, gs_ref, wg_ref, wu_ref, wd_ref, y_ref, *, tm):
    g = grp_ref[pl.program_id(0)]

    @pl.when(g < N_GROUPS)
    def _():
        x = xb_ref[...]
        gs = gs_ref[...]
        acts = []
        for i in range(EXPERTS_PER_GROUP):
            hg = jnp.dot(x, wg_ref[i], preferred_element_type=F32)
            hu = jnp.dot(x, wu_ref[i], preferred_element_type=F32)
            acts.append((_silu(hg) * hu * gs[:, i:i + 1]).astype(BF16))
        y = jnp.dot(jnp.concatenate(acts, axis=1), wd_ref[0], preferred_element_type=F32)
        _to_token_tiles(y_ref, y)

    @pl.when(g >= N_GROUPS)
    def _():
        y_ref[...] = jnp.zeros_like(y_ref)


def _experts_call(tile_blk, tile_grp, xb, gs, wg, wu, wd, *, layer, tm):
    n_rows = xb.shape[0]
    n_slots = n_rows // tm
    wmap = lambda s, blk, grp: (layer * N_GROUPS + jnp.minimum(grp[s], N_GROUPS - 1), 0, 0)
    grid_spec = pltpu.PrefetchScalarGridSpec(
        num_scalar_prefetch=2,
        grid=(n_slots,),
        in_specs=[
            pl.BlockSpec((tm, D_MODEL), lambda s, blk, grp: (blk[s], 0)),
            pl.BlockSpec((tm, LANES), lambda s, blk, grp: (blk[s], 0)),
            pl.BlockSpec((EXPERTS_PER_GROUP, D_MODEL, D_FF), wmap),
            pl.BlockSpec((EXPERTS_PER_GROUP, D_MODEL, D_FF), wmap),
            pl.BlockSpec((1, EXPERTS_PER_GROUP * D_FF, D_MODEL), wmap),
        ],
        out_specs=pl.BlockSpec((tm * TOK_ROWS, LANES), lambda s, blk, grp: (blk[s], 0)),
    )
    return pl.pallas_call(
        functools.partial(_experts_kernel, tm=tm),
        grid_spec=grid_spec,
        out_shape=jax.ShapeDtypeStruct((n_rows * TOK_ROWS, LANES), F32),
        compiler_params=_cparams(1, 48),
        name="moe_experts",
    )(tile_blk, tile_grp, xb, gs, wg, wu, wd)


def _gather_kernel(pos_ref, ys_ref, o_ref, *, chunk):
    def body(t, carry):
        p = pos_ref[0, 0, t]
        src = pl.multiple_of(p * TOK_ROWS, TOK_ROWS)
        dst = pl.multiple_of(t * TOK_ROWS, TOK_ROWS)
        o_ref[pl.ds(dst, TOK_ROWS), :] = ys_ref[pl.ds(src, TOK_ROWS), :]
        return carry

    lax.fori_loop(0, chunk, body, 0, unroll=8)


def _gather_call(pos, ys_tt, *, chunk, rows, n_tok):
    n_chunks = n_tok // chunk
    return pl.pallas_call(
        functools.partial(_gather_kernel, chunk=chunk),
        grid=(n_chunks,),
        in_specs=[
            pl.BlockSpec((1, 1, chunk), lambda c: (c, 0, 0), memory_space=pltpu.SMEM),
            pl.BlockSpec((rows * TOK_ROWS, LANES), lambda c: (c, 0)),
        ],
        out_specs=pl.BlockSpec((chunk * TOK_ROWS, LANES), lambda c: (c, 0)),
        out_shape=jax.ShapeDtypeStruct((n_tok * TOK_ROWS, LANES), F32),
        compiler_params=_cparams(1, 56),
        name="moe_gather",
    )(pos, ys_tt)


def _gather_final_kernel(pos_ref, ys_ref, x_ref, g2_ref, nf_ref, o_ref, o_sc, *, part, sub):
    base = pl.program_id(1) * part

    def body(t, carry):
        p = pos_ref[0, 0, base + t]
        src = pl.multiple_of(p * TOK_ROWS, TOK_ROWS)
        dst = pl.multiple_of(t * TOK_ROWS, TOK_ROWS)
        o_sc[pl.ds(dst, TOK_ROWS), :] = ys_ref[pl.ds(src, TOK_ROWS), :]
        return carry

    lax.fori_loop(0, part, body, 0, unroll=8)
    for r in range(part // sub):
        rs = slice(r * sub, (r + 1) * sub)
        moe = jnp.concatenate(
            [o_sc[pl.ds(r * sub * TOK_ROWS + c, sub, stride=TOK_ROWS), :] for c in range(TOK_ROWS)], axis=1)
        x = x_ref[rs, :] + g2_ref[0] * moe
        ms = jnp.mean(x * x, axis=1, keepdims=True)
        o_ref[rs, :] = x * lax.rsqrt(ms + EPS) * nf_ref[...]


def _gather_final_call(pos, ys_tt, x, g2, nf, *, chunk, rows, seq):
    n_tok, d = x.shape
    n_chunks = n_tok // chunk
    parts = 2
    part = chunk // parts
    return pl.pallas_call(
        functools.partial(_gather_final_kernel, part=part, sub=256),
        grid=(n_chunks, parts),
        in_specs=[
            pl.BlockSpec((1, 1, chunk), lambda c, s: (c, 0, 0), memory_space=pltpu.SMEM),
            pl.BlockSpec((rows * TOK_ROWS, LANES), lambda c, s: (c, 0)),
            pl.BlockSpec((part, d), lambda c, s: (c * parts + s, 0)),
            pl.BlockSpec((1, 1, d), lambda c, s: ((c * chunk + s * part) // seq, 0, 0)),
            pl.BlockSpec((1, d), lambda c, s: (0, 0)),
        ],
        out_specs=pl.BlockSpec((part, d), lambda c, s: (c * parts + s, 0)),
        out_shape=jax.ShapeDtypeStruct((n_tok, d), F32),
        scratch_shapes=[pltpu.VMEM((part * TOK_ROWS, LANES), F32)],
        compiler_params=_cparams(2, 56),
        name="moe_gather_final",
    )(pos, ys_tt, x, g2, nf)


def _moe_tiles(tot, *, tm, tiles_per_chunk):
    nt = (tot + tm - 1) // tm
    ts = jnp.cumsum(nt, axis=1) - nt
    off = (ts * tm).astype(jnp.int32)
    j = jnp.arange(tiles_per_chunk, dtype=jnp.int32)[None, :, None]
    inside = (j >= ts[:, None, :]) & (j < (ts + nt)[:, None, :])
    key = jnp.where(jnp.any(inside, axis=-1), jnp.argmax(inside, axis=-1), N_GROUPS).reshape(-1)
    order = jnp.argsort(key, stable=True).astype(jnp.int32)
    return off, order, key[order].astype(jnp.int32)


def kernel(x, c, w_in, conv_w, conv_b, fox_f_bias, mlstm_i_bias, mlstm_f_bias, fox_out_norm,
           mlstm_out_norm, w_out, w_ada, b_ada, norm_mix, norm_ffn, w_router, b_router, w_gate,
           w_up, w_down, norm_final):
    bsz, seq, d = x.shape
    depth = w_in.shape[0]
    t = bsz * seq
    assert d == D_MODEL and w_in.shape[-1] == IN_COLS
    chunk = min(2048, seq)
    tm = 128
    assert chunk % tm == 0
    rows = chunk + (N_GROUPS - 1) * tm
    te = min(1024, seq)
    assert seq % 512 == 0 and seq % chunk == 0 and chunk % te == 0

    mods = _ada_call(c, w_ada, b_ada).reshape(depth, bsz, N_ADA, 1, d)
    wm_all = jnp.concatenate([w_in[:, :, FOX_Q:FOX_F], w_in[:, :, ML_Q:ML_I], w_in[:, :, ML_O:IN_COLS]],
                             axis=2).astype(BF16)
    zpad = lambda n: jnp.zeros((depth, d, n), F32)
    wg_all = jnp.concatenate([
        w_in[:, :, FOX_F:ML_Q], w_in[:, :, ML_F:ML_O], zpad(LANES - FOX_HEADS - ML_HEADS),
        zpad(GATE_ML), w_in[:, :, ML_I:ML_F], zpad(LANES - GATE_ML - ML_HEADS)], axis=2).astype(BF16)
    wo_all = w_out.astype(BF16)
    wge_all = w_gate.astype(BF16).reshape(depth * N_EXPERTS, d, D_FF)
    wue_all = w_up.astype(BF16).reshape(depth * N_EXPERTS, d, D_FF)
    wde_all = w_down.astype(BF16).reshape(depth * N_GROUPS, EXPERTS_PER_GROUP * D_FF, d)
    xf = x.reshape(t, d)

    tri = (jnp.arange(te)[:, None] < jnp.arange(te)[None, :]).astype(BF16)
    wr_f = jnp.pad(w_router, ((0, 0), (0, LANES - N_EXPERTS))).astype(F32)
    wr_hi = wr_f.astype(BF16)
    wr = jnp.concatenate([wr_hi, (wr_f - wr_hi.astype(F32)).astype(BF16)], axis=1)
    br = b_router.reshape(N_EXPERTS, 1).astype(F32)

    moe_tt = None
    g2_prev = None
    for l in range(depth):
        sh1, sc1, g1, sh2, sc2, g2 = [mods[l, :, i] for i in range(N_ADA)]
        zb = lambda n: jnp.zeros((n,), F32)
        gb = jnp.concatenate([
            fox_f_bias[l], mlstm_f_bias[l], zb(LANES - FOX_HEADS - ML_HEADS),
            zb(GATE_ML), mlstm_i_bias[l], zb(LANES - GATE_ML - ML_HEADS)]).reshape(1, 2 * LANES)

        xf, qat, ka, fvt, qkv, mkt, fc, gr = _in_proj_call(
            xf, moe_tt, g2_prev, norm_mix[l].reshape(1, d), sc1, sh1, wm_all, wg_all, gb,
            conv_w[l], conv_b[l].reshape(1, -1), layer=l, bsz=bsz, seq=seq)
        hf = _fox_call(qat, ka, fvt, fox_out_norm[l].reshape(1, FOX_WIDTH), bsz=bsz, seq=seq)
        hm = _mlstm_call(qkv, mkt, fc, gr, mlstm_out_norm[l].reshape(1, ML_WIDTH), bsz=bsz, seq=seq)
        xf, hn_tt, gsel, grp, rank, tot = _post_call(
            xf, hf, hm, wo_all, g1, norm_ffn[l].reshape(1, d), sc2, sh2, wr, br, tri,
            layer=l, bsz=bsz, seq=seq, chunk=chunk, te=te)

        n_chunks = t // chunk
        grp = grp.reshape(n_chunks, 1, chunk)
        rank = rank.reshape(n_chunks, 1, chunk)
        off, tile_blk, tile_grp = _moe_tiles(tot[:, :N_GROUPS, 0], tm=tm, tiles_per_chunk=rows // tm)
        pos = rank
        for g in range(N_GROUPS):
            pos = pos + jnp.where(grp == g, off[:, g].reshape(n_chunks, 1, 1), 0)
        xb, gs = _scatter_call(pos, hn_tt, gsel, chunk=chunk, rows=rows, tm=tm)
        ys_tt = _experts_call(tile_blk, tile_grp, xb, gs, wge_all, wue_all, wde_all, layer=l, tm=tm)
        if l < depth - 1:
            moe_tt = _gather_call(pos, ys_tt, chunk=chunk, rows=rows, n_tok=t)
            g2_prev = g2
        else:
            out = _gather_final_call(pos, ys_tt, xf, g2, norm_final.reshape(1, d),
                                     chunk=chunk, rows=rows, seq=seq)
    return out.reshape(bsz, seq, d)
```

```python
import functools

import numpy as np
import jax
import jax.numpy as jnp
from jax import lax
from jax.experimental import pallas as pl
from jax.experimental.pallas import tpu as pltpu

F32 = jnp.float32
BF16 = jnp.bfloat16

LANES = 128
SUBLANES = 8
VMEM_BYTES_V7X = 64 * 1024 * 1024

D_MODEL = 1024
FOX_HEADS = 8
FOX_HEAD_DIM = 64
FOX_WIDTH = FOX_HEADS * FOX_HEAD_DIM
ML_HEADS = 4
ML_HEAD_DIM = 128
ML_WIDTH = ML_HEADS * ML_HEAD_DIM
CONV_WIDTH = 4
N_EXPERTS = 16
N_GROUPS = 4
EXPERTS_PER_GROUP = 4
D_FF = 512
N_ADA = 6
EPS = 1e-6
NEG = -1e30

FOX_Q = 0
FOX_F = 3 * FOX_WIDTH
ML_Q = FOX_F + FOX_HEADS
ML_I = ML_Q + 3 * ML_WIDTH
ML_F = ML_I + ML_HEADS
ML_O = ML_F + ML_HEADS
IN_COLS = ML_O + ML_WIDTH

MAIN_COLS = 7 * 512
BLK_FQ, BLK_FK, BLK_FV, BLK_MQ, BLK_MK, BLK_MV, BLK_MO = range(7)
REST_COLS = 3 * 512
OUT_MQ, OUT_MV, OUT_MO = range(3)
AUG_COLS = FOX_HEADS * 128
BIAS_TERMS = 3
VT_PAD = 16
LOG2E = 1.4426950408889634
GATE_FOX = 0
GATE_ML = FOX_HEADS
GATE_ROWS = 16

TOK_ROWS = D_MODEL // LANES


def _cparams(n_grid, vmem_mb):
    return pltpu.CompilerParams(
        dimension_semantics=("arbitrary",) * n_grid,
        vmem_limit_bytes=vmem_mb * 1024 * 1024)


def _silu(x):
    return x * jax.nn.sigmoid(x)


def _log_sigmoid(z):
    return jnp.minimum(z, 0.0) - jnp.log1p(jnp.exp(-jnp.abs(z)))


def _cumsum_rows(x):
    n = x.shape[0]
    row = lax.broadcasted_iota(jnp.int32, x.shape, 0)
    s = 1
    while s < n:
        x = x + jnp.where(row >= s, pltpu.roll(x, s, axis=0), 0.0)
        s *= 2
    return x


def _from_token_tiles(ref, n_tok):
    return jnp.concatenate(
        [ref[pl.ds(c, n_tok, stride=TOK_ROWS), :] for c in range(TOK_ROWS)], axis=1)


def _to_token_tiles(ref, val, row0=0):
    n = val.shape[0]
    for c in range(TOK_ROWS):
        ref[pl.ds(row0 * TOK_ROWS + c, n, stride=TOK_ROWS), :] = val[:, c * LANES:(c + 1) * LANES]


def _ada_kernel(c_ref, w_ref, b_ref, o_ref):
    c = c_ref[...]
    o_ref[0] = jnp.dot(_silu(c), w_ref[0], preferred_element_type=F32,
                       precision=lax.Precision.HIGHEST) + b_ref[0]


def _ada_call(c, w_ada, b_ada):
    depth, d, n = w_ada.shape
    bsz = c.shape[0]
    tn = 1536
    return pl.pallas_call(
        _ada_kernel,
        grid=(depth, n // tn),
        in_specs=[
            pl.BlockSpec((bsz, d), lambda l, j: (0, 0)),
            pl.BlockSpec((1, d, tn), lambda l, j: (l, 0, j)),
            pl.BlockSpec((1, 1, tn), lambda l, j: (l, 0, j)),
        ],
        out_specs=pl.BlockSpec((1, bsz, tn), lambda l, j: (l, 0, j)),
        out_shape=jax.ShapeDtypeStruct((depth, bsz, n), F32),
        compiler_params=_cparams(2, 32),
        name="ada_mod",
    )(c, w_ada, b_ada.reshape(depth, 1, n))


def _in_proj_kernel(*refs, tm, combine):
    if combine:
        (x_ref, ott_ref, g2_ref, nm_ref, sc_ref, sh_ref, wm_ref, wg_ref, gb_ref, cw_ref, cb_ref,
         place_ref, ones_ref,
         xn_ref, qa_ref, ka_ref, vt_ref, qkv_ref, kt_ref, fc_ref, gr_ref, fcar, ccar) = refs
    else:
        (x_ref, nm_ref, sc_ref, sh_ref, wm_ref, wg_ref, gb_ref, cw_ref, cb_ref,
         place_ref, ones_ref,
         qa_ref, ka_ref, vt_ref, qkv_ref, kt_ref, fc_ref, gr_ref, fcar, ccar) = refs

    @pl.when(pl.program_id(1) == 0)
    def _():
        fcar[...] = jnp.zeros_like(fcar)
        ccar[...] = jnp.zeros_like(ccar)

    x = x_ref[...]
    if combine:
        x = x + g2_ref[0] * _from_token_tiles(ott_ref, tm)
        xn_ref[...] = x
    ms = jnp.mean(x * x, axis=1, keepdims=True)
    hn = x * lax.rsqrt(ms + EPS) * nm_ref[...]
    hn = hn * (1.0 + sc_ref[0]) + sh_ref[0]
    hb = hn.astype(BF16)

    gp = jnp.dot(hb, wg_ref[0], preferred_element_type=F32) + gb_ref[...]
    lf = _log_sigmoid(gp[:, :LANES])
    fcum = _cumsum_rows(lf) + fcar[...]
    fcar[...] = fcum[tm - 1:tm, :]
    gml = gp[:, LANES:] - fcum
    fc_ref[...] = fcum
    gr_ref[0] = gml.T[:GATE_ROWS, :]

    def mm(j):
        return jnp.dot(hb, wm_ref[0, :, j * 512:(j + 1) * 512], preferred_element_type=F32)

    def put(j, v):
        qkv_ref[:, j * 512:(j + 1) * 512] = v.astype(BF16)

    lane = lax.broadcasted_iota(jnp.int32, (tm, LANES), 1)
    fs = fcum * LOG2E
    hi = fs.astype(BF16).astype(F32)
    mid = (fs - hi).astype(BF16).astype(F32)
    low = (fs - hi - mid).astype(BF16).astype(F32)
    packed = jnp.where(lane < FOX_HEADS, hi,
                       jnp.where(lane < 2 * FOX_HEADS, pltpu.roll(mid, FOX_HEADS, axis=1),
                                 jnp.where(lane < 3 * FOX_HEADS, pltpu.roll(low, 2 * FOX_HEADS, axis=1), 0.0)))
    bias = jnp.dot(packed.astype(BF16), place_ref[...], preferred_element_type=F32) + ones_ref[...]

    def put_heads(ref, val, col0, transposed):
        lo = lane < FOX_HEAD_DIM
        for p in range(FOX_HEADS // 2):
            slab = val[:, p * LANES:(p + 1) * LANES]
            for h, data in ((2 * p, slab), (2 * p + 1, pltpu.roll(slab, FOX_HEAD_DIM, axis=1))):
                blk = jnp.where(lo, data, 0.0) + bias[:, col0 + h * LANES:col0 + (h + 1) * LANES]
                if transposed:
                    ref[0, 0, h * LANES:(h + 1) * LANES, :] = blk.T.astype(BF16)
                else:
                    ref[:, h * LANES:(h + 1) * LANES] = blk.astype(BF16)

    u = jnp.concatenate([mm(BLK_MQ), mm(BLK_MK)], axis=1)
    prev = ccar[...]
    ccar[...] = u[tm - SUBLANES:tm, :]
    row8 = lax.broadcasted_iota(jnp.int32, prev.shape, 0)
    y = cb_ref[...] + cw_ref[CONV_WIDTH - 1:CONV_WIDTH, :] * u
    for k in range(1, CONV_WIDTH):
        r = pltpu.roll(u, k, axis=0)
        top = jnp.where(row8 < k, pltpu.roll(prev, k, axis=0), r[:SUBLANES])
        shifted = jnp.concatenate([top, r[SUBLANES:]], axis=0)
        y = y + cw_ref[CONV_WIDTH - 1 - k:CONV_WIDTH - k, :] * shifted
    act = _silu(y)
    put(OUT_MQ, act[:, :ML_WIDTH])
    kt_ref[0, 0] = (act[:, ML_WIDTH:] * (ML_HEAD_DIM ** -0.5)).T.astype(BF16)

    put(OUT_MO, jax.nn.sigmoid(mm(BLK_MO)))
    put(OUT_MV, mm(BLK_MV))
    vt_ref[0, 0] = mm(BLK_FV).astype(BF16).T
    put_heads(qa_ref, mm(BLK_FQ) * (FOX_HEAD_DIM ** -0.5 * LOG2E), 0, True)
    put_heads(ka_ref, mm(BLK_FK), AUG_COLS, False)


def _bias_placement():
    place = np.zeros((LANES, 2 * AUG_COLS), np.float32)
    ones = np.zeros((1, 2 * AUG_COLS), np.float32)
    for h in range(FOX_HEADS):
        for term in range(BIAS_TERMS):
            src = term * FOX_HEADS + h
            place[src, h * LANES + FOX_HEAD_DIM + term] = 1.0
            ones[0, h * LANES + FOX_HEAD_DIM + BIAS_TERMS + term] = 1.0
            place[src, AUG_COLS + h * LANES + FOX_HEAD_DIM + BIAS_TERMS + term] = -1.0
            ones[0, AUG_COLS + h * LANES + FOX_HEAD_DIM + term] = 1.0
    return jnp.asarray(place, BF16), jnp.asarray(ones, F32)


def _in_proj_call(x, moe_tt, g2, nm, sc, sh, wm, wg, gb, cw, cb, *, layer, bsz, seq, tm=512):
    t, d = x.shape
    ns = seq // tm
    combine = moe_tt is not None
    row = lambda b, s: (b * ns + s, 0)
    per_b = lambda b, s: (b, 0, 0)
    const2 = lambda b, s: (0, 0)
    in_specs = [pl.BlockSpec((tm, d), row)]
    args = [x]
    if combine:
        in_specs += [pl.BlockSpec((tm * TOK_ROWS, LANES), row), pl.BlockSpec((1, 1, d), per_b)]
        args += [moe_tt, g2]
    in_specs += [
        pl.BlockSpec((1, d), const2),
        pl.BlockSpec((1, 1, d), per_b),
        pl.BlockSpec((1, 1, d), per_b),
        pl.BlockSpec((1, d, MAIN_COLS), lambda b, s: (layer, 0, 0)),
        pl.BlockSpec((1, d, 2 * LANES), lambda b, s: (layer, 0, 0)),
        pl.BlockSpec((1, 2 * LANES), const2),
        pl.BlockSpec((CONV_WIDTH, 2 * ML_WIDTH), const2),
        pl.BlockSpec((1, 2 * ML_WIDTH), const2),
        pl.BlockSpec((LANES, 2 * AUG_COLS), const2),
        pl.BlockSpec((1, 2 * AUG_COLS), const2),
    ]
    place, ones = _bias_placement()
    args += [nm, sc, sh, wm, wg, gb, cw, cb, place, ones]
    out_specs = []
    out_shape = []
    if combine:
        out_specs.append(pl.BlockSpec((tm, d), row))
        out_shape.append(jax.ShapeDtypeStruct((t, d), F32))
    tiled = lambda b, s: (b, s, 0, 0)
    out_specs += [
        pl.BlockSpec((1, 1, AUG_COLS, tm), tiled),
        pl.BlockSpec((tm, AUG_COLS), row),
        pl.BlockSpec((1, 1, FOX_WIDTH, tm), tiled),
        pl.BlockSpec((tm, REST_COLS), row),
        pl.BlockSpec((1, 1, ML_WIDTH, tm), tiled),
        pl.BlockSpec((tm, LANES), row),
        pl.BlockSpec((1, GATE_ROWS, tm), lambda b, s: (b, 0, s)),
    ]
    out_shape += [
        jax.ShapeDtypeStruct((bsz, ns, AUG_COLS, tm), BF16),
        jax.ShapeDtypeStruct((t, AUG_COLS), BF16),
        jax.ShapeDtypeStruct((bsz, ns, FOX_WIDTH, tm), BF16),
        jax.ShapeDtypeStruct((t, REST_COLS), BF16),
        jax.ShapeDtypeStruct((bsz, ns, ML_WIDTH, tm), BF16),
        jax.ShapeDtypeStruct((t, LANES), F32),
        jax.ShapeDtypeStruct((bsz, GATE_ROWS, seq), F32),
    ]
    outs = pl.pallas_call(
        functools.partial(_in_proj_kernel, tm=tm, combine=combine),
        grid=(bsz, ns),
        in_specs=in_specs,
        out_specs=out_specs,
        out_shape=out_shape,
        scratch_shapes=[pltpu.VMEM((1, LANES), F32), pltpu.VMEM((SUBLANES, 2 * ML_WIDTH), F32)],
        compiler_params=_cparams(2, 48),
        name="in_proj",
    )(*args)
    if combine:
        return outs
    return [x] + list(outs)


def _fox_kernel(qt_ref, k_ref, vt_ref, ng_ref, cm_ref, o_ref, m_sc, acc_sc, *, tq, tk, nh):
    qi = pl.program_id(2)
    ones_rows = jnp.where(lax.broadcasted_iota(jnp.int32, (VT_PAD, tk), 0) == 0, 1.0, 0.0).astype(BF16)
    for h in range(nh):
        m_sc[h] = jnp.full((SUBLANES, tq), NEG, F32)
        acc_sc[h] = jnp.zeros((FOX_HEAD_DIM + VT_PAD, tq), F32)

    half = tk // 2

    def body(j, carry, diagonal):
        k0 = pl.multiple_of(j * tk, tk)

        def logits(h):
            hs = slice(h * LANES, (h + 1) * LANES)
            qt = qt_ref[0, 0, hs, :]
            if not diagonal:
                return jnp.dot(k_ref[pl.ds(k0, tk), hs], qt, preferred_element_type=F32)
            za = jnp.dot(k_ref[pl.ds(k0, half), hs], qt, preferred_element_type=F32) + cm_ref[:half, :]
            zb = (jnp.dot(k_ref[pl.ds(k0 + half, half), hs], qt[:, half:], preferred_element_type=F32)
                  + cm_ref[half:, :][:, half:])
            zb = jnp.concatenate([jnp.full((half, half), NEG, F32), zb], axis=1)
            return jnp.concatenate([za, zb], axis=0)

        ahead = 2
        zs = [logits(h) for h in range(min(ahead, nh))]
        for h in range(nh):
            ds_ = slice(h * FOX_HEAD_DIM, (h + 1) * FOX_HEAD_DIM)
            z = zs[h]
            if h + ahead < nh:
                zs.append(logits(h + ahead))
            m_prev = m_sc[h]
            m_new = jnp.maximum(m_prev, jnp.max(z, axis=0, keepdims=True))
            alpha = jnp.exp2(m_prev - m_new)
            p = jnp.exp2(z - m_new[:1, :])
            va = jnp.concatenate([vt_ref[0, j, ds_, :], ones_rows], axis=0)
            acc_sc[h] = alpha[:1, :] * acc_sc[h] + jnp.dot(va, p.astype(BF16), preferred_element_type=F32)
            m_sc[h] = m_new
        return carry

    lax.fori_loop(0, qi, functools.partial(body, diagonal=False), 0)
    lax.fori_loop(qi, qi + 1, functools.partial(body, diagonal=True), 0)

    for p in range(nh // 2):
        outs = []
        for h in (2 * p, 2 * p + 1):
            acc = acc_sc[h]
            num = acc[:FOX_HEAD_DIM, :]
            l = acc[FOX_HEAD_DIM:FOX_HEAD_DIM + 1, :]
            ms = jnp.mean(num * num, axis=0, keepdims=True)
            outs.append(num * lax.rsqrt(ms + EPS * l * l))
        ps = slice(p * LANES, (p + 1) * LANES)
        o_ref[:, ps] = (jnp.concatenate(outs, axis=0).T * ng_ref[:, ps]).astype(BF16)


def _fox_call(qat, ka, fvt, ng, *, bsz, seq, tq=512, nh=8):
    t = ka.shape[0]
    nq = seq // tq
    ngrp = FOX_HEADS // nh
    vw = nh * FOX_HEAD_DIM
    cmask = jnp.where(jnp.arange(tq)[:, None] <= jnp.arange(tq)[None, :], 0.0, NEG).astype(F32)
    return pl.pallas_call(
        functools.partial(_fox_kernel, tq=tq, tk=tq, nh=nh),
        grid=(bsz, ngrp, nq),
        in_specs=[
            pl.BlockSpec((1, 1, nh * LANES, tq), lambda b, p, i: (b, i, p, 0)),
            pl.BlockSpec((seq, nh * LANES), lambda b, p, i: (b, p)),
            pl.BlockSpec((1, nq, vw, tq), lambda b, p, i: (b, 0, p, 0)),
            pl.BlockSpec((1, vw), lambda b, p, i: (0, p)),
            pl.BlockSpec((tq, tq), lambda b, p, i: (0, 0)),
        ],
        out_specs=pl.BlockSpec((tq, vw), lambda b, p, i: (b * nq + i, p)),
        out_shape=jax.ShapeDtypeStruct((t, FOX_WIDTH), BF16),
        scratch_shapes=[pltpu.VMEM((nh, SUBLANES, tq), F32),
                        pltpu.VMEM((nh, FOX_HEAD_DIM + VT_PAD, tq), F32)],
        compiler_params=_cparams(3, 48),
        name="fox_attn",
    )(qat, ka, fvt, ng, cmask)


def _mlstm_kernel(q_ref, kt_ref, v_ref, og_ref, fc_ref, gr_ref, ng_ref, o_ref, ct_sc, u_sc, *, ch, per_tile):
    @pl.when(pl.program_id(1) == 0)
    def _():
        ct_sc[...] = jnp.zeros_like(ct_sc)
        u_sc[...] = jnp.zeros_like(u_sc)

    causal = (lax.broadcasted_iota(jnp.int32, (ch, ch), 1)
              <= lax.broadcasted_iota(jnp.int32, (ch, ch), 0))
    lane = lax.broadcasted_iota(jnp.int32, (ch, LANES), 1)
    fc = fc_ref[...]
    koff = pl.multiple_of((pl.program_id(1) % per_tile) * ch, ch)
    for h in range(ML_HEADS):
        sl = slice(h * ML_HEAD_DIM, (h + 1) * ML_HEAD_DIM)
        gl = GATE_ML + h
        q = q_ref[:, sl]
        kt = kt_ref[0, 0, sl, pl.ds(koff, ch)]
        vp = jnp.concatenate([v_ref[:, sl], jnp.where(lane == gl, 1.0, 0.0).astype(BF16)], axis=1)
        g_row = gr_ref[0, gl:gl + 1, :]
        u_prev = u_sc[h][:, :1]
        gm = jnp.where(causal, g_row, NEG)
        u_i = jnp.maximum(u_prev, jnp.max(gm, axis=1, keepdims=True))
        dmat = jnp.exp(gm - u_i)
        s = jnp.dot(q, kt, preferred_element_type=F32)
        scores = (s * dmat).astype(BF16)
        inter = jnp.exp(u_prev - u_i)
        ct = ct_sc[h]
        nd = (jnp.dot(scores, vp, preferred_element_type=F32)
              + jnp.dot(q, ct.astype(BF16), preferred_element_type=F32) * inter)
        num = nd[:, :ML_HEAD_DIM]
        den = jnp.maximum(jnp.abs(nd[:, ML_HEAD_DIM:]), jnp.exp(-(jnp.where(lane == gl, fc, 0.0) + u_i)))
        ms = jnp.mean(num * num, axis=1, keepdims=True)
        scale = lax.rsqrt(ms + EPS * den * den)[:, gl:gl + 1]
        y = num * scale * ng_ref[:, sl] * og_ref[:, sl].astype(F32)
        o_ref[:, sl] = y.astype(BF16)
        u_new = jnp.maximum(u_prev, jnp.max(g_row, axis=1, keepdims=True))
        ktw = (kt.astype(F32) * jnp.exp(g_row - u_new)).astype(BF16)
        ct_sc[h] = jnp.exp(u_prev - u_new) * ct + jnp.dot(ktw, vp, preferred_element_type=F32)
        u_sc[h] = jnp.broadcast_to(u_new, (1, LANES))


def _mlstm_call(qkv, mkt, fc, gr, ng, *, bsz, seq, ch=512):
    t = qkv.shape[0]
    nc = seq // ch
    ktile = mkt.shape[-1]
    per_tile = ktile // ch
    row = lambda b, c: (b * nc + c, 0)
    return pl.pallas_call(
        functools.partial(_mlstm_kernel, ch=ch, per_tile=per_tile),
        grid=(bsz, nc),
        in_specs=[
            pl.BlockSpec((ch, ML_WIDTH), lambda b, c: (b * nc + c, OUT_MQ)),
            pl.BlockSpec((1, 1, ML_WIDTH, ktile), lambda b, c: (b, c // per_tile, 0, 0)),
            pl.BlockSpec((ch, ML_WIDTH), lambda b, c: (b * nc + c, OUT_MV)),
            pl.BlockSpec((ch, ML_WIDTH), lambda b, c: (b * nc + c, OUT_MO)),
            pl.BlockSpec((ch, LANES), row),
            pl.BlockSpec((1, GATE_ROWS, ch), lambda b, c: (b, 0, c)),
            pl.BlockSpec((1, ML_WIDTH), lambda b, c: (0, 0)),
        ],
        out_specs=pl.BlockSpec((ch, ML_WIDTH), row),
        out_shape=jax.ShapeDtypeStruct((t, ML_WIDTH), BF16),
        scratch_shapes=[pltpu.VMEM((ML_HEADS, ML_HEAD_DIM, 2 * ML_HEAD_DIM), F32),
                        pltpu.VMEM((ML_HEADS, 1, LANES), F32)],
        compiler_params=_cparams(2, 32),
        name="mlstm",
    )(qkv, mkt, qkv, qkv, fc, gr, ng)


def _post_kernel(x_ref, hf_ref, hm_ref, wo_ref, g1_ref, nf_ref, sc_ref, sh_ref, wr_ref, br_ref, tri_ref,
                 x1_ref, hn_ref, gsel_ref, grp_ref, rank_ref, tot_ref, cnt_sc, *, te, steps_per_chunk):
    @pl.when(pl.program_id(0) % steps_per_chunk == 0)
    def _():
        cnt_sc[...] = jnp.zeros_like(cnt_sc)

    mix = (jnp.dot(hf_ref[...], wo_ref[0, :FOX_WIDTH, :], preferred_element_type=F32)
           + jnp.dot(hm_ref[...], wo_ref[0, FOX_WIDTH:, :], preferred_element_type=F32))
    x1 = x_ref[...] + g1_ref[0] * mix
    x1_ref[...] = x1
    ms = jnp.mean(x1 * x1, axis=1, keepdims=True)
    hn = x1 * lax.rsqrt(ms + EPS) * nf_ref[...]
    hn = hn * (1.0 + sc_ref[0]) + sh_ref[0]
    _to_token_tiles(hn_ref, hn)

    hn_hi = hn.astype(BF16)
    hn_lo = (hn - hn_hi.astype(F32)).astype(BF16)
    l2 = jnp.dot(hn_hi, wr_ref[...], preferred_element_type=F32)
    logits = (l2[:, :LANES] + l2[:, LANES:]
              + jnp.dot(hn_lo, wr_ref[:, :LANES], preferred_element_type=F32))
    aff = jax.nn.sigmoid(logits.T[:N_EXPERTS, :])
    sel = aff + br_ref[...]
    selr = [sel[e:e + 1, :] for e in range(N_EXPERTS)]
    affr = [aff[e:e + 1, :] for e in range(N_EXPERTS)]
    keep = [None] * N_EXPERTS
    score = []
    for g in range(N_GROUPS):
        vs = selr[g * EXPERTS_PER_GROUP:(g + 1) * EXPERTS_PER_GROUP]
        sg = jnp.zeros_like(vs[0])
        for i in range(EXPERTS_PER_GROUP):
            beaten = jnp.zeros_like(vs[0])
            for j in range(EXPERTS_PER_GROUP):
                if j != i:
                    b = (vs[j] >= vs[i]) if j < i else (vs[j] > vs[i])
                    beaten = beaten + jnp.where(b, 1.0, 0.0)
            kp = beaten < 2.0
            keep[g * EXPERTS_PER_GROUP + i] = kp
            sg = sg + jnp.where(kp, vs[i], 0.0)
        score.append(sg)
    chosen = []
    for g in range(N_GROUPS):
        lost = jnp.zeros_like(score[0])
        for g2 in range(N_GROUPS):
            if g2 != g:
                b = (score[g2] >= score[g]) if g2 < g else (score[g2] > score[g])
                lost = lost + jnp.where(b, 1.0, 0.0)
        chosen.append(jnp.where(lost < 0.5, 1.0, 0.0))
    wsel = []
    for i in range(EXPERTS_PER_GROUP):
        wi = jnp.zeros_like(score[0])
        for g in range(N_GROUPS):
            e = g * EXPERTS_PER_GROUP + i
            wi = wi + chosen[g] * jnp.where(keep[e], affr[e], 0.0)
        wsel.append(wi)
    wsum = wsel[0] + wsel[1] + wsel[2] + wsel[3]
    wsel = [w / wsum for w in wsel]

    row8 = lax.broadcasted_iota(jnp.int32, (SUBLANES, te), 0)
    gmat = jnp.zeros((SUBLANES, te), F32)
    wmat = jnp.zeros((SUBLANES, te), F32)
    grp = jnp.zeros_like(score[0])
    for g in range(N_GROUPS):
        gmat = jnp.where(row8 == g, chosen[g], gmat)
        wmat = jnp.where(row8 == g, wsel[g], wmat)
        grp = grp + g * chosen[g]
    pref = jnp.dot(gmat.astype(BF16), tri_ref[...], preferred_element_type=F32) + cnt_sc[:, :1]
    rank = jnp.sum(gmat * pref, axis=0, keepdims=True)
    tot = cnt_sc[:, :1] + jnp.sum(gmat, axis=1, keepdims=True)
    cnt_sc[...] = jnp.broadcast_to(tot, cnt_sc.shape)
    rank_ref[0] = rank.astype(jnp.int32)
    grp_ref[0] = grp.astype(jnp.int32)
    tot_ref[0] = jnp.broadcast_to(tot, (SUBLANES, LANES)).astype(jnp.int32)
    wfull = jnp.concatenate([wmat, jnp.zeros((LANES - SUBLANES, te), F32)], axis=0)
    gsel_ref[...] = wfull.T


def _post_call(x, hf, hm, wo, g1, nf, sc, sh, wr, br, tri, *, layer, bsz, seq, chunk, te=512):
    t, d = x.shape
    steps_per_chunk = chunk // te
    per_seq = seq // te
    n_steps = t // te
    n_chunks = t // chunk
    row = lambda i: (i, 0)
    per_b = lambda i: (i // per_seq, 0, 0)
    const2 = lambda i: (0, 0)
    return pl.pallas_call(
        functools.partial(_post_kernel, te=te, steps_per_chunk=steps_per_chunk),
        grid=(n_steps,),
        in_specs=[
            pl.BlockSpec((te, d), row),
            pl.BlockSpec((te, FOX_WIDTH), row),
            pl.BlockSpec((te, ML_WIDTH), row),
            pl.BlockSpec((1, d, d), lambda i: (layer, 0, 0)),
            pl.BlockSpec((1, 1, d), per_b),
            pl.BlockSpec((1, d), const2),
            pl.BlockSpec((1, 1, d), per_b),
            pl.BlockSpec((1, 1, d), per_b),
            pl.BlockSpec((d, 2 * LANES), const2),
            pl.BlockSpec((N_EXPERTS, 1), const2),
            pl.BlockSpec((te, te), const2),
        ],
        out_specs=[
            pl.BlockSpec((te, d), row),
            pl.BlockSpec((te * TOK_ROWS, LANES), row),
            pl.BlockSpec((te, LANES), row),
            pl.BlockSpec((1, 1, te), lambda i: (i, 0, 0)),
            pl.BlockSpec((1, 1, te), lambda i: (i, 0, 0)),
            pl.BlockSpec((1, SUBLANES, LANES), lambda i: (i // steps_per_chunk, 0, 0)),
        ],
        out_shape=[
            jax.ShapeDtypeStruct((t, d), F32),
            jax.ShapeDtypeStruct((t * TOK_ROWS, LANES), F32),
            jax.ShapeDtypeStruct((t, LANES), F32),
            jax.ShapeDtypeStruct((n_steps, 1, te), jnp.int32),
            jax.ShapeDtypeStruct((n_steps, 1, te), jnp.int32),
            jax.ShapeDtypeStruct((n_chunks, SUBLANES, LANES), jnp.int32),
        ],
        scratch_shapes=[pltpu.VMEM((SUBLANES, LANES), F32)],
        compiler_params=_cparams(1, 48),
        name="post_router",
    )(x, hf, hm, wo, g1, nf, sc, sh, wr, br, tri)


def _scatter_kernel(pos_ref, hn_ref, gsel_ref, xb_ref, gs_ref, xs_sc, *, chunk, rows, tm):
    xs_sc[...] = jnp.zeros_like(xs_sc)
    gs_ref[...] = jnp.zeros_like(gs_ref)

    def body(t, carry):
        p = pos_ref[0, 0, t]
        src = pl.multiple_of(t * TOK_ROWS, TOK_ROWS)
        dst = pl.multiple_of(p * TOK_ROWS, TOK_ROWS)
        xs_sc[pl.ds(dst, TOK_ROWS), :] = hn_ref[pl.ds(src, TOK_ROWS), :]
        gs_ref[pl.ds(p, 1), :] = gsel_ref[pl.ds(t, 1), :]
        return carry

    lax.fori_loop(0, chunk, body, 0, unroll=8)
    for j in range(rows // tm):
        for c in range(TOK_ROWS):
            xb_ref[j * tm:(j + 1) * tm, c * LANES:(c + 1) * LANES] = (
                xs_sc[pl.ds(j * tm * TOK_ROWS + c, tm, stride=TOK_ROWS), :].astype(BF16))


def _scatter_call(pos, hn_tt, gsel, *, chunk, rows, tm):
    t = gsel.shape[0]
    n_chunks = t // chunk
    return pl.pallas_call(
        functools.partial(_scatter_kernel, chunk=chunk, rows=rows, tm=tm),
        grid=(n_chunks,),
        in_specs=[
            pl.BlockSpec((1, 1, chunk), lambda c: (c, 0, 0), memory_space=pltpu.SMEM),
            pl.BlockSpec((chunk * TOK_ROWS, LANES), lambda c: (c, 0)),
            pl.BlockSpec((chunk, LANES), lambda c: (c, 0)),
        ],
        out_specs=[
            pl.BlockSpec((rows, D_MODEL), lambda c: (c, 0)),
            pl.BlockSpec((rows, LANES), lambda c: (c, 0)),
        ],
        out_shape=[
            jax.ShapeDtypeStruct((n_chunks * rows, D_MODEL), BF16),
            jax.ShapeDtypeStruct((n_chunks * rows, LANES), F32),
        ],
        scratch_shapes=[pltpu.VMEM((rows * TOK_ROWS, LANES), F32)],
        compiler_params=_cparams(1, 56),
        name="moe_scatter",
    )(pos, hn_tt, gsel)


def _experts_kernel(blk_ref, grp_ref, xb_ref, gs_ref, wg_ref, wu_ref, wd_ref, y_ref, *, tm):
    g = grp_ref[pl.program_id(0)]

    @pl.when(g < N_GROUPS)
    def _():
        x = xb_ref[...]
        gs = gs_ref[...]
        acts = []
        for i in range(EXPERTS_PER_GROUP):
            hg = jnp.dot(x, wg_ref[i], preferred_element_type=F32)
            hu = jnp.dot(x, wu_ref[i], preferred_element_type=F32)
            acts.append((_silu(hg) * hu * gs[:, i:i + 1]).astype(BF16))
        y = jnp.dot(jnp.concatenate(acts, axis=1), wd_ref[0], preferred_element_type=F32)
        _to_token_tiles(y_ref, y)

    @pl.when(g >= N_GROUPS)
    def _():
        y_ref[...] = jnp.zeros_like(y_ref)


def _experts_call(tile_blk, tile_grp, xb, gs, wg, wu, wd, *, layer, tm):
    n_rows = xb.shape[0]
    n_slots = n_rows // tm
    wmap = lambda s, blk, grp: (layer * N_GROUPS + jnp.minimum(grp[s], N_GROUPS - 1), 0, 0)
    grid_spec = pltpu.PrefetchScalarGridSpec(
        num_scalar_prefetch=2,
        grid=(n_slots,),
        in_specs=[
            pl.BlockSpec((tm, D_MODEL), lambda s, blk, grp: (blk[s], 0)),
            pl.BlockSpec((tm, LANES), lambda s, blk, grp: (blk[s], 0)),
            pl.BlockSpec((EXPERTS_PER_GROUP, D_MODEL, D_FF), wmap),
            pl.BlockSpec((EXPERTS_PER_GROUP, D_MODEL, D_FF), wmap),
            pl.BlockSpec((1, EXPERTS_PER_GROUP * D_FF, D_MODEL), wmap),
        ],
        out_specs=pl.BlockSpec((tm * TOK_ROWS, LANES), lambda s, blk, grp: (blk[s], 0)),
    )
    return pl.pallas_call(
        functools.partial(_experts_kernel, tm=tm),
        grid_spec=grid_spec,
        out_shape=jax.ShapeDtypeStruct((n_rows * TOK_ROWS, LANES), F32),
        compiler_params=_cparams(1, 48),
        name="moe_experts",
    )(tile_blk, tile_grp, xb, gs, wg, wu, wd)


def _gather_kernel(pos_ref, ys_ref, o_ref, *, chunk):
    def body(t, carry):
        p = pos_ref[0, 0, t]
        src = pl.multiple_of(p * TOK_ROWS, TOK_ROWS)
        dst = pl.multiple_of(t * TOK_ROWS, TOK_ROWS)
        o_ref[pl.ds(dst, TOK_ROWS), :] = ys_ref[pl.ds(src, TOK_ROWS), :]
        return carry

    lax.fori_loop(0, chunk, body, 0, unroll=8)


def _gather_call(pos, ys_tt, *, chunk, rows, n_tok):
    n_chunks = n_tok // chunk
    return pl.pallas_call(
        functools.partial(_gather_kernel, chunk=chunk),
        grid=(n_chunks,),
        in_specs=[
            pl.BlockSpec((1, 1, chunk), lambda c: (c, 0, 0), memory_space=pltpu.SMEM),
            pl.BlockSpec((rows * TOK_ROWS, LANES), lambda c: (c, 0)),
        ],
        out_specs=pl.BlockSpec((chunk * TOK_ROWS, LANES), lambda c: (c, 0)),
        out_shape=jax.ShapeDtypeStruct((n_tok * TOK_ROWS, LANES), F32),
        compiler_params=_cparams(1, 56),
        name="moe_gather",
    )(pos, ys_tt)


def _gather_final_kernel(pos_ref, ys_ref, x_ref, g2_ref, nf_ref, o_ref, o_sc, *, part, sub):
    base = pl.program_id(1) * part

    def body(t, carry):
        p = pos_ref[0, 0, base + t]
        src = pl.multiple_of(p * TOK_ROWS, TOK_ROWS)
        dst = pl.multiple_of(t * TOK_ROWS, TOK_ROWS)
        o_sc[pl.ds(dst, TOK_ROWS), :] = ys_ref[pl.ds(src, TOK_ROWS), :]
        return carry

    lax.fori_loop(0, part, body, 0, unroll=8)
    for r in range(part // sub):
        rs = slice(r * sub, (r + 1) * sub)
        moe = jnp.concatenate(
            [o_sc[pl.ds(r * sub * TOK_ROWS + c, sub, stride=TOK_ROWS), :] for c in range(TOK_ROWS)], axis=1)
        x = x_ref[rs, :] + g2_ref[0] * moe
        ms = jnp.mean(x * x, axis=1, keepdims=True)
        o_ref[rs, :] = x * lax.rsqrt(ms + EPS) * nf_ref[...]


def _gather_final_call(pos, ys_tt, x, g2, nf, *, chunk, rows, seq):
    n_tok, d = x.shape
    n_chunks = n_tok // chunk
    parts = 2
    part = chunk // parts
    return pl.pallas_call(
        functools.partial(_gather_final_kernel, part=part, sub=256),
        grid=(n_chunks, parts),
        in_specs=[
            pl.BlockSpec((1, 1, chunk), lambda c, s: (c, 0, 0), memory_space=pltpu.SMEM),
            pl.BlockSpec((rows * TOK_ROWS, LANES), lambda c, s: (c, 0)),
            pl.BlockSpec((part, d), lambda c, s: (c * parts + s, 0)),
            pl.BlockSpec((1, 1, d), lambda c, s: ((c * chunk + s * part) // seq, 0, 0)),
            pl.BlockSpec((1, d), lambda c, s: (0, 0)),
        ],
        out_specs=pl.BlockSpec((part, d), lambda c, s: (c * parts + s, 0)),
        out_shape=jax.ShapeDtypeStruct((n_tok, d), F32),
        scratch_shapes=[pltpu.VMEM((part * TOK_ROWS, LANES), F32)],
        compiler_params=_cparams(2, 56),
        name="moe_gather_final",
    )(pos, ys_tt, x, g2, nf)


def _moe_tiles(tot, *, tm, tiles_per_chunk):
    nt = (tot + tm - 1) // tm
    ts = jnp.cumsum(nt, axis=1) - nt
    off = (ts * tm).astype(jnp.int32)
    j = jnp.arange(tiles_per_chunk, dtype=jnp.int32)[None, :, None]
    inside = (j >= ts[:, None, :]) & (j < (ts + nt)[:, None, :])
    key = jnp.where(jnp.any(inside, axis=-1), jnp.argmax(inside, axis=-1), N_GROUPS).reshape(-1)
    order = jnp.argsort(key, stable=True).astype(jnp.int32)
    return off, order, key[order].astype(jnp.int32)


def kernel(x, c, w_in, conv_w, conv_b, fox_f_bias, mlstm_i_bias, mlstm_f_bias, fox_out_norm,
           mlstm_out_norm, w_out, w_ada, b_ada, norm_mix, norm_ffn, w_router, b_router, w_gate,
           w_up, w_down, norm_final):
    bsz, seq, d = x.shape
    depth = w_in.shape[0]
    t = bsz * seq
    assert d == D_MODEL and w_in.shape[-1] == IN_COLS
    chunk = min(2048, seq)
    tm = 256
    assert chunk % tm == 0
    rows = chunk + (N_GROUPS - 1) * tm
    te = min(1024, seq)
    assert seq % 512 == 0 and seq % chunk == 0 and chunk % te == 0

    mods = _ada_call(c, w_ada, b_ada).reshape(depth, bsz, N_ADA, 1, d)
    wm_all = jnp.concatenate([w_in[:, :, FOX_Q:FOX_F], w_in[:, :, ML_Q:ML_I], w_in[:, :, ML_O:IN_COLS]],
                             axis=2).astype(BF16)
    zpad = lambda n: jnp.zeros((depth, d, n), F32)
    wg_all = jnp.concatenate([
        w_in[:, :, FOX_F:ML_Q], w_in[:, :, ML_F:ML_O], zpad(LANES - FOX_HEADS - ML_HEADS),
        zpad(GATE_ML), w_in[:, :, ML_I:ML_F], zpad(LANES - GATE_ML - ML_HEADS)], axis=2).astype(BF16)
    wo_all = w_out.astype(BF16)
    wge_all = w_gate.astype(BF16).reshape(depth * N_EXPERTS, d, D_FF)
    wue_all = w_up.astype(BF16).reshape(depth * N_EXPERTS, d, D_FF)
    wde_all = w_down.astype(BF16).reshape(depth * N_GROUPS, EXPERTS_PER_GROUP * D_FF, d)
    xf = x.reshape(t, d)

    tri = (jnp.arange(te)[:, None] < jnp.arange(te)[None, :]).astype(BF16)
    wr_f = jnp.pad(w_router, ((0, 0), (0, LANES - N_EXPERTS))).astype(F32)
    wr_hi = wr_f.astype(BF16)
    wr = jnp.concatenate([wr_hi, (wr_f - wr_hi.astype(F32)).astype(BF16)], axis=1)
    br = b_router.reshape(N_EXPERTS, 1).astype(F32)

    moe_tt = None
    g2_prev = None
    for l in range(depth):
        sh1, sc1, g1, sh2, sc2, g2 = [mods[l, :, i] for i in range(N_ADA)]
        zb = lambda n: jnp.zeros((n,), F32)
        gb = jnp.concatenate([
            fox_f_bias[l], mlstm_f_bias[l], zb(LANES - FOX_HEADS - ML_HEADS),
            zb(GATE_ML), mlstm_i_bias[l], zb(LANES - GATE_ML - ML_HEADS)]).reshape(1, 2 * LANES)

        xf, qat, ka, fvt, qkv, mkt, fc, gr = _in_proj_call(
            xf, moe_tt, g2_prev, norm_mix[l].reshape(1, d), sc1, sh1, wm_all, wg_all, gb,
            conv_w[l], conv_b[l].reshape(1, -1), layer=l, bsz=bsz, seq=seq)
        hf = _fox_call(qat, ka, fvt, fox_out_norm[l].reshape(1, FOX_WIDTH), bsz=bsz, seq=seq)
        hm = _mlstm_call(qkv, mkt, fc, gr, mlstm_out_norm[l].reshape(1, ML_WIDTH), bsz=bsz, seq=seq)
        xf, hn_tt, gsel, grp, rank, tot = _post_call(
            xf, hf, hm, wo_all, g1, norm_ffn[l].reshape(1, d), sc2, sh2, wr, br, tri,
            layer=l, bsz=bsz, seq=seq, chunk=chunk, te=te)

        n_chunks = t // chunk
        grp = grp.reshape(n_chunks, 1, chunk)
        rank = rank.reshape(n_chunks, 1, chunk)
        off, tile_blk, tile_grp = _moe_tiles(tot[:, :N_GROUPS, 0], tm=tm, tiles_per_chunk=rows // tm)
        pos = rank
        for g in range(N_GROUPS):
            pos = pos + jnp.where(grp == g, off[:, g].reshape(n_chunks, 1, 1), 0)
        xb, gs = _scatter_call(pos, hn_tt, gsel, chunk=chunk, rows=rows, tm=tm)
        ys_tt = _experts_call(tile_blk, tile_grp, xb, gs, wge_all, wue_all, wde_all, layer=l, tm=tm)
        if l < depth - 1:
            moe_tt = _gather_call(pos, ys_tt, chunk=chunk, rows=rows, n_tok=t)
            g2_prev = g2
        else:
            out = _gather_final_call(pos, ys_tt, xf, g2, norm_final.reshape(1, d),
                                     chunk=chunk, rows=rows, seq=seq)
    return out.reshape(bsz, seq, d)
```

```python
import functools

import numpy as np
import jax
import jax.numpy as jnp
from jax import lax
from jax.experimental import pallas as pl
from jax.experimental.pallas import tpu as pltpu

F32 = jnp.float32
BF16 = jnp.bfloat16

LANES = 128
SUBLANES = 8
VMEM_BYTES_V7X = 64 * 1024 * 1024

D_MODEL = 1024
FOX_HEADS = 8
FOX_HEAD_DIM = 64
FOX_WIDTH = FOX_HEADS * FOX_HEAD_DIM
ML_HEADS = 4
ML_HEAD_DIM = 128
ML_WIDTH = ML_HEADS * ML_HEAD_DIM
CONV_WIDTH = 4
N_EXPERTS = 16
N_GROUPS = 4
EXPERTS_PER_GROUP = 4
D_FF = 512
N_ADA = 6
EPS = 1e-6
NEG = -1e30

FOX_Q = 0
FOX_F = 3 * FOX_WIDTH
ML_Q = FOX_F + FOX_HEADS
ML_I = ML_Q + 3 * ML_WIDTH
ML_F = ML_I + ML_HEADS
ML_O = ML_F + ML_HEADS
IN_COLS = ML_O + ML_WIDTH

MAIN_COLS = 7 * 512
BLK_FQ, BLK_FK, BLK_FV, BLK_MQ, BLK_MK, BLK_MV, BLK_MO = range(7)
REST_COLS = 3 * 512
OUT_MQ, OUT_MV, OUT_MO = range(3)
AUG_COLS = FOX_HEADS * 128
BIAS_TERMS = 3
VT_PAD = 16
LOG2E = 1.4426950408889634
GATE_FOX = 0
GATE_ML = FOX_HEADS
GATE_ROWS = 16

TOK_ROWS = D_MODEL // LANES


def _cparams(n_grid, vmem_mb):
    return pltpu.CompilerParams(
        dimension_semantics=("arbitrary",) * n_grid,
        vmem_limit_bytes=vmem_mb * 1024 * 1024)


def _silu(x):
    return x * jax.nn.sigmoid(x)


def _log_sigmoid(z):
    return jnp.minimum(z, 0.0) - jnp.log1p(jnp.exp(-jnp.abs(z)))


def _cumsum_rows(x):
    n = x.shape[0]
    row = lax.broadcasted_iota(jnp.int32, x.shape, 0)
    s = 1
    while s < n:
        x = x + jnp.where(row >= s, pltpu.roll(x, s, axis=0), 0.0)
        s *= 2
    return x


def _from_token_tiles(ref, n_tok):
    return jnp.concatenate(
        [ref[pl.ds(c, n_tok, stride=TOK_ROWS), :] for c in range(TOK_ROWS)], axis=1)


def _to_token_tiles(ref, val, row0=0):
    n = val.shape[0]
    for c in range(TOK_ROWS):
        ref[pl.ds(row0 * TOK_ROWS + c, n, stride=TOK_ROWS), :] = val[:, c * LANES:(c + 1) * LANES]


def _ada_kernel(c_ref, w_ref, b_ref, o_ref):
    c = c_ref[...]
    o_ref[0] = jnp.dot(_silu(c), w_ref[0], preferred_element_type=F32,
                       precision=lax.Precision.HIGHEST) + b_ref[0]


def _ada_call(c, w_ada, b_ada):
    depth, d, n = w_ada.shape
    bsz = c.shape[0]
    tn = 1536
    return pl.pallas_call(
        _ada_kernel,
        grid=(depth, n // tn),
        in_specs=[
            pl.BlockSpec((bsz, d), lambda l, j: (0, 0)),
            pl.BlockSpec((1, d, tn), lambda l, j: (l, 0, j)),
            pl.BlockSpec((1, 1, tn), lambda l, j: (l, 0, j)),
        ],
        out_specs=pl.BlockSpec((1, bsz, tn), lambda l, j: (l, 0, j)),
        out_shape=jax.ShapeDtypeStruct((depth, bsz, n), F32),
        compiler_params=_cparams(2, 32),
        name="ada_mod",
    )(c, w_ada, b_ada.reshape(depth, 1, n))


def _in_proj_kernel(*refs, tm, combine):
    if combine:
        (x_ref, ott_ref, g2_ref, nm_ref, sc_ref, sh_ref, wm_ref, wg_ref, gb_ref, cw_ref, cb_ref,
         place_ref, ones_ref,
         xn_ref, qa_ref, ka_ref, vt_ref, qkv_ref, kt_ref, fc_ref, gr_ref, fcar, ccar) = refs
    else:
        (x_ref, nm_ref, sc_ref, sh_ref, wm_ref, wg_ref, gb_ref, cw_ref, cb_ref,
         place_ref, ones_ref,
         qa_ref, ka_ref, vt_ref, qkv_ref, kt_ref, fc_ref, gr_ref, fcar, ccar) = refs

    @pl.when(pl.program_id(1) == 0)
    def _():
        fcar[...] = jnp.zeros_like(fcar)
        ccar[...] = jnp.zeros_like(ccar)

    x = x_ref[...]
    if combine:
        x = x + g2_ref[0] * _from_token_tiles(ott_ref, tm)
        xn_ref[...] = x
    ms = jnp.mean(x * x, axis=1, keepdims=True)
    hn = x * lax.rsqrt(ms + EPS) * nm_ref[...]
    hn = hn * (1.0 + sc_ref[0]) + sh_ref[0]
    hb = hn.astype(BF16)

    gp = jnp.dot(hb, wg_ref[0], preferred_element_type=F32) + gb_ref[...]
    lf = _log_sigmoid(gp[:, :LANES])
    fcum = _cumsum_rows(lf) + fcar[...]
    fcar[...] = fcum[tm - 1:tm, :]
    gml = gp[:, LANES:] - fcum
    fc_ref[...] = fcum
    gr_ref[0] = gml.T[:GATE_ROWS, :]

    def mm(j):
        return jnp.dot(hb, wm_ref[0, :, j * 512:(j + 1) * 512], preferred_element_type=F32)

    def put(j, v):
        qkv_ref[:, j * 512:(j + 1) * 512] = v.astype(BF16)

    lane = lax.broadcasted_iota(jnp.int32, (tm, LANES), 1)
    fs = fcum * LOG2E
    hi = fs.astype(BF16).astype(F32)
    mid = (fs - hi).astype(BF16).astype(F32)
    low = (fs - hi - mid).astype(BF16).astype(F32)
    packed = jnp.where(lane < FOX_HEADS, hi,
                       jnp.where(lane < 2 * FOX_HEADS, pltpu.roll(mid, FOX_HEADS, axis=1),
                                 jnp.where(lane < 3 * FOX_HEADS, pltpu.roll(low, 2 * FOX_HEADS, axis=1), 0.0)))
    bias = jnp.dot(packed.astype(BF16), place_ref[...], preferred_element_type=F32) + ones_ref[...]

    def put_heads(ref, val, col0, transposed):
        lo = lane < FOX_HEAD_DIM
        for p in range(FOX_HEADS // 2):
            slab = val[:, p * LANES:(p + 1) * LANES]
            for h, data in ((2 * p, slab), (2 * p + 1, pltpu.roll(slab, FOX_HEAD_DIM, axis=1))):
                blk = jnp.where(lo, data, 0.0) + bias[:, col0 + h * LANES:col0 + (h + 1) * LANES]
                if transposed:
                    ref[0, 0, h * LANES:(h + 1) * LANES, :] = blk.T.astype(BF16)
                else:
                    ref[:, h * LANES:(h + 1) * LANES] = blk.astype(BF16)

    u = jnp.concatenate([mm(BLK_MQ), mm(BLK_MK)], axis=1)
    prev = ccar[...]
    ccar[...] = u[tm - SUBLANES:tm, :]
    row8 = lax.broadcasted_iota(jnp.int32, prev.shape, 0)
    y = cb_ref[...] + cw_ref[CONV_WIDTH - 1:CONV_WIDTH, :] * u
    for k in range(1, CONV_WIDTH):
        r = pltpu.roll(u, k, axis=0)
        top = jnp.where(row8 < k, pltpu.roll(prev, k, axis=0), r[:SUBLANES])
        shifted = jnp.concatenate([top, r[SUBLANES:]], axis=0)
        y = y + cw_ref[CONV_WIDTH - 1 - k:CONV_WIDTH - k, :] * shifted
    act = _silu(y)
    put(OUT_MQ, act[:, :ML_WIDTH])
    kt_ref[0, 0] = (act[:, ML_WIDTH:] * (ML_HEAD_DIM ** -0.5)).T.astype(BF16)

    put(OUT_MO, jax.nn.sigmoid(mm(BLK_MO)))
    put(OUT_MV, mm(BLK_MV))
    vt_ref[0, 0] = mm(BLK_FV).astype(BF16).T
    put_heads(qa_ref, mm(BLK_FQ) * (FOX_HEAD_DIM ** -0.5 * LOG2E), 0, True)
    put_heads(ka_ref, mm(BLK_FK), AUG_COLS, False)


def _bias_placement():
    place = np.zeros((LANES, 2 * AUG_COLS), np.float32)
    ones = np.zeros((1, 2 * AUG_COLS), np.float32)
    for h in range(FOX_HEADS):
        for term in range(BIAS_TERMS):
            src = term * FOX_HEADS + h
            place[src, h * LANES + FOX_HEAD_DIM + term] = 1.0
            ones[0, h * LANES + FOX_HEAD_DIM + BIAS_TERMS + term] = 1.0
            place[src, AUG_COLS + h * LANES + FOX_HEAD_DIM + BIAS_TERMS + term] = -1.0
            ones[0, AUG_COLS + h * LANES + FOX_HEAD_DIM + term] = 1.0
    return jnp.asarray(place, BF16), jnp.asarray(ones, F32)


def _in_proj_call(x, moe_tt, g2, nm, sc, sh, wm, wg, gb, cw, cb, *, layer, bsz, seq, tm=512):
    t, d = x.shape
    ns = seq // tm
    combine = moe_tt is not None
    row = lambda b, s: (b * ns + s, 0)
    per_b = lambda b, s: (b, 0, 0)
    const2 = lambda b, s: (0, 0)
    in_specs = [pl.BlockSpec((tm, d), row)]
    args = [x]
    if combine:
        in_specs += [pl.BlockSpec((tm * TOK_ROWS, LANES), row), pl.BlockSpec((1, 1, d), per_b)]
        args += [moe_tt, g2]
    in_specs += [
        pl.BlockSpec((1, d), const2),
        pl.BlockSpec((1, 1, d), per_b),
        pl.BlockSpec((1, 1, d), per_b),
        pl.BlockSpec((1, d, MAIN_COLS), lambda b, s: (layer, 0, 0)),
        pl.BlockSpec((1, d, 2 * LANES), lambda b, s: (layer, 0, 0)),
        pl.BlockSpec((1, 2 * LANES), const2),
        pl.BlockSpec((CONV_WIDTH, 2 * ML_WIDTH), const2),
        pl.BlockSpec((1, 2 * ML_WIDTH), const2),
        pl.BlockSpec((LANES, 2 * AUG_COLS), const2),
        pl.BlockSpec((1, 2 * AUG_COLS), const2),
    ]
    place, ones = _bias_placement()
    args += [nm, sc, sh, wm, wg, gb, cw, cb, place, ones]
    out_specs = []
    out_shape = []
    if combine:
        out_specs.append(pl.BlockSpec((tm, d), row))
        out_shape.append(jax.ShapeDtypeStruct((t, d), F32))
    tiled = lambda b, s: (b, s, 0, 0)
    out_specs += [
        pl.BlockSpec((1, 1, AUG_COLS, tm), tiled),
        pl.BlockSpec((tm, AUG_COLS), row),
        pl.BlockSpec((1, 1, FOX_WIDTH, tm), tiled),
        pl.BlockSpec((tm, REST_COLS), row),
        pl.BlockSpec((1, 1, ML_WIDTH, tm), tiled),
        pl.BlockSpec((tm, LANES), row),
        pl.BlockSpec((1, GATE_ROWS, tm), lambda b, s: (b, 0, s)),
    ]
    out_shape += [
        jax.ShapeDtypeStruct((bsz, ns, AUG_COLS, tm), BF16),
        jax.ShapeDtypeStruct((t, AUG_COLS), BF16),
        jax.ShapeDtypeStruct((bsz, ns, FOX_WIDTH, tm), BF16),
        jax.ShapeDtypeStruct((t, REST_COLS), BF16),
        jax.ShapeDtypeStruct((bsz, ns, ML_WIDTH, tm), BF16),
        jax.ShapeDtypeStruct((t, LANES), F32),
        jax.ShapeDtypeStruct((bsz, GATE_ROWS, seq), F32),
    ]
    outs = pl.pallas_call(
        functools.partial(_in_proj_kernel, tm=tm, combine=combine),
        grid=(bsz, ns),
        in_specs=in_specs,
        out_specs=out_specs,
        out_shape=out_shape,
        scratch_shapes=[pltpu.VMEM((1, LANES), F32), pltpu.VMEM((SUBLANES, 2 * ML_WIDTH), F32)],
        compiler_params=_cparams(2, 48),
        name="in_proj",
    )(*args)
    if combine:
        return outs
    return [x] + list(outs)


def _fox_kernel(qt_ref, k_ref, vt_ref, ng_ref, cm_ref, o_ref, m_sc, acc_sc, *, tq, tk, nh):
    qi = pl.program_id(2)
    ones_rows = jnp.where(lax.broadcasted_iota(jnp.int32, (VT_PAD, tk), 0) == 0, 1.0, 0.0).astype(BF16)
    for n in range(2 * nh):
        m_sc[n] = jnp.full((SUBLANES, tq // 2), NEG, F32)
        acc_sc[n] = jnp.zeros((FOX_HEAD_DIM + VT_PAD, tq // 2), F32)

    half = tk // 2
    qh = tq // 2

    def update(n, z, va):
        m_prev = m_sc[n]
        m_new = jnp.maximum(m_prev, jnp.max(z, axis=0, keepdims=True))
        alpha = jnp.exp2(m_prev - m_new)
        p = jnp.exp2(z - m_new[:1, :])
        acc_sc[n] = alpha[:1, :] * acc_sc[n] + jnp.dot(va, p.astype(BF16), preferred_element_type=F32)
        m_sc[n] = m_new

    def v_aug(j, h):
        return jnp.concatenate(
            [vt_ref[0, j, h * FOX_HEAD_DIM:(h + 1) * FOX_HEAD_DIM, :], ones_rows], axis=0)

    def below_diagonal(j, carry):
        k0 = pl.multiple_of(j * tk, tk)
        items = [(h, c) for h in range(nh) for c in range(2)]

        def logits(item):
            h, c = item
            hs = slice(h * LANES, (h + 1) * LANES)
            return jnp.dot(k_ref[pl.ds(k0, tk), hs], qt_ref[0, 0, hs, :][:, c * qh:(c + 1) * qh],
                           preferred_element_type=F32)

        ahead = 4
        zs = [logits(it) for it in items[:ahead]]
        for n, (h, c) in enumerate(items):
            z = zs[n]
            if n + ahead < len(items):
                zs.append(logits(items[n + ahead]))
            update(n, z, v_aug(j, h))
        return carry

    def diagonal(j, carry):
        k0 = pl.multiple_of(j * tk, tk)

        def logits(h):
            hs = slice(h * LANES, (h + 1) * LANES)
            qt = qt_ref[0, 0, hs, :]
            za = jnp.dot(k_ref[pl.ds(k0, half), hs], qt, preferred_element_type=F32) + cm_ref[:half, :]
            zb = (jnp.dot(k_ref[pl.ds(k0 + half, half), hs], qt[:, half:], preferred_element_type=F32)
                  + cm_ref[half:, :][:, half:])
            zb = jnp.concatenate([jnp.full((half, half), NEG, F32), zb], axis=1)
            return jnp.concatenate([za, zb], axis=0)

        ahead = 2
        zs = [logits(h) for h in range(min(ahead, nh))]
        for h in range(nh):
            z = zs[h]
            if h + ahead < nh:
                zs.append(logits(h + ahead))
            va = v_aug(j, h)
            for c in range(2):
                update(2 * h + c, z[:, c * qh:(c + 1) * qh], va)
        return carry

    lax.fori_loop(0, qi, below_diagonal, 0)
    lax.fori_loop(qi, qi + 1, diagonal, 0)

    for p in range(nh // 2):
        outs = []
        for h in (2 * p, 2 * p + 1):
            acc = jnp.concatenate([acc_sc[2 * h], acc_sc[2 * h + 1]], axis=1)
            num = acc[:FOX_HEAD_DIM, :]
            l = acc[FOX_HEAD_DIM:FOX_HEAD_DIM + 1, :]
            ms = jnp.mean(num * num, axis=0, keepdims=True)
            outs.append(num * lax.rsqrt(ms + EPS * l * l))
        ps = slice(p * LANES, (p + 1) * LANES)
        o_ref[:, ps] = (jnp.concatenate(outs, axis=0).T * ng_ref[:, ps]).astype(BF16)


def _fox_call(qat, ka, fvt, ng, *, bsz, seq, tq=512, nh=8):
    t = ka.shape[0]
    nq = seq // tq
    ngrp = FOX_HEADS // nh
    vw = nh * FOX_HEAD_DIM
    cmask = jnp.where(jnp.arange(tq)[:, None] <= jnp.arange(tq)[None, :], 0.0, NEG).astype(F32)
    return pl.pallas_call(
        functools.partial(_fox_kernel, tq=tq, tk=tq, nh=nh),
        grid=(bsz, ngrp, nq),
        in_specs=[
            pl.BlockSpec((1, 1, nh * LANES, tq), lambda b, p, i: (b, i, p, 0)),
            pl.BlockSpec((seq, nh * LANES), lambda b, p, i: (b, p)),
            pl.BlockSpec((1, nq, vw, tq), lambda b, p, i: (b, 0, p, 0)),
            pl.BlockSpec((1, vw), lambda b, p, i: (0, p)),
            pl.BlockSpec((tq, tq), lambda b, p, i: (0, 0)),
        ],
        out_specs=pl.BlockSpec((tq, vw), lambda b, p, i: (b * nq + i, p)),
        out_shape=jax.ShapeDtypeStruct((t, FOX_WIDTH), BF16),
        scratch_shapes=[pltpu.VMEM((2 * nh, SUBLANES, tq // 2), F32),
                        pltpu.VMEM((2 * nh, FOX_HEAD_DIM + VT_PAD, tq // 2), F32)],
        compiler_params=_cparams(3, 48),
        name="fox_attn",
    )(qat, ka, fvt, ng, cmask)


def _mlstm_kernel(q_ref, kt_ref, v_ref, og_ref, fc_ref, gr_ref, ng_ref, o_ref, ct_sc, u_sc, *, ch, per_tile):
    @pl.when(pl.program_id(1) == 0)
    def _():
        ct_sc[...] = jnp.zeros_like(ct_sc)
        u_sc[...] = jnp.zeros_like(u_sc)

    causal = (lax.broadcasted_iota(jnp.int32, (ch, ch), 1)
              <= lax.broadcasted_iota(jnp.int32, (ch, ch), 0))
    lane = lax.broadcasted_iota(jnp.int32, (ch, LANES), 1)
    fc = fc_ref[...]
    koff = pl.multiple_of((pl.program_id(1) % per_tile) * ch, ch)
    for h in range(ML_HEADS):
        sl = slice(h * ML_HEAD_DIM, (h + 1) * ML_HEAD_DIM)
        gl = GATE_ML + h
        q = q_ref[:, sl]
        kt = kt_ref[0, 0, sl, pl.ds(koff, ch)]
        vp = jnp.concatenate([v_ref[:, sl], jnp.where(lane == gl, 1.0, 0.0).astype(BF16)], axis=1)
        g_row = gr_ref[0, gl:gl + 1, :]
        u_prev = u_sc[h][:, :1]
        gm = jnp.where(causal, g_row, NEG)
        u_i = jnp.maximum(u_prev, jnp.max(gm, axis=1, keepdims=True))
        dmat = jnp.exp(gm - u_i)
        s = jnp.dot(q, kt, preferred_element_type=F32)
        scores = (s * dmat).astype(BF16)
        inter = jnp.exp(u_prev - u_i)
        ct = ct_sc[h]
        nd = (jnp.dot(scores, vp, preferred_element_type=F32)
              + jnp.dot(q, ct.astype(BF16), preferred_element_type=F32) * inter)
        num = nd[:, :ML_HEAD_DIM]
        den = jnp.maximum(jnp.abs(nd[:, ML_HEAD_DIM:]), jnp.exp(-(jnp.where(lane == gl, fc, 0.0) + u_i)))
        ms = jnp.mean(num * num, axis=1, keepdims=True)
        scale = lax.rsqrt(ms + EPS * den * den)[:, gl:gl + 1]
        y = num * scale * ng_ref[:, sl] * og_ref[:, sl].astype(F32)
        o_ref[:, sl] = y.astype(BF16)
        u_new = jnp.maximum(u_prev, jnp.max(g_row, axis=1, keepdims=True))
        ktw = (kt.astype(F32) * jnp.exp(g_row - u_new)).astype(BF16)
        ct_sc[h] = jnp.exp(u_prev - u_new) * ct + jnp.dot(ktw, vp, preferred_element_type=F32)
        u_sc[h] = jnp.broadcast_to(u_new, (1, LANES))


def _mlstm_call(qkv, mkt, fc, gr, ng, *, bsz, seq, ch=512):
    t = qkv.shape[0]
    nc = seq // ch
    ktile = mkt.shape[-1]
    per_tile = ktile // ch
    row = lambda b, c: (b * nc + c, 0)
    return pl.pallas_call(
        functools.partial(_mlstm_kernel, ch=ch, per_tile=per_tile),
        grid=(bsz, nc),
        in_specs=[
            pl.BlockSpec((ch, ML_WIDTH), lambda b, c: (b * nc + c, OUT_MQ)),
            pl.BlockSpec((1, 1, ML_WIDTH, ktile), lambda b, c: (b, c // per_tile, 0, 0)),
            pl.BlockSpec((ch, ML_WIDTH), lambda b, c: (b * nc + c, OUT_MV)),
            pl.BlockSpec((ch, ML_WIDTH), lambda b, c: (b * nc + c, OUT_MO)),
            pl.BlockSpec((ch, LANES), row),
            pl.BlockSpec((1, GATE_ROWS, ch), lambda b, c: (b, 0, c)),
            pl.BlockSpec((1, ML_WIDTH), lambda b, c: (0, 0)),
        ],
        out_specs=pl.BlockSpec((ch, ML_WIDTH), row),
        out_shape=jax.ShapeDtypeStruct((t, ML_WIDTH), BF16),
        scratch_shapes=[pltpu.VMEM((ML_HEADS, ML_HEAD_DIM, 2 * ML_HEAD_DIM), F32),
                        pltpu.VMEM((ML_HEADS, 1, LANES), F32)],
        compiler_params=_cparams(2, 32),
        name="mlstm",
    )(qkv, mkt, qkv, qkv, fc, gr, ng)


def _post_kernel(x_ref, hf_ref, hm_ref, wo_ref, g1_ref, nf_ref, sc_ref, sh_ref, wr_ref, br_ref, tri_ref,
                 x1_ref, hn_ref, gsel_ref, grp_ref, rank_ref, tot_ref, cnt_sc, *, te, steps_per_chunk):
    @pl.when(pl.program_id(0) % steps_per_chunk == 0)
    def _():
        cnt_sc[...] = jnp.zeros_like(cnt_sc)

    mix = (jnp.dot(hf_ref[...], wo_ref[0, :FOX_WIDTH, :], preferred_element_type=F32)
           + jnp.dot(hm_ref[...], wo_ref[0, FOX_WIDTH:, :], preferred_element_type=F32))
    x1 = x_ref[...] + g1_ref[0] * mix
    x1_ref[...] = x1
    ms = jnp.mean(x1 * x1, axis=1, keepdims=True)
    hn = x1 * lax.rsqrt(ms + EPS) * nf_ref[...]
    hn = hn * (1.0 + sc_ref[0]) + sh_ref[0]
    _to_token_tiles(hn_ref, hn)

    hn_hi = hn.astype(BF16)
    hn_lo = (hn - hn_hi.astype(F32)).astype(BF16)
    l2 = jnp.dot(hn_hi, wr_ref[...], preferred_element_type=F32)
    logits = (l2[:, :LANES] + l2[:, LANES:]
              + jnp.dot(hn_lo, wr_ref[:, :LANES], preferred_element_type=F32))
    aff = jax.nn.sigmoid(logits.T[:N_EXPERTS, :])
    sel = aff + br_ref[...]
    selr = [sel[e:e + 1, :] for e in range(N_EXPERTS)]
    affr = [aff[e:e + 1, :] for e in range(N_EXPERTS)]
    keep = [None] * N_EXPERTS
    score = []
    for g in range(N_GROUPS):
        vs = selr[g * EXPERTS_PER_GROUP:(g + 1) * EXPERTS_PER_GROUP]
        sg = jnp.zeros_like(vs[0])
        for i in range(EXPERTS_PER_GROUP):
            beaten = jnp.zeros_like(vs[0])
            for j in range(EXPERTS_PER_GROUP):
                if j != i:
                    b = (vs[j] >= vs[i]) if j < i else (vs[j] > vs[i])
                    beaten = beaten + jnp.where(b, 1.0, 0.0)
            kp = beaten < 2.0
            keep[g * EXPERTS_PER_GROUP + i] = kp
            sg = sg + jnp.where(kp, vs[i], 0.0)
        score.append(sg)
    chosen = []
    for g in range(N_GROUPS):
        lost = jnp.zeros_like(score[0])
        for g2 in range(N_GROUPS):
            if g2 != g:
                b = (score[g2] >= score[g]) if g2 < g else (score[g2] > score[g])
                lost = lost + jnp.where(b, 1.0, 0.0)
        chosen.append(jnp.where(lost < 0.5, 1.0, 0.0))
    wsel = []
    for i in range(EXPERTS_PER_GROUP):
        wi = jnp.zeros_like(score[0])
        for g in range(N_GROUPS):
            e = g * EXPERTS_PER_GROUP + i
            wi = wi + chosen[g] * jnp.where(keep[e], affr[e], 0.0)
        wsel.append(wi)
    wsum = wsel[0] + wsel[1] + wsel[2] + wsel[3]
    wsel = [w / wsum for w in wsel]

    row8 = lax.broadcasted_iota(jnp.int32, (SUBLANES, te), 0)
    gmat = jnp.zeros((SUBLANES, te), F32)
    wmat = jnp.zeros((SUBLANES, te), F32)
    grp = jnp.zeros_like(score[0])
    for g in range(N_GROUPS):
        gmat = jnp.where(row8 == g, chosen[g], gmat)
        wmat = jnp.where(row8 == g, wsel[g], wmat)
        grp = grp + g * chosen[g]
    pref = jnp.dot(gmat.astype(BF16), tri_ref[...], preferred_element_type=F32) + cnt_sc[:, :1]
    rank = jnp.sum(gmat * pref, axis=0, keepdims=True)
    tot = cnt_sc[:, :1] + jnp.sum(gmat, axis=1, keepdims=True)
    cnt_sc[...] = jnp.broadcast_to(tot, cnt_sc.shape)
    rank_ref[0] = rank.astype(jnp.int32)
    grp_ref[0] = grp.astype(jnp.int32)
    tot_ref[0] = jnp.broadcast_to(tot, (SUBLANES, LANES)).astype(jnp.int32)
    wfull = jnp.concatenate([wmat, jnp.zeros((LANES - SUBLANES, te), F32)], axis=0)
    gsel_ref[...] = wfull.T


def _post_call(x, hf, hm, wo, g1, nf, sc, sh, wr, br, tri, *, layer, bsz, seq, chunk, te=512):
    t, d = x.shape
    steps_per_chunk = chunk // te
    per_seq = seq // te
    n_steps = t // te
    n_chunks = t // chunk
    row = lambda i: (i, 0)
    per_b = lambda i: (i // per_seq, 0, 0)
    const2 = lambda i: (0, 0)
    return pl.pallas_call(
        functools.partial(_post_kernel, te=te, steps_per_chunk=steps_per_chunk),
        grid=(n_steps,),
        in_specs=[
            pl.BlockSpec((te, d), row),
            pl.BlockSpec((te, FOX_WIDTH), row),
            pl.BlockSpec((te, ML_WIDTH), row),
            pl.BlockSpec((1, d, d), lambda i: (layer, 0, 0)),
            pl.BlockSpec((1, 1, d), per_b),
            pl.BlockSpec((1, d), const2),
            pl.BlockSpec((1, 1, d), per_b),
            pl.BlockSpec((1, 1, d), per_b),
            pl.BlockSpec((d, 2 * LANES), const2),
            pl.BlockSpec((N_EXPERTS, 1), const2),
            pl.BlockSpec((te, te), const2),
        ],
        out_specs=[
            pl.BlockSpec((te, d), row),
            pl.BlockSpec((te * TOK_ROWS, LANES), row),
            pl.BlockSpec((te, LANES), row),
            pl.BlockSpec((1, 1, te), lambda i: (i, 0, 0)),
            pl.BlockSpec((1, 1, te), lambda i: (i, 0, 0)),
            pl.BlockSpec((1, SUBLANES, LANES), lambda i: (i // steps_per_chunk, 0, 0)),
        ],
        out_shape=[
            jax.ShapeDtypeStruct((t, d), F32),
            jax.ShapeDtypeStruct((t * TOK_ROWS, LANES), F32),
            jax.ShapeDtypeStruct((t, LANES), F32),
            jax.ShapeDtypeStruct((n_steps, 1, te), jnp.int32),
            jax.ShapeDtypeStruct((n_steps, 1, te), jnp.int32),
            jax.ShapeDtypeStruct((n_chunks, SUBLANES, LANES), jnp.int32),
        ],
        scratch_shapes=[pltpu.VMEM((SUBLANES, LANES), F32)],
        compiler_params=_cparams(1, 48),
        name="post_router",
    )(x, hf, hm, wo, g1, nf, sc, sh, wr, br, tri)


def _scatter_kernel(pos_ref, hn_ref, gsel_ref, xb_ref, gs_ref, xs_sc, *, chunk, rows, tm):
    xs_sc[...] = jnp.zeros_like(xs_sc)
    gs_ref[...] = jnp.zeros_like(gs_ref)

    def body(t, carry):
        p = pos_ref[0, 0, t]
        src = pl.multiple_of(t * TOK_ROWS, TOK_ROWS)
        dst = pl.multiple_of(p * TOK_ROWS, TOK_ROWS)
        xs_sc[pl.ds(dst, TOK_ROWS), :] = hn_ref[pl.ds(src, TOK_ROWS), :]
        gs_ref[pl.ds(p, 1), :] = gsel_ref[pl.ds(t, 1), :]
        return carry

    lax.fori_loop(0, chunk, body, 0, unroll=8)
    for j in range(rows // tm):
        for c in range(TOK_ROWS):
            xb_ref[j * tm:(j + 1) * tm, c * LANES:(c + 1) * LANES] = (
                xs_sc[pl.ds(j * tm * TOK_ROWS + c, tm, stride=TOK_ROWS), :].astype(BF16))


def _scatter_call(pos, hn_tt, gsel, *, chunk, rows, tm):
    t = gsel.shape[0]
    n_chunks = t // chunk
    return pl.pallas_call(
        functools.partial(_scatter_kernel, chunk=chunk, rows=rows, tm=tm),
        grid=(n_chunks,),
        in_specs=[
            pl.BlockSpec((1, 1, chunk), lambda c: (c, 0, 0), memory_space=pltpu.SMEM),
            pl.BlockSpec((chunk * TOK_ROWS, LANES), lambda c: (c, 0)),
            pl.BlockSpec((chunk, LANES), lambda c: (c, 0)),
        ],
        out_specs=[
            pl.BlockSpec((rows, D_MODEL), lambda c: (c, 0)),
            pl.BlockSpec((rows, LANES), lambda c: (c, 0)),
        ],
        out_shape=[
            jax.ShapeDtypeStruct((n_chunks * rows, D_MODEL), BF16),
            jax.ShapeDtypeStruct((n_chunks * rows, LANES), F32),
        ],
        scratch_shapes=[pltpu.VMEM((rows * TOK_ROWS, LANES), F32)],
        compiler_params=_cparams(1, 56),
        name="moe_scatter",
    )(pos, hn_tt, gsel)


def _experts_kernel(blk_ref, grp_ref, xb_ref, gs_ref, wg_ref, wu_ref, wd_ref, y_ref, *, tm):
    g = grp_ref[pl.program_id(0)]

    @pl.when(g < N_GROUPS)
    def _():
        x = xb_ref[...]
        gs = gs_ref[...]
        acts = []
        for i in range(EXPERTS_PER_GROUP):
            hg = jnp.dot(x, wg_ref[i], preferred_element_type=F32)
            hu = jnp.dot(x, wu_ref[i], preferred_element_type=F32)
            acts.append((_silu(hg) * hu * gs[:, i:i + 1]).astype(BF16))
        y = jnp.dot(jnp.concatenate(acts, axis=1), wd_ref[0], preferred_element_type=F32)
        _to_token_tiles(y_ref, y)

    @pl.when(g >= N_GROUPS)
    def _():
        y_ref[...] = jnp.zeros_like(y_ref)


def _experts_call(tile_blk, tile_grp, xb, gs, wg, wu, wd, *, layer, tm):
    n_rows = xb.shape[0]
    n_slots = n_rows // tm
    wmap = lambda s, blk, grp: (layer * N_GROUPS + jnp.minimum(grp[s], N_GROUPS - 1), 0, 0)
    grid_spec = pltpu.PrefetchScalarGridSpec(
        num_scalar_prefetch=2,
        grid=(n_slots,),
        in_specs=[
            pl.BlockSpec((tm, D_MODEL), lambda s, blk, grp: (blk[s], 0)),
            pl.BlockSpec((tm, LANES), lambda s, blk, grp: (blk[s], 0)),
            pl.BlockSpec((EXPERTS_PER_GROUP, D_MODEL, D_FF), wmap),
            pl.BlockSpec((EXPERTS_PER_GROUP, D_MODEL, D_FF), wmap),
            pl.BlockSpec((1, EXPERTS_PER_GROUP * D_FF, D_MODEL), wmap),
        ],
        out_specs=pl.BlockSpec((tm * TOK_ROWS, LANES), lambda s, blk, grp: (blk[s], 0)),
    )
    return pl.pallas_call(
        functools.partial(_experts_kernel, tm=tm),
        grid_spec=grid_spec,
        out_shape=jax.ShapeDtypeStruct((n_rows * TOK_ROWS, LANES), F32),
        compiler_params=_cparams(1, 48),
        name="moe_experts",
    )(tile_blk, tile_grp, xb, gs, wg, wu, wd)


def _gather_kernel(pos_ref, ys_ref, o_ref, *, chunk):
    def body(t, carry):
        p = pos_ref[0, 0, t]
        src = pl.multiple_of(p * TOK_ROWS, TOK_ROWS)
        dst = pl.multiple_of(t * TOK_ROWS, TOK_ROWS)
        o_ref[pl.ds(dst, TOK_ROWS), :] = ys_ref[pl.ds(src, TOK_ROWS), :]
        return carry

    lax.fori_loop(0, chunk, body, 0, unroll=8)


def _gather_call(pos, ys_tt, *, chunk, rows, n_tok):
    n_chunks = n_tok // chunk
    return pl.pallas_call(
        functools.partial(_gather_kernel, chunk=chunk),
        grid=(n_chunks,),
        in_specs=[
            pl.BlockSpec((1, 1, chunk), lambda c: (c, 0, 0), memory_space=pltpu.SMEM),
            pl.BlockSpec((rows * TOK_ROWS, LANES), lambda c: (c, 0)),
        ],
        out_specs=pl.BlockSpec((chunk * TOK_ROWS, LANES), lambda c: (c, 0)),
        out_shape=jax.ShapeDtypeStruct((n_tok * TOK_ROWS, LANES), F32),
        compiler_params=_cparams(1, 56),
        name="moe_gather",
    )(pos, ys_tt)


def _gather_final_kernel(pos_ref, ys_ref, x_ref, g2_ref, nf_ref, o_ref, o_sc, *, part, sub):
    base = pl.program_id(1) * part

    def body(t, carry):
        p = pos_ref[0, 0, base + t]
        src = pl.multiple_of(p * TOK_ROWS, TOK_ROWS)
        dst = pl.multiple_of(t * TOK_ROWS, TOK_ROWS)
        o_sc[pl.ds(dst, TOK_ROWS), :] = ys_ref[pl.ds(src, TOK_ROWS), :]
        return carry

    lax.fori_loop(0, part, body, 0, unroll=8)
    for r in range(part // sub):
        rs = slice(r * sub, (r + 1) * sub)
        moe = jnp.concatenate(
            [o_sc[pl.ds(r * sub * TOK_ROWS + c, sub, stride=TOK_ROWS), :] for c in range(TOK_ROWS)], axis=1)
        x = x_ref[rs, :] + g2_ref[0] * moe
        ms = jnp.mean(x * x, axis=1, keepdims=True)
        o_ref[rs, :] = x * lax.rsqrt(ms + EPS) * nf_ref[...]


def _gather_final_call(pos, ys_tt, x, g2, nf, *, chunk, rows, seq):
    n_tok, d = x.shape
    n_chunks = n_tok // chunk
    parts = 2
    part = chunk // parts
    return pl.pallas_call(
        functools.partial(_gather_final_kernel, part=part, sub=256),
        grid=(n_chunks, parts),
        in_specs=[
            pl.BlockSpec((1, 1, chunk), lambda c, s: (c, 0, 0), memory_space=pltpu.SMEM),
            pl.BlockSpec((rows * TOK_ROWS, LANES), lambda c, s: (c, 0)),
            pl.BlockSpec((part, d), lambda c, s: (c * parts + s, 0)),
            pl.BlockSpec((1, 1, d), lambda c, s: ((c * chunk + s * part) // seq, 0, 0)),
            pl.BlockSpec((1, d), lambda c, s: (0, 0)),
        ],
        out_specs=pl.BlockSpec((part, d), lambda c, s: (c * parts + s, 0)),
        out_shape=jax.ShapeDtypeStruct((n_tok, d), F32),
        scratch_shapes=[pltpu.VMEM((part * TOK_ROWS, LANES), F32)],
        compiler_params=_cparams(2, 56),
        name="moe_gather_final",
    )(pos, ys_tt, x, g2, nf)


def _moe_tiles(tot, *, tm, tiles_per_chunk):
    nt = (tot + tm - 1) // tm
    ts = jnp.cumsum(nt, axis=1) - nt
    off = (ts * tm).astype(jnp.int32)
    j = jnp.arange(tiles_per_chunk, dtype=jnp.int32)[None, :, None]
    inside = (j >= ts[:, None, :]) & (j < (ts + nt)[:, None, :])
    key = jnp.where(jnp.any(inside, axis=-1), jnp.argmax(inside, axis=-1), N_GROUPS).reshape(-1)
    order = jnp.argsort(key, stable=True).astype(jnp.int32)
    return off, order, key[order].astype(jnp.int32)


def kernel(x, c, w_in, conv_w, conv_b, fox_f_bias, mlstm_i_bias, mlstm_f_bias, fox_out_norm,
           mlstm_out_norm, w_out, w_ada, b_ada, norm_mix, norm_ffn, w_router, b_router, w_gate,
           w_up, w_down, norm_final):
    bsz, seq, d = x.shape
    depth = w_in.shape[0]
    t = bsz * seq
    assert d == D_MODEL and w_in.shape[-1] == IN_COLS
    chunk = min(2048, seq)
    tm = 256
    assert chunk % tm == 0
    rows = chunk + (N_GROUPS - 1) * tm
    te = min(1024, seq)
    assert seq % 512 == 0 and seq % chunk == 0 and chunk % te == 0

    mods = _ada_call(c, w_ada, b_ada).reshape(depth, bsz, N_ADA, 1, d)
    wm_all = jnp.concatenate([w_in[:, :, FOX_Q:FOX_F], w_in[:, :, ML_Q:ML_I], w_in[:, :, ML_O:IN_COLS]],
                             axis=2).astype(BF16)
    zpad = lambda n: jnp.zeros((depth, d, n), F32)
    wg_all = jnp.concatenate([
        w_in[:, :, FOX_F:ML_Q], w_in[:, :, ML_F:ML_O], zpad(LANES - FOX_HEADS - ML_HEADS),
        zpad(GATE_ML), w_in[:, :, ML_I:ML_F], zpad(LANES - GATE_ML - ML_HEADS)], axis=2).astype(BF16)
    wo_all = w_out.astype(BF16)
    wge_all = w_gate.astype(BF16).reshape(depth * N_EXPERTS, d, D_FF)
    wue_all = w_up.astype(BF16).reshape(depth * N_EXPERTS, d, D_FF)
    wde_all = w_down.astype(BF16).reshape(depth * N_GROUPS, EXPERTS_PER_GROUP * D_FF, d)
    xf = x.reshape(t, d)

    tri = (jnp.arange(te)[:, None] < jnp.arange(te)[None, :]).astype(BF16)
    wr_f = jnp.pad(w_router, ((0, 0), (0, LANES - N_EXPERTS))).astype(F32)
    wr_hi = wr_f.astype(BF16)
    wr = jnp.concatenate([wr_hi, (wr_f - wr_hi.astype(F32)).astype(BF16)], axis=1)
    br = b_router.reshape(N_EXPERTS, 1).astype(F32)

    moe_tt = None
    g2_prev = None
    for l in range(depth):
        sh1, sc1, g1, sh2, sc2, g2 = [mods[l, :, i] for i in range(N_ADA)]
        zb = lambda n: jnp.zeros((n,), F32)
        gb = jnp.concatenate([
            fox_f_bias[l], mlstm_f_bias[l], zb(LANES - FOX_HEADS - ML_HEADS),
            zb(GATE_ML), mlstm_i_bias[l], zb(LANES - GATE_ML - ML_HEADS)]).reshape(1, 2 * LANES)

        xf, qat, ka, fvt, qkv, mkt, fc, gr = _in_proj_call(
            xf, moe_tt, g2_prev, norm_mix[l].reshape(1, d), sc1, sh1, wm_all, wg_all, gb,
            conv_w[l], conv_b[l].reshape(1, -1), layer=l, bsz=bsz, seq=seq)
        hf = _fox_call(qat, ka, fvt, fox_out_norm[l].reshape(1, FOX_WIDTH), bsz=bsz, seq=seq)
        hm = _mlstm_call(qkv, mkt, fc, gr, mlstm_out_norm[l].reshape(1, ML_WIDTH), bsz=bsz, seq=seq)
        xf, hn_tt, gsel, grp, rank, tot = _post_call(
            xf, hf, hm, wo_all, g1, norm_ffn[l].reshape(1, d), sc2, sh2, wr, br, tri,
            layer=l, bsz=bsz, seq=seq, chunk=chunk, te=te)

        n_chunks = t // chunk
        grp = grp.reshape(n_chunks, 1, chunk)
        rank = rank.reshape(n_chunks, 1, chunk)
        off, tile_blk, tile_grp = _moe_tiles(tot[:, :N_GROUPS, 0], tm=tm, tiles_per_chunk=rows // tm)
        pos = rank
        for g in range(N_GROUPS):
            pos = pos + jnp.where(grp == g, off[:, g].reshape(n_chunks, 1, 1), 0)
        xb, gs = _scatter_call(pos, hn_tt, gsel, chunk=chunk, rows=rows, tm=tm)
        ys_tt = _experts_call(tile_blk, tile_grp, xb, gs, wge_all, wue_all, wde_all, layer=l, tm=tm)
        if l < depth - 1:
            moe_tt = _gather_call(pos, ys_tt, chunk=chunk, rows=rows, n_tok=t)
            g2_prev = g2
        else:
            out = _gather_final_call(pos, ys_tt, xf, g2, norm_final.reshape(1, d),
                                     chunk=chunk, rows=rows, seq=seq)
    return out.reshape(bsz, seq, d)
```

```python
import functools

import numpy as np
import jax
import jax.numpy as jnp
from jax import lax
from jax.experimental import pallas as pl
from jax.experimental.pallas import tpu as pltpu

F32 = jnp.float32
BF16 = jnp.bfloat16

LANES = 128
SUBLANES = 8
VMEM_BYTES_V7X = 64 * 1024 * 1024

D_MODEL = 1024
FOX_HEADS = 8
FOX_HEAD_DIM = 64
FOX_WIDTH = FOX_HEADS * FOX_HEAD_DIM
ML_HEADS = 4
ML_HEAD_DIM = 128
ML_WIDTH = ML_HEADS * ML_HEAD_DIM
CONV_WIDTH = 4
N_EXPERTS = 16
N_GROUPS = 4
EXPERTS_PER_GROUP = 4
D_FF = 512
N_ADA = 6
EPS = 1e-6
NEG = -1e30

FOX_Q = 0
FOX_F = 3 * FOX_WIDTH
ML_Q = FOX_F + FOX_HEADS
ML_I = ML_Q + 3 * ML_WIDTH
ML_F = ML_I + ML_HEADS
ML_O = ML_F + ML_HEADS
IN_COLS = ML_O + ML_WIDTH

MAIN_COLS = 7 * 512
BLK_FQ, BLK_FK, BLK_FV, BLK_MQ, BLK_MK, BLK_MV, BLK_MO = range(7)
REST_COLS = 3 * 512
OUT_MQ, OUT_MV, OUT_MO = range(3)
AUG_COLS = FOX_HEADS * 128
BIAS_TERMS = 3
VT_PAD = 16
LOG2E = 1.4426950408889634
GATE_FOX = 0
GATE_ML = FOX_HEADS
GATE_ROWS = 16

TOK_ROWS = D_MODEL // LANES


def _cparams(n_grid, vmem_mb):
    return pltpu.CompilerParams(
        dimension_semantics=("arbitrary",) * n_grid,
        vmem_limit_bytes=vmem_mb * 1024 * 1024)


def _silu(x):
    return x * jax.nn.sigmoid(x)


def _log_sigmoid(z):
    return jnp.minimum(z, 0.0) - jnp.log1p(jnp.exp(-jnp.abs(z)))


def _cumsum_rows(x):
    n = x.shape[0]
    row = lax.broadcasted_iota(jnp.int32, x.shape, 0)
    s = 1
    while s < n:
        x = x + jnp.where(row >= s, pltpu.roll(x, s, axis=0), 0.0)
        s *= 2
    return x


def _from_token_tiles(ref, n_tok):
    return jnp.concatenate(
        [ref[pl.ds(c, n_tok, stride=TOK_ROWS), :] for c in range(TOK_ROWS)], axis=1)


def _to_token_tiles(ref, val, row0=0):
    n = val.shape[0]
    for c in range(TOK_ROWS):
        ref[pl.ds(row0 * TOK_ROWS + c, n, stride=TOK_ROWS), :] = val[:, c * LANES:(c + 1) * LANES]


def _ada_kernel(c_ref, w_ref, b_ref, o_ref):
    c = c_ref[...]
    o_ref[0] = jnp.dot(_silu(c), w_ref[0], preferred_element_type=F32,
                       precision=lax.Precision.HIGHEST) + b_ref[0]


def _ada_call(c, w_ada, b_ada):
    depth, d, n = w_ada.shape
    bsz = c.shape[0]
    tn = 1536
    return pl.pallas_call(
        _ada_kernel,
        grid=(depth, n // tn),
        in_specs=[
            pl.BlockSpec((bsz, d), lambda l, j: (0, 0)),
            pl.BlockSpec((1, d, tn), lambda l, j: (l, 0, j)),
            pl.BlockSpec((1, 1, tn), lambda l, j: (l, 0, j)),
        ],
        out_specs=pl.BlockSpec((1, bsz, tn), lambda l, j: (l, 0, j)),
        out_shape=jax.ShapeDtypeStruct((depth, bsz, n), F32),
        compiler_params=_cparams(2, 32),
        name="ada_mod",
    )(c, w_ada, b_ada.reshape(depth, 1, n))


def _in_proj_kernel(*refs, tm, combine):
    if combine:
        (x_ref, ott_ref, g2_ref, nm_ref, sc_ref, sh_ref, wm_ref, wg_ref, gb_ref, cw_ref, cb_ref,
         place_ref, ones_ref,
         xn_ref, qa_ref, ka_ref, vt_ref, qkv_ref, kt_ref, fc_ref, gr_ref, fcar, ccar) = refs
    else:
        (x_ref, nm_ref, sc_ref, sh_ref, wm_ref, wg_ref, gb_ref, cw_ref, cb_ref,
         place_ref, ones_ref,
         qa_ref, ka_ref, vt_ref, qkv_ref, kt_ref, fc_ref, gr_ref, fcar, ccar) = refs

    @pl.when(pl.program_id(1) == 0)
    def _():
        fcar[...] = jnp.zeros_like(fcar)
        ccar[...] = jnp.zeros_like(ccar)

    x = x_ref[...]
    if combine:
        x = x + g2_ref[0] * _from_token_tiles(ott_ref, tm)
        xn_ref[...] = x
    ms = jnp.mean(x * x, axis=1, keepdims=True)
    hn = x * lax.rsqrt(ms + EPS) * nm_ref[...]
    hn = hn * (1.0 + sc_ref[0]) + sh_ref[0]
    hb = hn.astype(BF16)

    gp = jnp.dot(hb, wg_ref[0], preferred_element_type=F32) + gb_ref[...]
    lf = _log_sigmoid(gp[:, :LANES])
    fcum = _cumsum_rows(lf) + fcar[...]
    fcar[...] = fcum[tm - 1:tm, :]
    gml = gp[:, LANES:] - fcum
    fc_ref[...] = fcum
    gr_ref[0] = gml.T[:GATE_ROWS, :]

    def mm(j):
        return jnp.dot(hb, wm_ref[0, :, j * 512:(j + 1) * 512], preferred_element_type=F32)

    def put(j, v):
        qkv_ref[:, j * 512:(j + 1) * 512] = v.astype(BF16)

    lane = lax.broadcasted_iota(jnp.int32, (tm, LANES), 1)
    fs = fcum * LOG2E
    hi = fs.astype(BF16).astype(F32)
    mid = (fs - hi).astype(BF16).astype(F32)
    low = (fs - hi - mid).astype(BF16).astype(F32)
    packed = jnp.where(lane < FOX_HEADS, hi,
                       jnp.where(lane < 2 * FOX_HEADS, pltpu.roll(mid, FOX_HEADS, axis=1),
                                 jnp.where(lane < 3 * FOX_HEADS, pltpu.roll(low, 2 * FOX_HEADS, axis=1), 0.0)))
    bias = jnp.dot(packed.astype(BF16), place_ref[...], preferred_element_type=F32) + ones_ref[...]

    def put_heads(ref, val, col0, transposed):
        lo = lane < FOX_HEAD_DIM
        for p in range(FOX_HEADS // 2):
            slab = val[:, p * LANES:(p + 1) * LANES]
            for h, data in ((2 * p, slab), (2 * p + 1, pltpu.roll(slab, FOX_HEAD_DIM, axis=1))):
                blk = jnp.where(lo, data, 0.0) + bias[:, col0 + h * LANES:col0 + (h + 1) * LANES]
                if transposed:
                    ref[0, 0, h * LANES:(h + 1) * LANES, :] = blk.T.astype(BF16)
                else:
                    ref[:, h * LANES:(h + 1) * LANES] = blk.astype(BF16)

    u = jnp.concatenate([mm(BLK_MQ), mm(BLK_MK)], axis=1)
    prev = ccar[...]
    ccar[...] = u[tm - SUBLANES:tm, :]
    row8 = lax.broadcasted_iota(jnp.int32, prev.shape, 0)
    y = cb_ref[...] + cw_ref[CONV_WIDTH - 1:CONV_WIDTH, :] * u
    for k in range(1, CONV_WIDTH):
        r = pltpu.roll(u, k, axis=0)
        top = jnp.where(row8 < k, pltpu.roll(prev, k, axis=0), r[:SUBLANES])
        shifted = jnp.concatenate([top, r[SUBLANES:]], axis=0)
        y = y + cw_ref[CONV_WIDTH - 1 - k:CONV_WIDTH - k, :] * shifted
    act = _silu(y)
    put(OUT_MQ, act[:, :ML_WIDTH])
    kt_ref[0, 0] = (act[:, ML_WIDTH:] * (ML_HEAD_DIM ** -0.5)).T.astype(BF16)

    put(OUT_MO, jax.nn.sigmoid(mm(BLK_MO)))
    put(OUT_MV, mm(BLK_MV))
    vt_ref[0, 0] = mm(BLK_FV).astype(BF16).T
    put_heads(qa_ref, mm(BLK_FQ) * (FOX_HEAD_DIM ** -0.5 * LOG2E), 0, True)
    put_heads(ka_ref, mm(BLK_FK), AUG_COLS, False)


def _bias_placement():
    place = np.zeros((LANES, 2 * AUG_COLS), np.float32)
    ones = np.zeros((1, 2 * AUG_COLS), np.float32)
    for h in range(FOX_HEADS):
        for term in range(BIAS_TERMS):
            src = term * FOX_HEADS + h
            place[src, h * LANES + FOX_HEAD_DIM + term] = 1.0
            ones[0, h * LANES + FOX_HEAD_DIM + BIAS_TERMS + term] = 1.0
            place[src, AUG_COLS + h * LANES + FOX_HEAD_DIM + BIAS_TERMS + term] = -1.0
            ones[0, AUG_COLS + h * LANES + FOX_HEAD_DIM + term] = 1.0
    return jnp.asarray(place, BF16), jnp.asarray(ones, F32)


def _in_proj_call(x, moe_tt, g2, nm, sc, sh, wm, wg, gb, cw, cb, *, layer, bsz, seq, tm=512):
    t, d = x.shape
    ns = seq // tm
    combine = moe_tt is not None
    row = lambda b, s: (b * ns + s, 0)
    per_b = lambda b, s: (b, 0, 0)
    const2 = lambda b, s: (0, 0)
    in_specs = [pl.BlockSpec((tm, d), row)]
    args = [x]
    if combine:
        in_specs += [pl.BlockSpec((tm * TOK_ROWS, LANES), row), pl.BlockSpec((1, 1, d), per_b)]
        args += [moe_tt, g2]
    in_specs += [
        pl.BlockSpec((1, d), const2),
        pl.BlockSpec((1, 1, d), per_b),
        pl.BlockSpec((1, 1, d), per_b),
        pl.BlockSpec((1, d, MAIN_COLS), lambda b, s: (layer, 0, 0)),
        pl.BlockSpec((1, d, 2 * LANES), lambda b, s: (layer, 0, 0)),
        pl.BlockSpec((1, 2 * LANES), const2),
        pl.BlockSpec((CONV_WIDTH, 2 * ML_WIDTH), const2),
        pl.BlockSpec((1, 2 * ML_WIDTH), const2),
        pl.BlockSpec((LANES, 2 * AUG_COLS), const2),
        pl.BlockSpec((1, 2 * AUG_COLS), const2),
    ]
    place, ones = _bias_placement()
    args += [nm, sc, sh, wm, wg, gb, cw, cb, place, ones]
    out_specs = []
    out_shape = []
    if combine:
        out_specs.append(pl.BlockSpec((tm, d), row))
        out_shape.append(jax.ShapeDtypeStruct((t, d), F32))
    tiled = lambda b, s: (b, s, 0, 0)
    out_specs += [
        pl.BlockSpec((1, 1, AUG_COLS, tm), tiled),
        pl.BlockSpec((tm, AUG_COLS), row),
        pl.BlockSpec((1, 1, FOX_WIDTH, tm), tiled),
        pl.BlockSpec((tm, REST_COLS), row),
        pl.BlockSpec((1, 1, ML_WIDTH, tm), tiled),
        pl.BlockSpec((tm, LANES), row),
        pl.BlockSpec((1, GATE_ROWS, tm), lambda b, s: (b, 0, s)),
    ]
    out_shape += [
        jax.ShapeDtypeStruct((bsz, ns, AUG_COLS, tm), BF16),
        jax.ShapeDtypeStruct((t, AUG_COLS), BF16),
        jax.ShapeDtypeStruct((bsz, ns, FOX_WIDTH, tm), BF16),
        jax.ShapeDtypeStruct((t, REST_COLS), BF16),
        jax.ShapeDtypeStruct((bsz, ns, ML_WIDTH, tm), BF16),
        jax.ShapeDtypeStruct((t, LANES), F32),
        jax.ShapeDtypeStruct((bsz, GATE_ROWS, seq), F32),
    ]
    outs = pl.pallas_call(
        functools.partial(_in_proj_kernel, tm=tm, combine=combine),
        grid=(bsz, ns),
        in_specs=in_specs,
        out_specs=out_specs,
        out_shape=out_shape,
        scratch_shapes=[pltpu.VMEM((1, LANES), F32), pltpu.VMEM((SUBLANES, 2 * ML_WIDTH), F32)],
        compiler_params=_cparams(2, 48),
        name="in_proj",
    )(*args)
    if combine:
        return outs
    return [x] + list(outs)


def _fox_kernel(qt_ref, k_ref, vt_ref, ng_ref, cm_ref, o_ref, m_sc, acc_sc, *, tq, tk, nh):
    qi = pl.program_id(2)
    ones_rows = jnp.where(lax.broadcasted_iota(jnp.int32, (VT_PAD, tk), 0) == 0, 1.0, 0.0).astype(BF16)
    for n in range(2 * nh):
        m_sc[n] = jnp.full((SUBLANES, tq // 2), NEG, F32)
        acc_sc[n] = jnp.zeros((FOX_HEAD_DIM + VT_PAD, tq // 2), F32)

    half = tk // 2
    qh = tq // 2

    def update(n, z, va):
        m_prev = m_sc[n]
        m_new = jnp.maximum(m_prev, jnp.max(z, axis=0, keepdims=True))
        alpha = jnp.exp2(m_prev - m_new)
        p = jnp.exp2(z - m_new[:1, :])
        acc_sc[n] = alpha[:1, :] * acc_sc[n] + jnp.dot(va, p.astype(BF16), preferred_element_type=F32)
        m_sc[n] = m_new

    def v_aug(j, h):
        return jnp.concatenate(
            [vt_ref[0, j, h * FOX_HEAD_DIM:(h + 1) * FOX_HEAD_DIM, :], ones_rows], axis=0)

    def below_diagonal(j, carry):
        k0 = pl.multiple_of(j * tk, tk)
        items = [(h, c) for h in range(nh) for c in range(2)]

        def logits(item):
            h, c = item
            hs = slice(h * LANES, (h + 1) * LANES)
            return jnp.dot(k_ref[pl.ds(k0, tk), hs], qt_ref[0, 0, hs, :][:, c * qh:(c + 1) * qh],
                           preferred_element_type=F32)

        ahead = 4
        zs = [logits(it) for it in items[:ahead]]
        for n, (h, c) in enumerate(items):
            z = zs[n]
            if n + ahead < len(items):
                zs.append(logits(items[n + ahead]))
            update(n, z, v_aug(j, h))
        return carry

    def diagonal(j, carry):
        k0 = pl.multiple_of(j * tk, tk)

        def logits(h):
            hs = slice(h * LANES, (h + 1) * LANES)
            qt = qt_ref[0, 0, hs, :]
            za = jnp.dot(k_ref[pl.ds(k0, half), hs], qt, preferred_element_type=F32) + cm_ref[:half, :]
            zb = (jnp.dot(k_ref[pl.ds(k0 + half, half), hs], qt[:, half:], preferred_element_type=F32)
                  + cm_ref[half:, :][:, half:])
            zb = jnp.concatenate([jnp.full((half, half), NEG, F32), zb], axis=1)
            return jnp.concatenate([za, zb], axis=0)

        ahead = 2
        zs = [logits(h) for h in range(min(ahead, nh))]
        for h in range(nh):
            z = zs[h]
            if h + ahead < nh:
                zs.append(logits(h + ahead))
            va = v_aug(j, h)
            for c in range(2):
                update(2 * h + c, z[:, c * qh:(c + 1) * qh], va)
        return carry

    lax.fori_loop(0, qi, below_diagonal, 0)
    lax.fori_loop(qi, qi + 1, diagonal, 0)

    for p in range(nh // 2):
        outs = []
        for h in (2 * p, 2 * p + 1):
            acc = jnp.concatenate([acc_sc[2 * h], acc_sc[2 * h + 1]], axis=1)
            num = acc[:FOX_HEAD_DIM, :]
            l = acc[FOX_HEAD_DIM:FOX_HEAD_DIM + 1, :]
            ms = jnp.mean(num * num, axis=0, keepdims=True)
            outs.append(num * lax.rsqrt(ms + EPS * l * l))
        ps = slice(p * LANES, (p + 1) * LANES)
        o_ref[:, ps] = (jnp.concatenate(outs, axis=0).T * ng_ref[:, ps]).astype(BF16)


def _fox_call(qat, ka, fvt, ng, *, bsz, seq, tq=512, nh=8):
    t = ka.shape[0]
    nq = seq // tq
    ngrp = FOX_HEADS // nh
    vw = nh * FOX_HEAD_DIM
    cmask = jnp.where(jnp.arange(tq)[:, None] <= jnp.arange(tq)[None, :], 0.0, NEG).astype(F32)
    return pl.pallas_call(
        functools.partial(_fox_kernel, tq=tq, tk=tq, nh=nh),
        grid=(bsz, ngrp, nq),
        in_specs=[
            pl.BlockSpec((1, 1, nh * LANES, tq), lambda b, p, i: (b, i, p, 0)),
            pl.BlockSpec((seq, nh * LANES), lambda b, p, i: (b, p)),
            pl.BlockSpec((1, nq, vw, tq), lambda b, p, i: (b, 0, p, 0)),
            pl.BlockSpec((1, vw), lambda b, p, i: (0, p)),
            pl.BlockSpec((tq, tq), lambda b, p, i: (0, 0)),
        ],
        out_specs=pl.BlockSpec((tq, vw), lambda b, p, i: (b * nq + i, p)),
        out_shape=jax.ShapeDtypeStruct((t, FOX_WIDTH), BF16),
        scratch_shapes=[pltpu.VMEM((2 * nh, SUBLANES, tq // 2), F32),
                        pltpu.VMEM((2 * nh, FOX_HEAD_DIM + VT_PAD, tq // 2), F32)],
        compiler_params=_cparams(3, 48),
        name="fox_attn",
    )(qat, ka, fvt, ng, cmask)


def _mlstm_kernel(q_ref, kt_ref, v_ref, og_ref, fc_ref, gr_ref, ng_ref, o_ref, ct_sc, u_sc, *, ch, per_tile):
    @pl.when(pl.program_id(1) == 0)
    def _():
        ct_sc[...] = jnp.zeros_like(ct_sc)
        u_sc[...] = jnp.zeros_like(u_sc)

    causal = (lax.broadcasted_iota(jnp.int32, (ch, ch), 1)
              <= lax.broadcasted_iota(jnp.int32, (ch, ch), 0))
    lane = lax.broadcasted_iota(jnp.int32, (ch, LANES), 1)
    fc = fc_ref[...]
    koff = pl.multiple_of((pl.program_id(1) % per_tile) * ch, ch)
    for h in range(ML_HEADS):
        sl = slice(h * ML_HEAD_DIM, (h + 1) * ML_HEAD_DIM)
        gl = GATE_ML + h
        q = q_ref[:, sl]
        kt = kt_ref[0, 0, sl, pl.ds(koff, ch)]
        vp = jnp.concatenate([v_ref[:, sl], jnp.where(lane == gl, 1.0, 0.0).astype(BF16)], axis=1)
        g_row = gr_ref[0, gl:gl + 1, :]
        u_prev = u_sc[h][:, :1]
        gm = jnp.where(causal, g_row, NEG)
        u_i = jnp.maximum(u_prev, jnp.max(gm, axis=1, keepdims=True))
        dmat = jnp.exp(gm - u_i)
        s = jnp.dot(q, kt, preferred_element_type=F32)
        scores = (s * dmat).astype(BF16)
        inter = jnp.exp(u_prev - u_i)
        ct = ct_sc[h]
        nd = (jnp.dot(scores, vp, preferred_element_type=F32)
              + jnp.dot(q, ct.astype(BF16), preferred_element_type=F32) * inter)
        num = nd[:, :ML_HEAD_DIM]
        den = jnp.maximum(jnp.abs(nd[:, ML_HEAD_DIM:]), jnp.exp(-(jnp.where(lane == gl, fc, 0.0) + u_i)))
        ms = jnp.mean(num * num, axis=1, keepdims=True)
        scale = lax.rsqrt(ms + EPS * den * den)[:, gl:gl + 1]
        y = num * scale * ng_ref[:, sl] * og_ref[:, sl].astype(F32)
        o_ref[:, sl] = y.astype(BF16)
        u_new = jnp.maximum(u_prev, jnp.max(g_row, axis=1, keepdims=True))
        ktw = (kt.astype(F32) * jnp.exp(g_row - u_new)).astype(BF16)
        ct_sc[h] = jnp.exp(u_prev - u_new) * ct + jnp.dot(ktw, vp, preferred_element_type=F32)
        u_sc[h] = jnp.broadcast_to(u_new, (1, LANES))


def _mlstm_call(qkv, mkt, fc, gr, ng, *, bsz, seq, ch=512):
    t = qkv.shape[0]
    nc = seq // ch
    ktile = mkt.shape[-1]
    per_tile = ktile // ch
    row = lambda b, c: (b * nc + c, 0)
    return pl.pallas_call(
        functools.partial(_mlstm_kernel, ch=ch, per_tile=per_tile),
        grid=(bsz, nc),
        in_specs=[
            pl.BlockSpec((ch, ML_WIDTH), lambda b, c: (b * nc + c, OUT_MQ)),
            pl.BlockSpec((1, 1, ML_WIDTH, ktile), lambda b, c: (b, c // per_tile, 0, 0)),
            pl.BlockSpec((ch, ML_WIDTH), lambda b, c: (b * nc + c, OUT_MV)),
            pl.BlockSpec((ch, ML_WIDTH), lambda b, c: (b * nc + c, OUT_MO)),
            pl.BlockSpec((ch, LANES), row),
            pl.BlockSpec((1, GATE_ROWS, ch), lambda b, c: (b, 0, c)),
            pl.BlockSpec((1, ML_WIDTH), lambda b, c: (0, 0)),
        ],
        out_specs=pl.BlockSpec((ch, ML_WIDTH), row),
        out_shape=jax.ShapeDtypeStruct((t, ML_WIDTH), BF16),
        scratch_shapes=[pltpu.VMEM((ML_HEADS, ML_HEAD_DIM, 2 * ML_HEAD_DIM), F32),
                        pltpu.VMEM((ML_HEADS, 1, LANES), F32)],
        compiler_params=_cparams(2, 32),
        name="mlstm",
    )(qkv, mkt, qkv, qkv, fc, gr, ng)


def _post_kernel(x_ref, hf_ref, hm_ref, wo_ref, g1_ref, nf_ref, sc_ref, sh_ref, wr_ref, br_ref, tri_ref,
                 x1_ref, hn_ref, gsel_ref, grp_ref, rank_ref, tot_ref, cnt_sc, *, te, steps_per_chunk):
    @pl.when(pl.program_id(0) % steps_per_chunk == 0)
    def _():
        cnt_sc[...] = jnp.zeros_like(cnt_sc)

    mix = (jnp.dot(hf_ref[...], wo_ref[0, :FOX_WIDTH, :], preferred_element_type=F32)
           + jnp.dot(hm_ref[...], wo_ref[0, FOX_WIDTH:, :], preferred_element_type=F32))
    x1 = x_ref[...] + g1_ref[0] * mix
    x1_ref[...] = x1
    ms = jnp.mean(x1 * x1, axis=1, keepdims=True)
    hn = x1 * lax.rsqrt(ms + EPS) * nf_ref[...]
    hn = hn * (1.0 + sc_ref[0]) + sh_ref[0]
    _to_token_tiles(hn_ref, hn)

    hn_hi = hn.astype(BF16)
    hn_lo = (hn - hn_hi.astype(F32)).astype(BF16)
    l2 = jnp.dot(hn_hi, wr_ref[...], preferred_element_type=F32)
    logits = (l2[:, :LANES] + l2[:, LANES:]
              + jnp.dot(hn_lo, wr_ref[:, :LANES], preferred_element_type=F32))
    aff = jax.nn.sigmoid(logits.T[:N_EXPERTS, :])
    sel = aff + br_ref[...]
    selr = [sel[e:e + 1, :] for e in range(N_EXPERTS)]
    affr = [aff[e:e + 1, :] for e in range(N_EXPERTS)]
    keep = [None] * N_EXPERTS
    score = []
    for g in range(N_GROUPS):
        vs = selr[g * EXPERTS_PER_GROUP:(g + 1) * EXPERTS_PER_GROUP]
        sg = jnp.zeros_like(vs[0])
        for i in range(EXPERTS_PER_GROUP):
            beaten = jnp.zeros_like(vs[0])
            for j in range(EXPERTS_PER_GROUP):
                if j != i:
                    b = (vs[j] >= vs[i]) if j < i else (vs[j] > vs[i])
                    beaten = beaten + jnp.where(b, 1.0, 0.0)
            kp = beaten < 2.0
            keep[g * EXPERTS_PER_GROUP + i] = kp
            sg = sg + jnp.where(kp, vs[i], 0.0)
        score.append(sg)
    chosen = []
    for g in range(N_GROUPS):
        lost = jnp.zeros_like(score[0])
        for g2 in range(N_GROUPS):
            if g2 != g:
                b = (score[g2] >= score[g]) if g2 < g else (score[g2] > score[g])
                lost = lost + jnp.where(b, 1.0, 0.0)
        chosen.append(jnp.where(lost < 0.5, 1.0, 0.0))
    wsel = []
    for i in range(EXPERTS_PER_GROUP):
        wi = jnp.zeros_like(score[0])
        for g in range(N_GROUPS):
            e = g * EXPERTS_PER_GROUP + i
            wi = wi + chosen[g] * jnp.where(keep[e], affr[e], 0.0)
        wsel.append(wi)
    wsum = wsel[0] + wsel[1] + wsel[2] + wsel[3]
    wsel = [w / wsum for w in wsel]

    row8 = lax.broadcasted_iota(jnp.int32, (SUBLANES, te), 0)
    gmat = jnp.zeros((SUBLANES, te), F32)
    wmat = jnp.zeros((SUBLANES, te), F32)
    grp = jnp.zeros_like(score[0])
    for g in range(N_GROUPS):
        gmat = jnp.where(row8 == g, chosen[g], gmat)
        wmat = jnp.where(row8 == g, wsel[g], wmat)
        grp = grp + g * chosen[g]
    pref = jnp.dot(gmat.astype(BF16), tri_ref[...], preferred_element_type=F32) + cnt_sc[:, :1]
    rank = jnp.sum(gmat * pref, axis=0, keepdims=True)
    tot = cnt_sc[:, :1] + jnp.sum(gmat, axis=1, keepdims=True)
    cnt_sc[...] = jnp.broadcast_to(tot, cnt_sc.shape)
    rank_ref[0] = rank.astype(jnp.int32)
    grp_ref[0] = grp.astype(jnp.int32)
    tot_ref[0] = jnp.broadcast_to(tot, (SUBLANES, LANES)).astype(jnp.int32)
    wfull = jnp.concatenate([wmat, jnp.zeros((LANES - SUBLANES, te), F32)], axis=0)
    gsel_ref[...] = wfull.T


def _post_call(x, hf, hm, wo, g1, nf, sc, sh, wr, br, tri, *, layer, bsz, seq, chunk, te=512):
    t, d = x.shape
    steps_per_chunk = chunk // te
    per_seq = seq // te
    n_steps = t // te
    n_chunks = t // chunk
    row = lambda i: (i, 0)
    per_b = lambda i: (i // per_seq, 0, 0)
    const2 = lambda i: (0, 0)
    return pl.pallas_call(
        functools.partial(_post_kernel, te=te, steps_per_chunk=steps_per_chunk),
        grid=(n_steps,),
        in_specs=[
            pl.BlockSpec((te, d), row),
            pl.BlockSpec((te, FOX_WIDTH), row),
            pl.BlockSpec((te, ML_WIDTH), row),
            pl.BlockSpec((1, d, d), lambda i: (layer, 0, 0)),
            pl.BlockSpec((1, 1, d), per_b),
            pl.BlockSpec((1, d), const2),
            pl.BlockSpec((1, 1, d), per_b),
            pl.BlockSpec((1, 1, d), per_b),
            pl.BlockSpec((d, 2 * LANES), const2),
            pl.BlockSpec((N_EXPERTS, 1), const2),
            pl.BlockSpec((te, te), const2),
        ],
        out_specs=[
            pl.BlockSpec((te, d), row),
            pl.BlockSpec((te * TOK_ROWS, LANES), row),
            pl.BlockSpec((te, LANES), row),
            pl.BlockSpec((1, 1, te), lambda i: (i, 0, 0)),
            pl.BlockSpec((1, 1, te), lambda i: (i, 0, 0)),
            pl.BlockSpec((1, SUBLANES, LANES), lambda i: (i // steps_per_chunk, 0, 0)),
        ],
        out_shape=[
            jax.ShapeDtypeStruct((t, d), F32),
            jax.ShapeDtypeStruct((t * TOK_ROWS, LANES), F32),
            jax.ShapeDtypeStruct((t, LANES), F32),
            jax.ShapeDtypeStruct((n_steps, 1, te), jnp.int32),
            jax.ShapeDtypeStruct((n_steps, 1, te), jnp.int32),
            jax.ShapeDtypeStruct((n_chunks, SUBLANES, LANES), jnp.int32),
        ],
        scratch_shapes=[pltpu.VMEM((SUBLANES, LANES), F32)],
        compiler_params=_cparams(1, 48),
        name="post_router",
    )(x, hf, hm, wo, g1, nf, sc, sh, wr, br, tri)


def _scatter_kernel(pos_ref, hn_ref, gsel_ref, xb_ref, gs_ref, xs_sc, *, chunk, rows, tm):
    xs_sc[...] = jnp.zeros_like(xs_sc)
    gs_ref[...] = jnp.zeros_like(gs_ref)

    def body(t, carry):
        p = pos_ref[0, 0, t]
        src = pl.multiple_of(t * TOK_ROWS, TOK_ROWS)
        dst = pl.multiple_of(p * TOK_ROWS, TOK_ROWS)
        xs_sc[pl.ds(dst, TOK_ROWS), :] = hn_ref[pl.ds(src, TOK_ROWS), :]
        gs_ref[pl.ds(p, 1), :] = gsel_ref[pl.ds(t, 1), :]
        return carry

    lax.fori_loop(0, chunk, body, 0, unroll=8)
    for j in range(rows // tm):
        for c in range(TOK_ROWS):
            xb_ref[j * tm:(j + 1) * tm, c * LANES:(c + 1) * LANES] = (
                xs_sc[pl.ds(j * tm * TOK_ROWS + c, tm, stride=TOK_ROWS), :].astype(BF16))


def _scatter_call(pos, hn_tt, gsel, *, chunk, rows, tm):
    t = gsel.shape[0]
    n_chunks = t // chunk
    return pl.pallas_call(
        functools.partial(_scatter_kernel, chunk=chunk, rows=rows, tm=tm),
        grid=(n_chunks,),
        in_specs=[
            pl.BlockSpec((1, 1, chunk), lambda c: (c, 0, 0), memory_space=pltpu.SMEM),
            pl.BlockSpec((chunk * TOK_ROWS, LANES), lambda c: (c, 0)),
            pl.BlockSpec((chunk, LANES), lambda c: (c, 0)),
        ],
        out_specs=[
            pl.BlockSpec((rows, D_MODEL), lambda c: (c, 0)),
            pl.BlockSpec((rows, LANES), lambda c: (c, 0)),
        ],
        out_shape=[
            jax.ShapeDtypeStruct((n_chunks * rows, D_MODEL), BF16),
            jax.ShapeDtypeStruct((n_chunks * rows, LANES), F32),
        ],
        scratch_shapes=[pltpu.VMEM((rows * TOK_ROWS, LANES), F32)],
        compiler_params=_cparams(1, 56),
        name="moe_scatter",
    )(pos, hn_tt, gsel)


def _experts_kernel(blk_ref, grp_ref, xb_ref, gs_ref, wg_ref, wu_ref, wd_ref, y_ref, *, tm):
    g = grp_ref[pl.program_id(0)]

    @pl.when(g < N_GROUPS)
    def _():
        x = xb_ref[...]
        gs = gs_ref[...]
        acts = []
        hw = D_FF // 2
        for i in range(EXPERTS_PER_GROUP):
            for c in range(2):
                cs = slice(c * hw, (c + 1) * hw)
                hg = jnp.dot(x, wg_ref[i, :, cs], preferred_element_type=F32)
                hu = jnp.dot(x, wu_ref[i, :, cs], preferred_element_type=F32)
                acts.append((_silu(hg) * hu * gs[:, i:i + 1]).astype(BF16))
        y = jnp.dot(jnp.concatenate(acts, axis=1), wd_ref[0], preferred_element_type=F32)
        _to_token_tiles(y_ref, y)

    @pl.when(g >= N_GROUPS)
    def _():
        y_ref[...] = jnp.zeros_like(y_ref)


def _experts_call(tile_blk, tile_grp, xb, gs, wg, wu, wd, *, layer, tm):
    n_rows = xb.shape[0]
    n_slots = n_rows // tm
    wmap = lambda s, blk, grp: (layer * N_GROUPS + jnp.minimum(grp[s], N_GROUPS - 1), 0, 0)
    grid_spec = pltpu.PrefetchScalarGridSpec(
        num_scalar_prefetch=2,
        grid=(n_slots,),
        in_specs=[
            pl.BlockSpec((tm, D_MODEL), lambda s, blk, grp: (blk[s], 0)),
            pl.BlockSpec((tm, LANES), lambda s, blk, grp: (blk[s], 0)),
            pl.BlockSpec((EXPERTS_PER_GROUP, D_MODEL, D_FF), wmap),
            pl.BlockSpec((EXPERTS_PER_GROUP, D_MODEL, D_FF), wmap),
            pl.BlockSpec((1, EXPERTS_PER_GROUP * D_FF, D_MODEL), wmap),
        ],
        out_specs=pl.BlockSpec((tm * TOK_ROWS, LANES), lambda s, blk, grp: (blk[s], 0)),
    )
    return pl.pallas_call(
        functools.partial(_experts_kernel, tm=tm),
        grid_spec=grid_spec,
        out_shape=jax.ShapeDtypeStruct((n_rows * TOK_ROWS, LANES), F32),
        compiler_params=_cparams(1, 48),
        name="moe_experts",
    )(tile_blk, tile_grp, xb, gs, wg, wu, wd)


def _gather_kernel(pos_ref, ys_ref, o_ref, *, chunk):
    def body(t, carry):
        p = pos_ref[0, 0, t]
        src = pl.multiple_of(p * TOK_ROWS, TOK_ROWS)
        dst = pl.multiple_of(t * TOK_ROWS, TOK_ROWS)
        o_ref[pl.ds(dst, TOK_ROWS), :] = ys_ref[pl.ds(src, TOK_ROWS), :]
        return carry

    lax.fori_loop(0, chunk, body, 0, unroll=8)


def _gather_call(pos, ys_tt, *, chunk, rows, n_tok):
    n_chunks = n_tok // chunk
    return pl.pallas_call(
        functools.partial(_gather_kernel, chunk=chunk),
        grid=(n_chunks,),
        in_specs=[
            pl.BlockSpec((1, 1, chunk), lambda c: (c, 0, 0), memory_space=pltpu.SMEM),
            pl.BlockSpec((rows * TOK_ROWS, LANES), lambda c: (c, 0)),
        ],
        out_specs=pl.BlockSpec((chunk * TOK_ROWS, LANES), lambda c: (c, 0)),
        out_shape=jax.ShapeDtypeStruct((n_tok * TOK_ROWS, LANES), F32),
        compiler_params=_cparams(1, 56),
        name="moe_gather",
    )(pos, ys_tt)


def _gather_final_kernel(pos_ref, ys_ref, x_ref, g2_ref, nf_ref, o_ref, o_sc, *, part, sub):
    base = pl.program_id(1) * part

    def body(t, carry):
        p = pos_ref[0, 0, base + t]
        src = pl.multiple_of(p * TOK_ROWS, TOK_ROWS)
        dst = pl.multiple_of(t * TOK_ROWS, TOK_ROWS)
        o_sc[pl.ds(dst, TOK_ROWS), :] = ys_ref[pl.ds(src, TOK_ROWS), :]
        return carry

    lax.fori_loop(0, part, body, 0, unroll=8)
    for r in range(part // sub):
        rs = slice(r * sub, (r + 1) * sub)
        moe = jnp.concatenate(
            [o_sc[pl.ds(r * sub * TOK_ROWS + c, sub, stride=TOK_ROWS), :] for c in range(TOK_ROWS)], axis=1)
        x = x_ref[rs, :] + g2_ref[0] * moe
        ms = jnp.mean(x * x, axis=1, keepdims=True)
        o_ref[rs, :] = x * lax.rsqrt(ms + EPS) * nf_ref[...]


def _gather_final_call(pos, ys_tt, x, g2, nf, *, chunk, rows, seq):
    n_tok, d = x.shape
    n_chunks = n_tok // chunk
    parts = 2
    part = chunk // parts
    return pl.pallas_call(
        functools.partial(_gather_final_kernel, part=part, sub=256),
        grid=(n_chunks, parts),
        in_specs=[
            pl.BlockSpec((1, 1, chunk), lambda c, s: (c, 0, 0), memory_space=pltpu.SMEM),
            pl.BlockSpec((rows * TOK_ROWS, LANES), lambda c, s: (c, 0)),
            pl.BlockSpec((part, d), lambda c, s: (c * parts + s, 0)),
            pl.BlockSpec((1, 1, d), lambda c, s: ((c * chunk + s * part) // seq, 0, 0)),
            pl.BlockSpec((1, d), lambda c, s: (0, 0)),
        ],
        out_specs=pl.BlockSpec((part, d), lambda c, s: (c * parts + s, 0)),
        out_shape=jax.ShapeDtypeStruct((n_tok, d), F32),
        scratch_shapes=[pltpu.VMEM((part * TOK_ROWS, LANES), F32)],
        compiler_params=_cparams(2, 56),
        name="moe_gather_final",
    )(pos, ys_tt, x, g2, nf)


def _moe_tiles(tot, *, tm, tiles_per_chunk):
    nt = (tot + tm - 1) // tm
    ts = jnp.cumsum(nt, axis=1) - nt
    off = (ts * tm).astype(jnp.int32)
    j = jnp.arange(tiles_per_chunk, dtype=jnp.int32)[None, :, None]
    inside = (j >= ts[:, None, :]) & (j < (ts + nt)[:, None, :])
    key = jnp.where(jnp.any(inside, axis=-1), jnp.argmax(inside, axis=-1), N_GROUPS).reshape(-1)
    order = jnp.argsort(key, stable=True).astype(jnp.int32)
    return off, order, key[order].astype(jnp.int32)


def kernel(x, c, w_in, conv_w, conv_b, fox_f_bias, mlstm_i_bias, mlstm_f_bias, fox_out_norm,
           mlstm_out_norm, w_out, w_ada, b_ada, norm_mix, norm_ffn, w_router, b_router, w_gate,
           w_up, w_down, norm_final):
    bsz, seq, d = x.shape
    depth = w_in.shape[0]
    t = bsz * seq
    assert d == D_MODEL and w_in.shape[-1] == IN_COLS
    chunk = min(2048, seq)
    tm = 256
    assert chunk % tm == 0
    rows = chunk + (N_GROUPS - 1) * tm
    te = min(1024, seq)
    assert seq % 512 == 0 and seq % chunk == 0 and chunk % te == 0

    mods = _ada_call(c, w_ada, b_ada).reshape(depth, bsz, N_ADA, 1, d)
    wm_all = jnp.concatenate([w_in[:, :, FOX_Q:FOX_F], w_in[:, :, ML_Q:ML_I], w_in[:, :, ML_O:IN_COLS]],
                             axis=2).astype(BF16)
    zpad = lambda n: jnp.zeros((depth, d, n), F32)
    wg_all = jnp.concatenate([
        w_in[:, :, FOX_F:ML_Q], w_in[:, :, ML_F:ML_O], zpad(LANES - FOX_HEADS - ML_HEADS),
        zpad(GATE_ML), w_in[:, :, ML_I:ML_F], zpad(LANES - GATE_ML - ML_HEADS)], axis=2).astype(BF16)
    wo_all = w_out.astype(BF16)
    wge_all = w_gate.astype(BF16).reshape(depth * N_EXPERTS, d, D_FF)
    wue_all = w_up.astype(BF16).reshape(depth * N_EXPERTS, d, D_FF)
    wde_all = w_down.astype(BF16).reshape(depth * N_GROUPS, EXPERTS_PER_GROUP * D_FF, d)
    xf = x.reshape(t, d)

    tri = (jnp.arange(te)[:, None] < jnp.arange(te)[None, :]).astype(BF16)
    wr_f = jnp.pad(w_router, ((0, 0), (0, LANES - N_EXPERTS))).astype(F32)
    wr_hi = wr_f.astype(BF16)
    wr = jnp.concatenate([wr_hi, (wr_f - wr_hi.astype(F32)).astype(BF16)], axis=1)
    br = b_router.reshape(N_EXPERTS, 1).astype(F32)

    moe_tt = None
    g2_prev = None
    for l in range(depth):
        sh1, sc1, g1, sh2, sc2, g2 = [mods[l, :, i] for i in range(N_ADA)]
        zb = lambda n: jnp.zeros((n,), F32)
        gb = jnp.concatenate([
            fox_f_bias[l], mlstm_f_bias[l], zb(LANES - FOX_HEADS - ML_HEADS),
            zb(GATE_ML), mlstm_i_bias[l], zb(LANES - GATE_ML - ML_HEADS)]).reshape(1, 2 * LANES)

        xf, qat, ka, fvt, qkv, mkt, fc, gr = _in_proj_call(
            xf, moe_tt, g2_prev, norm_mix[l].reshape(1, d), sc1, sh1, wm_all, wg_all, gb,
            conv_w[l], conv_b[l].reshape(1, -1), layer=l, bsz=bsz, seq=seq)
        hf = _fox_call(qat, ka, fvt, fox_out_norm[l].reshape(1, FOX_WIDTH), bsz=bsz, seq=seq)
        hm = _mlstm_call(qkv, mkt, fc, gr, mlstm_out_norm[l].reshape(1, ML_WIDTH), bsz=bsz, seq=seq)
        xf, hn_tt, gsel, grp, rank, tot = _post_call(
            xf, hf, hm, wo_all, g1, norm_ffn[l].reshape(1, d), sc2, sh2, wr, br, tri,
            layer=l, bsz=bsz, seq=seq, chunk=chunk, te=te)

        n_chunks = t // chunk
        grp = grp.reshape(n_chunks, 1, chunk)
        rank = rank.reshape(n_chunks, 1, chunk)
        off, tile_blk, tile_grp = _moe_tiles(tot[:, :N_GROUPS, 0], tm=tm, tiles_per_chunk=rows // tm)
        pos = rank
        for g in range(N_GROUPS):
            pos = pos + jnp.where(grp == g, off[:, g].reshape(n_chunks, 1, 1), 0)
        xb, gs = _scatter_call(pos, hn_tt, gsel, chunk=chunk, rows=rows, tm=tm)
        ys_tt = _experts_call(tile_blk, tile_grp, xb, gs, wge_all, wue_all, wde_all, layer=l, tm=tm)
        if l < depth - 1:
            moe_tt = _gather_call(pos, ys_tt, chunk=chunk, rows=rows, n_tok=t)
            g2_prev = g2
        else:
            out = _gather_final_call(pos, ys_tt, xf, g2, norm_final.reshape(1, d),
                                     chunk=chunk, rows=rows, seq=seq)
    return out.reshape(bsz, seq, d)
```

```python
import functools

import numpy as np
import jax
import jax.numpy as jnp
from jax import lax
from jax.experimental import pallas as pl
from jax.experimental.pallas import tpu as pltpu

F32 = jnp.float32
BF16 = jnp.bfloat16

LANES = 128
SUBLANES = 8
VMEM_BYTES_V7X = 64 * 1024 * 1024

D_MODEL = 1024
FOX_HEADS = 8
FOX_HEAD_DIM = 64
FOX_WIDTH = FOX_HEADS * FOX_HEAD_DIM
ML_HEADS = 4
ML_HEAD_DIM = 128
ML_WIDTH = ML_HEADS * ML_HEAD_DIM
CONV_WIDTH = 4
N_EXPERTS = 16
N_GROUPS = 4
EXPERTS_PER_GROUP = 4
D_FF = 512
N_ADA = 6
EPS = 1e-6
NEG = -1e30

FOX_Q = 0
FOX_F = 3 * FOX_WIDTH
ML_Q = FOX_F + FOX_HEADS
ML_I = ML_Q + 3 * ML_WIDTH
ML_F = ML_I + ML_HEADS
ML_O = ML_F + ML_HEADS
IN_COLS = ML_O + ML_WIDTH

MAIN_COLS = 7 * 512
BLK_FQ, BLK_FK, BLK_FV, BLK_MQ, BLK_MK, BLK_MV, BLK_MO = range(7)
REST_COLS = 3 * 512
OUT_MQ, OUT_MV, OUT_MO = range(3)
AUG_COLS = FOX_HEADS * 128
BIAS_TERMS = 3
VT_PAD = 16
LOG2E = 1.4426950408889634
GATE_FOX = 0
GATE_ML = FOX_HEADS
GATE_ROWS = 16

TOK_ROWS = D_MODEL // LANES


def _cparams(n_grid, vmem_mb):
    return pltpu.CompilerParams(
        dimension_semantics=("arbitrary",) * n_grid,
        vmem_limit_bytes=vmem_mb * 1024 * 1024)


def _silu(x):
    return x * jax.nn.sigmoid(x)


def _log_sigmoid(z):
    return jnp.minimum(z, 0.0) - jnp.log1p(jnp.exp(-jnp.abs(z)))


def _cumsum_rows(x):
    n = x.shape[0]
    row = lax.broadcasted_iota(jnp.int32, x.shape, 0)
    s = 1
    while s < n:
        x = x + jnp.where(row >= s, pltpu.roll(x, s, axis=0), 0.0)
        s *= 2
    return x


def _from_token_tiles(ref, n_tok):
    return jnp.concatenate(
        [ref[pl.ds(c, n_tok, stride=TOK_ROWS), :] for c in range(TOK_ROWS)], axis=1)


def _to_token_tiles(ref, val, row0=0):
    n = val.shape[0]
    for c in range(TOK_ROWS):
        ref[pl.ds(row0 * TOK_ROWS + c, n, stride=TOK_ROWS), :] = val[:, c * LANES:(c + 1) * LANES]


def _ada_kernel(c_ref, w_ref, b_ref, o_ref):
    c = c_ref[...]
    o_ref[0] = jnp.dot(_silu(c), w_ref[0], preferred_element_type=F32,
                       precision=lax.Precision.HIGHEST) + b_ref[0]


def _ada_call(c, w_ada, b_ada):
    depth, d, n = w_ada.shape
    bsz = c.shape[0]
    tn = 1536
    return pl.pallas_call(
        _ada_kernel,
        grid=(depth, n // tn),
        in_specs=[
            pl.BlockSpec((bsz, d), lambda l, j: (0, 0)),
            pl.BlockSpec((1, d, tn), lambda l, j: (l, 0, j)),
            pl.BlockSpec((1, 1, tn), lambda l, j: (l, 0, j)),
        ],
        out_specs=pl.BlockSpec((1, bsz, tn), lambda l, j: (l, 0, j)),
        out_shape=jax.ShapeDtypeStruct((depth, bsz, n), F32),
        compiler_params=_cparams(2, 32),
        name="ada_mod",
    )(c, w_ada, b_ada.reshape(depth, 1, n))


def _in_proj_kernel(*refs, tm, combine):
    if combine:
        (x_ref, ott_ref, g2_ref, nm_ref, sc_ref, sh_ref, wm_ref, wg_ref, gb_ref, cw_ref, cb_ref,
         place_ref, ones_ref,
         xn_ref, qa_ref, ka_ref, vt_ref, qkv_ref, kt_ref, fc_ref, gr_ref, fcar, ccar, hb_sc) = refs
    else:
        (x_ref, nm_ref, sc_ref, sh_ref, wm_ref, wg_ref, gb_ref, cw_ref, cb_ref,
         place_ref, ones_ref,
         qa_ref, ka_ref, vt_ref, qkv_ref, kt_ref, fc_ref, gr_ref, fcar, ccar, hb_sc) = refs

    @pl.when(pl.program_id(1) == 0)
    def _():
        fcar[...] = jnp.zeros_like(fcar)
        ccar[...] = jnp.zeros_like(ccar)

    rc = 128

    def prologue(r, carry):
        r0 = pl.multiple_of(r * rc, rc)
        x = x_ref[pl.ds(r0, rc), :]
        if combine:
            moe = jnp.concatenate(
                [ott_ref[pl.ds(r0 * TOK_ROWS + c, rc, stride=TOK_ROWS), :] for c in range(TOK_ROWS)], axis=1)
            x = x + g2_ref[0] * moe
            xn_ref[pl.ds(r0, rc), :] = x
        ms = jnp.mean(x * x, axis=1, keepdims=True)
        hn = x * lax.rsqrt(ms + EPS) * nm_ref[...]
        hb_sc[pl.ds(r0, rc), :] = (hn * (1.0 + sc_ref[0]) + sh_ref[0]).astype(BF16)
        return carry

    lax.fori_loop(0, tm // rc, prologue, 0)
    hb = hb_sc[...]

    gp = jnp.dot(hb, wg_ref[0], preferred_element_type=F32) + gb_ref[...]
    lf = _log_sigmoid(gp[:, :LANES])
    fcum = _cumsum_rows(lf) + fcar[...]
    fcar[...] = fcum[tm - 1:tm, :]
    gml = gp[:, LANES:] - fcum
    fc_ref[...] = fcum
    gr_ref[0] = gml.T[:GATE_ROWS, :]

    def mm(j):
        return jnp.dot(hb, wm_ref[0, :, j * 512:(j + 1) * 512], preferred_element_type=F32)

    def put(j, v):
        qkv_ref[:, j * 512:(j + 1) * 512] = v.astype(BF16)

    lane = lax.broadcasted_iota(jnp.int32, (tm, LANES), 1)
    fs = fcum * LOG2E
    hi = fs.astype(BF16).astype(F32)
    mid = (fs - hi).astype(BF16).astype(F32)
    low = (fs - hi - mid).astype(BF16).astype(F32)
    packed = jnp.where(lane < FOX_HEADS, hi,
                       jnp.where(lane < 2 * FOX_HEADS, pltpu.roll(mid, FOX_HEADS, axis=1),
                                 jnp.where(lane < 3 * FOX_HEADS, pltpu.roll(low, 2 * FOX_HEADS, axis=1), 0.0)))
    bias = jnp.dot(packed.astype(BF16), place_ref[...], preferred_element_type=F32) + ones_ref[...]

    def put_heads(ref, val, col0, transposed):
        lo = lane < FOX_HEAD_DIM
        for p in range(FOX_HEADS // 2):
            slab = val[:, p * LANES:(p + 1) * LANES]
            for h, data in ((2 * p, slab), (2 * p + 1, pltpu.roll(slab, FOX_HEAD_DIM, axis=1))):
                blk = jnp.where(lo, data, 0.0) + bias[:, col0 + h * LANES:col0 + (h + 1) * LANES]
                if transposed:
                    ref[0, 0, h * LANES:(h + 1) * LANES, :] = blk.T.astype(BF16)
                else:
                    ref[:, h * LANES:(h + 1) * LANES] = blk.astype(BF16)

    u = jnp.concatenate([mm(BLK_MQ), mm(BLK_MK)], axis=1)
    prev = ccar[...]
    ccar[...] = u[tm - SUBLANES:tm, :]
    row8 = lax.broadcasted_iota(jnp.int32, prev.shape, 0)
    y = cb_ref[...] + cw_ref[CONV_WIDTH - 1:CONV_WIDTH, :] * u
    for k in range(1, CONV_WIDTH):
        r = pltpu.roll(u, k, axis=0)
        top = jnp.where(row8 < k, pltpu.roll(prev, k, axis=0), r[:SUBLANES])
        shifted = jnp.concatenate([top, r[SUBLANES:]], axis=0)
        y = y + cw_ref[CONV_WIDTH - 1 - k:CONV_WIDTH - k, :] * shifted
    act = _silu(y)
    put(OUT_MQ, act[:, :ML_WIDTH])
    kt_ref[0, 0] = (act[:, ML_WIDTH:] * (ML_HEAD_DIM ** -0.5)).T.astype(BF16)

    put(OUT_MO, jax.nn.sigmoid(mm(BLK_MO)))
    put(OUT_MV, mm(BLK_MV))
    vt_ref[0, 0] = mm(BLK_FV).astype(BF16).T
    put_heads(qa_ref, mm(BLK_FQ) * (FOX_HEAD_DIM ** -0.5 * LOG2E), 0, True)
    put_heads(ka_ref, mm(BLK_FK), AUG_COLS, False)


def _bias_placement():
    place = np.zeros((LANES, 2 * AUG_COLS), np.float32)
    ones = np.zeros((1, 2 * AUG_COLS), np.float32)
    for h in range(FOX_HEADS):
        for term in range(BIAS_TERMS):
            src = term * FOX_HEADS + h
            place[src, h * LANES + FOX_HEAD_DIM + term] = 1.0
            ones[0, h * LANES + FOX_HEAD_DIM + BIAS_TERMS + term] = 1.0
            place[src, AUG_COLS + h * LANES + FOX_HEAD_DIM + BIAS_TERMS + term] = -1.0
            ones[0, AUG_COLS + h * LANES + FOX_HEAD_DIM + term] = 1.0
    return jnp.asarray(place, BF16), jnp.asarray(ones, F32)


def _in_proj_call(x, moe_tt, g2, nm, sc, sh, wm, wg, gb, cw, cb, *, layer, bsz, seq, tm=512):
    t, d = x.shape
    ns = seq // tm
    combine = moe_tt is not None
    row = lambda b, s: (b * ns + s, 0)
    per_b = lambda b, s: (b, 0, 0)
    const2 = lambda b, s: (0, 0)
    in_specs = [pl.BlockSpec((tm, d), row)]
    args = [x]
    if combine:
        in_specs += [pl.BlockSpec((tm * TOK_ROWS, LANES), row), pl.BlockSpec((1, 1, d), per_b)]
        args += [moe_tt, g2]
    in_specs += [
        pl.BlockSpec((1, d), const2),
        pl.BlockSpec((1, 1, d), per_b),
        pl.BlockSpec((1, 1, d), per_b),
        pl.BlockSpec((1, d, MAIN_COLS), lambda b, s: (layer, 0, 0)),
        pl.BlockSpec((1, d, 2 * LANES), lambda b, s: (layer, 0, 0)),
        pl.BlockSpec((1, 2 * LANES), const2),
        pl.BlockSpec((CONV_WIDTH, 2 * ML_WIDTH), const2),
        pl.BlockSpec((1, 2 * ML_WIDTH), const2),
        pl.BlockSpec((LANES, 2 * AUG_COLS), const2),
        pl.BlockSpec((1, 2 * AUG_COLS), const2),
    ]
    place, ones = _bias_placement()
    args += [nm, sc, sh, wm, wg, gb, cw, cb, place, ones]
    out_specs = []
    out_shape = []
    if combine:
        out_specs.append(pl.BlockSpec((tm, d), row))
        out_shape.append(jax.ShapeDtypeStruct((t, d), F32))
    tiled = lambda b, s: (b, s, 0, 0)
    out_specs += [
        pl.BlockSpec((1, 1, AUG_COLS, tm), tiled),
        pl.BlockSpec((tm, AUG_COLS), row),
        pl.BlockSpec((1, 1, FOX_WIDTH, tm), tiled),
        pl.BlockSpec((tm, REST_COLS), row),
        pl.BlockSpec((1, 1, ML_WIDTH, tm), tiled),
        pl.BlockSpec((tm, LANES), row),
        pl.BlockSpec((1, GATE_ROWS, tm), lambda b, s: (b, 0, s)),
    ]
    out_shape += [
        jax.ShapeDtypeStruct((bsz, ns, AUG_COLS, tm), BF16),
        jax.ShapeDtypeStruct((t, AUG_COLS), BF16),
        jax.ShapeDtypeStruct((bsz, ns, FOX_WIDTH, tm), BF16),
        jax.ShapeDtypeStruct((t, REST_COLS), BF16),
        jax.ShapeDtypeStruct((bsz, ns, ML_WIDTH, tm), BF16),
        jax.ShapeDtypeStruct((t, LANES), F32),
        jax.ShapeDtypeStruct((bsz, GATE_ROWS, seq), F32),
    ]
    outs = pl.pallas_call(
        functools.partial(_in_proj_kernel, tm=tm, combine=combine),
        grid=(bsz, ns),
        in_specs=in_specs,
        out_specs=out_specs,
        out_shape=out_shape,
        scratch_shapes=[pltpu.VMEM((1, LANES), F32), pltpu.VMEM((SUBLANES, 2 * ML_WIDTH), F32),
                        pltpu.VMEM((tm, d), BF16)],
        compiler_params=_cparams(2, 48),
        name="in_proj",
    )(*args)
    if combine:
        return outs
    return [x] + list(outs)


def _fox_kernel(qt_ref, k_ref, vt_ref, ng_ref, cm_ref, o_ref, m_sc, acc_sc, *, tq, tk, nh):
    qi = pl.program_id(2)
    ones_rows = jnp.where(lax.broadcasted_iota(jnp.int32, (VT_PAD, tk), 0) == 0, 1.0, 0.0).astype(BF16)
    for n in range(2 * nh):
        m_sc[n] = jnp.full((SUBLANES, tq // 2), NEG, F32)
        acc_sc[n] = jnp.zeros((FOX_HEAD_DIM + VT_PAD, tq // 2), F32)

    half = tk // 2
    qh = tq // 2

    def update(n, z, va):
        m_prev = m_sc[n]
        m_new = jnp.maximum(m_prev, jnp.max(z, axis=0, keepdims=True))
        alpha = jnp.exp2(m_prev - m_new)
        p = jnp.exp2(z - m_new[:1, :])
        acc_sc[n] = alpha[:1, :] * acc_sc[n] + jnp.dot(va, p.astype(BF16), preferred_element_type=F32)
        m_sc[n] = m_new

    def v_aug(j, h):
        return jnp.concatenate(
            [vt_ref[0, j, h * FOX_HEAD_DIM:(h + 1) * FOX_HEAD_DIM, :], ones_rows], axis=0)

    def below_diagonal(j, carry):
        k0 = pl.multiple_of(j * tk, tk)
        items = [(h, c) for h in range(nh) for c in range(2)]

        def logits(item):
            h, c = item
            hs = slice(h * LANES, (h + 1) * LANES)
            return jnp.dot(k_ref[pl.ds(k0, tk), hs], qt_ref[0, 0, hs, :][:, c * qh:(c + 1) * qh],
                           preferred_element_type=F32)

        ahead = 4
        zs = [logits(it) for it in items[:ahead]]
        for n, (h, c) in enumerate(items):
            z = zs[n]
            if n + ahead < len(items):
                zs.append(logits(items[n + ahead]))
            update(n, z, v_aug(j, h))
        return carry

    def diagonal(j, carry):
        k0 = pl.multiple_of(j * tk, tk)

        def logits(h):
            hs = slice(h * LANES, (h + 1) * LANES)
            qt = qt_ref[0, 0, hs, :]
            za = jnp.dot(k_ref[pl.ds(k0, half), hs], qt, preferred_element_type=F32) + cm_ref[:half, :]
            zb = (jnp.dot(k_ref[pl.ds(k0 + half, half), hs], qt[:, half:], preferred_element_type=F32)
                  + cm_ref[half:, :][:, half:])
            zb = jnp.concatenate([jnp.full((half, half), NEG, F32), zb], axis=1)
            return jnp.concatenate([za, zb], axis=0)

        ahead = 2
        zs = [logits(h) for h in range(min(ahead, nh))]
        for h in range(nh):
            z = zs[h]
            if h + ahead < nh:
                zs.append(logits(h + ahead))
            va = v_aug(j, h)
            for c in range(2):
                update(2 * h + c, z[:, c * qh:(c + 1) * qh], va)
        return carry

    lax.fori_loop(0, qi, below_diagonal, 0)
    lax.fori_loop(qi, qi + 1, diagonal, 0)

    for p in range(nh // 2):
        outs = []
        for h in (2 * p, 2 * p + 1):
            acc = jnp.concatenate([acc_sc[2 * h], acc_sc[2 * h + 1]], axis=1)
            num = acc[:FOX_HEAD_DIM, :]
            l = acc[FOX_HEAD_DIM:FOX_HEAD_DIM + 1, :]
            ms = jnp.mean(num * num, axis=0, keepdims=True)
            outs.append(num * lax.rsqrt(ms + EPS * l * l))
        ps = slice(p * LANES, (p + 1) * LANES)
        o_ref[:, ps] = (jnp.concatenate(outs, axis=0).T * ng_ref[:, ps]).astype(BF16)


def _fox_call(qat, ka, fvt, ng, *, bsz, seq, tq=512, nh=8):
    t = ka.shape[0]
    nq = seq // tq
    ngrp = FOX_HEADS // nh
    vw = nh * FOX_HEAD_DIM
    cmask = jnp.where(jnp.arange(tq)[:, None] <= jnp.arange(tq)[None, :], 0.0, NEG).astype(F32)
    return pl.pallas_call(
        functools.partial(_fox_kernel, tq=tq, tk=tq, nh=nh),
        grid=(bsz, ngrp, nq),
        in_specs=[
            pl.BlockSpec((1, 1, nh * LANES, tq), lambda b, p, i: (b, i, p, 0)),
            pl.BlockSpec((seq, nh * LANES), lambda b, p, i: (b, p)),
            pl.BlockSpec((1, nq, vw, tq), lambda b, p, i: (b, 0, p, 0)),
            pl.BlockSpec((1, vw), lambda b, p, i: (0, p)),
            pl.BlockSpec((tq, tq), lambda b, p, i: (0, 0)),
        ],
        out_specs=pl.BlockSpec((tq, vw), lambda b, p, i: (b * nq + i, p)),
        out_shape=jax.ShapeDtypeStruct((t, FOX_WIDTH), BF16),
        scratch_shapes=[pltpu.VMEM((2 * nh, SUBLANES, tq // 2), F32),
                        pltpu.VMEM((2 * nh, FOX_HEAD_DIM + VT_PAD, tq // 2), F32)],
        compiler_params=_cparams(3, 48),
        name="fox_attn",
    )(qat, ka, fvt, ng, cmask)


def _mlstm_kernel(q_ref, kt_ref, v_ref, og_ref, fc_ref, gr_ref, ng_ref, o_ref, ct_sc, u_sc, *, ch, per_tile):
    @pl.when(pl.program_id(1) == 0)
    def _():
        ct_sc[...] = jnp.zeros_like(ct_sc)
        u_sc[...] = jnp.zeros_like(u_sc)

    causal = (lax.broadcasted_iota(jnp.int32, (ch, ch), 1)
              <= lax.broadcasted_iota(jnp.int32, (ch, ch), 0))
    lane = lax.broadcasted_iota(jnp.int32, (ch, LANES), 1)
    fc = fc_ref[...]
    koff = pl.multiple_of((pl.program_id(1) % per_tile) * ch, ch)
    for h in range(ML_HEADS):
        sl = slice(h * ML_HEAD_DIM, (h + 1) * ML_HEAD_DIM)
        gl = GATE_ML + h
        q = q_ref[:, sl]
        kt = kt_ref[0, 0, sl, pl.ds(koff, ch)]
        vp = jnp.concatenate([v_ref[:, sl], jnp.where(lane == gl, 1.0, 0.0).astype(BF16)], axis=1)
        g_row = gr_ref[0, gl:gl + 1, :]
        u_prev = u_sc[h][:, :1]
        gm = jnp.where(causal, g_row, NEG)
        u_i = jnp.maximum(u_prev, jnp.max(gm, axis=1, keepdims=True))
        dmat = jnp.exp(gm - u_i)
        s = jnp.dot(q, kt, preferred_element_type=F32)
        scores = (s * dmat).astype(BF16)
        inter = jnp.exp(u_prev - u_i)
        ct = ct_sc[h]
        nd = (jnp.dot(scores, vp, preferred_element_type=F32)
              + jnp.dot(q, ct.astype(BF16), preferred_element_type=F32) * inter)
        num = nd[:, :ML_HEAD_DIM]
        den = jnp.maximum(jnp.abs(nd[:, ML_HEAD_DIM:]), jnp.exp(-(jnp.where(lane == gl, fc, 0.0) + u_i)))
        ms = jnp.mean(num * num, axis=1, keepdims=True)
        scale = lax.rsqrt(ms + EPS * den * den)[:, gl:gl + 1]
        y = num * scale * ng_ref[:, sl] * og_ref[:, sl].astype(F32)
        o_ref[:, sl] = y.astype(BF16)
        u_new = jnp.maximum(u_prev, jnp.max(g_row, axis=1, keepdims=True))
        ktw = (kt.astype(F32) * jnp.exp(g_row - u_new)).astype(BF16)
        ct_sc[h] = jnp.exp(u_prev - u_new) * ct + jnp.dot(ktw, vp, preferred_element_type=F32)
        u_sc[h] = jnp.broadcast_to(u_new, (1, LANES))


def _mlstm_call(qkv, mkt, fc, gr, ng, *, bsz, seq, ch=512):
    t = qkv.shape[0]
    nc = seq // ch
    ktile = mkt.shape[-1]
    per_tile = ktile // ch
    row = lambda b, c: (b * nc + c, 0)
    return pl.pallas_call(
        functools.partial(_mlstm_kernel, ch=ch, per_tile=per_tile),
        grid=(bsz, nc),
        in_specs=[
            pl.BlockSpec((ch, ML_WIDTH), lambda b, c: (b * nc + c, OUT_MQ)),
            pl.BlockSpec((1, 1, ML_WIDTH, ktile), lambda b, c: (b, c // per_tile, 0, 0)),
            pl.BlockSpec((ch, ML_WIDTH), lambda b, c: (b * nc + c, OUT_MV)),
            pl.BlockSpec((ch, ML_WIDTH), lambda b, c: (b * nc + c, OUT_MO)),
            pl.BlockSpec((ch, LANES), row),
            pl.BlockSpec((1, GATE_ROWS, ch), lambda b, c: (b, 0, c)),
            pl.BlockSpec((1, ML_WIDTH), lambda b, c: (0, 0)),
        ],
        out_specs=pl.BlockSpec((ch, ML_WIDTH), row),
        out_shape=jax.ShapeDtypeStruct((t, ML_WIDTH), BF16),
        scratch_shapes=[pltpu.VMEM((ML_HEADS, ML_HEAD_DIM, 2 * ML_HEAD_DIM), F32),
                        pltpu.VMEM((ML_HEADS, 1, LANES), F32)],
        compiler_params=_cparams(2, 32),
        name="mlstm",
    )(qkv, mkt, qkv, qkv, fc, gr, ng)


def _post_kernel(x_ref, hf_ref, hm_ref, wo_ref, g1_ref, nf_ref, sc_ref, sh_ref, wr_ref, br_ref, tri_ref,
                 x1_ref, hn_ref, gsel_ref, grp_ref, rank_ref, tot_ref, cnt_sc, *, te, steps_per_chunk):
    @pl.when(pl.program_id(0) % steps_per_chunk == 0)
    def _():
        cnt_sc[...] = jnp.zeros_like(cnt_sc)

    mix = (jnp.dot(hf_ref[...], wo_ref[0, :FOX_WIDTH, :], preferred_element_type=F32)
           + jnp.dot(hm_ref[...], wo_ref[0, FOX_WIDTH:, :], preferred_element_type=F32))
    x1 = x_ref[...] + g1_ref[0] * mix
    x1_ref[...] = x1
    ms = jnp.mean(x1 * x1, axis=1, keepdims=True)
    hn = x1 * lax.rsqrt(ms + EPS) * nf_ref[...]
    hn = hn * (1.0 + sc_ref[0]) + sh_ref[0]
    _to_token_tiles(hn_ref, hn)

    hn_hi = hn.astype(BF16)
    hn_lo = (hn - hn_hi.astype(F32)).astype(BF16)
    l2 = jnp.dot(hn_hi, wr_ref[...], preferred_element_type=F32)
    logits = (l2[:, :LANES] + l2[:, LANES:]
              + jnp.dot(hn_lo, wr_ref[:, :LANES], preferred_element_type=F32))
    aff = jax.nn.sigmoid(logits.T[:N_EXPERTS, :])
    sel = aff + br_ref[...]
    selr = [sel[e:e + 1, :] for e in range(N_EXPERTS)]
    affr = [aff[e:e + 1, :] for e in range(N_EXPERTS)]
    keep = [None] * N_EXPERTS
    score = []
    for g in range(N_GROUPS):
        vs = selr[g * EXPERTS_PER_GROUP:(g + 1) * EXPERTS_PER_GROUP]
        sg = jnp.zeros_like(vs[0])
        for i in range(EXPERTS_PER_GROUP):
            beaten = jnp.zeros_like(vs[0])
            for j in range(EXPERTS_PER_GROUP):
                if j != i:
                    b = (vs[j] >= vs[i]) if j < i else (vs[j] > vs[i])
                    beaten = beaten + jnp.where(b, 1.0, 0.0)
            kp = beaten < 2.0
            keep[g * EXPERTS_PER_GROUP + i] = kp
            sg = sg + jnp.where(kp, vs[i], 0.0)
        score.append(sg)
    chosen = []
    for g in range(N_GROUPS):
        lost = jnp.zeros_like(score[0])
        for g2 in range(N_GROUPS):
            if g2 != g:
                b = (score[g2] >= score[g]) if g2 < g else (score[g2] > score[g])
                lost = lost + jnp.where(b, 1.0, 0.0)
        chosen.append(jnp.where(lost < 0.5, 1.0, 0.0))
    wsel = []
    for i in range(EXPERTS_PER_GROUP):
        wi = jnp.zeros_like(score[0])
        for g in range(N_GROUPS):
            e = g * EXPERTS_PER_GROUP + i
            wi = wi + chosen[g] * jnp.where(keep[e], affr[e], 0.0)
        wsel.append(wi)
    wsum = wsel[0] + wsel[1] + wsel[2] + wsel[3]
    wsel = [w / wsum for w in wsel]

    row8 = lax.broadcasted_iota(jnp.int32, (SUBLANES, te), 0)
    gmat = jnp.zeros((SUBLANES, te), F32)
    wmat = jnp.zeros((SUBLANES, te), F32)
    grp = jnp.zeros_like(score[0])
    for g in range(N_GROUPS):
        gmat = jnp.where(row8 == g, chosen[g], gmat)
        wmat = jnp.where(row8 == g, wsel[g], wmat)
        grp = grp + g * chosen[g]
    pref = jnp.dot(gmat.astype(BF16), tri_ref[...], preferred_element_type=F32) + cnt_sc[:, :1]
    rank = jnp.sum(gmat * pref, axis=0, keepdims=True)
    tot = cnt_sc[:, :1] + jnp.sum(gmat, axis=1, keepdims=True)
    cnt_sc[...] = jnp.broadcast_to(tot, cnt_sc.shape)
    rank_ref[0] = rank.astype(jnp.int32)
    grp_ref[0] = grp.astype(jnp.int32)
    tot_ref[0] = jnp.broadcast_to(tot, (SUBLANES, LANES)).astype(jnp.int32)
    wfull = jnp.concatenate([wmat, jnp.zeros((LANES - SUBLANES, te), F32)], axis=0)
    gsel_ref[...] = wfull.T


def _post_call(x, hf, hm, wo, g1, nf, sc, sh, wr, br, tri, *, layer, bsz, seq, chunk, te=512):
    t, d = x.shape
    steps_per_chunk = chunk // te
    per_seq = seq // te
    n_steps = t // te
    n_chunks = t // chunk
    row = lambda i: (i, 0)
    per_b = lambda i: (i // per_seq, 0, 0)
    const2 = lambda i: (0, 0)
    return pl.pallas_call(
        functools.partial(_post_kernel, te=te, steps_per_chunk=steps_per_chunk),
        grid=(n_steps,),
        in_specs=[
            pl.BlockSpec((te, d), row),
            pl.BlockSpec((te, FOX_WIDTH), row),
            pl.BlockSpec((te, ML_WIDTH), row),
            pl.BlockSpec((1, d, d), lambda i: (layer, 0, 0)),
            pl.BlockSpec((1, 1, d), per_b),
            pl.BlockSpec((1, d), const2),
            pl.BlockSpec((1, 1, d), per_b),
            pl.BlockSpec((1, 1, d), per_b),
            pl.BlockSpec((d, 2 * LANES), const2),
            pl.BlockSpec((N_EXPERTS, 1), const2),
            pl.BlockSpec((te, te), const2),
        ],
        out_specs=[
            pl.BlockSpec((te, d), row),
            pl.BlockSpec((te * TOK_ROWS, LANES), row),
            pl.BlockSpec((te, LANES), row),
            pl.BlockSpec((1, 1, te), lambda i: (i, 0, 0)),
            pl.BlockSpec((1, 1, te), lambda i: (i, 0, 0)),
            pl.BlockSpec((1, SUBLANES, LANES), lambda i: (i // steps_per_chunk, 0, 0)),
        ],
        out_shape=[
            jax.ShapeDtypeStruct((t, d), F32),
            jax.ShapeDtypeStruct((t * TOK_ROWS, LANES), F32),
            jax.ShapeDtypeStruct((t, LANES), F32),
            jax.ShapeDtypeStruct((n_steps, 1, te), jnp.int32),
            jax.ShapeDtypeStruct((n_steps, 1, te), jnp.int32),
            jax.ShapeDtypeStruct((n_chunks, SUBLANES, LANES), jnp.int32),
        ],
        scratch_shapes=[pltpu.VMEM((SUBLANES, LANES), F32)],
        compiler_params=_cparams(1, 48),
        name="post_router",
    )(x, hf, hm, wo, g1, nf, sc, sh, wr, br, tri)


def _scatter_kernel(pos_ref, hn_ref, gsel_ref, xb_ref, gs_ref, xs_sc, *, chunk, rows, tm):
    xs_sc[...] = jnp.zeros_like(xs_sc)
    gs_ref[...] = jnp.zeros_like(gs_ref)

    def body(t, carry):
        p = pos_ref[0, 0, t]
        src = pl.multiple_of(t * TOK_ROWS, TOK_ROWS)
        dst = pl.multiple_of(p * TOK_ROWS, TOK_ROWS)
        xs_sc[pl.ds(dst, TOK_ROWS), :] = hn_ref[pl.ds(src, TOK_ROWS), :]
        gs_ref[pl.ds(p, 1), :] = gsel_ref[pl.ds(t, 1), :]
        return carry

    lax.fori_loop(0, chunk, body, 0, unroll=8)
    for j in range(rows // tm):
        for c in range(TOK_ROWS):
            xb_ref[j * tm:(j + 1) * tm, c * LANES:(c + 1) * LANES] = (
                xs_sc[pl.ds(j * tm * TOK_ROWS + c, tm, stride=TOK_ROWS), :].astype(BF16))


def _scatter_call(pos, hn_tt, gsel, *, chunk, rows, tm):
    t = gsel.shape[0]
    n_chunks = t // chunk
    return pl.pallas_call(
        functools.partial(_scatter_kernel, chunk=chunk, rows=rows, tm=tm),
        grid=(n_chunks,),
        in_specs=[
            pl.BlockSpec((1, 1, chunk), lambda c: (c, 0, 0), memory_space=pltpu.SMEM),
            pl.BlockSpec((chunk * TOK_ROWS, LANES), lambda c: (c, 0)),
            pl.BlockSpec((chunk, LANES), lambda c: (c, 0)),
        ],
        out_specs=[
            pl.BlockSpec((rows, D_MODEL), lambda c: (c, 0)),
            pl.BlockSpec((rows, LANES), lambda c: (c, 0)),
        ],
        out_shape=[
            jax.ShapeDtypeStruct((n_chunks * rows, D_MODEL), BF16),
            jax.ShapeDtypeStruct((n_chunks * rows, LANES), F32),
        ],
        scratch_shapes=[pltpu.VMEM((rows * TOK_ROWS, LANES), F32)],
        compiler_params=_cparams(1, 56),
        name="moe_scatter",
    )(pos, hn_tt, gsel)


def _experts_kernel(blk_ref, grp_ref, xb_ref, gs_ref, wg_ref, wu_ref, wd_ref, y_ref, *, tm):
    g = grp_ref[pl.program_id(0)]

    @pl.when(g < N_GROUPS)
    def _():
        x = xb_ref[...]
        gs = gs_ref[...]
        acts = []
        hw = D_FF // 2
        for i in range(EXPERTS_PER_GROUP):
            for c in range(2):
                cs = slice(c * hw, (c + 1) * hw)
                hg = jnp.dot(x, wg_ref[i, :, cs], preferred_element_type=F32)
                hu = jnp.dot(x, wu_ref[i, :, cs], preferred_element_type=F32)
                acts.append((_silu(hg) * hu * gs[:, i:i + 1]).astype(BF16))
        y = jnp.dot(jnp.concatenate(acts, axis=1), wd_ref[0], preferred_element_type=F32)
        _to_token_tiles(y_ref, y)

    @pl.when(g >= N_GROUPS)
    def _():
        y_ref[...] = jnp.zeros_like(y_ref)


def _experts_call(tile_blk, tile_grp, xb, gs, wg, wu, wd, *, layer, tm):
    n_rows = xb.shape[0]
    n_slots = n_rows // tm
    wmap = lambda s, blk, grp: (layer * N_GROUPS + jnp.minimum(grp[s], N_GROUPS - 1), 0, 0)
    grid_spec = pltpu.PrefetchScalarGridSpec(
        num_scalar_prefetch=2,
        grid=(n_slots,),
        in_specs=[
            pl.BlockSpec((tm, D_MODEL), lambda s, blk, grp: (blk[s], 0)),
            pl.BlockSpec((tm, LANES), lambda s, blk, grp: (blk[s], 0)),
            pl.BlockSpec((EXPERTS_PER_GROUP, D_MODEL, D_FF), wmap),
            pl.BlockSpec((EXPERTS_PER_GROUP, D_MODEL, D_FF), wmap),
            pl.BlockSpec((1, EXPERTS_PER_GROUP * D_FF, D_MODEL), wmap),
        ],
        out_specs=pl.BlockSpec((tm * TOK_ROWS, LANES), lambda s, blk, grp: (blk[s], 0)),
    )
    return pl.pallas_call(
        functools.partial(_experts_kernel, tm=tm),
        grid_spec=grid_spec,
        out_shape=jax.ShapeDtypeStruct((n_rows * TOK_ROWS, LANES), F32),
        compiler_params=_cparams(1, 48),
        name="moe_experts",
    )(tile_blk, tile_grp, xb, gs, wg, wu, wd)


def _gather_kernel(pos_ref, ys_ref, o_ref, *, chunk):
    def body(t, carry):
        p = pos_ref[0, 0, t]
        src = pl.multiple_of(p * TOK_ROWS, TOK_ROWS)
        dst = pl.multiple_of(t * TOK_ROWS, TOK_ROWS)
        o_ref[pl.ds(dst, TOK_ROWS), :] = ys_ref[pl.ds(src, TOK_ROWS), :]
        return carry

    lax.fori_loop(0, chunk, body, 0, unroll=8)


def _gather_call(pos, ys_tt, *, chunk, rows, n_tok):
    n_chunks = n_tok // chunk
    return pl.pallas_call(
        functools.partial(_gather_kernel, chunk=chunk),
        grid=(n_chunks,),
        in_specs=[
            pl.BlockSpec((1, 1, chunk), lambda c: (c, 0, 0), memory_space=pltpu.SMEM),
            pl.BlockSpec((rows * TOK_ROWS, LANES), lambda c: (c, 0)),
        ],
        out_specs=pl.BlockSpec((chunk * TOK_ROWS, LANES), lambda c: (c, 0)),
        out_shape=jax.ShapeDtypeStruct((n_tok * TOK_ROWS, LANES), F32),
        compiler_params=_cparams(1, 56),
        name="moe_gather",
    )(pos, ys_tt)


def _gather_final_kernel(pos_ref, ys_ref, x_ref, g2_ref, nf_ref, o_ref, o_sc, *, part, sub):
    base = pl.program_id(1) * part

    def body(t, carry):
        p = pos_ref[0, 0, base + t]
        src = pl.multiple_of(p * TOK_ROWS, TOK_ROWS)
        dst = pl.multiple_of(t * TOK_ROWS, TOK_ROWS)
        o_sc[pl.ds(dst, TOK_ROWS), :] = ys_ref[pl.ds(src, TOK_ROWS), :]
        return carry

    lax.fori_loop(0, part, body, 0, unroll=8)
    for r in range(part // sub):
        rs = slice(r * sub, (r + 1) * sub)
        moe = jnp.concatenate(
            [o_sc[pl.ds(r * sub * TOK_ROWS + c, sub, stride=TOK_ROWS), :] for c in range(TOK_ROWS)], axis=1)
        x = x_ref[rs, :] + g2_ref[0] * moe
        ms = jnp.mean(x * x, axis=1, keepdims=True)
        o_ref[rs, :] = x * lax.rsqrt(ms + EPS) * nf_ref[...]


def _gather_final_call(pos, ys_tt, x, g2, nf, *, chunk, rows, seq):
    n_tok, d = x.shape
    n_chunks = n_tok // chunk
    parts = 2
    part = chunk // parts
    return pl.pallas_call(
        functools.partial(_gather_final_kernel, part=part, sub=256),
        grid=(n_chunks, parts),
        in_specs=[
            pl.BlockSpec((1, 1, chunk), lambda c, s: (c, 0, 0), memory_space=pltpu.SMEM),
            pl.BlockSpec((rows * TOK_ROWS, LANES), lambda c, s: (c, 0)),
            pl.BlockSpec((part, d), lambda c, s: (c * parts + s, 0)),
            pl.BlockSpec((1, 1, d), lambda c, s: ((c * chunk + s * part) // seq, 0, 0)),
            pl.BlockSpec((1, d), lambda c, s: (0, 0)),
        ],
        out_specs=pl.BlockSpec((part, d), lambda c, s: (c * parts + s, 0)),
        out_shape=jax.ShapeDtypeStruct((n_tok, d), F32),
        scratch_shapes=[pltpu.VMEM((part * TOK_ROWS, LANES), F32)],
        compiler_params=_cparams(2, 56),
        name="moe_gather_final",
    )(pos, ys_tt, x, g2, nf)


def _moe_tiles(tot, *, tm, tiles_per_chunk):
    nt = (tot + tm - 1) // tm
    ts = jnp.cumsum(nt, axis=1) - nt
    off = (ts * tm).astype(jnp.int32)
    j = jnp.arange(tiles_per_chunk, dtype=jnp.int32)[None, :, None]
    inside = (j >= ts[:, None, :]) & (j < (ts + nt)[:, None, :])
    key = jnp.where(jnp.any(inside, axis=-1), jnp.argmax(inside, axis=-1), N_GROUPS).reshape(-1)
    order = jnp.argsort(key, stable=True).astype(jnp.int32)
    return off, order, key[order].astype(jnp.int32)


def kernel(x, c, w_in, conv_w, conv_b, fox_f_bias, mlstm_i_bias, mlstm_f_bias, fox_out_norm,
           mlstm_out_norm, w_out, w_ada, b_ada, norm_mix, norm_ffn, w_router, b_router, w_gate,
           w_up, w_down, norm_final):
    bsz, seq, d = x.shape
    depth = w_in.shape[0]
    t = bsz * seq
    assert d == D_MODEL and w_in.shape[-1] == IN_COLS
    chunk = min(2048, seq)
    tm = 256
    assert chunk % tm == 0
    rows = chunk + (N_GROUPS - 1) * tm
    te = min(1024, seq)
    assert seq % 512 == 0 and seq % chunk == 0 and chunk % te == 0

    mods = _ada_call(c, w_ada, b_ada).reshape(depth, bsz, N_ADA, 1, d)
    wm_all = jnp.concatenate([w_in[:, :, FOX_Q:FOX_F], w_in[:, :, ML_Q:ML_I], w_in[:, :, ML_O:IN_COLS]],
                             axis=2).astype(BF16)
    zpad = lambda n: jnp.zeros((depth, d, n), F32)
    wg_all = jnp.concatenate([
        w_in[:, :, FOX_F:ML_Q], w_in[:, :, ML_F:ML_O], zpad(LANES - FOX_HEADS - ML_HEADS),
        zpad(GATE_ML), w_in[:, :, ML_I:ML_F], zpad(LANES - GATE_ML - ML_HEADS)], axis=2).astype(BF16)
    wo_all = w_out.astype(BF16)
    wge_all = w_gate.astype(BF16).reshape(depth * N_EXPERTS, d, D_FF)
    wue_all = w_up.astype(BF16).reshape(depth * N_EXPERTS, d, D_FF)
    wde_all = w_down.astype(BF16).reshape(depth * N_GROUPS, EXPERTS_PER_GROUP * D_FF, d)
    xf = x.reshape(t, d)

    tri = (jnp.arange(te)[:, None] < jnp.arange(te)[None, :]).astype(BF16)
    wr_f = jnp.pad(w_router, ((0, 0), (0, LANES - N_EXPERTS))).astype(F32)
    wr_hi = wr_f.astype(BF16)
    wr = jnp.concatenate([wr_hi, (wr_f - wr_hi.astype(F32)).astype(BF16)], axis=1)
    br = b_router.reshape(N_EXPERTS, 1).astype(F32)

    moe_tt = None
    g2_prev = None
    for l in range(depth):
        sh1, sc1, g1, sh2, sc2, g2 = [mods[l, :, i] for i in range(N_ADA)]
        zb = lambda n: jnp.zeros((n,), F32)
        gb = jnp.concatenate([
            fox_f_bias[l], mlstm_f_bias[l], zb(LANES - FOX_HEADS - ML_HEADS),
            zb(GATE_ML), mlstm_i_bias[l], zb(LANES - GATE_ML - ML_HEADS)]).reshape(1, 2 * LANES)

        xf, qat, ka, fvt, qkv, mkt, fc, gr = _in_proj_call(
            xf, moe_tt, g2_prev, norm_mix[l].reshape(1, d), sc1, sh1, wm_all, wg_all, gb,
            conv_w[l], conv_b[l].reshape(1, -1), layer=l, bsz=bsz, seq=seq)
        hf = _fox_call(qat, ka, fvt, fox_out_norm[l].reshape(1, FOX_WIDTH), bsz=bsz, seq=seq)
        hm = _mlstm_call(qkv, mkt, fc, gr, mlstm_out_norm[l].reshape(1, ML_WIDTH), bsz=bsz, seq=seq)
        xf, hn_tt, gsel, grp, rank, tot = _post_call(
            xf, hf, hm, wo_all, g1, norm_ffn[l].reshape(1, d), sc2, sh2, wr, br, tri,
            layer=l, bsz=bsz, seq=seq, chunk=chunk, te=te)

        n_chunks = t // chunk
        grp = grp.reshape(n_chunks, 1, chunk)
        rank = rank.reshape(n_chunks, 1, chunk)
        off, tile_blk, tile_grp = _moe_tiles(tot[:, :N_GROUPS, 0], tm=tm, tiles_per_chunk=rows // tm)
        pos = rank
        for g in range(N_GROUPS):
            pos = pos + jnp.where(grp == g, off[:, g].reshape(n_chunks, 1, 1), 0)
        xb, gs = _scatter_call(pos, hn_tt, gsel, chunk=chunk, rows=rows, tm=tm)
        ys_tt = _experts_call(tile_blk, tile_grp, xb, gs, wge_all, wue_all, wde_all, layer=l, tm=tm)
        if l < depth - 1:
            moe_tt = _gather_call(pos, ys_tt, chunk=chunk, rows=rows, n_tok=t)
            g2_prev = g2
        else:
            out = _gather_final_call(pos, ys_tt, xf, g2, norm_final.reshape(1, d),
                                     chunk=chunk, rows=rows, seq=seq)
    return out.reshape(bsz, seq, d)
```
